```python
import jax
import jax.numpy as jnp
from jax import lax
import numpy as np

D_MODEL = 1024
BATCH = 4
SEQ = 8192
DEPTH = 1

ML_HEADS = 4
ML_HEAD_DIM = 128
ML_WIDTH = ML_HEADS * ML_HEAD_DIM
ML_CHUNK = 128
CONV_WIDTH = 4
ATT_Q_HEADS = 8
ATT_KV_HEADS = 2
ATT_HEAD_DIM = 64
ATT_WIDTH = ATT_Q_HEADS * ATT_HEAD_DIM
ATT_KV_WIDTH = ATT_KV_HEADS * ATT_HEAD_DIM
WINDOW = 128
ROPE_THETA = 10000.0
D_MIX = ML_WIDTH + ATT_WIDTH
COL_SIZES = (2 * ML_WIDTH, ML_WIDTH, ML_WIDTH, ML_HEADS, ML_HEADS, ATT_WIDTH, ATT_KV_WIDTH, ATT_KV_WIDTH)
SPLIT_POINTS = tuple(sum(COL_SIZES[:i + 1]) for i in range(len(COL_SIZES) - 1))
D_IN_PROJ = sum(COL_SIZES)
N_GROUPS = 4
EXPERTS_PER_GROUP = 8
N_EXPERTS = N_GROUPS * EXPERTS_PER_GROUP
TOP_K_INNER = 2
D_EXPERT = 256
LN_EPS = 1e-5
ALPHA = (2.0 * DEPTH) ** 0.25
BETA = (8.0 * DEPTH) ** -0.25

kernel_name = "hymba_mlstm_swa_hiermoe_deepnorm"


def layer_norm(x, w, b):
    xf = x.astype(jnp.float32)
    mu = jnp.mean(xf, axis=-1, keepdims=True)
    var = jnp.mean(jnp.square(xf - mu), axis=-1, keepdims=True)
    return ((xf - mu) * lax.rsqrt(var + LN_EPS) * w + b).astype(x.dtype)


def causal_depthwise_conv(u, w, b):
    S = u.shape[1]
    up = jnp.pad(u, ((0, 0), (CONV_WIDTH - 1, 0), (0, 0)))
    out = b
    for j in range(CONV_WIDTH):
        out = out + w[j] * up[:, j:j + S]
    return out


def _to_chunks(a):
    B, S, H = a.shape[:3]
    a = a.reshape((B, S // ML_CHUNK, ML_CHUNK, H) + a.shape[3:])
    perm = (1, 0, 3, 2) + tuple(range(4, a.ndim))
    return a.transpose(perm)


def mlstm_chunkwise(q, k, v, i_pre, f_pre):
    B, S, H, D = q.shape
    L = ML_CHUNK
    f32 = jnp.float32
    qc = _to_chunks(q.astype(f32))
    kc = _to_chunks(k.astype(f32) * (D ** -0.5))
    vc = _to_chunks(v.astype(f32))
    ic = _to_chunks(i_pre.astype(f32))
    lfc = _to_chunks(jax.nn.log_sigmoid(f_pre.astype(f32)))
    causal = jnp.tril(jnp.ones((L, L), dtype=bool))

    def step(carry, xs):
        C, n, m = carry
        q_, k_, v_, i_, lf_ = xs
        b = jnp.cumsum(lf_, axis=-1)
        log_intra = jnp.where(causal, b[..., :, None] - b[..., None, :] + i_[..., None, :], -jnp.inf)
        log_inter = b + m[..., None]
        m_row = jnp.maximum(log_inter, jnp.max(log_intra, axis=-1))
        w_intra = jnp.exp(log_intra - m_row[..., None])
        w_inter = jnp.exp(log_inter - m_row)
        s = jnp.einsum("bhld,bhsd->bhls", q_, k_) * w_intra
        num = (w_inter[..., None] * jnp.einsum("bhed,bhld->bhle", C, q_)
               + jnp.einsum("bhls,bhse->bhle", s, v_))
        den = w_inter * jnp.einsum("bhd,bhld->bhl", n, q_) + jnp.sum(s, axis=-1)
        h = num / jnp.maximum(jnp.abs(den), jnp.exp(-m_row))[..., None]
        b_last = b[..., -1]
        log_w = b_last[..., None] - b + i_
        m_new = jnp.maximum(b_last + m, jnp.max(log_w, axis=-1))
        decay = jnp.exp(b_last + m - m_new)
        w = jnp.exp(log_w - m_new[..., None])
        C_new = decay[..., None, None] * C + jnp.einsum("bhs,bhse,bhsd->bhed", w, v_, k_)
        n_new = decay[..., None] * n + jnp.einsum("bhs,bhsd->bhd", w, k_)
        return (C_new, n_new, m_new), h

    init = (jnp.zeros((B, H, D, D), f32), jnp.zeros((B, H, D), f32), jnp.zeros((B, H), f32))
    _, hc = lax.scan(step, init, (qc, kc, vc, ic, lfc))
    return hc.transpose(1, 0, 3, 2, 4).reshape(B, S, H, D)


def rope(x):
    S, D = x.shape[1], x.shape[3]
    half = D // 2
    inv_freq = ROPE_THETA ** (-jnp.arange(half, dtype=jnp.float32) / half)
    ang = jnp.arange(S, dtype=jnp.float32)[:, None] * inv_freq[None, :]
    cos = jnp.cos(ang)[None, :, None, :]
    sin = jnp.sin(ang)[None, :, None, :]
    xf = x.astype(jnp.float32)
    x1, x2 = xf[..., :half], xf[..., half:]
    return jnp.concatenate([x1 * cos - x2 * sin, x1 * sin + x2 * cos], axis=-1).astype(x.dtype)


def swa_with_sinks(q, k, v, sinks):
    B, S, HQ, D = q.shape
    HKV = k.shape[2]
    G = HQ // HKV
    Lb = WINDOW
    NB = S // Lb
    qb = q.reshape(B, NB, Lb, HKV, G, D)
    kb = k.reshape(B, NB, Lb, HKV, D)
    vb = v.reshape(B, NB, Lb, HKV, D)
    pad = ((0, 0), (1, 0), (0, 0), (0, 0), (0, 0))
    kk = jnp.concatenate([jnp.pad(kb, pad)[:, :-1], kb], axis=2)
    vv = jnp.concatenate([jnp.pad(vb, pad)[:, :-1], vb], axis=2)
    scores = jnp.einsum("bnqhgd,bnkhd->bnhgqk", qb, kk).astype(jnp.float32) * (D ** -0.5)
    qi = jnp.arange(Lb)[:, None]
    kj = jnp.arange(2 * Lb)[None, :]
    diff = Lb + qi - kj
    band = (diff >= 0) & (diff < WINDOW)
    blk = jnp.arange(NB)[:, None, None]
    valid = band[None] & ((blk * Lb + kj[None] - Lb) >= 0)
    scores = jnp.where(valid[None, :, None, None], scores, -jnp.inf)
    sink = jnp.broadcast_to(sinks.astype(jnp.float32).reshape(1, 1, HKV, G, 1, 1),
                            scores.shape[:-1] + (1,))
    probs = jax.nn.softmax(jnp.concatenate([scores, sink], axis=-1), axis=-1)[..., :-1]
    out = jnp.einsum("bnhgqk,bnkhd->bnqhgd", probs.astype(v.dtype), vv)
    return out.reshape(B, S, HQ * D)


def hybrid_mixer(x, w_in, conv_w, conv_b, gate_bias, norm_w, sinks, w_out):
    B, S, _ = x.shape
    proj = x @ w_in
    ml_qk, ml_v, ml_o, ml_i, ml_f, a_q, a_k, a_v = jnp.split(proj, SPLIT_POINTS, axis=-1)
    ml_qk = jax.nn.silu(causal_depthwise_conv(ml_qk, conv_w, conv_b))
    ml_q, ml_k = jnp.split(ml_qk, 2, axis=-1)
    hd = (B, S, ML_HEADS, ML_HEAD_DIM)
    h = mlstm_chunkwise(ml_q.reshape(hd), ml_k.reshape(hd), ml_v.reshape(hd),
                        ml_i + gate_bias[0], ml_f + gate_bias[1])
    mu = jnp.mean(h, axis=-1, keepdims=True)
    var = jnp.mean(jnp.square(h - mu), axis=-1, keepdims=True)
    h = ((h - mu) * lax.rsqrt(var + LN_EPS)).reshape(B, S, ML_WIDTH) * norm_w
    ml_out = (jax.nn.sigmoid(ml_o.astype(jnp.float32)) * h).astype(x.dtype)
    q = rope(a_q.reshape(B, S, ATT_Q_HEADS, ATT_HEAD_DIM))
    k = rope(a_k.reshape(B, S, ATT_KV_HEADS, ATT_HEAD_DIM))
    v = a_v.reshape(B, S, ATT_KV_HEADS, ATT_HEAD_DIM)
    att_out = swa_with_sinks(q, k, v, sinks)
    return jnp.concatenate([ml_out, att_out], axis=-1) @ w_out


def hierarchical_moe(x, w_group_router, b_group_router, w_expert_router, b_expert_router,
                     w_exp_gate, w_exp_up, w_exp_down):
    B, S, Dm = x.shape
    xt = x.reshape(B * S, Dm)
    g_logits = (xt @ w_group_router + b_group_router).astype(jnp.float32)
    g_prob = jax.nn.softmax(g_logits, axis=-1)
    g_idx = jnp.argmax(g_logits, axis=-1)
    g_p = jnp.take_along_axis(g_prob, g_idx[:, None], axis=-1)[:, 0]
    e_logits = (xt @ w_expert_router + b_expert_router).astype(jnp.float32)
    e_logits = e_logits.reshape(-1, N_GROUPS, EXPERTS_PER_GROUP)
    e_in = jnp.take_along_axis(e_logits, g_idx[:, None, None], axis=1)[:, 0]
    top_v, top_i = lax.top_k(e_in, TOP_K_INNER)
    top_w = jax.nn.softmax(top_v, axis=-1) * g_p[:, None]
    expert_id = g_idx[:, None] * EXPERTS_PER_GROUP + top_i
    combine = jnp.sum(jax.nn.one_hot(expert_id, N_EXPERTS, dtype=jnp.float32) * top_w[..., None],
                      axis=1).astype(x.dtype)
    out = jnp.zeros_like(xt)
    for e in range(N_EXPERTS):
        h = jax.nn.silu(xt @ w_exp_gate[e]) * (xt @ w_exp_up[e])
        out = out + combine[:, e:e + 1] * (h @ w_exp_down[e])
    return out.reshape(B, S, Dm)


def setup_inputs(seed: int = 0) -> dict:
    key = jax.random.key(seed)
    ks = jax.random.split(key, 21)
    f32 = jnp.float32

    def nrm(k, shape, scale):
        return jax.random.normal(k, shape, f32) * scale

    x = nrm(ks[0], (BATCH, SEQ, D_MODEL), 1.0)
    w_in = nrm(ks[1], (DEPTH, D_MODEL, D_IN_PROJ), D_MODEL ** -0.5)
    conv_w = nrm(ks[2], (DEPTH, CONV_WIDTH, 2 * ML_WIDTH), CONV_WIDTH ** -0.5)
    conv_b = nrm(ks[3], (DEPTH, 2 * ML_WIDTH), 0.02)
    i_bias = nrm(ks[4], (DEPTH, ML_HEADS), 0.1)
    f_bias = jnp.linspace(3.0, 6.0, ML_HEADS, dtype=f32)[None, :] + nrm(ks[5], (DEPTH, ML_HEADS), 0.1)
    mlstm_gate_bias = jnp.stack([i_bias, f_bias], axis=1)
    mlstm_norm_w = 1.0 + nrm(ks[6], (DEPTH, ML_WIDTH), 0.02)
    attn_sinks = nrm(ks[7], (DEPTH, ATT_Q_HEADS), 0.5)
    w_out = nrm(ks[8], (DEPTH, D_MIX, D_MODEL), (D_MIX ** -0.5) * BETA)
    ln1_w = 1.0 + nrm(ks[9], (DEPTH, D_MODEL), 0.02)
    ln1_b = nrm(ks[10], (DEPTH, D_MODEL), 0.02)
    w_group_router = nrm(ks[11], (DEPTH, D_MODEL, N_GROUPS), D_MODEL ** -0.5)
    b_group_router = nrm(ks[12], (DEPTH, N_GROUPS), 0.01)
    w_expert_router = nrm(ks[13], (DEPTH, D_MODEL, N_EXPERTS), D_MODEL ** -0.5)
    b_expert_router = nrm(ks[14], (DEPTH, N_EXPERTS), 0.01)
    w_exp_gate = nrm(ks[15], (DEPTH, N_EXPERTS, D_MODEL, D_EXPERT), D_MODEL ** -0.5)
    w_exp_up = nrm(ks[16], (DEPTH, N_EXPERTS, D_MODEL, D_EXPERT), D_MODEL ** -0.5)
    w_exp_down = nrm(ks[17], (DEPTH, N_EXPERTS, D_EXPERT, D_MODEL), (D_EXPERT ** -0.5) * BETA)
    ln2_w = 1.0 + nrm(ks[18], (DEPTH, D_MODEL), 0.02)
    ln2_b = nrm(ks[19], (DEPTH, D_MODEL), 0.02)
    return {"x": x, "w_in": w_in, "conv_w": conv_w, "conv_b": conv_b,
            "mlstm_gate_bias": mlstm_gate_bias, "mlstm_norm_w": mlstm_norm_w,
            "attn_sinks": attn_sinks, "w_out": w_out, "ln1_w": ln1_w, "ln1_b": ln1_b,
            "w_group_router": w_group_router, "b_group_router": b_group_router,
            "w_expert_router": w_expert_router, "b_expert_router": b_expert_router,
            "w_exp_gate": w_exp_gate, "w_exp_up": w_exp_up, "w_exp_down": w_exp_down,
            "ln2_w": ln2_w, "ln2_b": ln2_b}


def reference(x, w_in, conv_w, conv_b, mlstm_gate_bias, mlstm_norm_w, attn_sinks, w_out,
              ln1_w, ln1_b, w_group_router, b_group_router, w_expert_router, b_expert_router,
              w_exp_gate, w_exp_up, w_exp_down, ln2_w, ln2_b):
    for l in range(DEPTH):
        y = hybrid_mixer(x, w_in[l], conv_w[l], conv_b[l], mlstm_gate_bias[l], mlstm_norm_w[l],
                         attn_sinks[l], w_out[l])
        x = layer_norm(ALPHA * x + y, ln1_w[l], ln1_b[l])
        y = hierarchical_moe(x, w_group_router[l], b_group_router[l], w_expert_router[l],
                             b_expert_router[l], w_exp_gate[l], w_exp_up[l], w_exp_down[l])
        x = layer_norm(ALPHA * x + y, ln2_w[l], ln2_b[l])
    return x
```

```python
import functools
import math

import numpy as np
import jax
import jax.numpy as jnp
from jax import lax
from jax.experimental import pallas as pl
from jax.experimental.pallas import tpu as pltpu

F32 = jnp.float32
BF16 = jnp.bfloat16
I32 = jnp.int32

ML_HEADS = 4
ML_HEAD_DIM = 128
ML_WIDTH = ML_HEADS * ML_HEAD_DIM
ML_CHUNK = 128
CONV_WIDTH = 4
ATT_Q_HEADS = 8
ATT_KV_HEADS = 2
ATT_HEAD_DIM = 64
ATT_WIDTH = ATT_Q_HEADS * ATT_HEAD_DIM
ATT_KV_WIDTH = ATT_KV_HEADS * ATT_HEAD_DIM
WINDOW = 128
ROPE_THETA = 10000.0
N_GROUPS = 4
EXPERTS_PER_GROUP = 8
N_EXPERTS = N_GROUPS * EXPERTS_PER_GROUP
PAIRS_PER_GROUP = EXPERTS_PER_GROUP * (EXPERTS_PER_GROUP - 1) // 2
N_CLASSES = N_GROUPS * PAIRS_PER_GROUP
LN_EPS = 1e-5

LANES = 128
SUBLANES = 8
MOE_TILE = 128
VMEM_LIMIT = 56 * 1024 * 1024

NEG_INF = float("-inf")


def _sigmoid(x):
    return 1.0 / (1.0 + jnp.exp(-x))


def _log_sigmoid(x):
    return jnp.minimum(x, 0.0) - jnp.log(1.0 + jnp.exp(-jnp.abs(x)))


def _iota(shape, dim):
    return lax.broadcasted_iota(I32, shape, dim)


def _dot(a, b):
    return jnp.dot(a, b, preferred_element_type=F32)


def _dot_exact(a, b):
    return jnp.dot(a, b, preferred_element_type=F32, precision=lax.Precision.HIGHEST)


def _layer_norm(z, w, b):
    mu = jnp.mean(z, axis=-1, keepdims=True)
    zc = z - mu
    var = jnp.mean(zc * zc, axis=-1, keepdims=True)
    return zc * lax.rsqrt(var + LN_EPS) * w + b


C_QK = 0
C_V = C_QK + 2 * ML_WIDTH
C_O = C_V + ML_WIDTH
C_AQ = C_O + ML_WIDTH
C_AK = C_AQ + ATT_WIDTH
C_AV = C_AK + ATT_KV_HEADS * LANES
C_G = C_AV + ATT_KV_HEADS * LANES
C_END = C_G + LANES


def _pack_w_in(w_in):
    sizes = (2 * ML_WIDTH, ML_WIDTH, ML_WIDTH, ML_HEADS, ML_HEADS, ATT_WIDTH, ATT_KV_WIDTH, ATT_KV_WIDTH)
    splits = np.cumsum(sizes)[:-1].tolist()
    w_qk, w_v, w_o, w_i, w_f, w_aq, w_ak, w_av = jnp.split(w_in, splits, axis=-1)

    def dup(w):
        heads = [w[:, h * ATT_HEAD_DIM:(h + 1) * ATT_HEAD_DIM] for h in range(ATT_KV_HEADS)]
        return jnp.concatenate([t for h in heads for t in (h, h)], axis=-1)

    w_g = jnp.concatenate([w_i, w_f, jnp.zeros((w_in.shape[0], LANES - 2 * ML_HEADS), w_in.dtype)], axis=-1)
    packed = jnp.concatenate([w_qk, w_v, w_o, w_aq, dup(w_ak), dup(w_av), w_g], axis=-1)
    return packed.astype(BF16)


def _inproj_kernel(x_ref, w_ref, qk_ref, v_ref, o_ref, aq_ref, ak_ref, av_ref, g_ref):
    x = x_ref[...].astype(BF16)

    def mm(lo, hi):
        return _dot(x, w_ref[:, lo:hi])

    qk_ref[...] = mm(C_QK, C_V)
    v_ref[...] = mm(C_V, C_O).astype(BF16)
    o_ref[...] = mm(C_O, C_AQ)
    aq_ref[...] = mm(C_AQ, C_AK)
    ak_ref[...] = mm(C_AK, C_AV)
    av_ref[...] = mm(C_AV, C_G).astype(BF16)
    g_ref[...] = mm(C_G, C_END)


def _inproj(x2d, w_packed, tm):
    t, d = x2d.shape
    widths = (C_V - C_QK, C_O - C_V, C_AQ - C_O, C_AK - C_AQ, C_AV - C_AK, C_G - C_AV, C_END - C_G)
    dtypes = (F32, BF16, F32, F32, F32, BF16, F32)
    return pl.pallas_call(
        _inproj_kernel,
        grid=(t // tm,),
        in_specs=[pl.BlockSpec((tm, d), lambda i: (i, 0)),
                  pl.BlockSpec((d, C_END), lambda i: (0, 0))],
        out_specs=[pl.BlockSpec((tm, w), lambda i: (i, 0)) for w in widths],
        out_shape=[jax.ShapeDtypeStruct((t, w), dt) for w, dt in zip(widths, dtypes)],
        compiler_params=pltpu.CompilerParams(dimension_semantics=("arbitrary",),
                                             vmem_limit_bytes=VMEM_LIMIT),
        name="inproj",
    )(x2d, w_packed)


def _mlstm_kernel(qk_ref, v_ref, o_ref, g_ref, cw_ref, cb_ref, gb_ref, nw_ref, out_ref,
                  ext_ref, ct_ref, m_ref, *, tq):
    s_idx = pl.program_id(1)
    L = ML_CHUNK
    D = ML_HEAD_DIM
    halo = SUBLANES

    @pl.when(s_idx == 0)
    def _():
        ext_ref[0:halo, :] = jnp.zeros((halo, 2 * ML_WIDTH), F32)
        ct_ref[...] = jnp.zeros_like(ct_ref)
        m_ref[...] = jnp.zeros_like(m_ref)

    ext_ref[halo:halo + tq, :] = qk_ref[...]

    row = _iota((L, L), 0)
    col = _iota((L, L), 1)
    causal = col <= row
    tril = jnp.where(causal, 1.0, 0.0).astype(F32)
    ones_col = jnp.where(_iota((L, LANES), 1) == 0, 1.0, 0.0).astype(BF16)
    scale = D ** -0.5

    for c in range(tq // L):
        r0 = c * L
        conv = cb_ref[...]
        for j in range(CONV_WIDTH):
            start = halo + r0 - (CONV_WIDTH - 1) + j
            conv = conv + cw_ref[j:j + 1, :] * ext_ref[start:start + L, :]
        qk_act = conv * _sigmoid(conv)

        gates = g_ref[r0:r0 + L, :] + gb_ref[...]
        bcum = _dot_exact(tril, _log_sigmoid(gates))
        bcum_t = bcum.T
        gates_t = gates.T

        for h in range(ML_HEADS):
            bc = bcum[:, ML_HEADS + h:ML_HEADS + h + 1]
            ic = gates[:, h:h + 1]
            br = bcum_t[ML_HEADS + h:ML_HEADS + h + 1, :]
            ir = gates_t[h:h + 1, :]
            m_prev = m_ref[h:h + 1, 0:1]

            log_intra = jnp.where(causal, bc - br + ir, NEG_INF)
            log_inter = bc + m_prev
            m_row = jnp.maximum(log_inter, jnp.max(log_intra, axis=-1, keepdims=True))
            w_intra = jnp.exp(log_intra - m_row)
            w_inter = jnp.exp(log_inter - m_row)

            q_b = qk_act[:, h * D:(h + 1) * D].astype(BF16)
            k_b = (qk_act[:, ML_WIDTH + h * D:ML_WIDTH + (h + 1) * D] * scale).astype(BF16)
            v_b = v_ref[r0:r0 + L, h * D:(h + 1) * D]
            v_aug = jnp.concatenate([v_b, ones_col], axis=-1)

            s = lax.dot_general(q_b, k_b, (((1,), (1,)), ((), ())), preferred_element_type=F32) * w_intra
            ct = ct_ref[h]
            acc = w_inter * _dot(q_b, ct.astype(BF16)) + _dot(s.astype(BF16), v_aug)
            num = acc[:, :D]
            den = acc[:, D:D + 1]
            hh = num / jnp.maximum(jnp.abs(den), jnp.exp(-m_row))

            mu = jnp.mean(hh, axis=-1, keepdims=True)
            hc = hh - mu
            var = jnp.mean(hc * hc, axis=-1, keepdims=True)
            hn = hc * lax.rsqrt(var + LN_EPS) * nw_ref[:, h * D:(h + 1) * D]
            gate_o = _sigmoid(o_ref[r0:r0 + L, h * D:(h + 1) * D])
            out_ref[r0:r0 + L, h * D:(h + 1) * D] = (gate_o * hn).astype(out_ref.dtype)

            b_last = bcum[L - 1:L, ML_HEADS + h:ML_HEADS + h + 1]
            log_w_r = b_last - br + ir
            m_new = jnp.maximum(b_last + m_prev, jnp.max(log_w_r, axis=-1, keepdims=True))
            decay = jnp.exp(b_last + m_prev - m_new)
            w_c = jnp.exp(b_last - bc + ic - m_new)
            wv = (w_c * v_aug.astype(F32)).astype(BF16)
            upd = lax.dot_general(k_b, wv, (((0,), (0,)), ((), ())), preferred_element_type=F32)
            ct_ref[h] = decay * ct + upd
            m_ref[h:h + 1, :] = jnp.broadcast_to(m_new, (1, LANES))

    ext_ref[0:halo, :] = ext_ref[tq:tq + halo, :]


def _mlstm(qk, v, o, g, conv_w, conv_b, gate_bias_row, norm_w_row, tq):
    b, s, _ = qk.shape
    kern = functools.partial(_mlstm_kernel, tq=tq)

    def seq_spec(width):
        return pl.BlockSpec((None, tq, width), lambda bi, si: (bi, si, 0))

    def const_spec(shape):
        return pl.BlockSpec(shape, lambda bi, si: (0,) * len(shape))

    return pl.pallas_call(
        kern,
        grid=(b, s // tq),
        in_specs=[seq_spec(2 * ML_WIDTH), seq_spec(ML_WIDTH), seq_spec(ML_WIDTH), seq_spec(LANES),
                  const_spec((CONV_WIDTH, 2 * ML_WIDTH)), const_spec((1, 2 * ML_WIDTH)),
                  const_spec((1, LANES)), const_spec((1, ML_WIDTH))],
        out_specs=seq_spec(ML_WIDTH),
        out_shape=jax.ShapeDtypeStruct((b, s, ML_WIDTH), BF16),
        scratch_shapes=[pltpu.VMEM((tq + SUBLANES, 2 * ML_WIDTH), F32),
                        pltpu.VMEM((ML_HEADS, ML_HEAD_DIM, 2 * ML_HEAD_DIM), F32),
                        pltpu.VMEM((SUBLANES, LANES), F32)],
        compiler_params=pltpu.CompilerParams(dimension_semantics=("arbitrary", "arbitrary"),
                                             vmem_limit_bytes=VMEM_LIMIT),
        name="mlstm",
    )(qk, v, o, g, conv_w, conv_b, gate_bias_row, norm_w_row)


def _rope_tables(seq_len):
    half = ATT_HEAD_DIM // 2
    inv_freq = ROPE_THETA ** (-jnp.arange(half, dtype=F32) / half)
    ang = jnp.arange(seq_len, dtype=F32)[:, None] * inv_freq[None, :]
    cos = jnp.cos(ang)
    sin = jnp.sin(ang)
    cos_t = jnp.concatenate([cos, cos, cos, cos], axis=-1)
    sin_t = jnp.concatenate([-sin, sin, -sin, sin], axis=-1)
    return cos_t, sin_t


def _swa_kernel(sink_ref, aq_ref, ak_ref, av_ref, cos_ref, sin_ref, out_ref, kprev_ref, vprev_ref):
    i = pl.program_id(1)
    Lb = WINDOW
    half = ATT_HEAD_DIM // 2
    pairs = ATT_Q_HEADS // 2
    pairs_per_kv = pairs // ATT_KV_HEADS

    @pl.when(i == 0)
    def _():
        kprev_ref[...] = jnp.zeros_like(kprev_ref)
        vprev_ref[...] = jnp.zeros_like(vprev_ref)

    cos = cos_ref[...]
    sin = sin_ref[...]

    lower = (_iota((Lb, LANES), 1) & (ATT_HEAD_DIM - 1)) < half

    def rope(x):
        tiles = []
        for c in range(x.shape[-1] // LANES):
            xt = x[:, c * LANES:(c + 1) * LANES]
            swapped = jnp.where(lower, pltpu.roll(xt, LANES - half, 1), pltpu.roll(xt, half, 1))
            tiles.append(xt * cos + swapped * sin)
        return jnp.concatenate(tiles, axis=-1)

    q = rope(aq_ref[...]) * (ATT_HEAD_DIM ** -0.5)
    k_cur = rope(ak_ref[...]).astype(BF16)
    v_cur = av_ref[...]

    ql = _iota((Lb, 2 * Lb), 0)
    kj = _iota((Lb, 2 * Lb), 1)
    diff = Lb + ql - kj
    kpos = i * Lb + kj - Lb
    visible = jnp.where(diff >= 0, jnp.where(diff < WINDOW, jnp.where(kpos >= 0, 1, 0), 0), 0)
    bias = jnp.where(visible > 0, 0.0, NEG_INF).astype(F32)
    bias = jnp.concatenate([bias] * (2 * pairs_per_kv), axis=0)

    lane = _iota((Lb, LANES), 1)
    low_half = lane < ATT_HEAD_DIM

    for g in range(ATT_KV_HEADS):
        kk = jnp.concatenate([kprev_ref[:, g * LANES:(g + 1) * LANES], k_cur[:, g * LANES:(g + 1) * LANES]], axis=0)
        vv = jnp.concatenate([vprev_ref[:, g * LANES:(g + 1) * LANES], v_cur[:, g * LANES:(g + 1) * LANES]], axis=0)
        rows = []
        sinks = []
        for p in range(pairs_per_kv):
            pair = g * pairs_per_kv + p
            q2 = q[:, pair * LANES:(pair + 1) * LANES]
            rows.append(jnp.where(low_half, q2, 0.0))
            rows.append(jnp.where(low_half, 0.0, q2))
            sinks.append(jnp.full((Lb, 1), sink_ref[2 * pair], F32))
            sinks.append(jnp.full((Lb, 1), sink_ref[2 * pair + 1], F32))
        qs = jnp.concatenate(rows, axis=0).astype(BF16)
        sink = jnp.concatenate(sinks, axis=0)

        sc = lax.dot_general(qs, kk, (((1,), (1,)), ((), ())), preferred_element_type=F32) + bias
        m = jnp.maximum(jnp.max(sc, axis=-1, keepdims=True), sink)
        p_un = jnp.exp(sc - m)
        denom = jnp.sum(p_un, axis=-1, keepdims=True) + jnp.exp(sink - m)
        o = _dot(p_un.astype(BF16), vv) / denom
        for p in range(pairs_per_kv):
            pair = g * pairs_per_kv + p
            even = o[(2 * p) * Lb:(2 * p + 1) * Lb, :]
            odd = o[(2 * p + 1) * Lb:(2 * p + 2) * Lb, :]
            out_ref[:, pair * LANES:(pair + 1) * LANES] = jnp.where(low_half, even, odd).astype(out_ref.dtype)

    kprev_ref[...] = k_cur
    vprev_ref[...] = v_cur


def _swa(aq, ak, av, cos_t, sin_t, sinks):
    b, s, _ = aq.shape
    kvw = ATT_KV_HEADS * LANES

    def seq_spec(width):
        return pl.BlockSpec((None, WINDOW, width), lambda bi, si: (bi, si, 0))

    tab_spec = pl.BlockSpec((WINDOW, LANES), lambda bi, si: (si, 0))
    return pl.pallas_call(
        _swa_kernel,
        grid=(b, s // WINDOW),
        in_specs=[pl.BlockSpec(memory_space=pltpu.SMEM),
                  seq_spec(ATT_WIDTH), seq_spec(kvw), seq_spec(kvw), tab_spec, tab_spec],
        out_specs=seq_spec(ATT_WIDTH),
        out_shape=jax.ShapeDtypeStruct((b, s, ATT_WIDTH), BF16),
        scratch_shapes=[pltpu.VMEM((WINDOW, kvw), BF16), pltpu.VMEM((WINDOW, kvw), BF16)],
        compiler_params=pltpu.CompilerParams(dimension_semantics=("arbitrary", "arbitrary"),
                                             vmem_limit_bytes=VMEM_LIMIT),
        name="swa",
    )(sinks, aq, ak, av, cos_t, sin_t)


def _outproj_kernel(x_ref, ml_ref, att_ref, wo_ref, lnw_ref, lnb_ref, wr_ref, br_ref,
                    x1_ref, cnt_ref, *, alpha):
    step = pl.program_id(0)
    tm, d = x_ref.shape
    y = _dot(ml_ref[...], wo_ref[0:ML_WIDTH, :]) + _dot(att_ref[...], wo_ref[ML_WIDTH:, :])
    x1 = _layer_norm(alpha * x_ref[...] + y, lnw_ref[...], lnb_ref[...])
    x1_ref[:, 0:d] = x1

    logits = _dot_exact(x1, wr_ref[...]) + br_ref[...]
    lane = _iota((tm, LANES), 1).astype(F32)
    big = float(LANES)

    def first_argmax(vals):
        top = jnp.max(vals, axis=-1, keepdims=True)
        idx = jnp.min(jnp.where(vals == top, lane, big), axis=-1, keepdims=True)
        return top, idx

    g_logits = jnp.where(lane < N_GROUPS, logits, NEG_INF)
    g_top, g_idx = first_argmax(g_logits)
    g_p = 1.0 / jnp.sum(jnp.exp(g_logits - g_top), axis=-1, keepdims=True)

    e_lo = N_GROUPS + EXPERTS_PER_GROUP * g_idx
    in_group = jnp.where(lane >= e_lo, jnp.where(lane < e_lo + EXPERTS_PER_GROUP, 1.0, 0.0), 0.0)
    e_logits = jnp.where(in_group > 0, logits, NEG_INF)
    v1, i1 = first_argmax(e_logits)
    v2, i2 = first_argmax(jnp.where(lane == i1, NEG_INF, e_logits))
    r = jnp.exp(v2 - v1)
    w1 = g_p / (1.0 + r)
    w2 = g_p * r / (1.0 + r)

    a1 = i1 - e_lo
    a2 = i2 - e_lo
    lo = jnp.minimum(a1, a2)
    hi = jnp.maximum(a1, a2)
    w_lo = jnp.where(a1 < a2, w1, w2)
    w_hi = jnp.where(a1 < a2, w2, w1)
    pair_idx = (EXPERTS_PER_GROUP - 1) * lo - lo * (lo - 1.0) * 0.5 + (hi - lo - 1.0)
    cls = g_idx * PAIRS_PER_GROUP + pair_idx

    meta = jnp.where(lane == 0.0, cls, jnp.where(lane == 1.0, w_lo, jnp.where(lane == 2.0, w_hi, 0.0)))
    x1_ref[:, d:d + LANES] = meta

    @pl.when(step == 0)
    def _():
        cnt_ref[...] = jnp.zeros_like(cnt_ref)

    onehot = jnp.where(lane == cls, 1.0, 0.0)
    cnt_ref[0:1, :] += jnp.sum(onehot, axis=0, keepdims=True)


def _outproj(x2d, ml2d, att2d, w_out_b, ln_w, ln_b, w_router, b_router, alpha, tm):
    t, d = x2d.shape
    kern = functools.partial(_outproj_kernel, alpha=alpha)

    def const_spec(shape):
        return pl.BlockSpec(shape, lambda i: (0,) * len(shape))

    return pl.pallas_call(
        kern,
        grid=(t // tm,),
        in_specs=[pl.BlockSpec((tm, d), lambda i: (i, 0)),
                  pl.BlockSpec((tm, ML_WIDTH), lambda i: (i, 0)),
                  pl.BlockSpec((tm, ATT_WIDTH), lambda i: (i, 0)),
                  const_spec((ML_WIDTH + ATT_WIDTH, d)), const_spec((1, d)), const_spec((1, d)),
                  const_spec((d, LANES)), const_spec((1, LANES))],
        out_specs=[pl.BlockSpec((tm, d + LANES), lambda i: (i, 0)), const_spec((SUBLANES, LANES))],
        out_shape=[jax.ShapeDtypeStruct((t, d + LANES), F32),
                   jax.ShapeDtypeStruct((SUBLANES, LANES), F32)],
        compiler_params=pltpu.CompilerParams(dimension_semantics=("arbitrary",),
                                             vmem_limit_bytes=VMEM_LIMIT),
        name="outproj",
    )(x2d, ml2d, att2d, w_out_b, ln_w, ln_b, w_router, b_router)


def _class_expert_table():
    tab = np.zeros((SUBLANES, LANES), np.float32)
    for g in range(N_GROUPS):
        idx = 0
        for lo in range(EXPERTS_PER_GROUP):
            for hi in range(lo + 1, EXPERTS_PER_GROUP):
                c = g * PAIRS_PER_GROUP + idx
                tab[0, c] = g * EXPERTS_PER_GROUP + lo
                tab[1, c] = g * EXPERTS_PER_GROUP + hi
                idx += 1
    return tab


def _rank_kernel(meta_ref, cnt_ref, tab_ref, pos_ref, tile_ref, base_ref, run_ref, *, tb, n_tiles_pad):
    step = pl.program_id(0)
    lane8 = _iota((SUBLANES, LANES), 1)

    @pl.when(step == 0)
    def _():
        cnt = jnp.broadcast_to(cnt_ref[0:1, :], (SUBLANES, LANES))
        tiles = jnp.floor((cnt + (MOE_TILE - 1.0)) * (1.0 / MOE_TILE))
        cum = tiles
        sh = 1
        while sh < LANES:
            cum = cum + jnp.where(lane8 >= sh, pltpu.roll(cum, sh, 1), 0.0)
            sh *= 2
        excl = cum - tiles
        base_ref[...] = excl * MOE_TILE
        run_ref[...] = jnp.zeros_like(run_ref)

        ti = _iota((n_tiles_pad, LANES), 0).astype(F32)
        lane = _iota((n_tiles_pad, LANES), 1)
        done = jnp.where(lane < N_CLASSES, jnp.where(cum[0:1, :] <= ti, 1.0, 0.0), 0.0)
        t_cls = jnp.sum(done, axis=-1, keepdims=True)
        sel = jnp.where(lane.astype(F32) == t_cls, 1.0, 0.0)
        cnt_i = jnp.sum(sel * cnt[0:1, :], axis=-1, keepdims=True)
        first_i = jnp.sum(sel * excl[0:1, :], axis=-1, keepdims=True)
        rows_i = jnp.clip(cnt_i - MOE_TILE * (ti[:, 0:1] - first_i), 0.0, float(MOE_TILE))
        e_lo = jnp.sum(sel * tab_ref[0:1, :], axis=-1, keepdims=True)
        e_hi = jnp.sum(sel * tab_ref[1:2, :], axis=-1, keepdims=True)
        n_tiles = jnp.sum(jnp.where(lane < N_CLASSES, jnp.broadcast_to(tiles[0:1, :], (n_tiles_pad, LANES)), 0.0),
                          axis=-1, keepdims=True)
        info = jnp.where(lane == 0, rows_i,
                         jnp.where(lane == 1, e_lo, jnp.where(lane == 2, e_hi, jnp.where(lane == 3, n_tiles, 0.0))))
        tile_ref[...] = info.astype(I32)

    cls = meta_ref[:, 0:1]
    lane = _iota((tb, LANES), 1).astype(F32)
    onehot = jnp.where(lane == cls, 1.0, 0.0)
    strict_lower = jnp.where(_iota((tb, tb), 1) < _iota((tb, tb), 0), 1.0, 0.0).astype(BF16)
    before = _dot(strict_lower, onehot.astype(BF16))
    slot = jnp.sum(onehot * (before + run_ref[0:1, :] + base_ref[0:1, :]), axis=-1, keepdims=True)
    run_ref[...] = run_ref[...] + jnp.sum(onehot, axis=0, keepdims=True)
    slot_t = jnp.broadcast_to(slot, (tb, LANES)).T
    pos_ref[...] = slot_t[0:SUBLANES, :].astype(I32)


def _rank(x1ext, counts, d, tb, n_tiles_pad):
    t = x1ext.shape[0]
    kern = functools.partial(_rank_kernel, tb=tb, n_tiles_pad=n_tiles_pad)
    tab = jnp.asarray(_class_expert_table())
    meta_block = d // LANES
    return pl.pallas_call(
        kern,
        grid=(t // tb,),
        in_specs=[pl.BlockSpec((tb, LANES), lambda i: (i, meta_block)),
                  pl.BlockSpec((SUBLANES, LANES), lambda i: (0, 0)),
                  pl.BlockSpec((SUBLANES, LANES), lambda i: (0, 0))],
        out_specs=[pl.BlockSpec((SUBLANES, tb), lambda i: (0, i)),
                   pl.BlockSpec((n_tiles_pad, LANES), lambda i: (0, 0))],
        out_shape=[jax.ShapeDtypeStruct((SUBLANES, t), I32),
                   jax.ShapeDtypeStruct((n_tiles_pad, LANES), I32)],
        scratch_shapes=[pltpu.VMEM((SUBLANES, LANES), F32), pltpu.VMEM((SUBLANES, LANES), F32)],
        compiler_params=pltpu.CompilerParams(dimension_semantics=("arbitrary",),
                                             vmem_limit_bytes=VMEM_LIMIT),
        name="rank",
    )(x1ext, counts, tab)


def _permute_kernel(pos_ref, src_ref, *rest, tb, scatter):
    dst_ref, sem = rest[-2:]
    base = pl.program_id(0) * tb

    def issue(j, carry):
        t = base + j
        p = pos_ref[t]
        if scatter:
            pltpu.make_async_copy(src_ref.at[pl.ds(t, 1)], dst_ref.at[pl.ds(p, 1)], sem).start()
        else:
            pltpu.make_async_copy(src_ref.at[pl.ds(p, 1)], dst_ref.at[pl.ds(t, 1)], sem).start()
        return carry

    lax.fori_loop(0, tb, issue, 0, unroll=8)
    pltpu.make_async_copy(src_ref.at[pl.ds(0, tb)], dst_ref.at[pl.ds(0, tb)], sem).wait()


def _permute_rows(pos, src, n_dst, tb, scatter):
    t = pos.shape[0]
    kern = functools.partial(_permute_kernel, tb=tb, scatter=scatter)
    operands = [pos, src]
    aliases = {}
    if scatter:
        operands.append(jnp.zeros((n_dst, src.shape[1]), src.dtype))
        aliases = {2: 0}
    return pl.pallas_call(
        kern,
        grid_spec=pltpu.PrefetchScalarGridSpec(
            num_scalar_prefetch=1,
            grid=(t // tb,),
            in_specs=[pl.BlockSpec(memory_space=pl.ANY)] * (len(operands) - 1),
            out_specs=pl.BlockSpec(memory_space=pl.ANY),
            scratch_shapes=[pltpu.SemaphoreType.DMA(())],
        ),
        out_shape=jax.ShapeDtypeStruct((n_dst, src.shape[1]), src.dtype),
        input_output_aliases=aliases,
        compiler_params=pltpu.CompilerParams(dimension_semantics=("arbitrary",),
                                             has_side_effects=True),
        name="scatter_rows" if scatter else "gather_rows",
    )(*operands)


def _moe_kernel(rows_ref, elo_ref, ehi_ref, nt_ref, xs_ref, wg_lo_ref, wu_lo_ref, wd_lo_ref,
                wg_hi_ref, wu_hi_ref, wd_hi_ref, lnw_ref, lnb_ref, ys_ref, *, alpha, d):
    i = pl.program_id(0)
    rows = rows_ref[i]

    @pl.when(rows > 0)
    def _():
        valid = _iota((MOE_TILE, 1), 0) < rows
        x = jnp.where(valid, xs_ref[:, 0:d], 0.0)
        meta = xs_ref[:, d:d + LANES]
        w_lo = jnp.where(valid, meta[:, 1:2], 0.0)
        w_hi = jnp.where(valid, meta[:, 2:3], 0.0)
        xb = x.astype(BF16)

        def expert(wg_ref, wu_ref, wd_ref):
            gate = _dot(xb, wg_ref[...])
            up = _dot(xb, wu_ref[...])
            hidden = gate * _sigmoid(gate) * up
            return _dot(hidden.astype(BF16), wd_ref[...])

        y = w_lo * expert(wg_lo_ref, wu_lo_ref, wd_lo_ref) + w_hi * expert(wg_hi_ref, wu_hi_ref, wd_hi_ref)
        ys_ref[...] = _layer_norm(alpha * x + y, lnw_ref[...], lnb_ref[...])

    @pl.when(rows <= 0)
    def _():
        ys_ref[...] = jnp.zeros_like(ys_ref)


def _moe(info, xs, wg, wu, wd, ln_w, ln_b, alpha, d, n_tiles_pad):
    de = wg.shape[-1]
    kern = functools.partial(_moe_kernel, alpha=alpha, d=d)

    def last_live(i, nt_ref):
        return jnp.minimum(i, jnp.maximum(nt_ref[0] - 1, 0))

    def up_spec(which):
        return pl.BlockSpec((None, d, de),
                            lambda i, rows, elo, ehi, nt: ((elo, ehi)[which][last_live(i, nt)], 0, 0))

    def down_spec(which):
        return pl.BlockSpec((None, de, d),
                            lambda i, rows, elo, ehi, nt: ((elo, ehi)[which][last_live(i, nt)], 0, 0))

    const = pl.BlockSpec((1, d), lambda i, rows, elo, ehi, nt: (0, 0))
    rows, elo, ehi, nt = info[:, 0], info[:, 1], info[:, 2], info[0:1, 3]
    return pl.pallas_call(
        kern,
        grid_spec=pltpu.PrefetchScalarGridSpec(
            num_scalar_prefetch=4,
            grid=(n_tiles_pad,),
            in_specs=[pl.BlockSpec((MOE_TILE, d + LANES), lambda i, rows, elo, ehi, nt: (last_live(i, nt), 0)),
                      up_spec(0), up_spec(0), down_spec(0), up_spec(1), up_spec(1), down_spec(1),
                      const, const],
            out_specs=pl.BlockSpec((MOE_TILE, d), lambda i, rows, elo, ehi, nt: (i, 0)),
        ),
        out_shape=jax.ShapeDtypeStruct((n_tiles_pad * MOE_TILE, d), F32),
        compiler_params=pltpu.CompilerParams(dimension_semantics=("arbitrary",),
                                             vmem_limit_bytes=VMEM_LIMIT),
        name="moe",
    )(rows, elo, ehi, nt, xs, wg, wu, wd, wg, wu, wd, ln_w, ln_b)


def _pick_block(n, target):
    blk = min(n, target)
    while n % blk:
        blk //= 2
    return blk


def kernel(x, w_in, conv_w, conv_b, mlstm_gate_bias, mlstm_norm_w, attn_sinks, w_out, ln1_w, ln1_b,
           w_group_router, b_group_router, w_expert_router, b_expert_router,
           w_exp_gate, w_exp_up, w_exp_down, ln2_w, ln2_b):
    b, s, d = x.shape
    t = b * s
    depth = w_in.shape[0]
    alpha = (2.0 * depth) ** 0.25
    assert s % ML_CHUNK == 0 and s % WINDOW == 0 and d % LANES == 0

    tm = _pick_block(t, 512)
    tq = _pick_block(s, 4 * ML_CHUNK)
    tb_rank = _pick_block(t, 512)
    tb_dma = _pick_block(t, 2048)
    n_tiles_pad = -(-(t // MOE_TILE + N_CLASSES) // SUBLANES) * SUBLANES
    cos_t, sin_t = _rope_tables(s)

    for l in range(depth):
        x2d = x.reshape(t, d)
        qk, v, o, aq, ak, av, g = _inproj(x2d, _pack_w_in(w_in[l]), tm)
        gate_bias_row = jnp.concatenate(
            [mlstm_gate_bias[l, 0], mlstm_gate_bias[l, 1], jnp.zeros((LANES - 2 * ML_HEADS,), F32)])[None, :]
        ml = _mlstm(qk.reshape(b, s, -1), v.reshape(b, s, -1), o.reshape(b, s, -1), g.reshape(b, s, -1),
                    conv_w[l], conv_b[l][None, :], gate_bias_row, mlstm_norm_w[l][None, :], tq)
        att = _swa(aq.reshape(b, s, -1), ak.reshape(b, s, -1), av.reshape(b, s, -1), cos_t, sin_t, attn_sinks[l])

        w_router = jnp.concatenate(
            [w_group_router[l], w_expert_router[l],
             jnp.zeros((d, LANES - N_GROUPS - N_EXPERTS), F32)], axis=-1)
        b_router = jnp.concatenate(
            [b_group_router[l], b_expert_router[l], jnp.zeros((LANES - N_GROUPS - N_EXPERTS,), F32)])[None, :]
        x1ext, counts = _outproj(x2d, ml.reshape(t, -1), att.reshape(t, -1), w_out[l].astype(BF16),
                                 ln1_w[l][None, :], ln1_b[l][None, :], w_router, b_router, alpha, tm)

        pos2d, info = _rank(x1ext, counts, d, tb_rank, n_tiles_pad)
        pos = pos2d[0]
        xs = _permute_rows(pos, x1ext, n_tiles_pad * MOE_TILE, tb_dma, scatter=True)
        ys = _moe(info, xs, w_exp_gate[l].astype(BF16), w_exp_up[l].astype(BF16), w_exp_down[l].astype(BF16),
                  ln2_w[l][None, :], ln2_b[l][None, :], alpha, d, n_tiles_pad)
        out = _permute_rows(pos, ys, t, tb_dma, scatter=False)
        x = out.reshape(b, s, d)
    return x
```

```python
import functools
import math

import numpy as np
import jax
import jax.numpy as jnp
from jax import lax
from jax.experimental import pallas as pl
from jax.experimental.pallas import tpu as pltpu

F32 = jnp.float32
BF16 = jnp.bfloat16
I32 = jnp.int32

ML_HEADS = 4
ML_HEAD_DIM = 128
ML_WIDTH = ML_HEADS * ML_HEAD_DIM
ML_CHUNK = 128
CONV_WIDTH = 4
ATT_Q_HEADS = 8
ATT_KV_HEADS = 2
ATT_HEAD_DIM = 64
ATT_WIDTH = ATT_Q_HEADS * ATT_HEAD_DIM
ATT_KV_WIDTH = ATT_KV_HEADS * ATT_HEAD_DIM
WINDOW = 128
ROPE_THETA = 10000.0
N_GROUPS = 4
EXPERTS_PER_GROUP = 8
N_EXPERTS = N_GROUPS * EXPERTS_PER_GROUP
PAIRS_PER_GROUP = EXPERTS_PER_GROUP * (EXPERTS_PER_GROUP - 1) // 2
N_CLASSES = N_GROUPS * PAIRS_PER_GROUP
LN_EPS = 1e-5

LANES = 128
SUBLANES = 8
MOE_TILE = 128
VMEM_LIMIT = 56 * 1024 * 1024

NEG_INF = float("-inf")


def _sigmoid(x):
    return 1.0 / (1.0 + jnp.exp(-x))


def _log_sigmoid(x):
    return jnp.minimum(x, 0.0) - jnp.log(1.0 + jnp.exp(-jnp.abs(x)))


def _iota(shape, dim):
    return lax.broadcasted_iota(I32, shape, dim)


def _dot(a, b):
    return jnp.dot(a, b, preferred_element_type=F32)


def _dot_exact(a, b):
    return jnp.dot(a, b, preferred_element_type=F32, precision=lax.Precision.HIGHEST)


def _layer_norm(z, w, b):
    mu = jnp.mean(z, axis=-1, keepdims=True)
    zc = z - mu
    var = jnp.mean(zc * zc, axis=-1, keepdims=True)
    return zc * lax.rsqrt(var + LN_EPS) * w + b


C_QK = 0
C_V = C_QK + 2 * ML_WIDTH
C_O = C_V + ML_WIDTH
C_AQ = C_O + ML_WIDTH
C_AK = C_AQ + ATT_WIDTH
C_AV = C_AK + ATT_KV_HEADS * LANES
C_G = C_AV + ATT_KV_HEADS * LANES
C_END = C_G + LANES


def _pack_w_in(w_in):
    sizes = (2 * ML_WIDTH, ML_WIDTH, ML_WIDTH, ML_HEADS, ML_HEADS, ATT_WIDTH, ATT_KV_WIDTH, ATT_KV_WIDTH)
    splits = np.cumsum(sizes)[:-1].tolist()
    w_qk, w_v, w_o, w_i, w_f, w_aq, w_ak, w_av = jnp.split(w_in, splits, axis=-1)

    def dup(w):
        heads = [w[:, h * ATT_HEAD_DIM:(h + 1) * ATT_HEAD_DIM] for h in range(ATT_KV_HEADS)]
        return jnp.concatenate([t for h in heads for t in (h, h)], axis=-1)

    w_g = jnp.concatenate([w_i, w_f, jnp.zeros((w_in.shape[0], LANES - 2 * ML_HEADS), w_in.dtype)], axis=-1)
    packed = jnp.concatenate([w_qk, w_v, w_o, w_aq, dup(w_ak), dup(w_av), w_g], axis=-1)
    return packed.astype(BF16)


def _inproj_kernel(x_ref, w_ref, qk_ref, v_ref, o_ref, aq_ref, ak_ref, av_ref, g_ref):
    x = x_ref[...].astype(BF16)

    def mm(lo, hi):
        return _dot(x, w_ref[:, lo:hi])

    qk_ref[...] = mm(C_QK, C_V)
    v_ref[...] = mm(C_V, C_O).astype(BF16)
    o_ref[...] = mm(C_O, C_AQ)
    aq_ref[...] = mm(C_AQ, C_AK)
    ak_ref[...] = mm(C_AK, C_AV)
    av_ref[...] = mm(C_AV, C_G).astype(BF16)
    g_ref[...] = mm(C_G, C_END)


def _inproj(x2d, w_packed, tm):
    t, d = x2d.shape
    widths = (C_V - C_QK, C_O - C_V, C_AQ - C_O, C_AK - C_AQ, C_AV - C_AK, C_G - C_AV, C_END - C_G)
    dtypes = (F32, BF16, F32, F32, F32, BF16, F32)
    return pl.pallas_call(
        _inproj_kernel,
        grid=(t // tm,),
        in_specs=[pl.BlockSpec((tm, d), lambda i: (i, 0)),
                  pl.BlockSpec((d, C_END), lambda i: (0, 0))],
        out_specs=[pl.BlockSpec((tm, w), lambda i: (i, 0)) for w in widths],
        out_shape=[jax.ShapeDtypeStruct((t, w), dt) for w, dt in zip(widths, dtypes)],
        compiler_params=pltpu.CompilerParams(dimension_semantics=("arbitrary",),
                                             vmem_limit_bytes=VMEM_LIMIT),
        name="inproj",
    )(x2d, w_packed)


def _mlstm_kernel(qk_ref, v_ref, o_ref, g_ref, cw_ref, cb_ref, gb_ref, nw_ref, out_ref,
                  ext_ref, ct_ref, m_ref, *, tq):
    s_idx = pl.program_id(1)
    L = ML_CHUNK
    D = ML_HEAD_DIM
    halo = SUBLANES

    @pl.when(s_idx == 0)
    def _():
        ext_ref[0:halo, :] = jnp.zeros((halo, 2 * ML_WIDTH), F32)
        ct_ref[...] = jnp.zeros_like(ct_ref)
        m_ref[...] = jnp.zeros_like(m_ref)

    ext_ref[halo:halo + tq, :] = qk_ref[...]

    row = _iota((L, L), 0)
    col = _iota((L, L), 1)
    causal = col <= row
    tril = jnp.where(causal, 1.0, 0.0).astype(F32)
    ones_col = jnp.where(_iota((L, LANES), 1) == 0, 1.0, 0.0).astype(BF16)
    scale = D ** -0.5

    for c in range(tq // L):
        r0 = c * L
        conv = cb_ref[...]
        for j in range(CONV_WIDTH):
            start = halo + r0 - (CONV_WIDTH - 1) + j
            conv = conv + cw_ref[j:j + 1, :] * ext_ref[start:start + L, :]
        qk_act = conv * _sigmoid(conv)

        gates = g_ref[r0:r0 + L, :] + gb_ref[...]
        bcum = _dot_exact(tril, _log_sigmoid(gates))
        bcum_t = bcum.T
        gates_t = gates.T

        for h in range(ML_HEADS):
            bc = bcum[:, ML_HEADS + h:ML_HEADS + h + 1]
            ic = gates[:, h:h + 1]
            br = bcum_t[ML_HEADS + h:ML_HEADS + h + 1, :]
            ir = gates_t[h:h + 1, :]
            m_prev = m_ref[h:h + 1, 0:1]

            log_intra = jnp.where(causal, bc - br + ir, NEG_INF)
            log_inter = bc + m_prev
            m_row = jnp.maximum(log_inter, jnp.max(log_intra, axis=-1, keepdims=True))
            w_intra = jnp.exp(log_intra - m_row)
            w_inter = jnp.exp(log_inter - m_row)

            q_b = qk_act[:, h * D:(h + 1) * D].astype(BF16)
            k_b = (qk_act[:, ML_WIDTH + h * D:ML_WIDTH + (h + 1) * D] * scale).astype(BF16)
            v_b = v_ref[r0:r0 + L, h * D:(h + 1) * D]
            v_aug = jnp.concatenate([v_b, ones_col], axis=-1)

            s = lax.dot_general(q_b, k_b, (((1,), (1,)), ((), ())), preferred_element_type=F32) * w_intra
            ct = ct_ref[h]
            acc = w_inter * _dot(q_b, ct.astype(BF16)) + _dot(s.astype(BF16), v_aug)
            num = acc[:, :D]
            den = acc[:, D:D + 1]
            hh = num / jnp.maximum(jnp.abs(den), jnp.exp(-m_row))

            mu = jnp.mean(hh, axis=-1, keepdims=True)
            hc = hh - mu
            var = jnp.mean(hc * hc, axis=-1, keepdims=True)
            hn = hc * lax.rsqrt(var + LN_EPS) * nw_ref[:, h * D:(h + 1) * D]
            gate_o = _sigmoid(o_ref[r0:r0 + L, h * D:(h + 1) * D])
            out_ref[r0:r0 + L, h * D:(h + 1) * D] = (gate_o * hn).astype(out_ref.dtype)

            b_last = bcum[L - 1:L, ML_HEADS + h:ML_HEADS + h + 1]
            log_w_r = b_last - br + ir
            m_new = jnp.maximum(b_last + m_prev, jnp.max(log_w_r, axis=-1, keepdims=True))
            decay = jnp.exp(b_last + m_prev - m_new)
            w_c = jnp.exp(b_last - bc + ic - m_new)
            wv = (w_c * v_aug.astype(F32)).astype(BF16)
            upd = lax.dot_general(k_b, wv, (((0,), (0,)), ((), ())), preferred_element_type=F32)
            ct_ref[h] = decay * ct + upd
            m_ref[h:h + 1, :] = jnp.broadcast_to(m_new, (1, LANES))

    ext_ref[0:halo, :] = ext_ref[tq:tq + halo, :]


def _mlstm(qk, v, o, g, conv_w, conv_b, gate_bias_row, norm_w_row, tq):
    b, s, _ = qk.shape
    kern = functools.partial(_mlstm_kernel, tq=tq)

    def seq_spec(width):
        return pl.BlockSpec((None, tq, width), lambda bi, si: (bi, si, 0))

    def const_spec(shape):
        return pl.BlockSpec(shape, lambda bi, si: (0,) * len(shape))

    return pl.pallas_call(
        kern,
        grid=(b, s // tq),
        in_specs=[seq_spec(2 * ML_WIDTH), seq_spec(ML_WIDTH), seq_spec(ML_WIDTH), seq_spec(LANES),
                  const_spec((CONV_WIDTH, 2 * ML_WIDTH)), const_spec((1, 2 * ML_WIDTH)),
                  const_spec((1, LANES)), const_spec((1, ML_WIDTH))],
        out_specs=seq_spec(ML_WIDTH),
        out_shape=jax.ShapeDtypeStruct((b, s, ML_WIDTH), BF16),
        scratch_shapes=[pltpu.VMEM((tq + SUBLANES, 2 * ML_WIDTH), F32),
                        pltpu.VMEM((ML_HEADS, ML_HEAD_DIM, 2 * ML_HEAD_DIM), F32),
                        pltpu.VMEM((SUBLANES, LANES), F32)],
        compiler_params=pltpu.CompilerParams(dimension_semantics=("arbitrary", "arbitrary"),
                                             vmem_limit_bytes=VMEM_LIMIT),
        name="mlstm",
    )(qk, v, o, g, conv_w, conv_b, gate_bias_row, norm_w_row)


def _rope_tables(seq_len):
    half = ATT_HEAD_DIM // 2
    inv_freq = ROPE_THETA ** (-jnp.arange(half, dtype=F32) / half)
    ang = jnp.arange(seq_len, dtype=F32)[:, None] * inv_freq[None, :]
    cos = jnp.cos(ang)
    sin = jnp.sin(ang)
    cos_t = jnp.concatenate([cos, cos, cos, cos], axis=-1)
    sin_t = jnp.concatenate([-sin, sin, -sin, sin], axis=-1)
    return cos_t, sin_t


def _swa_kernel(sink_ref, aq_ref, ak_ref, av_ref, cos_ref, sin_ref, out_ref, kprev_ref, vprev_ref):
    i = pl.program_id(1)
    Lb = WINDOW
    half = ATT_HEAD_DIM // 2
    pairs = ATT_Q_HEADS // 2
    pairs_per_kv = pairs // ATT_KV_HEADS

    @pl.when(i == 0)
    def _():
        kprev_ref[...] = jnp.zeros_like(kprev_ref)
        vprev_ref[...] = jnp.zeros_like(vprev_ref)

    cos = cos_ref[...]
    sin = sin_ref[...]

    lower = (_iota((Lb, LANES), 1) & (ATT_HEAD_DIM - 1)) < half

    def rope(x):
        tiles = []
        for c in range(x.shape[-1] // LANES):
            xt = x[:, c * LANES:(c + 1) * LANES]
            swapped = jnp.where(lower, pltpu.roll(xt, LANES - half, 1), pltpu.roll(xt, half, 1))
            tiles.append(xt * cos + swapped * sin)
        return jnp.concatenate(tiles, axis=-1)

    q = rope(aq_ref[...]) * (ATT_HEAD_DIM ** -0.5)
    k_cur = rope(ak_ref[...]).astype(BF16)
    v_cur = av_ref[...]

    ql = _iota((Lb, 2 * Lb), 0)
    kj = _iota((Lb, 2 * Lb), 1)
    diff = Lb + ql - kj
    kpos = i * Lb + kj - Lb
    visible = jnp.where(diff >= 0, jnp.where(diff < WINDOW, jnp.where(kpos >= 0, 1, 0), 0), 0)
    bias = jnp.where(visible > 0, 0.0, NEG_INF).astype(F32)
    bias = jnp.concatenate([bias] * (2 * pairs_per_kv), axis=0)

    lane = _iota((Lb, LANES), 1)
    low_half = lane < ATT_HEAD_DIM

    for g in range(ATT_KV_HEADS):
        kk = jnp.concatenate([kprev_ref[:, g * LANES:(g + 1) * LANES], k_cur[:, g * LANES:(g + 1) * LANES]], axis=0)
        vv = jnp.concatenate([vprev_ref[:, g * LANES:(g + 1) * LANES], v_cur[:, g * LANES:(g + 1) * LANES]], axis=0)
        rows = []
        sinks = []
        for p in range(pairs_per_kv):
            pair = g * pairs_per_kv + p
            q2 = q[:, pair * LANES:(pair + 1) * LANES]
            rows.append(jnp.where(low_half, q2, 0.0))
            rows.append(jnp.where(low_half, 0.0, q2))
            sinks.append(jnp.full((Lb, 1), sink_ref[2 * pair], F32))
            sinks.append(jnp.full((Lb, 1), sink_ref[2 * pair + 1], F32))
        qs = jnp.concatenate(rows, axis=0).astype(BF16)
        sink = jnp.concatenate(sinks, axis=0)

        sc = lax.dot_general(qs, kk, (((1,), (1,)), ((), ())), preferred_element_type=F32) + bias
        m = jnp.maximum(jnp.max(sc, axis=-1, keepdims=True), sink)
        p_un = jnp.exp(sc - m)
        denom = jnp.sum(p_un, axis=-1, keepdims=True) + jnp.exp(sink - m)
        o = _dot(p_un.astype(BF16), vv) / denom
        for p in range(pairs_per_kv):
            pair = g * pairs_per_kv + p
            even = o[(2 * p) * Lb:(2 * p + 1) * Lb, :]
            odd = o[(2 * p + 1) * Lb:(2 * p + 2) * Lb, :]
            out_ref[:, pair * LANES:(pair + 1) * LANES] = jnp.where(low_half, even, odd).astype(out_ref.dtype)

    kprev_ref[...] = k_cur
    vprev_ref[...] = v_cur


def _swa(aq, ak, av, cos_t, sin_t, sinks):
    b, s, _ = aq.shape
    kvw = ATT_KV_HEADS * LANES

    def seq_spec(width):
        return pl.BlockSpec((None, WINDOW, width), lambda bi, si: (bi, si, 0))

    tab_spec = pl.BlockSpec((WINDOW, LANES), lambda bi, si: (si, 0))
    return pl.pallas_call(
        _swa_kernel,
        grid=(b, s // WINDOW),
        in_specs=[pl.BlockSpec(memory_space=pltpu.SMEM),
                  seq_spec(ATT_WIDTH), seq_spec(kvw), seq_spec(kvw), tab_spec, tab_spec],
        out_specs=seq_spec(ATT_WIDTH),
        out_shape=jax.ShapeDtypeStruct((b, s, ATT_WIDTH), BF16),
        scratch_shapes=[pltpu.VMEM((WINDOW, kvw), BF16), pltpu.VMEM((WINDOW, kvw), BF16)],
        compiler_params=pltpu.CompilerParams(dimension_semantics=("arbitrary", "arbitrary"),
                                             vmem_limit_bytes=VMEM_LIMIT),
        name="swa",
    )(sinks, aq, ak, av, cos_t, sin_t)


def _store_token_tiles(ref, val, row0=0, rows_per_token=None):
    n, w = val.shape
    segs = w // LANES
    rpt = rows_per_token or segs
    for j in range(segs):
        ref[pl.ds(row0 + j, n, stride=rpt), :] = val[:, j * LANES:(j + 1) * LANES]


def _load_token_tiles(ref, n, segs, row0=0, rows_per_token=None):
    rpt = rows_per_token or segs
    return jnp.concatenate([ref[pl.ds(row0 + j, n, stride=rpt), :] for j in range(segs)], axis=-1)


def _outproj_kernel(x_ref, ml_ref, att_ref, wo_ref, lnw_ref, lnb_ref, wr_ref, br_ref,
                    x1t_ref, meta_ref, cnt_ref, *, alpha):
    step = pl.program_id(0)
    tm, d = x_ref.shape
    y = _dot(ml_ref[...], wo_ref[0:ML_WIDTH, :]) + _dot(att_ref[...], wo_ref[ML_WIDTH:, :])
    x1 = _layer_norm(alpha * x_ref[...] + y, lnw_ref[...], lnb_ref[...])
    _store_token_tiles(x1t_ref, x1)

    logits = _dot_exact(x1, wr_ref[...]) + br_ref[...]
    lane = _iota((tm, LANES), 1).astype(F32)
    big = float(LANES)

    def first_argmax(vals):
        top = jnp.max(vals, axis=-1, keepdims=True)
        idx = jnp.min(jnp.where(vals == top, lane, big), axis=-1, keepdims=True)
        return top, idx

    g_logits = jnp.where(lane < N_GROUPS, logits, NEG_INF)
    g_top, g_idx = first_argmax(g_logits)
    g_p = 1.0 / jnp.sum(jnp.exp(g_logits - g_top), axis=-1, keepdims=True)

    e_lo = N_GROUPS + EXPERTS_PER_GROUP * g_idx
    in_group = jnp.where(lane >= e_lo, jnp.where(lane < e_lo + EXPERTS_PER_GROUP, 1.0, 0.0), 0.0)
    e_logits = jnp.where(in_group > 0, logits, NEG_INF)
    v1, i1 = first_argmax(e_logits)
    v2, i2 = first_argmax(jnp.where(lane == i1, NEG_INF, e_logits))
    r = jnp.exp(v2 - v1)
    w1 = g_p / (1.0 + r)
    w2 = g_p * r / (1.0 + r)

    a1 = i1 - e_lo
    a2 = i2 - e_lo
    lo = jnp.minimum(a1, a2)
    hi = jnp.maximum(a1, a2)
    w_lo = jnp.where(a1 < a2, w1, w2)
    w_hi = jnp.where(a1 < a2, w2, w1)
    pair_idx = (EXPERTS_PER_GROUP - 1) * lo - lo * (lo - 1.0) * 0.5 + (hi - lo - 1.0)
    cls = g_idx * PAIRS_PER_GROUP + pair_idx

    meta = jnp.where(lane == 0.0, cls, jnp.where(lane == 1.0, w_lo, jnp.where(lane == 2.0, w_hi, 0.0)))
    meta_ref[...] = meta

    @pl.when(step == 0)
    def _():
        cnt_ref[...] = jnp.zeros_like(cnt_ref)

    onehot = jnp.where(lane == cls, 1.0, 0.0)
    cnt_ref[0:1, :] += jnp.sum(onehot, axis=0, keepdims=True)


def _outproj(x2d, ml2d, att2d, w_out_b, ln_w, ln_b, w_router, b_router, alpha, tm):
    t, d = x2d.shape
    kern = functools.partial(_outproj_kernel, alpha=alpha)

    def const_spec(shape):
        return pl.BlockSpec(shape, lambda i: (0,) * len(shape))

    return pl.pallas_call(
        kern,
        grid=(t // tm,),
        in_specs=[pl.BlockSpec((tm, d), lambda i: (i, 0)),
                  pl.BlockSpec((tm, ML_WIDTH), lambda i: (i, 0)),
                  pl.BlockSpec((tm, ATT_WIDTH), lambda i: (i, 0)),
                  const_spec((ML_WIDTH + ATT_WIDTH, d)), const_spec((1, d)), const_spec((1, d)),
                  const_spec((d, LANES)), const_spec((1, LANES))],
        out_specs=[pl.BlockSpec((tm * (d // LANES), LANES), lambda i: (i, 0)),
                   pl.BlockSpec((tm, LANES), lambda i: (i, 0)), const_spec((SUBLANES, LANES))],
        out_shape=[jax.ShapeDtypeStruct((t * (d // LANES), LANES), F32),
                   jax.ShapeDtypeStruct((t, LANES), F32),
                   jax.ShapeDtypeStruct((SUBLANES, LANES), F32)],
        compiler_params=pltpu.CompilerParams(dimension_semantics=("arbitrary",),
                                             vmem_limit_bytes=VMEM_LIMIT),
        name="outproj",
    )(x2d, ml2d, att2d, w_out_b, ln_w, ln_b, w_router, b_router)


def _class_expert_table():
    tab = np.zeros((SUBLANES, LANES), np.float32)
    for g in range(N_GROUPS):
        idx = 0
        for lo in range(EXPERTS_PER_GROUP):
            for hi in range(lo + 1, EXPERTS_PER_GROUP):
                c = g * PAIRS_PER_GROUP + idx
                tab[0, c] = g * EXPERTS_PER_GROUP + lo
                tab[1, c] = g * EXPERTS_PER_GROUP + hi
                idx += 1
    return tab


def _rank_kernel(meta_ref, cnt_ref, tab_ref, pos_ref, tile_ref, base_ref, run_ref, *, tb, n_tiles_pad):
    step = pl.program_id(0)
    lane8 = _iota((SUBLANES, LANES), 1)

    @pl.when(step == 0)
    def _():
        cnt = jnp.broadcast_to(cnt_ref[0:1, :], (SUBLANES, LANES))
        tiles = jnp.floor((cnt + (MOE_TILE - 1.0)) * (1.0 / MOE_TILE))
        cum = tiles
        sh = 1
        while sh < LANES:
            cum = cum + jnp.where(lane8 >= sh, pltpu.roll(cum, sh, 1), 0.0)
            sh *= 2
        excl = cum - tiles
        base_ref[...] = excl * MOE_TILE
        run_ref[...] = jnp.zeros_like(run_ref)

        ti = _iota((n_tiles_pad, LANES), 0).astype(F32)
        lane = _iota((n_tiles_pad, LANES), 1)
        done = jnp.where(lane < N_CLASSES, jnp.where(cum[0:1, :] <= ti, 1.0, 0.0), 0.0)
        t_cls = jnp.sum(done, axis=-1, keepdims=True)
        sel = jnp.where(lane.astype(F32) == t_cls, 1.0, 0.0)
        cnt_i = jnp.sum(sel * cnt[0:1, :], axis=-1, keepdims=True)
        first_i = jnp.sum(sel * excl[0:1, :], axis=-1, keepdims=True)
        rows_i = jnp.clip(cnt_i - MOE_TILE * (ti[:, 0:1] - first_i), 0.0, float(MOE_TILE))
        e_lo = jnp.sum(sel * tab_ref[0:1, :], axis=-1, keepdims=True)
        e_hi = jnp.sum(sel * tab_ref[1:2, :], axis=-1, keepdims=True)
        n_tiles = jnp.sum(jnp.where(lane < N_CLASSES, jnp.broadcast_to(tiles[0:1, :], (n_tiles_pad, LANES)), 0.0),
                          axis=-1, keepdims=True)
        info = jnp.where(lane == 0, rows_i,
                         jnp.where(lane == 1, e_lo, jnp.where(lane == 2, e_hi, jnp.where(lane == 3, n_tiles, 0.0))))
        tile_ref[...] = info.astype(I32)

    cls = meta_ref[:, 0:1]
    lane = _iota((tb, LANES), 1).astype(F32)
    onehot = jnp.where(lane == cls, 1.0, 0.0)
    strict_lower = jnp.where(_iota((tb, tb), 1) < _iota((tb, tb), 0), 1.0, 0.0).astype(BF16)
    before = _dot(strict_lower, onehot.astype(BF16))
    slot = jnp.sum(onehot * (before + run_ref[0:1, :] + base_ref[0:1, :]), axis=-1, keepdims=True)
    run_ref[...] = run_ref[...] + jnp.sum(onehot, axis=0, keepdims=True)
    slot_t = jnp.broadcast_to(slot, (tb, LANES)).T
    pos_ref[...] = slot_t[0:SUBLANES, :].astype(I32)


def _rank(meta, counts, tb, n_tiles_pad):
    t = meta.shape[0]
    kern = functools.partial(_rank_kernel, tb=tb, n_tiles_pad=n_tiles_pad)
    tab = jnp.asarray(_class_expert_table())
    return pl.pallas_call(
        kern,
        grid=(t // tb,),
        in_specs=[pl.BlockSpec((tb, LANES), lambda i: (i, 0)),
                  pl.BlockSpec((SUBLANES, LANES), lambda i: (0, 0)),
                  pl.BlockSpec((SUBLANES, LANES), lambda i: (0, 0))],
        out_specs=[pl.BlockSpec((SUBLANES, tb), lambda i: (0, i)),
                   pl.BlockSpec((n_tiles_pad, LANES), lambda i: (0, 0))],
        out_shape=[jax.ShapeDtypeStruct((SUBLANES, t), I32),
                   jax.ShapeDtypeStruct((n_tiles_pad, LANES), I32)],
        scratch_shapes=[pltpu.VMEM((SUBLANES, LANES), F32), pltpu.VMEM((SUBLANES, LANES), F32)],
        compiler_params=pltpu.CompilerParams(dimension_semantics=("arbitrary",),
                                             vmem_limit_bytes=VMEM_LIMIT),
        name="rank",
    )(meta, counts, tab)


def _dispatch_kernel(pos_ref, src_ref, zeros_ref, dst_ref, sem, *, tb, rpt):
    del zeros_ref
    base = pl.program_id(0) * tb

    def issue(j, carry):
        t = base + j
        src = src_ref.at[pl.ds(pl.multiple_of(t * rpt, rpt), rpt)]
        dst = dst_ref.at[pl.ds(pl.multiple_of(pos_ref[t] * rpt, rpt), rpt)]
        pltpu.make_async_copy(src, dst, sem).start()
        return carry

    lax.fori_loop(0, tb, issue, 0, unroll=8)
    pltpu.make_async_copy(src_ref.at[pl.ds(0, tb * rpt)], dst_ref.at[pl.ds(0, tb * rpt)], sem).wait()


def _dispatch(pos, src, n_slots, rpt, tb):
    t = pos.shape[0]
    kern = functools.partial(_dispatch_kernel, tb=tb, rpt=rpt)
    return pl.pallas_call(
        kern,
        grid_spec=pltpu.PrefetchScalarGridSpec(
            num_scalar_prefetch=1,
            grid=(t // tb,),
            in_specs=[pl.BlockSpec(memory_space=pl.ANY), pl.BlockSpec(memory_space=pl.ANY)],
            out_specs=pl.BlockSpec(memory_space=pl.ANY),
            scratch_shapes=[pltpu.SemaphoreType.DMA(())],
        ),
        out_shape=jax.ShapeDtypeStruct((n_slots * rpt, LANES), src.dtype),
        input_output_aliases={2: 0},
        compiler_params=pltpu.CompilerParams(dimension_semantics=("arbitrary",),
                                             has_side_effects=True),
        name="dispatch",
    )(pos, src, jnp.zeros((n_slots * rpt, LANES), src.dtype))


def _collect_kernel(pos_ref, ys_ref, x1t_ref, meta_ref, lnw_ref, lnb_ref, out_ref, buf_ref, sem,
                    *, tb, alpha, d):
    segs = d // LANES
    rpt = 2 * segs
    base = pl.program_id(0) * tb

    def issue(j, carry):
        src = ys_ref.at[pl.ds(pl.multiple_of(pos_ref[base + j] * rpt, rpt), rpt)]
        dst = buf_ref.at[pl.ds(pl.multiple_of(j * rpt, rpt), rpt)]
        pltpu.make_async_copy(src, dst, sem).start()
        return carry

    lax.fori_loop(0, tb, issue, 0, unroll=8)
    pltpu.make_async_copy(ys_ref.at[pl.ds(0, tb * rpt)], buf_ref, sem).wait()

    x1 = _load_token_tiles(x1t_ref, tb, segs)
    y_lo = _load_token_tiles(buf_ref, tb, segs, row0=0, rows_per_token=rpt)
    y_hi = _load_token_tiles(buf_ref, tb, segs, row0=segs, rows_per_token=rpt)
    meta = meta_ref[...]
    z = alpha * x1 + meta[:, 1:2] * y_lo + meta[:, 2:3] * y_hi
    out_ref[...] = _layer_norm(z, lnw_ref[...], lnb_ref[...])


def _collect(pos, ys, x1t, meta, ln_w, ln_b, alpha, d, tb):
    t = pos.shape[0]
    segs = d // LANES
    kern = functools.partial(_collect_kernel, tb=tb, alpha=alpha, d=d)
    return pl.pallas_call(
        kern,
        grid_spec=pltpu.PrefetchScalarGridSpec(
            num_scalar_prefetch=1,
            grid=(t // tb,),
            in_specs=[pl.BlockSpec(memory_space=pl.ANY),
                      pl.BlockSpec((tb * segs, LANES), lambda i, pos_ref: (i, 0)),
                      pl.BlockSpec((tb, LANES), lambda i, pos_ref: (i, 0)),
                      pl.BlockSpec((1, d), lambda i, pos_ref: (0, 0)),
                      pl.BlockSpec((1, d), lambda i, pos_ref: (0, 0))],
            out_specs=pl.BlockSpec((tb, d), lambda i, pos_ref: (i, 0)),
            scratch_shapes=[pltpu.VMEM((tb * 2 * segs, LANES), F32), pltpu.SemaphoreType.DMA(())],
        ),
        out_shape=jax.ShapeDtypeStruct((t, d), F32),
        compiler_params=pltpu.CompilerParams(dimension_semantics=("arbitrary",),
                                             vmem_limit_bytes=VMEM_LIMIT),
        name="collect",
    )(pos, ys, x1t, meta, ln_w, ln_b)


def _moe_kernel(rows_ref, elo_ref, ehi_ref, nt_ref, xs_ref, wg_lo_ref, wu_lo_ref, wd_lo_ref,
                wg_hi_ref, wu_hi_ref, wd_hi_ref, ys_ref, *, d):
    i = pl.program_id(0)
    rows = rows_ref[i]
    segs = d // LANES

    @pl.when(rows > 0)
    def _():
        xb = _load_token_tiles(xs_ref, MOE_TILE, segs).astype(BF16)

        def expert(wg_ref, wu_ref, wd_ref):
            gate = _dot(xb, wg_ref[...])
            up = _dot(xb, wu_ref[...])
            hidden = gate * _sigmoid(gate) * up
            return _dot(hidden.astype(BF16), wd_ref[...])

        _store_token_tiles(ys_ref, expert(wg_lo_ref, wu_lo_ref, wd_lo_ref), row0=0, rows_per_token=2 * segs)
        _store_token_tiles(ys_ref, expert(wg_hi_ref, wu_hi_ref, wd_hi_ref), row0=segs, rows_per_token=2 * segs)

    @pl.when(rows <= 0)
    def _():
        ys_ref[...] = jnp.zeros_like(ys_ref)


def _moe(info, xs, wg, wu, wd, d, n_tiles_pad):
    de = wg.shape[-1]
    segs = d // LANES
    kern = functools.partial(_moe_kernel, d=d)

    def last_live(i, nt_ref):
        return jnp.minimum(i, jnp.maximum(nt_ref[0] - 1, 0))

    def up_spec(which):
        return pl.BlockSpec((None, d, de),
                            lambda i, rows, elo, ehi, nt: ((elo, ehi)[which][last_live(i, nt)], 0, 0))

    def down_spec(which):
        return pl.BlockSpec((None, de, d),
                            lambda i, rows, elo, ehi, nt: ((elo, ehi)[which][last_live(i, nt)], 0, 0))

    rows, elo, ehi, nt = info[:, 0], info[:, 1], info[:, 2], info[0:1, 3]
    return pl.pallas_call(
        kern,
        grid_spec=pltpu.PrefetchScalarGridSpec(
            num_scalar_prefetch=4,
            grid=(n_tiles_pad,),
            in_specs=[pl.BlockSpec((MOE_TILE * segs, LANES),
                                   lambda i, rows, elo, ehi, nt: (last_live(i, nt), 0)),
                      up_spec(0), up_spec(0), down_spec(0), up_spec(1), up_spec(1), down_spec(1)],
            out_specs=pl.BlockSpec((MOE_TILE * 2 * segs, LANES), lambda i, rows, elo, ehi, nt: (i, 0)),
        ),
        out_shape=jax.ShapeDtypeStruct((n_tiles_pad * MOE_TILE * 2 * segs, LANES), F32),
        compiler_params=pltpu.CompilerParams(dimension_semantics=("arbitrary",),
                                             vmem_limit_bytes=VMEM_LIMIT),
        name="moe",
    )(rows, elo, ehi, nt, xs, wg, wu, wd, wg, wu, wd)


def _pick_block(n, target):
    blk = min(n, target)
    while n % blk:
        blk //= 2
    return blk


def kernel(x, w_in, conv_w, conv_b, mlstm_gate_bias, mlstm_norm_w, attn_sinks, w_out, ln1_w, ln1_b,
           w_group_router, b_group_router, w_expert_router, b_expert_router,
           w_exp_gate, w_exp_up, w_exp_down, ln2_w, ln2_b):
    b, s, d = x.shape
    t = b * s
    depth = w_in.shape[0]
    alpha = (2.0 * depth) ** 0.25
    assert s % ML_CHUNK == 0 and s % WINDOW == 0 and d % LANES == 0

    tm = _pick_block(t, 512)
    tq = _pick_block(s, 4 * ML_CHUNK)
    tb_rank = _pick_block(t, 512)
    tb_dma = _pick_block(t, 2048)
    tb_col = _pick_block(t, 256)
    n_tiles_pad = -(-(t // MOE_TILE + N_CLASSES) // SUBLANES) * SUBLANES
    cos_t, sin_t = _rope_tables(s)

    for l in range(depth):
        x2d = x.reshape(t, d)
        qk, v, o, aq, ak, av, g = _inproj(x2d, _pack_w_in(w_in[l]), tm)
        gate_bias_row = jnp.concatenate(
            [mlstm_gate_bias[l, 0], mlstm_gate_bias[l, 1], jnp.zeros((LANES - 2 * ML_HEADS,), F32)])[None, :]
        ml = _mlstm(qk.reshape(b, s, -1), v.reshape(b, s, -1), o.reshape(b, s, -1), g.reshape(b, s, -1),
                    conv_w[l], conv_b[l][None, :], gate_bias_row, mlstm_norm_w[l][None, :], tq)
        att = _swa(aq.reshape(b, s, -1), ak.reshape(b, s, -1), av.reshape(b, s, -1), cos_t, sin_t, attn_sinks[l])

        w_router = jnp.concatenate(
            [w_group_router[l], w_expert_router[l],
             jnp.zeros((d, LANES - N_GROUPS - N_EXPERTS), F32)], axis=-1)
        b_router = jnp.concatenate(
            [b_group_router[l], b_expert_router[l], jnp.zeros((LANES - N_GROUPS - N_EXPERTS,), F32)])[None, :]
        x1t, meta, counts = _outproj(x2d, ml.reshape(t, -1), att.reshape(t, -1), w_out[l].astype(BF16),
                                     ln1_w[l][None, :], ln1_b[l][None, :], w_router, b_router, alpha, tm)

        pos2d, info = _rank(meta, counts, tb_rank, n_tiles_pad)
        pos = pos2d[0]
        xs = _dispatch(pos, x1t, n_tiles_pad * MOE_TILE, d // LANES, tb_dma)
        ys = _moe(info, xs, w_exp_gate[l].astype(BF16), w_exp_up[l].astype(BF16), w_exp_down[l].astype(BF16),
                  d, n_tiles_pad)
        out = _collect(pos, ys, x1t, meta, ln2_w[l][None, :], ln2_b[l][None, :], alpha, d, tb_col)
        x = out.reshape(b, s, d)
    return x
```

```python
import functools
import math

import numpy as np
import jax
import jax.numpy as jnp
from jax import lax
from jax.experimental import pallas as pl
from jax.experimental.pallas import tpu as pltpu

F32 = jnp.float32
BF16 = jnp.bfloat16
I32 = jnp.int32

ML_HEADS = 4
ML_HEAD_DIM = 128
ML_WIDTH = ML_HEADS * ML_HEAD_DIM
ML_CHUNK = 128
CONV_WIDTH = 4
ATT_Q_HEADS = 8
ATT_KV_HEADS = 2
ATT_HEAD_DIM = 64
ATT_WIDTH = ATT_Q_HEADS * ATT_HEAD_DIM
ATT_KV_WIDTH = ATT_KV_HEADS * ATT_HEAD_DIM
WINDOW = 128
ROPE_THETA = 10000.0
N_GROUPS = 4
EXPERTS_PER_GROUP = 8
N_EXPERTS = N_GROUPS * EXPERTS_PER_GROUP
PAIRS_PER_GROUP = EXPERTS_PER_GROUP * (EXPERTS_PER_GROUP - 1) // 2
N_CLASSES = N_GROUPS * PAIRS_PER_GROUP
LN_EPS = 1e-5

LANES = 128
SUBLANES = 8
MOE_TILE = 128
VMEM_LIMIT = 56 * 1024 * 1024

NEG_INF = float("-inf")


def _sigmoid(x):
    return 1.0 / (1.0 + jnp.exp(-x))


def _log_sigmoid(x):
    return jnp.minimum(x, 0.0) - jnp.log(1.0 + jnp.exp(-jnp.abs(x)))


def _iota(shape, dim):
    return lax.broadcasted_iota(I32, shape, dim)


def _dot(a, b):
    return jnp.dot(a, b, preferred_element_type=F32)


def _dot_exact(a, b):
    return jnp.dot(a, b, preferred_element_type=F32, precision=lax.Precision.HIGHEST)


def _layer_norm(z, w, b):
    mu = jnp.mean(z, axis=-1, keepdims=True)
    zc = z - mu
    var = jnp.mean(zc * zc, axis=-1, keepdims=True)
    return zc * lax.rsqrt(var + LN_EPS) * w + b


C_QK = 0
C_V = C_QK + 2 * ML_WIDTH
C_O = C_V + ML_WIDTH
C_AQ = C_O + ML_WIDTH
C_AK = C_AQ + ATT_WIDTH
C_AV = C_AK + ATT_KV_HEADS * LANES
C_G = C_AV + ATT_KV_HEADS * LANES
C_END = C_G + LANES


def _pack_w_in(w_in):
    sizes = (2 * ML_WIDTH, ML_WIDTH, ML_WIDTH, ML_HEADS, ML_HEADS, ATT_WIDTH, ATT_KV_WIDTH, ATT_KV_WIDTH)
    splits = np.cumsum(sizes)[:-1].tolist()
    w_qk, w_v, w_o, w_i, w_f, w_aq, w_ak, w_av = jnp.split(w_in, splits, axis=-1)

    def dup(w):
        heads = [w[:, h * ATT_HEAD_DIM:(h + 1) * ATT_HEAD_DIM] for h in range(ATT_KV_HEADS)]
        return jnp.concatenate([t for h in heads for t in (h, h)], axis=-1)

    w_g = jnp.concatenate([w_i, w_f, jnp.zeros((w_in.shape[0], LANES - 2 * ML_HEADS), w_in.dtype)], axis=-1)
    packed = jnp.concatenate([w_qk, w_v, w_o, w_aq, dup(w_ak), dup(w_av), w_g], axis=-1)
    return packed.astype(BF16)


def _inproj_kernel(x_ref, w_ref, qk_ref, v_ref, o_ref, aq_ref, ak_ref, av_ref, g_ref):
    x = x_ref[...].astype(BF16)

    def mm(lo, hi):
        return _dot(x, w_ref[:, lo:hi])

    qk_ref[...] = mm(C_QK, C_V)
    v_ref[...] = mm(C_V, C_O).astype(BF16)
    o_ref[...] = mm(C_O, C_AQ)
    aq_ref[...] = mm(C_AQ, C_AK)
    ak_ref[...] = mm(C_AK, C_AV)
    av_ref[...] = mm(C_AV, C_G).astype(BF16)
    g_ref[...] = mm(C_G, C_END)


def _inproj(x2d, w_packed, tm):
    t, d = x2d.shape
    widths = (C_V - C_QK, C_O - C_V, C_AQ - C_O, C_AK - C_AQ, C_AV - C_AK, C_G - C_AV, C_END - C_G)
    dtypes = (F32, BF16, F32, F32, F32, BF16, F32)
    return pl.pallas_call(
        _inproj_kernel,
        grid=(t // tm,),
        in_specs=[pl.BlockSpec((tm, d), lambda i: (i, 0)),
                  pl.BlockSpec((d, C_END), lambda i: (0, 0))],
        out_specs=[pl.BlockSpec((tm, w), lambda i: (i, 0)) for w in widths],
        out_shape=[jax.ShapeDtypeStruct((t, w), dt) for w, dt in zip(widths, dtypes)],
        compiler_params=pltpu.CompilerParams(dimension_semantics=("arbitrary",),
                                             vmem_limit_bytes=VMEM_LIMIT),
        name="inproj",
    )(x2d, w_packed)


def _mlstm_kernel(qk_ref, v_ref, o_ref, g_ref, cw_ref, cb_ref, gb_ref, nw_ref, out_ref,
                  ext_ref, ct_ref, m_ref, *, tq):
    s_idx = pl.program_id(1)
    L = ML_CHUNK
    D = ML_HEAD_DIM
    halo = SUBLANES

    @pl.when(s_idx == 0)
    def _():
        ext_ref[0:halo, :] = jnp.zeros((halo, 2 * ML_WIDTH), F32)
        ct_ref[...] = jnp.zeros_like(ct_ref)
        m_ref[...] = jnp.zeros_like(m_ref)

    ext_ref[halo:halo + tq, :] = qk_ref[...]

    row = _iota((L, L), 0)
    col = _iota((L, L), 1)
    causal = col <= row
    tril = jnp.where(causal, 1.0, 0.0).astype(F32)
    ones_col = jnp.where(_iota((L, LANES), 1) == 0, 1.0, 0.0).astype(BF16)
    scale = D ** -0.5

    for c in range(tq // L):
        r0 = c * L
        conv = cb_ref[...]
        for j in range(CONV_WIDTH):
            start = halo + r0 - (CONV_WIDTH - 1) + j
            conv = conv + cw_ref[j:j + 1, :] * ext_ref[start:start + L, :]
        qk_act = conv * _sigmoid(conv)

        gates = g_ref[r0:r0 + L, :] + gb_ref[...]
        bcum = _dot_exact(tril, _log_sigmoid(gates))
        bcum_t = bcum.T
        gates_t = gates.T

        for h in range(ML_HEADS):
            bc = bcum[:, ML_HEADS + h:ML_HEADS + h + 1]
            ic = gates[:, h:h + 1]
            br = bcum_t[ML_HEADS + h:ML_HEADS + h + 1, :]
            ir = gates_t[h:h + 1, :]
            m_prev = m_ref[h:h + 1, 0:1]

            log_intra = jnp.where(causal, bc - br + ir, NEG_INF)
            log_inter = bc + m_prev
            m_row = jnp.maximum(log_inter, jnp.max(log_intra, axis=-1, keepdims=True))
            w_intra = jnp.exp(log_intra - m_row)
            w_inter = jnp.exp(log_inter - m_row)

            q_b = qk_act[:, h * D:(h + 1) * D].astype(BF16)
            k_b = (qk_act[:, ML_WIDTH + h * D:ML_WIDTH + (h + 1) * D] * scale).astype(BF16)
            v_b = v_ref[r0:r0 + L, h * D:(h + 1) * D]
            v_aug = jnp.concatenate([v_b, ones_col], axis=-1)

            s = lax.dot_general(q_b, k_b, (((1,), (1,)), ((), ())), preferred_element_type=F32) * w_intra
            ct = ct_ref[h]
            acc = w_inter * _dot(q_b, ct.astype(BF16)) + _dot(s.astype(BF16), v_aug)
            num = acc[:, :D]
            den = acc[:, D:D + 1]
            hh = num / jnp.maximum(jnp.abs(den), jnp.exp(-m_row))

            mu = jnp.mean(hh, axis=-1, keepdims=True)
            hc = hh - mu
            var = jnp.mean(hc * hc, axis=-1, keepdims=True)
            hn = hc * lax.rsqrt(var + LN_EPS) * nw_ref[:, h * D:(h + 1) * D]
            gate_o = _sigmoid(o_ref[r0:r0 + L, h * D:(h + 1) * D])
            out_ref[r0:r0 + L, h * D:(h + 1) * D] = (gate_o * hn).astype(out_ref.dtype)

            b_last = bcum[L - 1:L, ML_HEADS + h:ML_HEADS + h + 1]
            log_w_r = b_last - br + ir
            m_new = jnp.maximum(b_last + m_prev, jnp.max(log_w_r, axis=-1, keepdims=True))
            decay = jnp.exp(b_last + m_prev - m_new)
            w_c = jnp.exp(b_last - bc + ic - m_new)
            wv = (w_c * v_aug.astype(F32)).astype(BF16)
            upd = lax.dot_general(k_b, wv, (((0,), (0,)), ((), ())), preferred_element_type=F32)
            ct_ref[h] = decay * ct + upd
            m_ref[h:h + 1, :] = jnp.broadcast_to(m_new, (1, LANES))

    ext_ref[0:halo, :] = ext_ref[tq:tq + halo, :]


def _mlstm(qk, v, o, g, conv_w, conv_b, gate_bias_row, norm_w_row, tq):
    b, s, _ = qk.shape
    kern = functools.partial(_mlstm_kernel, tq=tq)

    def seq_spec(width):
        return pl.BlockSpec((None, tq, width), lambda bi, si: (bi, si, 0))

    def const_spec(shape):
        return pl.BlockSpec(shape, lambda bi, si: (0,) * len(shape))

    return pl.pallas_call(
        kern,
        grid=(b, s // tq),
        in_specs=[seq_spec(2 * ML_WIDTH), seq_spec(ML_WIDTH), seq_spec(ML_WIDTH), seq_spec(LANES),
                  const_spec((CONV_WIDTH, 2 * ML_WIDTH)), const_spec((1, 2 * ML_WIDTH)),
                  const_spec((1, LANES)), const_spec((1, ML_WIDTH))],
        out_specs=seq_spec(ML_WIDTH),
        out_shape=jax.ShapeDtypeStruct((b, s, ML_WIDTH), BF16),
        scratch_shapes=[pltpu.VMEM((tq + SUBLANES, 2 * ML_WIDTH), F32),
                        pltpu.VMEM((ML_HEADS, ML_HEAD_DIM, 2 * ML_HEAD_DIM), F32),
                        pltpu.VMEM((SUBLANES, LANES), F32)],
        compiler_params=pltpu.CompilerParams(dimension_semantics=("arbitrary", "arbitrary"),
                                             vmem_limit_bytes=VMEM_LIMIT),
        name="mlstm",
    )(qk, v, o, g, conv_w, conv_b, gate_bias_row, norm_w_row)


def _rope_tables(seq_len):
    half = ATT_HEAD_DIM // 2
    inv_freq = ROPE_THETA ** (-jnp.arange(half, dtype=F32) / half)
    ang = jnp.arange(seq_len, dtype=F32)[:, None] * inv_freq[None, :]
    cos = jnp.cos(ang)
    sin = jnp.sin(ang)
    cos_t = jnp.concatenate([cos, cos, cos, cos], axis=-1)
    sin_t = jnp.concatenate([-sin, sin, -sin, sin], axis=-1)
    return cos_t, sin_t


def _swa_kernel(sink_ref, aq_ref, ak_ref, av_ref, cos_ref, sin_ref, out_ref, kprev_ref, vprev_ref):
    i = pl.program_id(1)
    Lb = WINDOW
    half = ATT_HEAD_DIM // 2
    pairs = ATT_Q_HEADS // 2
    pairs_per_kv = pairs // ATT_KV_HEADS

    @pl.when(i == 0)
    def _():
        kprev_ref[...] = jnp.zeros_like(kprev_ref)
        vprev_ref[...] = jnp.zeros_like(vprev_ref)

    cos = cos_ref[...]
    sin = sin_ref[...]

    lower = (_iota((Lb, LANES), 1) & (ATT_HEAD_DIM - 1)) < half

    def rope(x):
        tiles = []
        for c in range(x.shape[-1] // LANES):
            xt = x[:, c * LANES:(c + 1) * LANES]
            swapped = jnp.where(lower, pltpu.roll(xt, LANES - half, 1), pltpu.roll(xt, half, 1))
            tiles.append(xt * cos + swapped * sin)
        return jnp.concatenate(tiles, axis=-1)

    q = rope(aq_ref[...]) * (ATT_HEAD_DIM ** -0.5)
    k_cur = rope(ak_ref[...]).astype(BF16)
    v_cur = av_ref[...]

    ql = _iota((Lb, 2 * Lb), 0)
    kj = _iota((Lb, 2 * Lb), 1)
    diff = Lb + ql - kj
    kpos = i * Lb + kj - Lb
    visible = jnp.where(diff >= 0, jnp.where(diff < WINDOW, jnp.where(kpos >= 0, 1, 0), 0), 0)
    bias = jnp.where(visible > 0, 0.0, NEG_INF).astype(F32)
    bias = jnp.concatenate([bias] * (2 * pairs_per_kv), axis=0)

    lane = _iota((Lb, LANES), 1)
    low_half = lane < ATT_HEAD_DIM

    for g in range(ATT_KV_HEADS):
        kk = jnp.concatenate([kprev_ref[:, g * LANES:(g + 1) * LANES], k_cur[:, g * LANES:(g + 1) * LANES]], axis=0)
        vv = jnp.concatenate([vprev_ref[:, g * LANES:(g + 1) * LANES], v_cur[:, g * LANES:(g + 1) * LANES]], axis=0)
        rows = []
        sinks = []
        for p in range(pairs_per_kv):
            pair = g * pairs_per_kv + p
            q2 = q[:, pair * LANES:(pair + 1) * LANES]
            rows.append(jnp.where(low_half, q2, 0.0))
            rows.append(jnp.where(low_half, 0.0, q2))
            sinks.append(jnp.full((Lb, 1), sink_ref[2 * pair], F32))
            sinks.append(jnp.full((Lb, 1), sink_ref[2 * pair + 1], F32))
        qs = jnp.concatenate(rows, axis=0).astype(BF16)
        sink = jnp.concatenate(sinks, axis=0)

        sc = lax.dot_general(qs, kk, (((1,), (1,)), ((), ())), preferred_element_type=F32) + bias
        m = jnp.maximum(jnp.max(sc, axis=-1, keepdims=True), sink)
        p_un = jnp.exp(sc - m)
        denom = jnp.sum(p_un, axis=-1, keepdims=True) + jnp.exp(sink - m)
        o = _dot(p_un.astype(BF16), vv) / denom
        for p in range(pairs_per_kv):
            pair = g * pairs_per_kv + p
            even = o[(2 * p) * Lb:(2 * p + 1) * Lb, :]
            odd = o[(2 * p + 1) * Lb:(2 * p + 2) * Lb, :]
            out_ref[:, pair * LANES:(pair + 1) * LANES] = jnp.where(low_half, even, odd).astype(out_ref.dtype)

    kprev_ref[...] = k_cur
    vprev_ref[...] = v_cur


def _swa(aq, ak, av, cos_t, sin_t, sinks):
    b, s, _ = aq.shape
    kvw = ATT_KV_HEADS * LANES

    def seq_spec(width):
        return pl.BlockSpec((None, WINDOW, width), lambda bi, si: (bi, si, 0))

    tab_spec = pl.BlockSpec((WINDOW, LANES), lambda bi, si: (si, 0))
    return pl.pallas_call(
        _swa_kernel,
        grid=(b, s // WINDOW),
        in_specs=[pl.BlockSpec(memory_space=pltpu.SMEM),
                  seq_spec(ATT_WIDTH), seq_spec(kvw), seq_spec(kvw), tab_spec, tab_spec],
        out_specs=seq_spec(ATT_WIDTH),
        out_shape=jax.ShapeDtypeStruct((b, s, ATT_WIDTH), BF16),
        scratch_shapes=[pltpu.VMEM((WINDOW, kvw), BF16), pltpu.VMEM((WINDOW, kvw), BF16)],
        compiler_params=pltpu.CompilerParams(dimension_semantics=("arbitrary", "arbitrary"),
                                             vmem_limit_bytes=VMEM_LIMIT),
        name="swa",
    )(sinks, aq, ak, av, cos_t, sin_t)


def _store_token_tiles(ref, val, row0=0, rows_per_token=None):
    n, w = val.shape
    segs = w // LANES
    rpt = rows_per_token or segs
    for j in range(segs):
        ref[pl.ds(row0 + j, n, stride=rpt), :] = val[:, j * LANES:(j + 1) * LANES]


def _load_token_tiles(ref, n, segs, row0=0, rows_per_token=None):
    rpt = rows_per_token or segs
    return jnp.concatenate([ref[pl.ds(row0 + j, n, stride=rpt), :] for j in range(segs)], axis=-1)


def _outproj_kernel(x_ref, ml_ref, att_ref, wo_ref, lnw_ref, lnb_ref, wr_ref, br_ref,
                    x1t_ref, meta_ref, cnt_ref, *, alpha):
    step = pl.program_id(0)
    tm, d = x_ref.shape
    y = _dot(ml_ref[...], wo_ref[0:ML_WIDTH, :]) + _dot(att_ref[...], wo_ref[ML_WIDTH:, :])
    x1 = _layer_norm(alpha * x_ref[...] + y, lnw_ref[...], lnb_ref[...])
    _store_token_tiles(x1t_ref, x1)

    logits = _dot_exact(x1, wr_ref[...]) + br_ref[...]
    lane = _iota((tm, LANES), 1).astype(F32)
    big = float(LANES)

    def first_argmax(vals):
        top = jnp.max(vals, axis=-1, keepdims=True)
        idx = jnp.min(jnp.where(vals == top, lane, big), axis=-1, keepdims=True)
        return top, idx

    g_logits = jnp.where(lane < N_GROUPS, logits, NEG_INF)
    g_top, g_idx = first_argmax(g_logits)
    g_p = 1.0 / jnp.sum(jnp.exp(g_logits - g_top), axis=-1, keepdims=True)

    e_lo = N_GROUPS + EXPERTS_PER_GROUP * g_idx
    in_group = jnp.where(lane >= e_lo, jnp.where(lane < e_lo + EXPERTS_PER_GROUP, 1.0, 0.0), 0.0)
    e_logits = jnp.where(in_group > 0, logits, NEG_INF)
    v1, i1 = first_argmax(e_logits)
    v2, i2 = first_argmax(jnp.where(lane == i1, NEG_INF, e_logits))
    r = jnp.exp(v2 - v1)
    w1 = g_p / (1.0 + r)
    w2 = g_p * r / (1.0 + r)

    a1 = i1 - e_lo
    a2 = i2 - e_lo
    lo = jnp.minimum(a1, a2)
    hi = jnp.maximum(a1, a2)
    w_lo = jnp.where(a1 < a2, w1, w2)
    w_hi = jnp.where(a1 < a2, w2, w1)
    pair_idx = (EXPERTS_PER_GROUP - 1) * lo - lo * (lo - 1.0) * 0.5 + (hi - lo - 1.0)
    cls = g_idx * PAIRS_PER_GROUP + pair_idx

    meta = jnp.where(lane == 0.0, cls, jnp.where(lane == 1.0, w_lo, jnp.where(lane == 2.0, w_hi, 0.0)))
    meta_ref[...] = meta

    @pl.when(step == 0)
    def _():
        cnt_ref[...] = jnp.zeros_like(cnt_ref)

    onehot = jnp.where(lane == cls, 1.0, 0.0)
    cnt_ref[0:1, :] += jnp.sum(onehot, axis=0, keepdims=True)


def _outproj(x2d, ml2d, att2d, w_out_b, ln_w, ln_b, w_router, b_router, alpha, tm):
    t, d = x2d.shape
    kern = functools.partial(_outproj_kernel, alpha=alpha)

    def const_spec(shape):
        return pl.BlockSpec(shape, lambda i: (0,) * len(shape))

    return pl.pallas_call(
        kern,
        grid=(t // tm,),
        in_specs=[pl.BlockSpec((tm, d), lambda i: (i, 0)),
                  pl.BlockSpec((tm, ML_WIDTH), lambda i: (i, 0)),
                  pl.BlockSpec((tm, ATT_WIDTH), lambda i: (i, 0)),
                  const_spec((ML_WIDTH + ATT_WIDTH, d)), const_spec((1, d)), const_spec((1, d)),
                  const_spec((d, LANES)), const_spec((1, LANES))],
        out_specs=[pl.BlockSpec((tm * (d // LANES), LANES), lambda i: (i, 0)),
                   pl.BlockSpec((tm, LANES), lambda i: (i, 0)), const_spec((SUBLANES, LANES))],
        out_shape=[jax.ShapeDtypeStruct((t * (d // LANES), LANES), F32),
                   jax.ShapeDtypeStruct((t, LANES), F32),
                   jax.ShapeDtypeStruct((SUBLANES, LANES), F32)],
        compiler_params=pltpu.CompilerParams(dimension_semantics=("arbitrary",),
                                             vmem_limit_bytes=VMEM_LIMIT),
        name="outproj",
    )(x2d, ml2d, att2d, w_out_b, ln_w, ln_b, w_router, b_router)


def _class_expert_table():
    tab = np.zeros((SUBLANES, LANES), np.float32)
    for g in range(N_GROUPS):
        idx = 0
        for lo in range(EXPERTS_PER_GROUP):
            for hi in range(lo + 1, EXPERTS_PER_GROUP):
                c = g * PAIRS_PER_GROUP + idx
                tab[0, c] = g * EXPERTS_PER_GROUP + lo
                tab[1, c] = g * EXPERTS_PER_GROUP + hi
                idx += 1
    return tab


def _rank_kernel(meta_ref, cnt_ref, tab_ref, pos_ref, tile_ref, base_ref, run_ref, *, tb, n_tiles_pad):
    step = pl.program_id(0)
    lane8 = _iota((SUBLANES, LANES), 1)

    @pl.when(step == 0)
    def _():
        cnt = jnp.broadcast_to(cnt_ref[0:1, :], (SUBLANES, LANES))
        tiles = jnp.floor((cnt + (MOE_TILE - 1.0)) * (1.0 / MOE_TILE))
        cum = tiles
        sh = 1
        while sh < LANES:
            cum = cum + jnp.where(lane8 >= sh, pltpu.roll(cum, sh, 1), 0.0)
            sh *= 2
        excl = cum - tiles
        base_ref[...] = excl * MOE_TILE
        run_ref[...] = jnp.zeros_like(run_ref)

        ti = _iota((n_tiles_pad, LANES), 0).astype(F32)
        lane = _iota((n_tiles_pad, LANES), 1)
        done = jnp.where(lane < N_CLASSES, jnp.where(cum[0:1, :] <= ti, 1.0, 0.0), 0.0)
        t_cls = jnp.sum(done, axis=-1, keepdims=True)
        sel = jnp.where(lane.astype(F32) == t_cls, 1.0, 0.0)
        cnt_i = jnp.sum(sel * cnt[0:1, :], axis=-1, keepdims=True)
        first_i = jnp.sum(sel * excl[0:1, :], axis=-1, keepdims=True)
        rows_i = jnp.clip(cnt_i - MOE_TILE * (ti[:, 0:1] - first_i), 0.0, float(MOE_TILE))
        e_lo = jnp.sum(sel * tab_ref[0:1, :], axis=-1, keepdims=True)
        e_hi = jnp.sum(sel * tab_ref[1:2, :], axis=-1, keepdims=True)
        n_tiles = jnp.sum(jnp.where(lane < N_CLASSES, jnp.broadcast_to(tiles[0:1, :], (n_tiles_pad, LANES)), 0.0),
                          axis=-1, keepdims=True)
        info = jnp.where(lane == 0, rows_i,
                         jnp.where(lane == 1, e_lo, jnp.where(lane == 2, e_hi, jnp.where(lane == 3, n_tiles, 0.0))))
        tile_ref[...] = info.astype(I32)

    cls = meta_ref[:, 0:1]
    lane = _iota((tb, LANES), 1).astype(F32)
    onehot = jnp.where(lane == cls, 1.0, 0.0)
    strict_lower = jnp.where(_iota((tb, tb), 1) < _iota((tb, tb), 0), 1.0, 0.0).astype(BF16)
    before = _dot(strict_lower, onehot.astype(BF16))
    slot = jnp.sum(onehot * (before + run_ref[0:1, :] + base_ref[0:1, :]), axis=-1, keepdims=True)
    run_ref[...] = run_ref[...] + jnp.sum(onehot, axis=0, keepdims=True)
    slot_t = jnp.broadcast_to(slot, (tb, LANES)).T
    pos_ref[...] = slot_t[0:SUBLANES, :].astype(I32)


def _rank(meta, counts, tb, n_tiles_pad):
    t = meta.shape[0]
    kern = functools.partial(_rank_kernel, tb=tb, n_tiles_pad=n_tiles_pad)
    tab = jnp.asarray(_class_expert_table())
    return pl.pallas_call(
        kern,
        grid=(t // tb,),
        in_specs=[pl.BlockSpec((tb, LANES), lambda i: (i, 0)),
                  pl.BlockSpec((SUBLANES, LANES), lambda i: (0, 0)),
                  pl.BlockSpec((SUBLANES, LANES), lambda i: (0, 0))],
        out_specs=[pl.BlockSpec((SUBLANES, tb), lambda i: (0, i)),
                   pl.BlockSpec((n_tiles_pad, LANES), lambda i: (0, 0))],
        out_shape=[jax.ShapeDtypeStruct((SUBLANES, t), I32),
                   jax.ShapeDtypeStruct((n_tiles_pad, LANES), I32)],
        scratch_shapes=[pltpu.VMEM((SUBLANES, LANES), F32), pltpu.VMEM((SUBLANES, LANES), F32)],
        compiler_params=pltpu.CompilerParams(dimension_semantics=("arbitrary",),
                                             vmem_limit_bytes=VMEM_LIMIT),
        name="rank",
    )(meta, counts, tab)


def _dispatch_kernel(pos_ref, src_ref, zeros_ref, dst_ref, sem, *, tb, rpt):
    del zeros_ref
    base = pl.program_id(0) * tb

    def issue(j, carry):
        src = src_ref.at[pl.ds(pl.multiple_of(j * rpt, rpt), rpt)]
        dst = dst_ref.at[pl.ds(pl.multiple_of(pos_ref[base + j] * rpt, rpt), rpt)]
        pltpu.make_async_copy(src, dst, sem).start()
        return carry

    lax.fori_loop(0, tb, issue, 0, unroll=8)
    pltpu.make_async_copy(src_ref, dst_ref.at[pl.ds(0, tb * rpt)], sem).wait()


def _dispatch(pos, src, n_slots, rpt, tb):
    t = pos.shape[0]
    kern = functools.partial(_dispatch_kernel, tb=tb, rpt=rpt)
    return pl.pallas_call(
        kern,
        grid_spec=pltpu.PrefetchScalarGridSpec(
            num_scalar_prefetch=1,
            grid=(t // tb,),
            in_specs=[pl.BlockSpec((tb * rpt, LANES), lambda i, pos_ref: (i, 0)),
                      pl.BlockSpec(memory_space=pl.ANY)],
            out_specs=pl.BlockSpec(memory_space=pl.ANY),
            scratch_shapes=[pltpu.SemaphoreType.DMA(())],
        ),
        out_shape=jax.ShapeDtypeStruct((n_slots * rpt, LANES), src.dtype),
        input_output_aliases={2: 0},
        compiler_params=pltpu.CompilerParams(dimension_semantics=("arbitrary",),
                                             has_side_effects=True, vmem_limit_bytes=VMEM_LIMIT),
        name="dispatch",
    )(pos, src, jnp.zeros((n_slots * rpt, LANES), src.dtype))


def _collect_kernel(pos_ref, ys_ref, x1t_ref, meta_ref, lnw_ref, lnb_ref, out_ref, buf_ref, sem,
                    *, tb, alpha, d):
    segs = d // LANES
    rpt = 2 * segs
    base = pl.program_id(0) * tb

    def issue(j, carry):
        src = ys_ref.at[pl.ds(pl.multiple_of(pos_ref[base + j] * rpt, rpt), rpt)]
        dst = buf_ref.at[pl.ds(pl.multiple_of(j * rpt, rpt), rpt)]
        pltpu.make_async_copy(src, dst, sem).start()
        return carry

    lax.fori_loop(0, tb, issue, 0, unroll=8)
    pltpu.make_async_copy(ys_ref.at[pl.ds(0, tb * rpt)], buf_ref, sem).wait()

    x1 = _load_token_tiles(x1t_ref, tb, segs)
    y_lo = _load_token_tiles(buf_ref, tb, segs, row0=0, rows_per_token=rpt)
    y_hi = _load_token_tiles(buf_ref, tb, segs, row0=segs, rows_per_token=rpt)
    meta = meta_ref[...]
    z = alpha * x1 + meta[:, 1:2] * y_lo + meta[:, 2:3] * y_hi
    out_ref[...] = _layer_norm(z, lnw_ref[...], lnb_ref[...])


def _collect(pos, ys, x1t, meta, ln_w, ln_b, alpha, d, tb):
    t = pos.shape[0]
    segs = d // LANES
    kern = functools.partial(_collect_kernel, tb=tb, alpha=alpha, d=d)
    return pl.pallas_call(
        kern,
        grid_spec=pltpu.PrefetchScalarGridSpec(
            num_scalar_prefetch=1,
            grid=(t // tb,),
            in_specs=[pl.BlockSpec(memory_space=pl.ANY),
                      pl.BlockSpec((tb * segs, LANES), lambda i, pos_ref: (i, 0)),
                      pl.BlockSpec((tb, LANES), lambda i, pos_ref: (i, 0)),
                      pl.BlockSpec((1, d), lambda i, pos_ref: (0, 0)),
                      pl.BlockSpec((1, d), lambda i, pos_ref: (0, 0))],
            out_specs=pl.BlockSpec((tb, d), lambda i, pos_ref: (i, 0)),
            scratch_shapes=[pltpu.VMEM((tb * 2 * segs, LANES), F32), pltpu.SemaphoreType.DMA(())],
        ),
        out_shape=jax.ShapeDtypeStruct((t, d), F32),
        compiler_params=pltpu.CompilerParams(dimension_semantics=("arbitrary",),
                                             vmem_limit_bytes=VMEM_LIMIT),
        name="collect",
    )(pos, ys, x1t, meta, ln_w, ln_b)


def _moe_kernel(rows_ref, elo_ref, ehi_ref, nt_ref, xs_ref, wg_lo_ref, wu_lo_ref, wd_lo_ref,
                wg_hi_ref, wu_hi_ref, wd_hi_ref, ys_ref, *, d):
    i = pl.program_id(0)
    rows = rows_ref[i]
    segs = d // LANES

    @pl.when(rows > 0)
    def _():
        xb = _load_token_tiles(xs_ref, MOE_TILE, segs).astype(BF16)

        def expert(wg_ref, wu_ref, wd_ref):
            gate = _dot(xb, wg_ref[...])
            up = _dot(xb, wu_ref[...])
            hidden = gate * _sigmoid(gate) * up
            return _dot(hidden.astype(BF16), wd_ref[...])

        _store_token_tiles(ys_ref, expert(wg_lo_ref, wu_lo_ref, wd_lo_ref), row0=0, rows_per_token=2 * segs)
        _store_token_tiles(ys_ref, expert(wg_hi_ref, wu_hi_ref, wd_hi_ref), row0=segs, rows_per_token=2 * segs)

    @pl.when(rows <= 0)
    def _():
        ys_ref[...] = jnp.zeros_like(ys_ref)


def _moe(info, xs, wg, wu, wd, d, n_tiles_pad):
    de = wg.shape[-1]
    segs = d // LANES
    kern = functools.partial(_moe_kernel, d=d)

    def last_live(i, nt_ref):
        return jnp.minimum(i, jnp.maximum(nt_ref[0] - 1, 0))

    def up_spec(which):
        return pl.BlockSpec((None, d, de),
                            lambda i, rows, elo, ehi, nt: ((elo, ehi)[which][last_live(i, nt)], 0, 0))

    def down_spec(which):
        return pl.BlockSpec((None, de, d),
                            lambda i, rows, elo, ehi, nt: ((elo, ehi)[which][last_live(i, nt)], 0, 0))

    rows, elo, ehi, nt = info[:, 0], info[:, 1], info[:, 2], info[0:1, 3]
    return pl.pallas_call(
        kern,
        grid_spec=pltpu.PrefetchScalarGridSpec(
            num_scalar_prefetch=4,
            grid=(n_tiles_pad,),
            in_specs=[pl.BlockSpec((MOE_TILE * segs, LANES),
                                   lambda i, rows, elo, ehi, nt: (last_live(i, nt), 0)),
                      up_spec(0), up_spec(0), down_spec(0), up_spec(1), up_spec(1), down_spec(1)],
            out_specs=pl.BlockSpec((MOE_TILE * 2 * segs, LANES), lambda i, rows, elo, ehi, nt: (i, 0)),
        ),
        out_shape=jax.ShapeDtypeStruct((n_tiles_pad * MOE_TILE * 2 * segs, LANES), F32),
        compiler_params=pltpu.CompilerParams(dimension_semantics=("arbitrary",),
                                             vmem_limit_bytes=VMEM_LIMIT),
        name="moe",
    )(rows, elo, ehi, nt, xs, wg, wu, wd, wg, wu, wd)


def _pick_block(n, target):
    blk = min(n, target)
    while n % blk:
        blk //= 2
    return blk


def kernel(x, w_in, conv_w, conv_b, mlstm_gate_bias, mlstm_norm_w, attn_sinks, w_out, ln1_w, ln1_b,
           w_group_router, b_group_router, w_expert_router, b_expert_router,
           w_exp_gate, w_exp_up, w_exp_down, ln2_w, ln2_b):
    b, s, d = x.shape
    t = b * s
    depth = w_in.shape[0]
    alpha = (2.0 * depth) ** 0.25
    assert s % ML_CHUNK == 0 and s % WINDOW == 0 and d % LANES == 0

    tm = _pick_block(t, 512)
    tq = _pick_block(s, 4 * ML_CHUNK)
    tb_rank = _pick_block(t, 512)
    tb_dma = _pick_block(t, 2048)
    tb_col = _pick_block(t, 256)
    n_tiles_pad = -(-(t // MOE_TILE + N_CLASSES) // SUBLANES) * SUBLANES
    cos_t, sin_t = _rope_tables(s)

    for l in range(depth):
        x2d = x.reshape(t, d)
        qk, v, o, aq, ak, av, g = _inproj(x2d, _pack_w_in(w_in[l]), tm)
        gate_bias_row = jnp.concatenate(
            [mlstm_gate_bias[l, 0], mlstm_gate_bias[l, 1], jnp.zeros((LANES - 2 * ML_HEADS,), F32)])[None, :]
        ml = _mlstm(qk.reshape(b, s, -1), v.reshape(b, s, -1), o.reshape(b, s, -1), g.reshape(b, s, -1),
                    conv_w[l], conv_b[l][None, :], gate_bias_row, mlstm_norm_w[l][None, :], tq)
        att = _swa(aq.reshape(b, s, -1), ak.reshape(b, s, -1), av.reshape(b, s, -1), cos_t, sin_t, attn_sinks[l])

        w_router = jnp.concatenate(
            [w_group_router[l], w_expert_router[l],
             jnp.zeros((d, LANES - N_GROUPS - N_EXPERTS), F32)], axis=-1)
        b_router = jnp.concatenate(
            [b_group_router[l], b_expert_router[l], jnp.zeros((LANES - N_GROUPS - N_EXPERTS,), F32)])[None, :]
        x1t, meta, counts = _outproj(x2d, ml.reshape(t, -1), att.reshape(t, -1), w_out[l].astype(BF16),
                                     ln1_w[l][None, :], ln1_b[l][None, :], w_router, b_router, alpha, tm)

        pos2d, info = _rank(meta, counts, tb_rank, n_tiles_pad)
        pos = pos2d[0]
        xs = _dispatch(pos, x1t, n_tiles_pad * MOE_TILE, d // LANES, tb_dma)
        ys = _moe(info, xs, w_exp_gate[l].astype(BF16), w_exp_up[l].astype(BF16), w_exp_down[l].astype(BF16),
                  d, n_tiles_pad)
        out = _collect(pos, ys, x1t, meta, ln2_w[l][None, :], ln2_b[l][None, :], alpha, d, tb_col)
        x = out.reshape(b, s, d)
    return x
```

```python
import functools
import math

import numpy as np
import jax
import jax.numpy as jnp
from jax import lax
from jax.experimental import pallas as pl
from jax.experimental.pallas import tpu as pltpu

F32 = jnp.float32
BF16 = jnp.bfloat16
I32 = jnp.int32

ML_HEADS = 4
ML_HEAD_DIM = 128
ML_WIDTH = ML_HEADS * ML_HEAD_DIM
ML_CHUNK = 128
CONV_WIDTH = 4
ATT_Q_HEADS = 8
ATT_KV_HEADS = 2
ATT_HEAD_DIM = 64
ATT_WIDTH = ATT_Q_HEADS * ATT_HEAD_DIM
ATT_KV_WIDTH = ATT_KV_HEADS * ATT_HEAD_DIM
WINDOW = 128
ROPE_THETA = 10000.0
N_GROUPS = 4
EXPERTS_PER_GROUP = 8
N_EXPERTS = N_GROUPS * EXPERTS_PER_GROUP
PAIRS_PER_GROUP = EXPERTS_PER_GROUP * (EXPERTS_PER_GROUP - 1) // 2
N_CLASSES = N_GROUPS * PAIRS_PER_GROUP
LN_EPS = 1e-5

LANES = 128
SUBLANES = 8
MOE_TILE = 128
VMEM_LIMIT = 56 * 1024 * 1024

NEG_INF = float("-inf")


def _sigmoid(x):
    return 1.0 / (1.0 + jnp.exp(-x))


def _log_sigmoid(x):
    return jnp.minimum(x, 0.0) - jnp.log(1.0 + jnp.exp(-jnp.abs(x)))


def _iota(shape, dim):
    return lax.broadcasted_iota(I32, shape, dim)


def _dot(a, b):
    return jnp.dot(a, b, preferred_element_type=F32)


def _dot_exact(a, b):
    return jnp.dot(a, b, preferred_element_type=F32, precision=lax.Precision.HIGHEST)


def _layer_norm(z, w, b):
    mu = jnp.mean(z, axis=-1, keepdims=True)
    zc = z - mu
    var = jnp.mean(zc * zc, axis=-1, keepdims=True)
    return zc * lax.rsqrt(var + LN_EPS) * w + b


C_QK = 0
C_V = C_QK + 2 * ML_WIDTH
C_O = C_V + ML_WIDTH
C_AQ = C_O + ML_WIDTH
C_AK = C_AQ + ATT_WIDTH
C_AV = C_AK + ATT_KV_HEADS * LANES
C_G = C_AV + ATT_KV_HEADS * LANES
C_END = C_G + 2 * LANES


def _pack_w_in(w_in):
    sizes = (2 * ML_WIDTH, ML_WIDTH, ML_WIDTH, ML_HEADS, ML_HEADS, ATT_WIDTH, ATT_KV_WIDTH, ATT_KV_WIDTH)
    splits = np.cumsum(sizes)[:-1].tolist()
    w_qk, w_v, w_o, w_i, w_f, w_aq, w_ak, w_av = jnp.split(w_in, splits, axis=-1)

    def dup(w):
        heads = [w[:, h * ATT_HEAD_DIM:(h + 1) * ATT_HEAD_DIM] for h in range(ATT_KV_HEADS)]
        return jnp.concatenate([t for h in heads for t in (h, h)], axis=-1)

    lane_pad = jnp.zeros((w_in.shape[0], LANES - ML_HEADS), w_in.dtype)
    w_g = jnp.concatenate([w_i, lane_pad, w_f, lane_pad], axis=-1)
    packed = jnp.concatenate([w_qk, w_v, w_o, w_aq, dup(w_ak), dup(w_av), w_g], axis=-1)
    return packed.astype(BF16)


def _inproj_kernel(x_ref, w_ref, cw_ref, cb_ref, q_ref, k_ref, v_ref, og_ref, aq_ref, ak_ref, av_ref, g_ref,
                   *scratch, blocks_per_seq):
    *ext_refs, xb_ref = scratch
    tm = x_ref.shape[0]
    halo = SUBLANES
    cs = ext_refs[0].shape[1]
    xb_ref[...] = x_ref[...].astype(BF16)

    def mm(lo, hi):
        return _dot(xb_ref[...], w_ref[:, lo:hi])

    @pl.when(pl.program_id(0) % blocks_per_seq == 0)
    def _():
        for ext_ref in ext_refs:
            ext_ref[0:halo, :] = jnp.zeros((halo, cs), F32)

    scale = ML_HEAD_DIM ** -0.5

    def conv_slice(idx):
        ext_ref = ext_refs[idx]
        c0 = idx * cs
        dst_ref, off, mul = (q_ref, c0, 1.0) if c0 < ML_WIDTH else (k_ref, c0 - ML_WIDTH, scale)
        rt = ML_CHUNK
        for r0 in range(0, tm, rt):
            conv = cb_ref[:, c0:c0 + cs]
            for j in range(CONV_WIDTH):
                start = halo + r0 - (CONV_WIDTH - 1) + j
                conv = conv + cw_ref[j:j + 1, c0:c0 + cs] * ext_ref[start:start + rt, :]
            act = conv * _sigmoid(conv)
            dst_ref[r0:r0 + rt, off:off + cs] = (act * mul).astype(BF16)
        ext_ref[0:halo, :] = ext_ref[tm:tm + halo, :]

    for idx in range(len(ext_refs)):
        ext_refs[idx][halo:halo + tm, :] = mm(C_QK + idx * cs, C_QK + (idx + 1) * cs)
        if idx > 0:
            conv_slice(idx - 1)
    half_v = ML_WIDTH // 2
    v_ref[:, 0:half_v] = mm(C_V, C_V + half_v).astype(BF16)
    conv_slice(len(ext_refs) - 1)
    v_ref[:, half_v:] = mm(C_V + half_v, C_O).astype(BF16)
    og_ref[...] = _sigmoid(mm(C_O, C_AQ)).astype(BF16)
    aq_ref[...] = mm(C_AQ, C_AK)
    ak_ref[...] = mm(C_AK, C_AV)
    av_ref[...] = mm(C_AV, C_G).astype(BF16)
    g_ref[...] = mm(C_G, C_END)


def _inproj(x2d, w_packed, conv_w, conv_b, tm, blocks_per_seq):
    t, d = x2d.shape
    widths = (ML_WIDTH, ML_WIDTH, ML_WIDTH, ML_WIDTH, C_AK - C_AQ, C_AV - C_AK, C_G - C_AV, C_END - C_G)
    dtypes = (BF16, BF16, BF16, BF16, F32, F32, BF16, F32)
    kern = functools.partial(_inproj_kernel, blocks_per_seq=blocks_per_seq)
    return pl.pallas_call(
        kern,
        grid=(t // tm,),
        in_specs=[pl.BlockSpec((tm, d), lambda i: (i, 0)),
                  pl.BlockSpec((d, C_END), lambda i: (0, 0)),
                  pl.BlockSpec((CONV_WIDTH, 2 * ML_WIDTH), lambda i: (0, 0)),
                  pl.BlockSpec((1, 2 * ML_WIDTH), lambda i: (0, 0))],
        out_specs=[pl.BlockSpec((tm, w), lambda i: (i, 0)) for w in widths],
        out_shape=[jax.ShapeDtypeStruct((t, w), dt) for w, dt in zip(widths, dtypes)],
        scratch_shapes=[pltpu.VMEM((tm + SUBLANES, 2 * LANES), F32)] * (2 * ML_WIDTH // (2 * LANES))
        + [pltpu.VMEM((tm, d), BF16)],
        compiler_params=pltpu.CompilerParams(dimension_semantics=("arbitrary",),
                                             vmem_limit_bytes=VMEM_LIMIT),
        name="inproj",
    )(x2d, w_packed, conv_w, conv_b)


def _time_scan(x, combine, identity):
    row = _iota(x.shape, 0)
    sh = 1
    while sh < x.shape[0]:
        x = combine(x, jnp.where(row >= sh, pltpu.roll(x, sh, 0), identity))
        sh *= 2
    return x


def _mlstm_kernel(q_ref, k_ref, v_ref, og_ref, g_ref, gb_ref, nw_ref, out_ref, ct_ref, m_ref, *, tq):
    s_idx = pl.program_id(1)
    L = ML_CHUNK
    D = ML_HEAD_DIM
    H = ML_HEADS
    heads = range(H)

    @pl.when(s_idx == 0)
    def _():
        ct_ref[...] = jnp.zeros_like(ct_ref)
        m_ref[...] = jnp.zeros_like(m_ref)

    causal = _iota((L, L), 1) <= _iota((L, L), 0)
    ones_blk = jnp.ones((L, D), BF16)
    mean_blk = jnp.full((D, D), 1.0 / D, BF16)
    m_prev = m_ref[0:1, :]
    head_lanes = _iota((L, LANES), 1) < H
    tile_of_lane = jnp.right_shift(_iota((LANES, H * L), 1), L.bit_length() - 1)
    spread = jnp.where(_iota((LANES, H * L), 0) == tile_of_lane, 1.0, 0.0).astype(BF16)

    def spread_heads(x):
        x = jnp.where(head_lanes, x, 0.0)
        hi = x.astype(BF16)
        lo = (x - hi.astype(F32)).astype(BF16)
        return _dot(hi, spread) + _dot(lo, spread)

    for c in range(tq // L):
        r0 = c * L
        gi = g_ref[r0:r0 + L, 0:LANES] + gb_ref[:, 0:LANES]
        gf = g_ref[r0:r0 + L, LANES:2 * LANES] + gb_ref[:, LANES:2 * LANES]
        b_cum = _time_scan(_log_sigmoid(gf), jnp.add, 0.0)
        r = gi - b_cum
        g = jnp.maximum(m_prev, _time_scan(r, jnp.maximum, NEG_INF))
        g_rep = spread_heads(g)
        b_rep = spread_heads(b_cum)
        b_last = b_cum[L - 1:L, :]
        m_new = jnp.maximum(b_last + m_prev, jnp.max(b_last + r, axis=0, keepdims=True))
        decay = jnp.exp(b_last + m_prev - m_new)
        shift = b_last - m_new
        r_t = r.T

        q_b = [q_ref[r0:r0 + L, h * D:(h + 1) * D] for h in heads]
        kt_b = [k_ref[r0:r0 + L, h * D:(h + 1) * D].T for h in heads]
        v_aug = [jnp.concatenate([v_ref[r0:r0 + L, h * D:(h + 1) * D], ones_blk], axis=-1) for h in heads]
        g_col = [g_rep[:, h * L:(h + 1) * L] for h in heads]
        w_intra = [jnp.exp(jnp.where(causal, r_t[h:h + 1, :] - g_col[h], NEG_INF)) for h in heads]
        s_b = [(_dot(q_b[h], kt_b[h]) * w_intra[h]).astype(BF16) for h in heads]
        ct = [ct_ref[h] for h in heads]
        inter = [_dot(q_b[h], ct[h].astype(BF16)) for h in heads]
        intra = [_dot(s_b[h], v_aug[h]) for h in heads]
        for h in heads:
            wi_col = jnp.exp(m_prev[:, h:h + 1] - g_col[h])
            clamp = jnp.exp(-(b_rep[:, h * L:(h + 1) * L] + g_col[h]))
            num = wi_col * inter[h][:, 0:D] + intra[h][:, 0:D]
            den = wi_col * inter[h][:, D:] + intra[h][:, D:]
            hh = num / jnp.maximum(jnp.abs(den), clamp)
            mu = _dot(hh.astype(BF16), mean_blk)
            hc = hh - mu
            var = _dot((hc * hc).astype(BF16), mean_blk)
            hn = hc * lax.rsqrt(var + LN_EPS) * nw_ref[:, h * D:(h + 1) * D]
            gate_o = og_ref[r0:r0 + L, h * D:(h + 1) * D].astype(F32)
            out_ref[r0:r0 + L, h * D:(h + 1) * D] = (gate_o * hn).astype(out_ref.dtype)
        for h in heads:
            w_row = jnp.exp(r_t[h:h + 1, :] + shift[:, h:h + 1])
            ktw = (kt_b[h].astype(F32) * w_row).astype(BF16)
            ct_ref[h] = decay[:, h:h + 1] * ct[h] + _dot(ktw, v_aug[h])
        m_prev = m_new

    m_ref[...] = jnp.broadcast_to(m_prev, m_ref.shape)


def _mlstm(q, k, v, og, g, gate_bias_row, norm_w_row, tq):
    b, s, _ = q.shape
    kern = functools.partial(_mlstm_kernel, tq=tq)

    def seq_spec(width):
        return pl.BlockSpec((None, tq, width), lambda bi, si: (bi, si, 0))

    def const_spec(shape):
        return pl.BlockSpec(shape, lambda bi, si: (0,) * len(shape))

    return pl.pallas_call(
        kern,
        grid=(b, s // tq),
        in_specs=[seq_spec(ML_WIDTH), seq_spec(ML_WIDTH), seq_spec(ML_WIDTH), seq_spec(ML_WIDTH),
                  seq_spec(2 * LANES), const_spec((1, 2 * LANES)), const_spec((1, ML_WIDTH))],
        out_specs=seq_spec(ML_WIDTH),
        out_shape=jax.ShapeDtypeStruct((b, s, ML_WIDTH), BF16),
        scratch_shapes=[pltpu.VMEM((ML_HEADS, ML_HEAD_DIM, 2 * ML_HEAD_DIM), F32),
                        pltpu.VMEM((SUBLANES, LANES), F32)],
        compiler_params=pltpu.CompilerParams(dimension_semantics=("arbitrary", "arbitrary"),
                                             vmem_limit_bytes=VMEM_LIMIT),
        name="mlstm",
    )(q, k, v, og, g, gate_bias_row, norm_w_row)


def _rope_tables(seq_len):
    half = ATT_HEAD_DIM // 2
    inv_freq = ROPE_THETA ** (-jnp.arange(half, dtype=F32) / half)
    ang = jnp.arange(seq_len, dtype=F32)[:, None] * inv_freq[None, :]
    cos = jnp.cos(ang)
    sin = jnp.sin(ang)
    cos_t = jnp.concatenate([cos, cos, cos, cos], axis=-1)
    sin_t = jnp.concatenate([-sin, sin, -sin, sin], axis=-1)
    return cos_t, sin_t


def _swa_kernel(sink_ref, aq_ref, ak_ref, av_ref, cos_ref, sin_ref, out_ref, kprev_ref, vprev_ref):
    i = pl.program_id(1)
    Lb = WINDOW
    half = ATT_HEAD_DIM // 2
    pairs = ATT_Q_HEADS // 2
    pairs_per_kv = pairs // ATT_KV_HEADS

    @pl.when(i == 0)
    def _():
        kprev_ref[...] = jnp.zeros_like(kprev_ref)
        vprev_ref[...] = jnp.zeros_like(vprev_ref)

    cos = cos_ref[...]
    sin = sin_ref[...]

    lower = (_iota((Lb, LANES), 1) & (ATT_HEAD_DIM - 1)) < half

    def rope(x):
        tiles = []
        for c in range(x.shape[-1] // LANES):
            xt = x[:, c * LANES:(c + 1) * LANES]
            swapped = jnp.where(lower, pltpu.roll(xt, LANES - half, 1), pltpu.roll(xt, half, 1))
            tiles.append(xt * cos + swapped * sin)
        return jnp.concatenate(tiles, axis=-1)

    q = rope(aq_ref[...]) * (ATT_HEAD_DIM ** -0.5)
    k_cur = rope(ak_ref[...]).astype(BF16)
    v_cur = av_ref[...]

    ql = _iota((Lb, 2 * Lb), 0)
    kj = _iota((Lb, 2 * Lb), 1)
    diff = Lb + ql - kj
    kpos = i * Lb + kj - Lb
    visible = jnp.where(diff >= 0, jnp.where(diff < WINDOW, jnp.where(kpos >= 0, 1, 0), 0), 0)
    bias = jnp.where(visible > 0, 0.0, NEG_INF).astype(F32)
    bias = jnp.concatenate([bias] * (2 * pairs_per_kv), axis=0)

    lane = _iota((Lb, LANES), 1)
    low_half = lane < ATT_HEAD_DIM

    for g in range(ATT_KV_HEADS):
        kk = jnp.concatenate([kprev_ref[:, g * LANES:(g + 1) * LANES], k_cur[:, g * LANES:(g + 1) * LANES]], axis=0)
        vv = jnp.concatenate([vprev_ref[:, g * LANES:(g + 1) * LANES], v_cur[:, g * LANES:(g + 1) * LANES]], axis=0)
        rows = []
        sinks = []
        for p in range(pairs_per_kv):
            pair = g * pairs_per_kv + p
            q2 = q[:, pair * LANES:(pair + 1) * LANES]
            rows.append(jnp.where(low_half, q2, 0.0))
            rows.append(jnp.where(low_half, 0.0, q2))
            sinks.append(jnp.full((Lb, 1), sink_ref[2 * pair], F32))
            sinks.append(jnp.full((Lb, 1), sink_ref[2 * pair + 1], F32))
        qs = jnp.concatenate(rows, axis=0).astype(BF16)
        sink = jnp.concatenate(sinks, axis=0)

        sc = lax.dot_general(qs, kk, (((1,), (1,)), ((), ())), preferred_element_type=F32) + bias
        m = jnp.maximum(jnp.max(sc, axis=-1, keepdims=True), sink)
        p_un = jnp.exp(sc - m)
        denom = jnp.sum(p_un, axis=-1, keepdims=True) + jnp.exp(sink - m)
        o = _dot(p_un.astype(BF16), vv) / denom
        for p in range(pairs_per_kv):
            pair = g * pairs_per_kv + p
            even = o[(2 * p) * Lb:(2 * p + 1) * Lb, :]
            odd = o[(2 * p + 1) * Lb:(2 * p + 2) * Lb, :]
            out_ref[:, pair * LANES:(pair + 1) * LANES] = jnp.where(low_half, even, odd).astype(out_ref.dtype)

    kprev_ref[...] = k_cur
    vprev_ref[...] = v_cur


def _swa(aq, ak, av, cos_t, sin_t, sinks):
    b, s, _ = aq.shape
    kvw = ATT_KV_HEADS * LANES

    def seq_spec(width):
        return pl.BlockSpec((None, WINDOW, width), lambda bi, si: (bi, si, 0))

    tab_spec = pl.BlockSpec((WINDOW, LANES), lambda bi, si: (si, 0))
    return pl.pallas_call(
        _swa_kernel,
        grid=(b, s // WINDOW),
        in_specs=[pl.BlockSpec(memory_space=pltpu.SMEM),
                  seq_spec(ATT_WIDTH), seq_spec(kvw), seq_spec(kvw), tab_spec, tab_spec],
        out_specs=seq_spec(ATT_WIDTH),
        out_shape=jax.ShapeDtypeStruct((b, s, ATT_WIDTH), BF16),
        scratch_shapes=[pltpu.VMEM((WINDOW, kvw), BF16), pltpu.VMEM((WINDOW, kvw), BF16)],
        compiler_params=pltpu.CompilerParams(dimension_semantics=("arbitrary", "arbitrary"),
                                             vmem_limit_bytes=VMEM_LIMIT),
        name="swa",
    )(sinks, aq, ak, av, cos_t, sin_t)


def _store_token_tiles(ref, val, row0=0, rows_per_token=None):
    n, w = val.shape
    segs = w // LANES
    rpt = rows_per_token or segs
    for j in range(segs):
        ref[pl.ds(row0 + j, n, stride=rpt), :] = val[:, j * LANES:(j + 1) * LANES]


def _load_token_tiles(ref, n, segs, row0=0, rows_per_token=None):
    rpt = rows_per_token or segs
    return jnp.concatenate([ref[pl.ds(row0 + j, n, stride=rpt), :] for j in range(segs)], axis=-1)


def _outproj_kernel(x_ref, ml_ref, att_ref, wo_ref, lnw_ref, lnb_ref, wr_ref, br_ref,
                    x1t_ref, meta_ref, cnt_ref, *, alpha):
    step = pl.program_id(0)
    tm, d = x_ref.shape
    y = _dot(ml_ref[...], wo_ref[0:ML_WIDTH, :]) + _dot(att_ref[...], wo_ref[ML_WIDTH:, :])
    x1 = _layer_norm(alpha * x_ref[...] + y, lnw_ref[...], lnb_ref[...])
    _store_token_tiles(x1t_ref, x1)

    logits = _dot_exact(x1, wr_ref[...]) + br_ref[...]
    lane = _iota((tm, LANES), 1).astype(F32)
    big = float(LANES)

    def first_argmax(vals):
        top = jnp.max(vals, axis=-1, keepdims=True)
        idx = jnp.min(jnp.where(vals == top, lane, big), axis=-1, keepdims=True)
        return top, idx

    g_logits = jnp.where(lane < N_GROUPS, logits, NEG_INF)
    g_top, g_idx = first_argmax(g_logits)
    g_p = 1.0 / jnp.sum(jnp.exp(g_logits - g_top), axis=-1, keepdims=True)

    e_lo = N_GROUPS + EXPERTS_PER_GROUP * g_idx
    in_group = jnp.where(lane >= e_lo, jnp.where(lane < e_lo + EXPERTS_PER_GROUP, 1.0, 0.0), 0.0)
    e_logits = jnp.where(in_group > 0, logits, NEG_INF)
    v1, i1 = first_argmax(e_logits)
    v2, i2 = first_argmax(jnp.where(lane == i1, NEG_INF, e_logits))
    r = jnp.exp(v2 - v1)
    w1 = g_p / (1.0 + r)
    w2 = g_p * r / (1.0 + r)

    a1 = i1 - e_lo
    a2 = i2 - e_lo
    lo = jnp.minimum(a1, a2)
    hi = jnp.maximum(a1, a2)
    w_lo = jnp.where(a1 < a2, w1, w2)
    w_hi = jnp.where(a1 < a2, w2, w1)
    pair_idx = (EXPERTS_PER_GROUP - 1) * lo - lo * (lo - 1.0) * 0.5 + (hi - lo - 1.0)
    cls = g_idx * PAIRS_PER_GROUP + pair_idx

    meta = jnp.where(lane == 0.0, cls, jnp.where(lane == 1.0, w_lo, jnp.where(lane == 2.0, w_hi, 0.0)))
    meta_ref[...] = meta

    @pl.when(step == 0)
    def _():
        cnt_ref[...] = jnp.zeros_like(cnt_ref)

    onehot = jnp.where(lane == cls, 1.0, 0.0)
    cnt_ref[0:1, :] += jnp.sum(onehot, axis=0, keepdims=True)


def _outproj(x2d, ml2d, att2d, w_out_b, ln_w, ln_b, w_router, b_router, alpha, tm):
    t, d = x2d.shape
    kern = functools.partial(_outproj_kernel, alpha=alpha)

    def const_spec(shape):
        return pl.BlockSpec(shape, lambda i: (0,) * len(shape))

    return pl.pallas_call(
        kern,
        grid=(t // tm,),
        in_specs=[pl.BlockSpec((tm, d), lambda i: (i, 0)),
                  pl.BlockSpec((tm, ML_WIDTH), lambda i: (i, 0)),
                  pl.BlockSpec((tm, ATT_WIDTH), lambda i: (i, 0)),
                  const_spec((ML_WIDTH + ATT_WIDTH, d)), const_spec((1, d)), const_spec((1, d)),
                  const_spec((d, LANES)), const_spec((1, LANES))],
        out_specs=[pl.BlockSpec((tm * (d // LANES), LANES), lambda i: (i, 0)),
                   pl.BlockSpec((tm, LANES), lambda i: (i, 0)), const_spec((SUBLANES, LANES))],
        out_shape=[jax.ShapeDtypeStruct((t * (d // LANES), LANES), F32),
                   jax.ShapeDtypeStruct((t, LANES), F32),
                   jax.ShapeDtypeStruct((SUBLANES, LANES), F32)],
        compiler_params=pltpu.CompilerParams(dimension_semantics=("arbitrary",),
                                             vmem_limit_bytes=VMEM_LIMIT),
        name="outproj",
    )(x2d, ml2d, att2d, w_out_b, ln_w, ln_b, w_router, b_router)


def _class_expert_table():
    tab = np.zeros((SUBLANES, LANES), np.float32)
    for g in range(N_GROUPS):
        idx = 0
        for lo in range(EXPERTS_PER_GROUP):
            for hi in range(lo + 1, EXPERTS_PER_GROUP):
                c = g * PAIRS_PER_GROUP + idx
                tab[0, c] = g * EXPERTS_PER_GROUP + lo
                tab[1, c] = g * EXPERTS_PER_GROUP + hi
                idx += 1
    return tab


def _rank_kernel(meta_ref, cnt_ref, tab_ref, pos_ref, tile_ref, base_ref, run_ref, *, tb, n_tiles_pad):
    step = pl.program_id(0)
    lane8 = _iota((SUBLANES, LANES), 1)

    @pl.when(step == 0)
    def _():
        cnt = jnp.broadcast_to(cnt_ref[0:1, :], (SUBLANES, LANES))
        tiles = jnp.floor((cnt + (MOE_TILE - 1.0)) * (1.0 / MOE_TILE))
        cum = tiles
        sh = 1
        while sh < LANES:
            cum = cum + jnp.where(lane8 >= sh, pltpu.roll(cum, sh, 1), 0.0)
            sh *= 2
        excl = cum - tiles
        base_ref[...] = excl * MOE_TILE
        run_ref[...] = jnp.zeros_like(run_ref)

        ti = _iota((n_tiles_pad, LANES), 0).astype(F32)
        lane = _iota((n_tiles_pad, LANES), 1)
        done = jnp.where(lane < N_CLASSES, jnp.where(cum[0:1, :] <= ti, 1.0, 0.0), 0.0)
        t_cls = jnp.sum(done, axis=-1, keepdims=True)
        sel = jnp.where(lane.astype(F32) == t_cls, 1.0, 0.0)
        cnt_i = jnp.sum(sel * cnt[0:1, :], axis=-1, keepdims=True)
        first_i = jnp.sum(sel * excl[0:1, :], axis=-1, keepdims=True)
        rows_i = jnp.clip(cnt_i - MOE_TILE * (ti[:, 0:1] - first_i), 0.0, float(MOE_TILE))
        e_lo = jnp.sum(sel * tab_ref[0:1, :], axis=-1, keepdims=True)
        e_hi = jnp.sum(sel * tab_ref[1:2, :], axis=-1, keepdims=True)
        n_tiles = jnp.sum(jnp.where(lane < N_CLASSES, jnp.broadcast_to(tiles[0:1, :], (n_tiles_pad, LANES)), 0.0),
                          axis=-1, keepdims=True)
        info = jnp.where(lane == 0, rows_i,
                         jnp.where(lane == 1, e_lo, jnp.where(lane == 2, e_hi, jnp.where(lane == 3, n_tiles, 0.0))))
        tile_ref[...] = info.astype(I32)

    cls = meta_ref[:, 0:1]
    lane = _iota((tb, LANES), 1).astype(F32)
    onehot = jnp.where(lane == cls, 1.0, 0.0)
    strict_lower = jnp.where(_iota((tb, tb), 1) < _iota((tb, tb), 0), 1.0, 0.0).astype(BF16)
    before = _dot(strict_lower, onehot.astype(BF16))
    slot = jnp.sum(onehot * (before + run_ref[0:1, :] + base_ref[0:1, :]), axis=-1, keepdims=True)
    run_ref[...] = run_ref[...] + jnp.sum(onehot, axis=0, keepdims=True)
    slot_t = jnp.broadcast_to(slot, (tb, LANES)).T
    pos_ref[...] = slot_t[0:SUBLANES, :].astype(I32)


def _rank(meta, counts, tb, n_tiles_pad):
    t = meta.shape[0]
    kern = functools.partial(_rank_kernel, tb=tb, n_tiles_pad=n_tiles_pad)
    tab = jnp.asarray(_class_expert_table())
    return pl.pallas_call(
        kern,
        grid=(t // tb,),
        in_specs=[pl.BlockSpec((tb, LANES), lambda i: (i, 0)),
                  pl.BlockSpec((SUBLANES, LANES), lambda i: (0, 0)),
                  pl.BlockSpec((SUBLANES, LANES), lambda i: (0, 0))],
        out_specs=[pl.BlockSpec((SUBLANES, tb), lambda i: (0, i)),
                   pl.BlockSpec((n_tiles_pad, LANES), lambda i: (0, 0))],
        out_shape=[jax.ShapeDtypeStruct((SUBLANES, t), I32),
                   jax.ShapeDtypeStruct((n_tiles_pad, LANES), I32)],
        scratch_shapes=[pltpu.VMEM((SUBLANES, LANES), F32), pltpu.VMEM((SUBLANES, LANES), F32)],
        compiler_params=pltpu.CompilerParams(dimension_semantics=("arbitrary",),
                                             vmem_limit_bytes=VMEM_LIMIT),
        name="rank",
    )(meta, counts, tab)


def _dispatch_kernel(pos_ref, src_ref, zeros_ref, dst_ref, sem, *, tb, rpt):
    del zeros_ref
    base = pl.program_id(0) * tb

    def issue(j, carry):
        src = src_ref.at[pl.ds(pl.multiple_of(j * rpt, rpt), rpt)]
        dst = dst_ref.at[pl.ds(pl.multiple_of(pos_ref[base + j] * rpt, rpt), rpt)]
        pltpu.make_async_copy(src, dst, sem).start()
        return carry

    lax.fori_loop(0, tb, issue, 0, unroll=8)
    pltpu.make_async_copy(src_ref, dst_ref.at[pl.ds(0, tb * rpt)], sem).wait()


def _dispatch(pos, src, n_slots, rpt, tb):
    t = pos.shape[0]
    kern = functools.partial(_dispatch_kernel, tb=tb, rpt=rpt)
    return pl.pallas_call(
        kern,
        grid_spec=pltpu.PrefetchScalarGridSpec(
            num_scalar_prefetch=1,
            grid=(t // tb,),
            in_specs=[pl.BlockSpec((tb * rpt, LANES), lambda i, pos_ref: (i, 0)),
                      pl.BlockSpec(memory_space=pl.ANY)],
            out_specs=pl.BlockSpec(memory_space=pl.ANY),
            scratch_shapes=[pltpu.SemaphoreType.DMA(())],
        ),
        out_shape=jax.ShapeDtypeStruct((n_slots * rpt, LANES), src.dtype),
        input_output_aliases={2: 0},
        compiler_params=pltpu.CompilerParams(dimension_semantics=("arbitrary",),
                                             has_side_effects=True, vmem_limit_bytes=VMEM_LIMIT),
        name="dispatch",
    )(pos, src, jnp.zeros((n_slots * rpt, LANES), src.dtype))


def _collect_kernel(pos_ref, ys_ref, x1t_ref, meta_ref, lnw_ref, lnb_ref, out_ref, buf_ref, sem,
                    *, tb, alpha, d):
    segs = d // LANES
    rpt = 2 * segs
    base = pl.program_id(0) * tb

    def issue(j, carry):
        src = ys_ref.at[pl.ds(pl.multiple_of(pos_ref[base + j] * rpt, rpt), rpt)]
        dst = buf_ref.at[pl.ds(pl.multiple_of(j * rpt, rpt), rpt)]
        pltpu.make_async_copy(src, dst, sem).start()
        return carry

    lax.fori_loop(0, tb, issue, 0, unroll=8)
    pltpu.make_async_copy(ys_ref.at[pl.ds(0, tb * rpt)], buf_ref, sem).wait()

    x1 = _load_token_tiles(x1t_ref, tb, segs)
    y_lo = _load_token_tiles(buf_ref, tb, segs, row0=0, rows_per_token=rpt)
    y_hi = _load_token_tiles(buf_ref, tb, segs, row0=segs, rows_per_token=rpt)
    meta = meta_ref[...]
    z = alpha * x1 + meta[:, 1:2] * y_lo + meta[:, 2:3] * y_hi
    out_ref[...] = _layer_norm(z, lnw_ref[...], lnb_ref[...])


def _collect(pos, ys, x1t, meta, ln_w, ln_b, alpha, d, tb):
    t = pos.shape[0]
    segs = d // LANES
    kern = functools.partial(_collect_kernel, tb=tb, alpha=alpha, d=d)
    return pl.pallas_call(
        kern,
        grid_spec=pltpu.PrefetchScalarGridSpec(
            num_scalar_prefetch=1,
            grid=(t // tb,),
            in_specs=[pl.BlockSpec(memory_space=pl.ANY),
                      pl.BlockSpec((tb * segs, LANES), lambda i, pos_ref: (i, 0)),
                      pl.BlockSpec((tb, LANES), lambda i, pos_ref: (i, 0)),
                      pl.BlockSpec((1, d), lambda i, pos_ref: (0, 0)),
                      pl.BlockSpec((1, d), lambda i, pos_ref: (0, 0))],
            out_specs=pl.BlockSpec((tb, d), lambda i, pos_ref: (i, 0)),
            scratch_shapes=[pltpu.VMEM((tb * 2 * segs, LANES), F32), pltpu.SemaphoreType.DMA(())],
        ),
        out_shape=jax.ShapeDtypeStruct((t, d), F32),
        compiler_params=pltpu.CompilerParams(dimension_semantics=("arbitrary",),
                                             vmem_limit_bytes=VMEM_LIMIT),
        name="collect",
    )(pos, ys, x1t, meta, ln_w, ln_b)


def _moe_kernel(rows_ref, elo_ref, ehi_ref, nt_ref, xs_ref, wg_lo_ref, wu_lo_ref, wd_lo_ref,
                wg_hi_ref, wu_hi_ref, wd_hi_ref, ys_ref, *, d):
    i = pl.program_id(0)
    rows = rows_ref[i]
    segs = d // LANES

    @pl.when(rows > 0)
    def _():
        xb = _load_token_tiles(xs_ref, MOE_TILE, segs).astype(BF16)

        def expert(wg_ref, wu_ref, wd_ref):
            gate = _dot(xb, wg_ref[...])
            up = _dot(xb, wu_ref[...])
            hidden = gate * _sigmoid(gate) * up
            return _dot(hidden.astype(BF16), wd_ref[...])

        _store_token_tiles(ys_ref, expert(wg_lo_ref, wu_lo_ref, wd_lo_ref), row0=0, rows_per_token=2 * segs)
        _store_token_tiles(ys_ref, expert(wg_hi_ref, wu_hi_ref, wd_hi_ref), row0=segs, rows_per_token=2 * segs)

    @pl.when(rows <= 0)
    def _():
        ys_ref[...] = jnp.zeros_like(ys_ref)


def _moe(info, xs, wg, wu, wd, d, n_tiles_pad):
    de = wg.shape[-1]
    segs = d // LANES
    kern = functools.partial(_moe_kernel, d=d)

    def last_live(i, nt_ref):
        return jnp.minimum(i, jnp.maximum(nt_ref[0] - 1, 0))

    def up_spec(which):
        return pl.BlockSpec((None, d, de),
                            lambda i, rows, elo, ehi, nt: ((elo, ehi)[which][last_live(i, nt)], 0, 0))

    def down_spec(which):
        return pl.BlockSpec((None, de, d),
                            lambda i, rows, elo, ehi, nt: ((elo, ehi)[which][last_live(i, nt)], 0, 0))

    rows, elo, ehi, nt = info[:, 0], info[:, 1], info[:, 2], info[0:1, 3]
    return pl.pallas_call(
        kern,
        grid_spec=pltpu.PrefetchScalarGridSpec(
            num_scalar_prefetch=4,
            grid=(n_tiles_pad,),
            in_specs=[pl.BlockSpec((MOE_TILE * segs, LANES),
                                   lambda i, rows, elo, ehi, nt: (last_live(i, nt), 0)),
                      up_spec(0), up_spec(0), down_spec(0), up_spec(1), up_spec(1), down_spec(1)],
            out_specs=pl.BlockSpec((MOE_TILE * 2 * segs, LANES), lambda i, rows, elo, ehi, nt: (i, 0)),
        ),
        out_shape=jax.ShapeDtypeStruct((n_tiles_pad * MOE_TILE * 2 * segs, LANES), F32),
        compiler_params=pltpu.CompilerParams(dimension_semantics=("arbitrary",),
                                             vmem_limit_bytes=VMEM_LIMIT),
        name="moe",
    )(rows, elo, ehi, nt, xs, wg, wu, wd, wg, wu, wd)


def _pick_block(n, target):
    blk = min(n, target)
    while n % blk:
        blk //= 2
    return blk


def kernel(x, w_in, conv_w, conv_b, mlstm_gate_bias, mlstm_norm_w, attn_sinks, w_out, ln1_w, ln1_b,
           w_group_router, b_group_router, w_expert_router, b_expert_router,
           w_exp_gate, w_exp_up, w_exp_down, ln2_w, ln2_b):
    b, s, d = x.shape
    t = b * s
    depth = w_in.shape[0]
    alpha = (2.0 * depth) ** 0.25
    assert s % ML_CHUNK == 0 and s % WINDOW == 0 and d % LANES == 0

    tm = _pick_block(t, 512)
    tq = _pick_block(s, 4 * ML_CHUNK)
    tb_rank = _pick_block(t, 512)
    tb_dma = _pick_block(t, 2048)
    tb_col = _pick_block(t, 256)
    n_tiles_pad = -(-(t // MOE_TILE + N_CLASSES) // SUBLANES) * SUBLANES
    cos_t, sin_t = _rope_tables(s)

    for l in range(depth):
        x2d = x.reshape(t, d)
        q, k, v, og, aq, ak, av, g = _inproj(x2d, _pack_w_in(w_in[l]), conv_w[l], conv_b[l][None, :], tm, s // tm)
        bias_pad = jnp.zeros((LANES - ML_HEADS,), F32)
        gate_bias_row = jnp.concatenate(
            [mlstm_gate_bias[l, 0], bias_pad, mlstm_gate_bias[l, 1], bias_pad])[None, :]
        ml = _mlstm(q.reshape(b, s, -1), k.reshape(b, s, -1), v.reshape(b, s, -1), og.reshape(b, s, -1),
                    g.reshape(b, s, -1), gate_bias_row, mlstm_norm_w[l][None, :], tq)
        att = _swa(aq.reshape(b, s, -1), ak.reshape(b, s, -1), av.reshape(b, s, -1), cos_t, sin_t, attn_sinks[l])

        w_router = jnp.concatenate(
            [w_group_router[l], w_expert_router[l],
             jnp.zeros((d, LANES - N_GROUPS - N_EXPERTS), F32)], axis=-1)
        b_router = jnp.concatenate(
            [b_group_router[l], b_expert_router[l], jnp.zeros((LANES - N_GROUPS - N_EXPERTS,), F32)])[None, :]
        x1t, meta, counts = _outproj(x2d, ml.reshape(t, -1), att.reshape(t, -1), w_out[l].astype(BF16),
                                     ln1_w[l][None, :], ln1_b[l][None, :], w_router, b_router, alpha, tm)

        pos2d, info = _rank(meta, counts, tb_rank, n_tiles_pad)
        pos = pos2d[0]
        xs = _dispatch(pos, x1t, n_tiles_pad * MOE_TILE, d // LANES, tb_dma)
        ys = _moe(info, xs, w_exp_gate[l].astype(BF16), w_exp_up[l].astype(BF16), w_exp_down[l].astype(BF16),
                  d, n_tiles_pad)
        out = _collect(pos, ys, x1t, meta, ln2_w[l][None, :], ln2_b[l][None, :], alpha, d, tb_col)
        x = out.reshape(b, s, d)
    return x
```

```python
import functools
import math

import numpy as np
import jax
import jax.numpy as jnp
from jax import lax
from jax.experimental import pallas as pl
from jax.experimental.pallas import tpu as pltpu

F32 = jnp.float32
BF16 = jnp.bfloat16
I32 = jnp.int32

ML_HEADS = 4
ML_HEAD_DIM = 128
ML_WIDTH = ML_HEADS * ML_HEAD_DIM
ML_CHUNK = 128
CONV_WIDTH = 4
ATT_Q_HEADS = 8
ATT_KV_HEADS = 2
ATT_HEAD_DIM = 64
ATT_WIDTH = ATT_Q_HEADS * ATT_HEAD_DIM
ATT_KV_WIDTH = ATT_KV_HEADS * ATT_HEAD_DIM
WINDOW = 128
ROPE_THETA = 10000.0
N_GROUPS = 4
EXPERTS_PER_GROUP = 8
N_EXPERTS = N_GROUPS * EXPERTS_PER_GROUP
PAIRS_PER_GROUP = EXPERTS_PER_GROUP * (EXPERTS_PER_GROUP - 1) // 2
N_CLASSES = N_GROUPS * PAIRS_PER_GROUP
LN_EPS = 1e-5

LANES = 128
SUBLANES = 8
MOE_TILE = 128
VMEM_LIMIT = 56 * 1024 * 1024

NEG_INF = float("-inf")


def _sigmoid(x):
    return 1.0 / (1.0 + jnp.exp(-x))


def _log_sigmoid(x):
    return jnp.minimum(x, 0.0) - jnp.log(1.0 + jnp.exp(-jnp.abs(x)))


def _iota(shape, dim):
    return lax.broadcasted_iota(I32, shape, dim)


def _dot(a, b):
    return jnp.dot(a, b, preferred_element_type=F32)


def _dot_exact(a, b):
    return jnp.dot(a, b, preferred_element_type=F32, precision=lax.Precision.HIGHEST)


def _layer_norm(z, w, b):
    mu = jnp.mean(z, axis=-1, keepdims=True)
    zc = z - mu
    var = jnp.mean(zc * zc, axis=-1, keepdims=True)
    return zc * lax.rsqrt(var + LN_EPS) * w + b


C_QK = 0
C_V = C_QK + 2 * ML_WIDTH
C_O = C_V + ML_WIDTH
C_AQ = C_O + ML_WIDTH
C_AK = C_AQ + ATT_WIDTH
C_AV = C_AK + ATT_KV_HEADS * LANES
C_G = C_AV + ATT_KV_HEADS * LANES
C_END = C_G + 2 * LANES


def _pack_w_in(w_in):
    sizes = (2 * ML_WIDTH, ML_WIDTH, ML_WIDTH, ML_HEADS, ML_HEADS, ATT_WIDTH, ATT_KV_WIDTH, ATT_KV_WIDTH)
    splits = np.cumsum(sizes)[:-1].tolist()
    w_qk, w_v, w_o, w_i, w_f, w_aq, w_ak, w_av = jnp.split(w_in, splits, axis=-1)
    half = ATT_HEAD_DIM // 2

    def head(w, h):
        return w[:, h * ATT_HEAD_DIM:(h + 1) * ATT_HEAD_DIM]

    def q_tile(a, b):
        return [a[:, :half], b[:, :half], a[:, half:], b[:, half:]]

    q_cols = [t for p in range(ATT_Q_HEADS // 2) for t in q_tile(head(w_aq, 2 * p), head(w_aq, 2 * p + 1))]
    k_cols = [t for h in range(ATT_KV_HEADS) for t in q_tile(head(w_ak, h), head(w_ak, h))]
    v_cols = [t for h in range(ATT_KV_HEADS) for t in (head(w_av, h), head(w_av, h))]
    lane_pad = jnp.zeros((w_in.shape[0], LANES - ML_HEADS), w_in.dtype)
    packed = jnp.concatenate([w_qk, w_v, w_o] + q_cols + k_cols + v_cols + [w_i, lane_pad, w_f, lane_pad], axis=-1)
    return packed.astype(BF16)


def _inproj_kernel(x_ref, w_ref, cw_ref, cb_ref, q_ref, k_ref, v_ref, og_ref, aq_ref, ak_ref, av_ref, g_ref,
                   *scratch, blocks_per_seq):
    *ext_refs, xb_ref = scratch
    tm = x_ref.shape[0]
    halo = SUBLANES
    cs = ext_refs[0].shape[1]
    xb_ref[...] = x_ref[...].astype(BF16)

    def mm(lo, hi):
        return _dot(xb_ref[...], w_ref[:, lo:hi])

    @pl.when(pl.program_id(0) % blocks_per_seq == 0)
    def _():
        for ext_ref in ext_refs:
            ext_ref[0:halo, :] = jnp.zeros((halo, cs), F32)

    scale = ML_HEAD_DIM ** -0.5

    def conv_slice(idx):
        ext_ref = ext_refs[idx]
        c0 = idx * cs
        dst_ref, off, mul = (q_ref, c0, 1.0) if c0 < ML_WIDTH else (k_ref, c0 - ML_WIDTH, scale)
        rt = ML_CHUNK
        for r0 in range(0, tm, rt):
            conv = cb_ref[:, c0:c0 + cs]
            for j in range(CONV_WIDTH):
                start = halo + r0 - (CONV_WIDTH - 1) + j
                conv = conv + cw_ref[j:j + 1, c0:c0 + cs] * ext_ref[start:start + rt, :]
            act = conv * _sigmoid(conv)
            dst_ref[r0:r0 + rt, off:off + cs] = (act * mul).astype(BF16)
        ext_ref[0:halo, :] = ext_ref[tm:tm + halo, :]

    for idx in range(len(ext_refs)):
        ext_refs[idx][halo:halo + tm, :] = mm(C_QK + idx * cs, C_QK + (idx + 1) * cs)
        if idx > 0:
            conv_slice(idx - 1)
    half_v = ML_WIDTH // 2
    v_ref[:, 0:half_v] = mm(C_V, C_V + half_v).astype(BF16)
    conv_slice(len(ext_refs) - 1)
    v_ref[:, half_v:] = mm(C_V + half_v, C_O).astype(BF16)
    og_ref[...] = _sigmoid(mm(C_O, C_AQ)).astype(BF16)
    aq_ref[...] = mm(C_AQ, C_AK)
    ak_ref[...] = mm(C_AK, C_AV)
    av_ref[...] = mm(C_AV, C_G).astype(BF16)
    g_ref[...] = mm(C_G, C_END)


def _inproj(x2d, w_packed, conv_w, conv_b, tm, blocks_per_seq):
    t, d = x2d.shape
    widths = (ML_WIDTH, ML_WIDTH, ML_WIDTH, ML_WIDTH, C_AK - C_AQ, C_AV - C_AK, C_G - C_AV, C_END - C_G)
    dtypes = (BF16, BF16, BF16, BF16, F32, F32, BF16, F32)
    kern = functools.partial(_inproj_kernel, blocks_per_seq=blocks_per_seq)
    return pl.pallas_call(
        kern,
        grid=(t // tm,),
        in_specs=[pl.BlockSpec((tm, d), lambda i: (i, 0)),
                  pl.BlockSpec((d, C_END), lambda i: (0, 0)),
                  pl.BlockSpec((CONV_WIDTH, 2 * ML_WIDTH), lambda i: (0, 0)),
                  pl.BlockSpec((1, 2 * ML_WIDTH), lambda i: (0, 0))],
        out_specs=[pl.BlockSpec((tm, w), lambda i: (i, 0)) for w in widths],
        out_shape=[jax.ShapeDtypeStruct((t, w), dt) for w, dt in zip(widths, dtypes)],
        scratch_shapes=[pltpu.VMEM((tm + SUBLANES, 2 * LANES), F32)] * (2 * ML_WIDTH // (2 * LANES))
        + [pltpu.VMEM((tm, d), BF16)],
        compiler_params=pltpu.CompilerParams(dimension_semantics=("arbitrary",),
                                             vmem_limit_bytes=VMEM_LIMIT),
        name="inproj",
    )(x2d, w_packed, conv_w, conv_b)


def _time_scan(x, combine, identity):
    row = _iota(x.shape, 0)
    sh = 1
    while sh < x.shape[0]:
        x = combine(x, jnp.where(row >= sh, pltpu.roll(x, sh, 0), identity))
        sh *= 2
    return x


def _mlstm_kernel(q_ref, k_ref, v_ref, og_ref, g_ref, gb_ref, nw_ref, out_ref, ct_ref, m_ref, *, tq):
    s_idx = pl.program_id(1)
    L = ML_CHUNK
    D = ML_HEAD_DIM
    H = ML_HEADS
    heads = range(H)

    @pl.when(s_idx == 0)
    def _():
        ct_ref[...] = jnp.zeros_like(ct_ref)
        m_ref[...] = jnp.zeros_like(m_ref)

    causal = _iota((L, L), 1) <= _iota((L, L), 0)
    ones_blk = jnp.ones((L, D), BF16)
    mean_blk = jnp.full((D, D), 1.0 / D, BF16)
    m_prev = m_ref[0:1, :]
    head_lanes = _iota((L, LANES), 1) < H
    tile_of_lane = jnp.right_shift(_iota((LANES, H * L), 1), L.bit_length() - 1)
    spread = jnp.where(_iota((LANES, H * L), 0) == tile_of_lane, 1.0, 0.0).astype(BF16)

    def spread_heads(x):
        x = jnp.where(head_lanes, x, 0.0)
        hi = x.astype(BF16)
        lo = (x - hi.astype(F32)).astype(BF16)
        return _dot(hi, spread) + _dot(lo, spread)

    for c in range(tq // L):
        r0 = c * L
        gi = g_ref[r0:r0 + L, 0:LANES] + gb_ref[:, 0:LANES]
        gf = g_ref[r0:r0 + L, LANES:2 * LANES] + gb_ref[:, LANES:2 * LANES]
        b_cum = _time_scan(_log_sigmoid(gf), jnp.add, 0.0)
        r = gi - b_cum
        g = jnp.maximum(m_prev, _time_scan(r, jnp.maximum, NEG_INF))
        g_rep = spread_heads(g)
        b_rep = spread_heads(b_cum)
        b_last = b_cum[L - 1:L, :]
        m_new = jnp.maximum(b_last + m_prev, jnp.max(b_last + r, axis=0, keepdims=True))
        decay = jnp.exp(b_last + m_prev - m_new)
        shift = b_last - m_new
        r_t = r.T

        q_b = [q_ref[r0:r0 + L, h * D:(h + 1) * D] for h in heads]
        kt_b = [k_ref[r0:r0 + L, h * D:(h + 1) * D].T for h in heads]
        v_aug = [jnp.concatenate([v_ref[r0:r0 + L, h * D:(h + 1) * D], ones_blk], axis=-1) for h in heads]
        g_col = [g_rep[:, h * L:(h + 1) * L] for h in heads]
        w_intra = [jnp.exp(jnp.where(causal, r_t[h:h + 1, :] - g_col[h], NEG_INF)) for h in heads]
        s_b = [(_dot(q_b[h], kt_b[h]) * w_intra[h]).astype(BF16) for h in heads]
        ct = [ct_ref[h] for h in heads]
        inter = [_dot(q_b[h], ct[h].astype(BF16)) for h in heads]
        intra = [_dot(s_b[h], v_aug[h]) for h in heads]
        for h in heads:
            wi_col = jnp.exp(m_prev[:, h:h + 1] - g_col[h])
            clamp = jnp.exp(-(b_rep[:, h * L:(h + 1) * L] + g_col[h]))
            num = wi_col * inter[h][:, 0:D] + intra[h][:, 0:D]
            den = wi_col * inter[h][:, D:] + intra[h][:, D:]
            hh = num / jnp.maximum(jnp.abs(den), clamp)
            mu = _dot(hh.astype(BF16), mean_blk)
            hc = hh - mu
            var = _dot((hc * hc).astype(BF16), mean_blk)
            hn = hc * lax.rsqrt(var + LN_EPS) * nw_ref[:, h * D:(h + 1) * D]
            gate_o = og_ref[r0:r0 + L, h * D:(h + 1) * D].astype(F32)
            out_ref[r0:r0 + L, h * D:(h + 1) * D] = (gate_o * hn).astype(out_ref.dtype)
        for h in heads:
            w_row = jnp.exp(r_t[h:h + 1, :] + shift[:, h:h + 1])
            ktw = (kt_b[h].astype(F32) * w_row).astype(BF16)
            ct_ref[h] = decay[:, h:h + 1] * ct[h] + _dot(ktw, v_aug[h])
        m_prev = m_new

    m_ref[...] = jnp.broadcast_to(m_prev, m_ref.shape)


def _mlstm(q, k, v, og, g, gate_bias_row, norm_w_row, tq):
    b, s, _ = q.shape
    kern = functools.partial(_mlstm_kernel, tq=tq)

    def seq_spec(width):
        return pl.BlockSpec((None, tq, width), lambda bi, si: (bi, si, 0))

    def const_spec(shape):
        return pl.BlockSpec(shape, lambda bi, si: (0,) * len(shape))

    return pl.pallas_call(
        kern,
        grid=(b, s // tq),
        in_specs=[seq_spec(ML_WIDTH), seq_spec(ML_WIDTH), seq_spec(ML_WIDTH), seq_spec(ML_WIDTH),
                  seq_spec(2 * LANES), const_spec((1, 2 * LANES)), const_spec((1, ML_WIDTH))],
        out_specs=seq_spec(ML_WIDTH),
        out_shape=jax.ShapeDtypeStruct((b, s, ML_WIDTH), BF16),
        scratch_shapes=[pltpu.VMEM((ML_HEADS, ML_HEAD_DIM, 2 * ML_HEAD_DIM), F32),
                        pltpu.VMEM((SUBLANES, LANES), F32)],
        compiler_params=pltpu.CompilerParams(dimension_semantics=("arbitrary", "arbitrary"),
                                             vmem_limit_bytes=VMEM_LIMIT),
        name="mlstm",
    )(q, k, v, og, g, gate_bias_row, norm_w_row)


def _rope_tables(seq_len):
    half = ATT_HEAD_DIM // 2
    inv_freq = ROPE_THETA ** (-jnp.arange(half, dtype=F32) / half)
    ang = jnp.arange(seq_len, dtype=F32)[:, None] * inv_freq[None, :]
    cos = jnp.cos(ang)
    sin = jnp.sin(ang)
    cos_t = jnp.concatenate([cos, cos, cos, cos], axis=-1)
    sin_t = jnp.concatenate([-sin, -sin, sin, sin], axis=-1)
    return cos_t, sin_t


def _swa_kernel(sink_ref, aq_ref, ak_ref, av_ref, cos_ref, sin_ref, out_ref, kprev_ref, vprev_ref):
    i = pl.program_id(1)
    Lb = WINDOW
    half = ATT_HEAD_DIM // 2
    pairs = ATT_Q_HEADS // 2
    pairs_per_kv = pairs // ATT_KV_HEADS

    @pl.when(i == 0)
    def _():
        kprev_ref[...] = jnp.zeros_like(kprev_ref)
        vprev_ref[...] = jnp.zeros_like(vprev_ref)

    cos = cos_ref[...]
    sin = sin_ref[...]

    def rope(x):
        tiles = []
        for c in range(x.shape[-1] // LANES):
            xt = x[:, c * LANES:(c + 1) * LANES]
            tiles.append(xt * cos + pltpu.roll(xt, LANES // 2, 1) * sin)
        return jnp.concatenate(tiles, axis=-1)

    q = rope(aq_ref[...]) * (ATT_HEAD_DIM ** -0.5)
    k_cur = rope(ak_ref[...]).astype(BF16)
    v_cur = av_ref[...]

    ql = _iota((Lb, 2 * Lb), 0)
    kj = _iota((Lb, 2 * Lb), 1)
    diff = Lb + ql - kj
    kpos = i * Lb + kj - Lb
    visible = jnp.where(diff >= 0, jnp.where(diff < WINDOW, jnp.where(kpos >= 0, 1, 0), 0), 0)
    bias = jnp.where(visible > 0, 0.0, NEG_INF).astype(F32)
    bias = jnp.concatenate([bias] * (2 * pairs_per_kv), axis=0)

    lane = _iota((Lb, LANES), 1)
    low_half = lane < ATT_HEAD_DIM
    first_head = (lane & half) == 0
    ones_blk = jnp.ones((2 * Lb, LANES), BF16)

    for g in range(ATT_KV_HEADS):
        kk = jnp.concatenate([kprev_ref[:, g * LANES:(g + 1) * LANES], k_cur[:, g * LANES:(g + 1) * LANES]], axis=0)
        vv = jnp.concatenate([vprev_ref[:, g * LANES:(g + 1) * LANES], v_cur[:, g * LANES:(g + 1) * LANES]], axis=0)
        vv_aug = jnp.concatenate([vv, ones_blk], axis=-1)
        rows = []
        sinks = []
        for p in range(pairs_per_kv):
            pair = g * pairs_per_kv + p
            q2 = q[:, pair * LANES:(pair + 1) * LANES]
            rows.append(jnp.where(first_head, q2, 0.0))
            rows.append(jnp.where(first_head, 0.0, q2))
            sinks.append(jnp.full((Lb, LANES), sink_ref[2 * pair], F32))
            sinks.append(jnp.full((Lb, LANES), sink_ref[2 * pair + 1], F32))
        qs = jnp.concatenate(rows, axis=0).astype(BF16)
        sink = jnp.concatenate(sinks, axis=0)

        sc = lax.dot_general(qs, kk, (((1,), (1,)), ((), ())), preferred_element_type=F32) + bias
        m = jnp.maximum(jnp.broadcast_to(jnp.max(sc, axis=-1, keepdims=True), sink.shape), sink)
        p_un = jnp.exp(sc - jnp.concatenate([m, m], axis=-1))
        acc = _dot(p_un.astype(BF16), vv_aug)
        o = acc[:, 0:LANES] / (acc[:, LANES:] + jnp.exp(sink - m))
        for p in range(pairs_per_kv):
            pair = g * pairs_per_kv + p
            even = o[(2 * p) * Lb:(2 * p + 1) * Lb, :]
            odd = o[(2 * p + 1) * Lb:(2 * p + 2) * Lb, :]
            out_ref[:, pair * LANES:(pair + 1) * LANES] = jnp.where(low_half, even, odd).astype(out_ref.dtype)

    kprev_ref[...] = k_cur
    vprev_ref[...] = v_cur


def _swa(aq, ak, av, cos_t, sin_t, sinks):
    b, s, _ = aq.shape
    kvw = ATT_KV_HEADS * LANES

    def seq_spec(width):
        return pl.BlockSpec((None, WINDOW, width), lambda bi, si: (bi, si, 0))

    tab_spec = pl.BlockSpec((WINDOW, LANES), lambda bi, si: (si, 0))
    return pl.pallas_call(
        _swa_kernel,
        grid=(b, s // WINDOW),
        in_specs=[pl.BlockSpec(memory_space=pltpu.SMEM),
                  seq_spec(ATT_WIDTH), seq_spec(kvw), seq_spec(kvw), tab_spec, tab_spec],
        out_specs=seq_spec(ATT_WIDTH),
        out_shape=jax.ShapeDtypeStruct((b, s, ATT_WIDTH), BF16),
        scratch_shapes=[pltpu.VMEM((WINDOW, kvw), BF16), pltpu.VMEM((WINDOW, kvw), BF16)],
        compiler_params=pltpu.CompilerParams(dimension_semantics=("arbitrary", "arbitrary"),
                                             vmem_limit_bytes=VMEM_LIMIT),
        name="swa",
    )(sinks, aq, ak, av, cos_t, sin_t)


def _store_token_tiles(ref, val, row0=0, rows_per_token=None):
    n, w = val.shape
    segs = w // LANES
    rpt = rows_per_token or segs
    for j in range(segs):
        ref[pl.ds(row0 + j, n, stride=rpt), :] = val[:, j * LANES:(j + 1) * LANES]


def _load_token_tiles(ref, n, segs, row0=0, rows_per_token=None):
    rpt = rows_per_token or segs
    return jnp.concatenate([ref[pl.ds(row0 + j, n, stride=rpt), :] for j in range(segs)], axis=-1)


ROUTER_ROWS = 48


def _pack_router(w_group, b_group, w_expert, b_expert):
    d = w_group.shape[0]
    wt = jnp.zeros((ROUTER_ROWS, d), F32)
    wt = wt.at[0:N_GROUPS].set(w_group.T).at[SUBLANES:SUBLANES + N_EXPERTS].set(w_expert.T)
    bias = jnp.zeros((ROUTER_ROWS,), F32)
    bias = bias.at[0:N_GROUPS].set(b_group).at[SUBLANES:SUBLANES + N_EXPERTS].set(b_expert)
    hi = wt.astype(BF16)
    lo = (wt - hi.astype(F32)).astype(BF16)
    return jnp.concatenate([hi, lo], axis=0), jnp.broadcast_to(bias[:, None], (ROUTER_ROWS, LANES))


def _outproj_kernel(x_ref, ml_ref, att_ref, wo_ref, lnw_ref, lnb_ref, wrt_ref, brt_ref,
                    x1t_ref, meta_ref, cnt_ref, *, alpha):
    step = pl.program_id(0)
    tm, d = x_ref.shape
    y = _dot(ml_ref[...], wo_ref[0:ML_WIDTH, :]) + _dot(att_ref[...], wo_ref[ML_WIDTH:, :])
    x1 = _layer_norm(alpha * x_ref[...] + y, lnw_ref[...], lnb_ref[...])
    _store_token_tiles(x1t_ref, x1)

    rr = wrt_ref.shape[0] // 2
    x1_hi = x1.astype(BF16)
    x1_lo = (x1 - x1_hi.astype(F32)).astype(BF16)
    nt = (((1,), (1,)), ((), ()))
    both = lax.dot_general(wrt_ref[...], x1_hi, nt, preferred_element_type=F32)
    cross = lax.dot_general(wrt_ref[0:rr, :], x1_lo, nt, preferred_element_type=F32)
    logits = both[0:rr] + both[rr:2 * rr] + cross + jnp.concatenate([brt_ref[...]] * (tm // LANES), axis=1)

    row = _iota((SUBLANES, tm), 0).astype(F32)

    def first_argmax(vals):
        top = jnp.max(vals, axis=0, keepdims=True)
        idx = jnp.min(jnp.where(vals == top, row, float(SUBLANES)), axis=0, keepdims=True)
        return top, idx

    g_logits = jnp.where(row < N_GROUPS, logits[0:SUBLANES], NEG_INF)
    g_top, g_idx = first_argmax(g_logits)
    g_p = 1.0 / jnp.sum(jnp.exp(g_logits - g_top), axis=0, keepdims=True)

    e_logits = logits[SUBLANES:2 * SUBLANES]
    for grp in range(1, N_GROUPS):
        e_logits = jnp.where(g_idx == grp, logits[(1 + grp) * SUBLANES:(2 + grp) * SUBLANES], e_logits)
    v1, a1 = first_argmax(e_logits)
    v2, a2 = first_argmax(jnp.where(row == a1, NEG_INF, e_logits))
    r = jnp.exp(v2 - v1)
    w1 = g_p / (1.0 + r)
    w2 = g_p * r / (1.0 + r)

    lo = jnp.minimum(a1, a2)
    hi = jnp.maximum(a1, a2)
    w_lo = jnp.where(a1 < a2, w1, w2)
    w_hi = jnp.where(a1 < a2, w2, w1)
    pair_idx = (EXPERTS_PER_GROUP - 1) * lo - lo * (lo - 1.0) * 0.5 + (hi - lo - 1.0)
    cls = g_idx * PAIRS_PER_GROUP + pair_idx

    meta_t = jnp.where(row == 0.0, cls, jnp.where(row == 1.0, w_lo, jnp.where(row == 2.0, w_hi, 0.0)))
    meta = jnp.concatenate([meta_t, jnp.zeros((LANES - SUBLANES, tm), F32)], axis=0).T
    meta_ref[...] = meta

    @pl.when(step == 0)
    def _():
        cnt_ref[...] = jnp.zeros_like(cnt_ref)

    lane = _iota((tm, LANES), 1).astype(F32)
    onehot = jnp.where(lane == meta[:, 0:1], 1.0, 0.0)
    cnt_ref[0:1, :] += jnp.sum(onehot, axis=0, keepdims=True)


def _outproj(x2d, ml2d, att2d, w_out_b, ln_w, ln_b, w_router, b_router, alpha, tm):
    t, d = x2d.shape
    kern = functools.partial(_outproj_kernel, alpha=alpha)

    def const_spec(shape):
        return pl.BlockSpec(shape, lambda i: (0,) * len(shape))

    return pl.pallas_call(
        kern,
        grid=(t // tm,),
        in_specs=[pl.BlockSpec((tm, d), lambda i: (i, 0)),
                  pl.BlockSpec((tm, ML_WIDTH), lambda i: (i, 0)),
                  pl.BlockSpec((tm, ATT_WIDTH), lambda i: (i, 0)),
                  const_spec((ML_WIDTH + ATT_WIDTH, d)), const_spec((1, d)), const_spec((1, d)),
                  const_spec((2 * ROUTER_ROWS, d)), const_spec((ROUTER_ROWS, LANES))],
        out_specs=[pl.BlockSpec((tm * (d // LANES), LANES), lambda i: (i, 0)),
                   pl.BlockSpec((tm, LANES), lambda i: (i, 0)), const_spec((SUBLANES, LANES))],
        out_shape=[jax.ShapeDtypeStruct((t * (d // LANES), LANES), F32),
                   jax.ShapeDtypeStruct((t, LANES), F32),
                   jax.ShapeDtypeStruct((SUBLANES, LANES), F32)],
        compiler_params=pltpu.CompilerParams(dimension_semantics=("arbitrary",),
                                             vmem_limit_bytes=VMEM_LIMIT),
        name="outproj",
    )(x2d, ml2d, att2d, w_out_b, ln_w, ln_b, w_router, b_router)


def _class_expert_table():
    tab = np.zeros((SUBLANES, LANES), np.float32)
    for g in range(N_GROUPS):
        idx = 0
        for lo in range(EXPERTS_PER_GROUP):
            for hi in range(lo + 1, EXPERTS_PER_GROUP):
                c = g * PAIRS_PER_GROUP + idx
                tab[0, c] = g * EXPERTS_PER_GROUP + lo
                tab[1, c] = g * EXPERTS_PER_GROUP + hi
                idx += 1
    return tab


def _rank_kernel(meta_ref, cnt_ref, tab_ref, pos_ref, tile_ref, base_ref, run_ref, *, tb, n_tiles_pad):
    step = pl.program_id(0)
    lane8 = _iota((SUBLANES, LANES), 1)

    @pl.when(step == 0)
    def _():
        cnt = jnp.broadcast_to(cnt_ref[0:1, :], (SUBLANES, LANES))
        tiles = jnp.floor((cnt + (MOE_TILE - 1.0)) * (1.0 / MOE_TILE))
        cum = tiles
        sh = 1
        while sh < LANES:
            cum = cum + jnp.where(lane8 >= sh, pltpu.roll(cum, sh, 1), 0.0)
            sh *= 2
        excl = cum - tiles
        base_ref[...] = excl * MOE_TILE
        run_ref[...] = jnp.zeros_like(run_ref)

        ti = _iota((n_tiles_pad, LANES), 0).astype(F32)
        lane = _iota((n_tiles_pad, LANES), 1)
        done = jnp.where(lane < N_CLASSES, jnp.where(cum[0:1, :] <= ti, 1.0, 0.0), 0.0)
        t_cls = jnp.sum(done, axis=-1, keepdims=True)
        sel = jnp.where(lane.astype(F32) == t_cls, 1.0, 0.0)
        cnt_i = jnp.sum(sel * cnt[0:1, :], axis=-1, keepdims=True)
        first_i = jnp.sum(sel * excl[0:1, :], axis=-1, keepdims=True)
        rows_i = jnp.clip(cnt_i - MOE_TILE * (ti[:, 0:1] - first_i), 0.0, float(MOE_TILE))
        e_lo = jnp.sum(sel * tab_ref[0:1, :], axis=-1, keepdims=True)
        e_hi = jnp.sum(sel * tab_ref[1:2, :], axis=-1, keepdims=True)
        n_tiles = jnp.sum(jnp.where(lane < N_CLASSES, jnp.broadcast_to(tiles[0:1, :], (n_tiles_pad, LANES)), 0.0),
                          axis=-1, keepdims=True)
        info = jnp.where(lane == 0, rows_i,
                         jnp.where(lane == 1, e_lo, jnp.where(lane == 2, e_hi, jnp.where(lane == 3, n_tiles, 0.0))))
        tile_ref[...] = info.astype(I32)

    cls = meta_ref[:, 0:1]
    lane = _iota((tb, LANES), 1).astype(F32)
    onehot = jnp.where(lane == cls, 1.0, 0.0)
    strict_lower = jnp.where(_iota((tb, tb), 1) < _iota((tb, tb), 0), 1.0, 0.0).astype(BF16)
    before = _dot(strict_lower, onehot.astype(BF16))
    slot = jnp.sum(onehot * (before + run_ref[0:1, :] + base_ref[0:1, :]), axis=-1, keepdims=True)
    run_ref[...] = run_ref[...] + jnp.sum(onehot, axis=0, keepdims=True)
    slot_t = jnp.broadcast_to(slot, (tb, LANES)).T
    pos_ref[...] = slot_t[0:SUBLANES, :].astype(I32)


def _rank(meta, counts, tb, n_tiles_pad):
    t = meta.shape[0]
    kern = functools.partial(_rank_kernel, tb=tb, n_tiles_pad=n_tiles_pad)
    tab = jnp.asarray(_class_expert_table())
    return pl.pallas_call(
        kern,
        grid=(t // tb,),
        in_specs=[pl.BlockSpec((tb, LANES), lambda i: (i, 0)),
                  pl.BlockSpec((SUBLANES, LANES), lambda i: (0, 0)),
                  pl.BlockSpec((SUBLANES, LANES), lambda i: (0, 0))],
        out_specs=[pl.BlockSpec((SUBLANES, tb), lambda i: (0, i)),
                   pl.BlockSpec((n_tiles_pad, LANES), lambda i: (0, 0))],
        out_shape=[jax.ShapeDtypeStruct((SUBLANES, t), I32),
                   jax.ShapeDtypeStruct((n_tiles_pad, LANES), I32)],
        scratch_shapes=[pltpu.VMEM((SUBLANES, LANES), F32), pltpu.VMEM((SUBLANES, LANES), F32)],
        compiler_params=pltpu.CompilerParams(dimension_semantics=("arbitrary",),
                                             vmem_limit_bytes=VMEM_LIMIT),
        name="rank",
    )(meta, counts, tab)


def _dispatch_kernel(pos_ref, src_ref, zeros_ref, dst_ref, sem, *, tb, rpt):
    del zeros_ref
    base = pl.program_id(0) * tb

    def issue(j, carry):
        src = src_ref.at[pl.ds(pl.multiple_of(j * rpt, rpt), rpt)]
        dst = dst_ref.at[pl.ds(pl.multiple_of(pos_ref[base + j] * rpt, rpt), rpt)]
        pltpu.make_async_copy(src, dst, sem).start()
        return carry

    lax.fori_loop(0, tb, issue, 0, unroll=8)
    pltpu.make_async_copy(src_ref, dst_ref.at[pl.ds(0, tb * rpt)], sem).wait()


def _dispatch(pos, src, n_slots, rpt, tb):
    t = pos.shape[0]
    kern = functools.partial(_dispatch_kernel, tb=tb, rpt=rpt)
    return pl.pallas_call(
        kern,
        grid_spec=pltpu.PrefetchScalarGridSpec(
            num_scalar_prefetch=1,
            grid=(t // tb,),
            in_specs=[pl.BlockSpec((tb * rpt, LANES), lambda i, pos_ref: (i, 0)),
                      pl.BlockSpec(memory_space=pl.ANY)],
            out_specs=pl.BlockSpec(memory_space=pl.ANY),
            scratch_shapes=[pltpu.SemaphoreType.DMA(())],
        ),
        out_shape=jax.ShapeDtypeStruct((n_slots * rpt, LANES), src.dtype),
        input_output_aliases={2: 0},
        compiler_params=pltpu.CompilerParams(dimension_semantics=("arbitrary",),
                                             has_side_effects=True, vmem_limit_bytes=VMEM_LIMIT),
        name="dispatch",
    )(pos, src, jnp.zeros((n_slots * rpt, LANES), src.dtype))


def _collect_kernel(pos_ref, ys_ref, x1t_ref, meta_ref, lnw_ref, lnb_ref, out_ref, buf_ref, sems,
                    *, tb, alpha, d):
    segs = d // LANES
    rpt = 2 * segs
    step = pl.program_id(0)
    n_steps = pl.num_programs(0)

    def gather(blk, slot):
        base = blk * tb

        def issue(j, carry):
            src = ys_ref.at[pl.ds(pl.multiple_of(pos_ref[base + j] * rpt, rpt), rpt)]
            dst = buf_ref.at[slot, pl.ds(pl.multiple_of(j * rpt, rpt), rpt)]
            pltpu.make_async_copy(src, dst, sems.at[slot]).start()
            return carry

        lax.fori_loop(0, tb, issue, 0, unroll=8)

    @pl.when(step == 0)
    def _():
        gather(0, 0)

    @pl.when(step + 1 < n_steps)
    def _():
        gather(step + 1, (step + 1) % 2)

    slot = step % 2
    pltpu.make_async_copy(ys_ref.at[pl.ds(0, tb * rpt)], buf_ref.at[slot], sems.at[slot]).wait()

    cur = buf_ref.at[slot]
    x1 = _load_token_tiles(x1t_ref, tb, segs)
    y_lo = _load_token_tiles(cur, tb, segs, row0=0, rows_per_token=rpt)
    y_hi = _load_token_tiles(cur, tb, segs, row0=segs, rows_per_token=rpt)
    meta = meta_ref[...]
    z = alpha * x1 + meta[:, 1:2] * y_lo + meta[:, 2:3] * y_hi
    out_ref[...] = _layer_norm(z, lnw_ref[...], lnb_ref[...])


def _collect(pos, ys, x1t, meta, ln_w, ln_b, alpha, d, tb):
    t = pos.shape[0]
    segs = d // LANES
    kern = functools.partial(_collect_kernel, tb=tb, alpha=alpha, d=d)
    return pl.pallas_call(
        kern,
        grid_spec=pltpu.PrefetchScalarGridSpec(
            num_scalar_prefetch=1,
            grid=(t // tb,),
            in_specs=[pl.BlockSpec(memory_space=pl.ANY),
                      pl.BlockSpec((tb * segs, LANES), lambda i, pos_ref: (i, 0)),
                      pl.BlockSpec((tb, LANES), lambda i, pos_ref: (i, 0)),
                      pl.BlockSpec((1, d), lambda i, pos_ref: (0, 0)),
                      pl.BlockSpec((1, d), lambda i, pos_ref: (0, 0))],
            out_specs=pl.BlockSpec((tb, d), lambda i, pos_ref: (i, 0)),
            scratch_shapes=[pltpu.VMEM((2, tb * 2 * segs, LANES), F32), pltpu.SemaphoreType.DMA((2,))],
        ),
        out_shape=jax.ShapeDtypeStruct((t, d), F32),
        compiler_params=pltpu.CompilerParams(dimension_semantics=("arbitrary",),
                                             vmem_limit_bytes=VMEM_LIMIT),
        name="collect",
    )(pos, ys, x1t, meta, ln_w, ln_b)


def _moe_kernel(rows_ref, elo_ref, ehi_ref, nt_ref, xs_ref, wg_lo_ref, wu_lo_ref, wd_lo_ref,
                wg_hi_ref, wu_hi_ref, wd_hi_ref, ys_ref, *, d):
    i = pl.program_id(0)
    rows = rows_ref[i]
    segs = d // LANES

    @pl.when(rows > 0)
    def _():
        xb = _load_token_tiles(xs_ref, MOE_TILE, segs).astype(BF16)

        def expert(wg_ref, wu_ref, wd_ref):
            gate = _dot(xb, wg_ref[...])
            up = _dot(xb, wu_ref[...])
            hidden = gate * _sigmoid(gate) * up
            return _dot(hidden.astype(BF16), wd_ref[...])

        _store_token_tiles(ys_ref, expert(wg_lo_ref, wu_lo_ref, wd_lo_ref), row0=0, rows_per_token=2 * segs)
        _store_token_tiles(ys_ref, expert(wg_hi_ref, wu_hi_ref, wd_hi_ref), row0=segs, rows_per_token=2 * segs)

    @pl.when(rows <= 0)
    def _():
        ys_ref[...] = jnp.zeros_like(ys_ref)


def _moe(info, xs, wg, wu, wd, d, n_tiles_pad):
    de = wg.shape[-1]
    segs = d // LANES
    kern = functools.partial(_moe_kernel, d=d)

    def last_live(i, nt_ref):
        return jnp.minimum(i, jnp.maximum(nt_ref[0] - 1, 0))

    def up_spec(which):
        return pl.BlockSpec((None, d, de),
                            lambda i, rows, elo, ehi, nt: ((elo, ehi)[which][last_live(i, nt)], 0, 0))

    def down_spec(which):
        return pl.BlockSpec((None, de, d),
                            lambda i, rows, elo, ehi, nt: ((elo, ehi)[which][last_live(i, nt)], 0, 0))

    rows, elo, ehi, nt = info[:, 0], info[:, 1], info[:, 2], info[0:1, 3]
    return pl.pallas_call(
        kern,
        grid_spec=pltpu.PrefetchScalarGridSpec(
            num_scalar_prefetch=4,
            grid=(n_tiles_pad,),
            in_specs=[pl.BlockSpec((MOE_TILE * segs, LANES),
                                   lambda i, rows, elo, ehi, nt: (last_live(i, nt), 0)),
                      up_spec(0), up_spec(0), down_spec(0), up_spec(1), up_spec(1), down_spec(1)],
            out_specs=pl.BlockSpec((MOE_TILE * 2 * segs, LANES), lambda i, rows, elo, ehi, nt: (i, 0)),
        ),
        out_shape=jax.ShapeDtypeStruct((n_tiles_pad * MOE_TILE * 2 * segs, LANES), F32),
        compiler_params=pltpu.CompilerParams(dimension_semantics=("arbitrary",),
                                             vmem_limit_bytes=VMEM_LIMIT),
        name="moe",
    )(rows, elo, ehi, nt, xs, wg, wu, wd, wg, wu, wd)


def _pick_block(n, target):
    blk = min(n, target)
    while n % blk:
        blk //= 2
    return blk


def kernel(x, w_in, conv_w, conv_b, mlstm_gate_bias, mlstm_norm_w, attn_sinks, w_out, ln1_w, ln1_b,
           w_group_router, b_group_router, w_expert_router, b_expert_router,
           w_exp_gate, w_exp_up, w_exp_down, ln2_w, ln2_b):
    b, s, d = x.shape
    t = b * s
    depth = w_in.shape[0]
    alpha = (2.0 * depth) ** 0.25
    assert s % ML_CHUNK == 0 and s % WINDOW == 0 and d % LANES == 0

    tm = _pick_block(t, 512)
    tq = _pick_block(s, 4 * ML_CHUNK)
    tb_rank = _pick_block(t, 512)
    tb_dma = _pick_block(t, 2048)
    tb_col = _pick_block(t, 256)
    n_tiles_pad = -(-(t // MOE_TILE + N_CLASSES) // SUBLANES) * SUBLANES
    cos_t, sin_t = _rope_tables(s)

    for l in range(depth):
        x2d = x.reshape(t, d)
        q, k, v, og, aq, ak, av, g = _inproj(x2d, _pack_w_in(w_in[l]), conv_w[l], conv_b[l][None, :], tm, s // tm)
        bias_pad = jnp.zeros((LANES - ML_HEADS,), F32)
        gate_bias_row = jnp.concatenate(
            [mlstm_gate_bias[l, 0], bias_pad, mlstm_gate_bias[l, 1], bias_pad])[None, :]
        ml = _mlstm(q.reshape(b, s, -1), k.reshape(b, s, -1), v.reshape(b, s, -1), og.reshape(b, s, -1),
                    g.reshape(b, s, -1), gate_bias_row, mlstm_norm_w[l][None, :], tq)
        att = _swa(aq.reshape(b, s, -1), ak.reshape(b, s, -1), av.reshape(b, s, -1), cos_t, sin_t, attn_sinks[l])

        w_router, b_router = _pack_router(w_group_router[l], b_group_router[l],
                                          w_expert_router[l], b_expert_router[l])
        x1t, meta, counts = _outproj(x2d, ml.reshape(t, -1), att.reshape(t, -1), w_out[l].astype(BF16),
                                     ln1_w[l][None, :], ln1_b[l][None, :], w_router, b_router, alpha, tm)

        pos2d, info = _rank(meta, counts, tb_rank, n_tiles_pad)
        pos = pos2d[0]
        xs = _dispatch(pos, x1t, n_tiles_pad * MOE_TILE, d // LANES, tb_dma)
        ys = _moe(info, xs, w_exp_gate[l].astype(BF16), w_exp_up[l].astype(BF16), w_exp_down[l].astype(BF16),
                  d, n_tiles_pad)
        out = _collect(pos, ys, x1t, meta, ln2_w[l][None, :], ln2_b[l][None, :], alpha, d, tb_col)
        x = out.reshape(b, s, d)
    return x
```

```python
import functools
import math

import numpy as np
import jax
import jax.numpy as jnp
from jax import lax
from jax.experimental import pallas as pl
from jax.experimental.pallas import tpu as pltpu

F32 = jnp.float32
BF16 = jnp.bfloat16
I32 = jnp.int32

ML_HEADS = 4
ML_HEAD_DIM = 128
ML_WIDTH = ML_HEADS * ML_HEAD_DIM
ML_CHUNK = 128
CONV_WIDTH = 4
ATT_Q_HEADS = 8
ATT_KV_HEADS = 2
ATT_HEAD_DIM = 64
ATT_WIDTH = ATT_Q_HEADS * ATT_HEAD_DIM
ATT_KV_WIDTH = ATT_KV_HEADS * ATT_HEAD_DIM
WINDOW = 128
ROPE_THETA = 10000.0
N_GROUPS = 4
EXPERTS_PER_GROUP = 8
N_EXPERTS = N_GROUPS * EXPERTS_PER_GROUP
PAIRS_PER_GROUP = EXPERTS_PER_GROUP * (EXPERTS_PER_GROUP - 1) // 2
N_CLASSES = N_GROUPS * PAIRS_PER_GROUP
LN_EPS = 1e-5

LANES = 128
SUBLANES = 8
MOE_TILE = 128
VMEM_LIMIT = 56 * 1024 * 1024

NEG_INF = float("-inf")


def _sigmoid(x):
    return 1.0 / (1.0 + jnp.exp(-x))


def _log_sigmoid(x):
    return jnp.minimum(x, 0.0) - jnp.log(1.0 + jnp.exp(-jnp.abs(x)))


def _iota(shape, dim):
    return lax.broadcasted_iota(I32, shape, dim)


def _dot(a, b):
    return jnp.dot(a, b, preferred_element_type=F32)


def _dot_exact(a, b):
    return jnp.dot(a, b, preferred_element_type=F32, precision=lax.Precision.HIGHEST)


def _layer_norm(z, w, b):
    mu = jnp.mean(z, axis=-1, keepdims=True)
    zc = z - mu
    var = jnp.mean(zc * zc, axis=-1, keepdims=True)
    return zc * lax.rsqrt(var + LN_EPS) * w + b


C_QK = 0
C_V = C_QK + 2 * ML_WIDTH
C_O = C_V + ML_WIDTH
C_AQ = C_O + ML_WIDTH
C_AK = C_AQ + ATT_WIDTH
C_AV = C_AK + ATT_KV_HEADS * LANES
C_G = C_AV + ATT_KV_HEADS * LANES
C_END = C_G + 2 * LANES


def _pack_w_in(w_in):
    sizes = (2 * ML_WIDTH, ML_WIDTH, ML_WIDTH, ML_HEADS, ML_HEADS, ATT_WIDTH, ATT_KV_WIDTH, ATT_KV_WIDTH)
    splits = np.cumsum(sizes)[:-1].tolist()
    w_qk, w_v, w_o, w_i, w_f, w_aq, w_ak, w_av = jnp.split(w_in, splits, axis=-1)
    half = ATT_HEAD_DIM // 2

    def head(w, h):
        return w[:, h * ATT_HEAD_DIM:(h + 1) * ATT_HEAD_DIM]

    def q_tile(a, b):
        return [a[:, :half], b[:, :half], a[:, half:], b[:, half:]]

    q_cols = [t for p in range(ATT_Q_HEADS // 2) for t in q_tile(head(w_aq, 2 * p), head(w_aq, 2 * p + 1))]
    k_cols = [t for h in range(ATT_KV_HEADS) for t in q_tile(head(w_ak, h), head(w_ak, h))]
    v_cols = [t for h in range(ATT_KV_HEADS) for t in (head(w_av, h), head(w_av, h))]
    lane_pad = jnp.zeros((w_in.shape[0], LANES - ML_HEADS), w_in.dtype)
    packed = jnp.concatenate([w_qk, w_v, w_o] + q_cols + k_cols + v_cols + [w_i, lane_pad, w_f, lane_pad], axis=-1)
    return packed.astype(BF16)


def _inproj_kernel(x_ref, w_ref, cw_ref, cb_ref, q_ref, k_ref, v_ref, og_ref, aq_ref, ak_ref, av_ref, g_ref,
                   *scratch, blocks_per_seq):
    *ext_refs, xb_ref = scratch
    tm = x_ref.shape[0]
    halo = SUBLANES
    cs = ext_refs[0].shape[1]
    xb_ref[...] = x_ref[...].astype(BF16)

    def mm(lo, hi):
        return _dot(xb_ref[...], w_ref[:, lo:hi])

    @pl.when(pl.program_id(0) % blocks_per_seq == 0)
    def _():
        for ext_ref in ext_refs:
            ext_ref[0:halo, :] = jnp.zeros((halo, cs), F32)

    scale = ML_HEAD_DIM ** -0.5

    def conv_slice(idx):
        ext_ref = ext_refs[idx]
        c0 = idx * cs
        dst_ref, off, mul = (q_ref, c0, 1.0) if c0 < ML_WIDTH else (k_ref, c0 - ML_WIDTH, scale)
        rt = ML_CHUNK
        for r0 in range(0, tm, rt):
            conv = cb_ref[:, c0:c0 + cs]
            for j in range(CONV_WIDTH):
                start = halo + r0 - (CONV_WIDTH - 1) + j
                conv = conv + cw_ref[j:j + 1, c0:c0 + cs] * ext_ref[start:start + rt, :]
            act = conv * _sigmoid(conv)
            dst_ref[r0:r0 + rt, off:off + cs] = (act * mul).astype(BF16)
        ext_ref[0:halo, :] = ext_ref[tm:tm + halo, :]

    for idx in range(len(ext_refs)):
        ext_refs[idx][halo:halo + tm, :] = mm(C_QK + idx * cs, C_QK + (idx + 1) * cs)
        if idx > 0:
            conv_slice(idx - 1)
    half_v = ML_WIDTH // 2
    v_ref[:, 0:half_v] = mm(C_V, C_V + half_v).astype(BF16)
    conv_slice(len(ext_refs) - 1)
    v_ref[:, half_v:] = mm(C_V + half_v, C_O).astype(BF16)
    og_ref[...] = _sigmoid(mm(C_O, C_AQ)).astype(BF16)
    aq_ref[...] = mm(C_AQ, C_AK)
    ak_ref[...] = mm(C_AK, C_AV)
    av_ref[...] = mm(C_AV, C_G).astype(BF16)
    g_ref[...] = mm(C_G, C_END)


def _inproj(x2d, w_packed, conv_w, conv_b, tm, blocks_per_seq):
    t, d = x2d.shape
    widths = (ML_WIDTH, ML_WIDTH, ML_WIDTH, ML_WIDTH, C_AK - C_AQ, C_AV - C_AK, C_G - C_AV, C_END - C_G)
    dtypes = (BF16, BF16, BF16, BF16, F32, F32, BF16, F32)
    kern = functools.partial(_inproj_kernel, blocks_per_seq=blocks_per_seq)
    return pl.pallas_call(
        kern,
        grid=(t // tm,),
        in_specs=[pl.BlockSpec((tm, d), lambda i: (i, 0)),
                  pl.BlockSpec((d, C_END), lambda i: (0, 0)),
                  pl.BlockSpec((CONV_WIDTH, 2 * ML_WIDTH), lambda i: (0, 0)),
                  pl.BlockSpec((1, 2 * ML_WIDTH), lambda i: (0, 0))],
        out_specs=[pl.BlockSpec((tm, w), lambda i: (i, 0)) for w in widths],
        out_shape=[jax.ShapeDtypeStruct((t, w), dt) for w, dt in zip(widths, dtypes)],
        scratch_shapes=[pltpu.VMEM((tm + SUBLANES, 2 * LANES), F32)] * (2 * ML_WIDTH // (2 * LANES))
        + [pltpu.VMEM((tm, d), BF16)],
        compiler_params=pltpu.CompilerParams(dimension_semantics=("arbitrary",),
                                             vmem_limit_bytes=VMEM_LIMIT),
        name="inproj",
    )(x2d, w_packed, conv_w, conv_b)


def _time_scan(x, combine, identity):
    row = _iota(x.shape, 0)
    sh = 1
    while sh < x.shape[0]:
        x = combine(x, jnp.where(row >= sh, pltpu.roll(x, sh, 0), identity))
        sh *= 2
    return x


def _mlstm_kernel(q_ref, k_ref, v_ref, og_ref, g_ref, gb_ref, nw_ref, out_ref, ct_ref, m_ref, *, tq):
    s_idx = pl.program_id(1)
    L = ML_CHUNK
    D = ML_HEAD_DIM
    H = ML_HEADS
    heads = range(H)

    @pl.when(s_idx == 0)
    def _():
        ct_ref[...] = jnp.zeros_like(ct_ref)
        m_ref[...] = jnp.zeros_like(m_ref)

    causal = _iota((L, L), 1) <= _iota((L, L), 0)
    ones_blk = jnp.ones((L, D), BF16)
    mean_blk = jnp.full((D, D), 1.0 / D, BF16)
    m_prev = m_ref[0:1, :]
    head_lanes = _iota((L, LANES), 1) < H
    tile_of_lane = jnp.right_shift(_iota((LANES, H * L), 1), L.bit_length() - 1)
    spread = jnp.where(_iota((LANES, H * L), 0) == tile_of_lane, 1.0, 0.0).astype(BF16)

    def spread_heads(x):
        x = jnp.where(head_lanes, x, 0.0)
        hi = x.astype(BF16)
        lo = (x - hi.astype(F32)).astype(BF16)
        return _dot(hi, spread) + _dot(lo, spread)

    for c in range(tq // L):
        r0 = c * L
        gi = g_ref[r0:r0 + L, 0:LANES] + gb_ref[:, 0:LANES]
        gf = g_ref[r0:r0 + L, LANES:2 * LANES] + gb_ref[:, LANES:2 * LANES]
        b_cum = _time_scan(_log_sigmoid(gf), jnp.add, 0.0)
        r = gi - b_cum
        g = jnp.maximum(m_prev, _time_scan(r, jnp.maximum, NEG_INF))
        g_rep = spread_heads(g)
        b_rep = spread_heads(b_cum)
        b_last = b_cum[L - 1:L, :]
        m_new = jnp.maximum(b_last + m_prev, jnp.max(b_last + r, axis=0, keepdims=True))
        decay = jnp.exp(b_last + m_prev - m_new)
        shift = b_last - m_new
        r_t = r.T

        q_b = [q_ref[r0:r0 + L, h * D:(h + 1) * D] for h in heads]
        kt_b = [k_ref[r0:r0 + L, h * D:(h + 1) * D].T for h in heads]
        v_aug = [jnp.concatenate([v_ref[r0:r0 + L, h * D:(h + 1) * D], ones_blk], axis=-1) for h in heads]
        g_col = [g_rep[:, h * L:(h + 1) * L] for h in heads]
        w_intra = [jnp.exp(jnp.where(causal, r_t[h:h + 1, :] - g_col[h], NEG_INF)) for h in heads]
        s_b = [(_dot(q_b[h], kt_b[h]) * w_intra[h]).astype(BF16) for h in heads]
        ct = [ct_ref[h] for h in heads]
        inter = [_dot(q_b[h], ct[h].astype(BF16)) for h in heads]
        intra = [_dot(s_b[h], v_aug[h]) for h in heads]
        for h in heads:
            wi_col = jnp.exp(m_prev[:, h:h + 1] - g_col[h])
            clamp = jnp.exp(-(b_rep[:, h * L:(h + 1) * L] + g_col[h]))
            num = wi_col * inter[h][:, 0:D] + intra[h][:, 0:D]
            den = wi_col * inter[h][:, D:] + intra[h][:, D:]
            hh = num / jnp.maximum(jnp.abs(den), clamp)
            mu = _dot(hh.astype(BF16), mean_blk)
            hc = hh - mu
            var = _dot((hc * hc).astype(BF16), mean_blk)
            hn = hc * lax.rsqrt(var + LN_EPS) * nw_ref[:, h * D:(h + 1) * D]
            gate_o = og_ref[r0:r0 + L, h * D:(h + 1) * D].astype(F32)
            out_ref[r0:r0 + L, h * D:(h + 1) * D] = (gate_o * hn).astype(out_ref.dtype)
        for h in heads:
            w_row = jnp.exp(r_t[h:h + 1, :] + shift[:, h:h + 1])
            ktw = (kt_b[h].astype(F32) * w_row).astype(BF16)
            ct_ref[h] = decay[:, h:h + 1] * ct[h] + _dot(ktw, v_aug[h])
        m_prev = m_new

    m_ref[...] = jnp.broadcast_to(m_prev, m_ref.shape)


def _mlstm(q, k, v, og, g, gate_bias_row, norm_w_row, tq):
    b, s, _ = q.shape
    kern = functools.partial(_mlstm_kernel, tq=tq)

    def seq_spec(width):
        return pl.BlockSpec((None, tq, width), lambda bi, si: (bi, si, 0))

    def const_spec(shape):
        return pl.BlockSpec(shape, lambda bi, si: (0,) * len(shape))

    return pl.pallas_call(
        kern,
        grid=(b, s // tq),
        in_specs=[seq_spec(ML_WIDTH), seq_spec(ML_WIDTH), seq_spec(ML_WIDTH), seq_spec(ML_WIDTH),
                  seq_spec(2 * LANES), const_spec((1, 2 * LANES)), const_spec((1, ML_WIDTH))],
        out_specs=seq_spec(ML_WIDTH),
        out_shape=jax.ShapeDtypeStruct((b, s, ML_WIDTH), BF16),
        scratch_shapes=[pltpu.VMEM((ML_HEADS, ML_HEAD_DIM, 2 * ML_HEAD_DIM), F32),
                        pltpu.VMEM((SUBLANES, LANES), F32)],
        compiler_params=pltpu.CompilerParams(dimension_semantics=("arbitrary", "arbitrary"),
                                             vmem_limit_bytes=VMEM_LIMIT),
        name="mlstm",
    )(q, k, v, og, g, gate_bias_row, norm_w_row)


def _rope_tables(seq_len):
    half = ATT_HEAD_DIM // 2
    inv_freq = ROPE_THETA ** (-jnp.arange(half, dtype=F32) / half)
    ang = jnp.arange(seq_len, dtype=F32)[:, None] * inv_freq[None, :]
    cos = jnp.cos(ang)
    sin = jnp.sin(ang)
    cos_t = jnp.concatenate([cos, cos, cos, cos], axis=-1)
    sin_t = jnp.concatenate([-sin, -sin, sin, sin], axis=-1)
    return cos_t, sin_t


def _swa_kernel(sink_ref, aq_ref, ak_ref, av_ref, cos_ref, sin_ref, out_ref, kprev_ref, vprev_ref):
    i = pl.program_id(1)
    Lb = WINDOW
    half = ATT_HEAD_DIM // 2
    pairs = ATT_Q_HEADS // 2
    pairs_per_kv = pairs // ATT_KV_HEADS

    @pl.when(i == 0)
    def _():
        kprev_ref[...] = jnp.zeros_like(kprev_ref)
        vprev_ref[...] = jnp.zeros_like(vprev_ref)

    cos = cos_ref[...]
    sin = sin_ref[...]

    def rope(x):
        tiles = []
        for c in range(x.shape[-1] // LANES):
            xt = x[:, c * LANES:(c + 1) * LANES]
            tiles.append(xt * cos + pltpu.roll(xt, LANES // 2, 1) * sin)
        return jnp.concatenate(tiles, axis=-1)

    q = rope(aq_ref[...]) * (ATT_HEAD_DIM ** -0.5)
    k_cur = rope(ak_ref[...]).astype(BF16)
    v_cur = av_ref[...]

    ql = _iota((Lb, 2 * Lb), 0)
    kj = _iota((Lb, 2 * Lb), 1)
    diff = Lb + ql - kj
    kpos = i * Lb + kj - Lb
    visible = jnp.where(diff >= 0, jnp.where(diff < WINDOW, jnp.where(kpos >= 0, 1, 0), 0), 0)
    bias = jnp.where(visible > 0, 0.0, NEG_INF).astype(F32)
    bias = jnp.concatenate([bias] * (2 * pairs_per_kv), axis=0)

    lane = _iota((Lb, LANES), 1)
    low_half = lane < ATT_HEAD_DIM
    first_head = (lane & half) == 0
    ones_blk = jnp.ones((2 * Lb, LANES), BF16)

    for g in range(ATT_KV_HEADS):
        kk = jnp.concatenate([kprev_ref[:, g * LANES:(g + 1) * LANES], k_cur[:, g * LANES:(g + 1) * LANES]], axis=0)
        vv = jnp.concatenate([vprev_ref[:, g * LANES:(g + 1) * LANES], v_cur[:, g * LANES:(g + 1) * LANES]], axis=0)
        vv_aug = jnp.concatenate([vv, ones_blk], axis=-1)
        rows = []
        sinks = []
        for p in range(pairs_per_kv):
            pair = g * pairs_per_kv + p
            q2 = q[:, pair * LANES:(pair + 1) * LANES]
            rows.append(jnp.where(first_head, q2, 0.0))
            rows.append(jnp.where(first_head, 0.0, q2))
            sinks.append(jnp.full((Lb, LANES), sink_ref[2 * pair], F32))
            sinks.append(jnp.full((Lb, LANES), sink_ref[2 * pair + 1], F32))
        qs = jnp.concatenate(rows, axis=0).astype(BF16)
        sink = jnp.concatenate(sinks, axis=0)

        sc = lax.dot_general(qs, kk, (((1,), (1,)), ((), ())), preferred_element_type=F32) + bias
        m = jnp.maximum(jnp.broadcast_to(jnp.max(sc, axis=-1, keepdims=True), sink.shape), sink)
        p_un = jnp.exp(sc - jnp.concatenate([m, m], axis=-1))
        acc = _dot(p_un.astype(BF16), vv_aug)
        o = acc[:, 0:LANES] / (acc[:, LANES:] + jnp.exp(sink - m))
        for p in range(pairs_per_kv):
            pair = g * pairs_per_kv + p
            even = o[(2 * p) * Lb:(2 * p + 1) * Lb, :]
            odd = o[(2 * p + 1) * Lb:(2 * p + 2) * Lb, :]
            out_ref[:, pair * LANES:(pair + 1) * LANES] = jnp.where(low_half, even, odd).astype(out_ref.dtype)

    kprev_ref[...] = k_cur
    vprev_ref[...] = v_cur


def _swa(aq, ak, av, cos_t, sin_t, sinks):
    b, s, _ = aq.shape
    kvw = ATT_KV_HEADS * LANES

    def seq_spec(width):
        return pl.BlockSpec((None, WINDOW, width), lambda bi, si: (bi, si, 0))

    tab_spec = pl.BlockSpec((WINDOW, LANES), lambda bi, si: (si, 0))
    return pl.pallas_call(
        _swa_kernel,
        grid=(b, s // WINDOW),
        in_specs=[pl.BlockSpec(memory_space=pltpu.SMEM),
                  seq_spec(ATT_WIDTH), seq_spec(kvw), seq_spec(kvw), tab_spec, tab_spec],
        out_specs=seq_spec(ATT_WIDTH),
        out_shape=jax.ShapeDtypeStruct((b, s, ATT_WIDTH), BF16),
        scratch_shapes=[pltpu.VMEM((WINDOW, kvw), BF16), pltpu.VMEM((WINDOW, kvw), BF16)],
        compiler_params=pltpu.CompilerParams(dimension_semantics=("arbitrary", "arbitrary"),
                                             vmem_limit_bytes=VMEM_LIMIT),
        name="swa",
    )(sinks, aq, ak, av, cos_t, sin_t)


def _store_token_tiles(ref, val, row0=0, rows_per_token=None):
    n, w = val.shape
    segs = w // LANES
    rpt = rows_per_token or segs
    for j in range(segs):
        ref[pl.ds(row0 + j, n, stride=rpt), :] = val[:, j * LANES:(j + 1) * LANES]


def _load_token_tiles(ref, n, segs, row0=0, rows_per_token=None):
    rpt = rows_per_token or segs
    return jnp.concatenate([ref[pl.ds(row0 + j, n, stride=rpt), :] for j in range(segs)], axis=-1)


ROUTER_ROWS = 48


def _pack_router(w_group, b_group, w_expert, b_expert):
    d = w_group.shape[0]
    wt = jnp.zeros((ROUTER_ROWS, d), F32)
    wt = wt.at[0:N_GROUPS].set(w_group.T).at[SUBLANES:SUBLANES + N_EXPERTS].set(w_expert.T)
    bias = jnp.zeros((ROUTER_ROWS,), F32)
    bias = bias.at[0:N_GROUPS].set(b_group).at[SUBLANES:SUBLANES + N_EXPERTS].set(b_expert)
    hi = wt.astype(BF16)
    lo = (wt - hi.astype(F32)).astype(BF16)
    return jnp.concatenate([hi, lo], axis=0), jnp.broadcast_to(bias[:, None], (ROUTER_ROWS, LANES))


def _outproj_kernel(x_ref, ml_ref, att_ref, wo_ref, lnw_ref, lnb_ref, wrt_ref, brt_ref,
                    x1t_ref, meta_ref, cnt_ref, *, alpha):
    step = pl.program_id(0)
    tm, d = x_ref.shape
    y = _dot(ml_ref[...], wo_ref[0:ML_WIDTH, :]) + _dot(att_ref[...], wo_ref[ML_WIDTH:, :])
    x1 = _layer_norm(alpha * x_ref[...] + y, lnw_ref[...], lnb_ref[...])
    _store_token_tiles(x1t_ref, x1)

    rr = wrt_ref.shape[0] // 2
    x1_hi = x1.astype(BF16)
    x1_lo = (x1 - x1_hi.astype(F32)).astype(BF16)
    nt = (((1,), (1,)), ((), ()))
    both = lax.dot_general(wrt_ref[...], x1_hi, nt, preferred_element_type=F32)
    cross = lax.dot_general(wrt_ref[0:rr, :], x1_lo, nt, preferred_element_type=F32)
    logits = both[0:rr] + both[rr:2 * rr] + cross + jnp.concatenate([brt_ref[...]] * (tm // LANES), axis=1)

    row = _iota((SUBLANES, tm), 0).astype(F32)

    def first_argmax(vals):
        top = jnp.max(vals, axis=0, keepdims=True)
        idx = jnp.min(jnp.where(vals == top, row, float(SUBLANES)), axis=0, keepdims=True)
        return top, idx

    g_logits = jnp.where(row < N_GROUPS, logits[0:SUBLANES], NEG_INF)
    g_top, g_idx = first_argmax(g_logits)
    g_p = 1.0 / jnp.sum(jnp.exp(g_logits - g_top), axis=0, keepdims=True)

    e_logits = logits[SUBLANES:2 * SUBLANES]
    for grp in range(1, N_GROUPS):
        e_logits = jnp.where(g_idx == grp, logits[(1 + grp) * SUBLANES:(2 + grp) * SUBLANES], e_logits)
    v1, a1 = first_argmax(e_logits)
    v2, a2 = first_argmax(jnp.where(row == a1, NEG_INF, e_logits))
    r = jnp.exp(v2 - v1)
    w1 = g_p / (1.0 + r)
    w2 = g_p * r / (1.0 + r)

    lo = jnp.minimum(a1, a2)
    hi = jnp.maximum(a1, a2)
    w_lo = jnp.where(a1 < a2, w1, w2)
    w_hi = jnp.where(a1 < a2, w2, w1)
    pair_idx = (EXPERTS_PER_GROUP - 1) * lo - lo * (lo - 1.0) * 0.5 + (hi - lo - 1.0)
    cls = g_idx * PAIRS_PER_GROUP + pair_idx

    meta_t = jnp.where(row == 0.0, cls, jnp.where(row == 1.0, w_lo, jnp.where(row == 2.0, w_hi, 0.0)))
    meta = jnp.concatenate([meta_t, jnp.zeros((LANES - SUBLANES, tm), F32)], axis=0).T
    meta_ref[...] = meta

    @pl.when(step == 0)
    def _():
        cnt_ref[...] = jnp.zeros_like(cnt_ref)

    lane = _iota((tm, LANES), 1).astype(F32)
    onehot = jnp.where(lane == meta[:, 0:1], 1.0, 0.0)
    cnt_ref[0:1, :] += jnp.sum(onehot, axis=0, keepdims=True)


def _outproj(x2d, ml2d, att2d, w_out_b, ln_w, ln_b, w_router, b_router, alpha, tm):
    t, d = x2d.shape
    kern = functools.partial(_outproj_kernel, alpha=alpha)

    def const_spec(shape):
        return pl.BlockSpec(shape, lambda i: (0,) * len(shape))

    return pl.pallas_call(
        kern,
        grid=(t // tm,),
        in_specs=[pl.BlockSpec((tm, d), lambda i: (i, 0)),
                  pl.BlockSpec((tm, ML_WIDTH), lambda i: (i, 0)),
                  pl.BlockSpec((tm, ATT_WIDTH), lambda i: (i, 0)),
                  const_spec((ML_WIDTH + ATT_WIDTH, d)), const_spec((1, d)), const_spec((1, d)),
                  const_spec((2 * ROUTER_ROWS, d)), const_spec((ROUTER_ROWS, LANES))],
        out_specs=[pl.BlockSpec((tm * (d // LANES), LANES), lambda i: (i, 0)),
                   pl.BlockSpec((tm, LANES), lambda i: (i, 0)), const_spec((SUBLANES, LANES))],
        out_shape=[jax.ShapeDtypeStruct((t * (d // LANES), LANES), F32),
                   jax.ShapeDtypeStruct((t, LANES), F32),
                   jax.ShapeDtypeStruct((SUBLANES, LANES), F32)],
        compiler_params=pltpu.CompilerParams(dimension_semantics=("arbitrary",),
                                             vmem_limit_bytes=VMEM_LIMIT),
        name="outproj",
    )(x2d, ml2d, att2d, w_out_b, ln_w, ln_b, w_router, b_router)


def _class_expert_table():
    tab = np.zeros((SUBLANES, LANES), np.float32)
    for g in range(N_GROUPS):
        idx = 0
        for lo in range(EXPERTS_PER_GROUP):
            for hi in range(lo + 1, EXPERTS_PER_GROUP):
                c = g * PAIRS_PER_GROUP + idx
                tab[0, c] = g * EXPERTS_PER_GROUP + lo
                tab[1, c] = g * EXPERTS_PER_GROUP + hi
                idx += 1
    return tab


def _rank_kernel(meta_ref, cnt_ref, tab_ref, pos_ref, tile_ref, base_ref, run_ref, *, tb, n_tiles_pad):
    step = pl.program_id(0)
    lane8 = _iota((SUBLANES, LANES), 1)

    @pl.when(step == 0)
    def _():
        cnt = jnp.broadcast_to(cnt_ref[0:1, :], (SUBLANES, LANES))
        tiles = jnp.floor((cnt + (MOE_TILE - 1.0)) * (1.0 / MOE_TILE))
        cum = tiles
        sh = 1
        while sh < LANES:
            cum = cum + jnp.where(lane8 >= sh, pltpu.roll(cum, sh, 1), 0.0)
            sh *= 2
        excl = cum - tiles
        base_ref[...] = excl * MOE_TILE
        run_ref[...] = jnp.zeros_like(run_ref)

        ti = _iota((n_tiles_pad, LANES), 0).astype(F32)
        lane = _iota((n_tiles_pad, LANES), 1)
        done = jnp.where(lane < N_CLASSES, jnp.where(cum[0:1, :] <= ti, 1.0, 0.0), 0.0)
        t_cls = jnp.sum(done, axis=-1, keepdims=True)
        sel = jnp.where(lane.astype(F32) == t_cls, 1.0, 0.0)
        cnt_i = jnp.sum(sel * cnt[0:1, :], axis=-1, keepdims=True)
        first_i = jnp.sum(sel * excl[0:1, :], axis=-1, keepdims=True)
        rows_i = jnp.clip(cnt_i - MOE_TILE * (ti[:, 0:1] - first_i), 0.0, float(MOE_TILE))
        e_lo = jnp.sum(sel * tab_ref[0:1, :], axis=-1, keepdims=True)
        e_hi = jnp.sum(sel * tab_ref[1:2, :], axis=-1, keepdims=True)
        n_tiles = jnp.sum(jnp.where(lane < N_CLASSES, jnp.broadcast_to(tiles[0:1, :], (n_tiles_pad, LANES)), 0.0),
                          axis=-1, keepdims=True)
        info = jnp.where(lane == 0, rows_i,
                         jnp.where(lane == 1, e_lo, jnp.where(lane == 2, e_hi, jnp.where(lane == 3, n_tiles, 0.0))))
        tile_ref[...] = info.astype(I32)

    cls = meta_ref[:, 0:1]
    lane = _iota((tb, LANES), 1).astype(F32)
    onehot = jnp.where(lane == cls, 1.0, 0.0)
    strict_lower = jnp.where(_iota((tb, tb), 1) < _iota((tb, tb), 0), 1.0, 0.0).astype(BF16)
    before = _dot(strict_lower, onehot.astype(BF16))
    slot = jnp.sum(onehot * (before + run_ref[0:1, :] + base_ref[0:1, :]), axis=-1, keepdims=True)
    run_ref[...] = run_ref[...] + jnp.sum(onehot, axis=0, keepdims=True)
    slot_t = jnp.broadcast_to(slot, (tb, LANES)).T
    pos_ref[...] = slot_t[0:SUBLANES, :].astype(I32)


def _rank(meta, counts, tb, n_tiles_pad):
    t = meta.shape[0]
    kern = functools.partial(_rank_kernel, tb=tb, n_tiles_pad=n_tiles_pad)
    tab = jnp.asarray(_class_expert_table())
    return pl.pallas_call(
        kern,
        grid=(t // tb,),
        in_specs=[pl.BlockSpec((tb, LANES), lambda i: (i, 0)),
                  pl.BlockSpec((SUBLANES, LANES), lambda i: (0, 0)),
                  pl.BlockSpec((SUBLANES, LANES), lambda i: (0, 0))],
        out_specs=[pl.BlockSpec((SUBLANES, tb), lambda i: (0, i)),
                   pl.BlockSpec((n_tiles_pad, LANES), lambda i: (0, 0))],
        out_shape=[jax.ShapeDtypeStruct((SUBLANES, t), I32),
                   jax.ShapeDtypeStruct((n_tiles_pad, LANES), I32)],
        scratch_shapes=[pltpu.VMEM((SUBLANES, LANES), F32), pltpu.VMEM((SUBLANES, LANES), F32)],
        compiler_params=pltpu.CompilerParams(dimension_semantics=("arbitrary",),
                                             vmem_limit_bytes=VMEM_LIMIT),
        name="rank",
    )(meta, counts, tab)


def _dispatch_kernel(pos_ref, src_ref, zeros_ref, dst_ref, sem, *, tb, rpt):
    del zeros_ref
    base = pl.program_id(0) * tb

    def issue(j, carry):
        src = src_ref.at[pl.ds(pl.multiple_of(j * rpt, rpt), rpt)]
        dst = dst_ref.at[pl.ds(pl.multiple_of(pos_ref[base + j] * rpt, rpt), rpt)]
        pltpu.make_async_copy(src, dst, sem).start()
        return carry

    lax.fori_loop(0, tb, issue, 0, unroll=8)
    pltpu.make_async_copy(src_ref, dst_ref.at[pl.ds(0, tb * rpt)], sem).wait()


def _dispatch(pos, src, n_slots, rpt, tb):
    t = pos.shape[0]
    kern = functools.partial(_dispatch_kernel, tb=tb, rpt=rpt)
    return pl.pallas_call(
        kern,
        grid_spec=pltpu.PrefetchScalarGridSpec(
            num_scalar_prefetch=1,
            grid=(t // tb,),
            in_specs=[pl.BlockSpec((tb * rpt, LANES), lambda i, pos_ref: (i, 0)),
                      pl.BlockSpec(memory_space=pl.ANY)],
            out_specs=pl.BlockSpec(memory_space=pl.ANY),
            scratch_shapes=[pltpu.SemaphoreType.DMA(())],
        ),
        out_shape=jax.ShapeDtypeStruct((n_slots * rpt, LANES), src.dtype),
        input_output_aliases={2: 0},
        compiler_params=pltpu.CompilerParams(dimension_semantics=("arbitrary",),
                                             has_side_effects=True, vmem_limit_bytes=VMEM_LIMIT),
        name="dispatch",
    )(pos, src, jnp.zeros((n_slots * rpt, LANES), src.dtype))


def _collect_kernel(pos_ref, ys_ref, x1t_ref, meta_ref, lnw_ref, lnb_ref, out_ref, buf_ref, sems,
                    *, tb, alpha, d):
    segs = d // LANES
    rpt = 2 * segs
    step = pl.program_id(0)
    n_steps = pl.num_programs(0)

    def gather(blk, slot):
        base = blk * tb

        def issue(j, carry):
            src = ys_ref.at[pl.ds(pos_ref[base + j], 1)]
            dst = buf_ref.at[slot, pl.ds(j, 1)]
            pltpu.make_async_copy(src, dst, sems.at[slot]).start()
            return carry

        lax.fori_loop(0, tb, issue, 0, unroll=8)

    @pl.when(step == 0)
    def _():
        gather(0, 0)

    @pl.when(step + 1 < n_steps)
    def _():
        gather(step + 1, (step + 1) % 2)

    slot = step % 2
    pltpu.make_async_copy(ys_ref.at[pl.ds(0, tb)], buf_ref.at[slot], sems.at[slot]).wait()

    x1 = _load_token_tiles(x1t_ref, tb, segs)
    y_lo = buf_ref[slot, :, 0:d]
    y_hi = buf_ref[slot, :, d:2 * d]
    meta = meta_ref[...]
    z = alpha * x1 + meta[:, 1:2] * y_lo + meta[:, 2:3] * y_hi
    out_ref[...] = _layer_norm(z, lnw_ref[...], lnb_ref[...])


def _collect(pos, ys, x1t, meta, ln_w, ln_b, alpha, d, tb):
    t = pos.shape[0]
    segs = d // LANES
    kern = functools.partial(_collect_kernel, tb=tb, alpha=alpha, d=d)
    return pl.pallas_call(
        kern,
        grid_spec=pltpu.PrefetchScalarGridSpec(
            num_scalar_prefetch=1,
            grid=(t // tb,),
            in_specs=[pl.BlockSpec(memory_space=pl.ANY),
                      pl.BlockSpec((tb * segs, LANES), lambda i, pos_ref: (i, 0)),
                      pl.BlockSpec((tb, LANES), lambda i, pos_ref: (i, 0)),
                      pl.BlockSpec((1, d), lambda i, pos_ref: (0, 0)),
                      pl.BlockSpec((1, d), lambda i, pos_ref: (0, 0))],
            out_specs=pl.BlockSpec((tb, d), lambda i, pos_ref: (i, 0)),
            scratch_shapes=[pltpu.VMEM((2, tb, 2 * d), F32), pltpu.SemaphoreType.DMA((2,))],
        ),
        out_shape=jax.ShapeDtypeStruct((t, d), F32),
        compiler_params=pltpu.CompilerParams(dimension_semantics=("arbitrary",),
                                             vmem_limit_bytes=VMEM_LIMIT),
        name="collect",
    )(pos, ys, x1t, meta, ln_w, ln_b)


def _moe_kernel(rows_ref, elo_ref, ehi_ref, nt_ref, xs_ref, wg_lo_ref, wu_lo_ref, wd_lo_ref,
                wg_hi_ref, wu_hi_ref, wd_hi_ref, ys_ref, *, d):
    i = pl.program_id(0)
    rows = rows_ref[i]
    segs = d // LANES

    @pl.when(rows > 0)
    def _():
        xb = _load_token_tiles(xs_ref, MOE_TILE, segs).astype(BF16)

        def expert(wg_ref, wu_ref, wd_ref):
            gate = _dot(xb, wg_ref[...])
            up = _dot(xb, wu_ref[...])
            hidden = gate * _sigmoid(gate) * up
            return _dot(hidden.astype(BF16), wd_ref[...])

        ys_ref[:, 0:d] = expert(wg_lo_ref, wu_lo_ref, wd_lo_ref)
        ys_ref[:, d:2 * d] = expert(wg_hi_ref, wu_hi_ref, wd_hi_ref)

    @pl.when(rows <= 0)
    def _():
        ys_ref[...] = jnp.zeros_like(ys_ref)


def _moe(info, xs, wg, wu, wd, d, n_tiles_pad):
    de = wg.shape[-1]
    segs = d // LANES
    kern = functools.partial(_moe_kernel, d=d)

    def last_live(i, nt_ref):
        return jnp.minimum(i, jnp.maximum(nt_ref[0] - 1, 0))

    def up_spec(which):
        return pl.BlockSpec((None, d, de),
                            lambda i, rows, elo, ehi, nt: ((elo, ehi)[which][last_live(i, nt)], 0, 0))

    def down_spec(which):
        return pl.BlockSpec((None, de, d),
                            lambda i, rows, elo, ehi, nt: ((elo, ehi)[which][last_live(i, nt)], 0, 0))

    rows, elo, ehi, nt = info[:, 0], info[:, 1], info[:, 2], info[0:1, 3]
    return pl.pallas_call(
        kern,
        grid_spec=pltpu.PrefetchScalarGridSpec(
            num_scalar_prefetch=4,
            grid=(n_tiles_pad,),
            in_specs=[pl.BlockSpec((MOE_TILE * segs, LANES),
                                   lambda i, rows, elo, ehi, nt: (last_live(i, nt), 0)),
                      up_spec(0), up_spec(0), down_spec(0), up_spec(1), up_spec(1), down_spec(1)],
            out_specs=pl.BlockSpec((MOE_TILE, 2 * d), lambda i, rows, elo, ehi, nt: (i, 0)),
        ),
        out_shape=jax.ShapeDtypeStruct((n_tiles_pad * MOE_TILE, 2 * d), F32),
        compiler_params=pltpu.CompilerParams(dimension_semantics=("arbitrary",),
                                             vmem_limit_bytes=VMEM_LIMIT),
        name="moe",
    )(rows, elo, ehi, nt, xs, wg, wu, wd, wg, wu, wd)


def _pick_block(n, target):
    blk = min(n, target)
    while n % blk:
        blk //= 2
    return blk


def kernel(x, w_in, conv_w, conv_b, mlstm_gate_bias, mlstm_norm_w, attn_sinks, w_out, ln1_w, ln1_b,
           w_group_router, b_group_router, w_expert_router, b_expert_router,
           w_exp_gate, w_exp_up, w_exp_down, ln2_w, ln2_b):
    b, s, d = x.shape
    t = b * s
    depth = w_in.shape[0]
    alpha = (2.0 * depth) ** 0.25
    assert s % ML_CHUNK == 0 and s % WINDOW == 0 and d % LANES == 0

    tm = _pick_block(t, 512)
    tq = _pick_block(s, 4 * ML_CHUNK)
    tb_rank = _pick_block(t, 512)
    tb_dma = _pick_block(t, 2048)
    tb_col = _pick_block(t, 256)
    n_tiles_pad = -(-(t // MOE_TILE + N_CLASSES) // SUBLANES) * SUBLANES
    cos_t, sin_t = _rope_tables(s)

    for l in range(depth):
        x2d = x.reshape(t, d)
        q, k, v, og, aq, ak, av, g = _inproj(x2d, _pack_w_in(w_in[l]), conv_w[l], conv_b[l][None, :], tm, s // tm)
        bias_pad = jnp.zeros((LANES - ML_HEADS,), F32)
        gate_bias_row = jnp.concatenate(
            [mlstm_gate_bias[l, 0], bias_pad, mlstm_gate_bias[l, 1], bias_pad])[None, :]
        ml = _mlstm(q.reshape(b, s, -1), k.reshape(b, s, -1), v.reshape(b, s, -1), og.reshape(b, s, -1),
                    g.reshape(b, s, -1), gate_bias_row, mlstm_norm_w[l][None, :], tq)
        att = _swa(aq.reshape(b, s, -1), ak.reshape(b, s, -1), av.reshape(b, s, -1), cos_t, sin_t, attn_sinks[l])

        w_router, b_router = _pack_router(w_group_router[l], b_group_router[l],
                                          w_expert_router[l], b_expert_router[l])
        x1t, meta, counts = _outproj(x2d, ml.reshape(t, -1), att.reshape(t, -1), w_out[l].astype(BF16),
                                     ln1_w[l][None, :], ln1_b[l][None, :], w_router, b_router, alpha, tm)

        pos2d, info = _rank(meta, counts, tb_rank, n_tiles_pad)
        pos = pos2d[0]
        xs = _dispatch(pos, x1t, n_tiles_pad * MOE_TILE, d // LANES, tb_dma)
        ys = _moe(info, xs, w_exp_gate[l].astype(BF16), w_exp_up[l].astype(BF16), w_exp_down[l].astype(BF16),
                  d, n_tiles_pad)
        out = _collect(pos, ys, x1t, meta, ln2_w[l][None, :], ln2_b[l][None, :], alpha, d, tb_col)
        x = out.reshape(b, s, d)
    return x
```

```python
import functools
import math

import numpy as np
import jax
import jax.numpy as jnp
from jax import lax
from jax.experimental import pallas as pl
from jax.experimental.pallas import tpu as pltpu

F32 = jnp.float32
BF16 = jnp.bfloat16
I32 = jnp.int32

ML_HEADS = 4
ML_HEAD_DIM = 128
ML_WIDTH = ML_HEADS * ML_HEAD_DIM
ML_CHUNK = 128
CONV_WIDTH = 4
ATT_Q_HEADS = 8
ATT_KV_HEADS = 2
ATT_HEAD_DIM = 64
ATT_WIDTH = ATT_Q_HEADS * ATT_HEAD_DIM
ATT_KV_WIDTH = ATT_KV_HEADS * ATT_HEAD_DIM
WINDOW = 128
ROPE_THETA = 10000.0
N_GROUPS = 4
EXPERTS_PER_GROUP = 8
N_EXPERTS = N_GROUPS * EXPERTS_PER_GROUP
PAIRS_PER_GROUP = EXPERTS_PER_GROUP * (EXPERTS_PER_GROUP - 1) // 2
N_CLASSES = N_GROUPS * PAIRS_PER_GROUP
LN_EPS = 1e-5

LANES = 128
SUBLANES = 8
MOE_TILE = 320
VMEM_LIMIT = 56 * 1024 * 1024

NEG_INF = float("-inf")


def _sigmoid(x):
    return 1.0 / (1.0 + jnp.exp(-x))


def _log_sigmoid(x):
    return jnp.minimum(x, 0.0) - jnp.log(1.0 + jnp.exp(-jnp.abs(x)))


def _iota(shape, dim):
    return lax.broadcasted_iota(I32, shape, dim)


def _dot(a, b):
    return jnp.dot(a, b, preferred_element_type=F32)


def _dot_exact(a, b):
    return jnp.dot(a, b, preferred_element_type=F32, precision=lax.Precision.HIGHEST)


def _layer_norm(z, w, b):
    mu = jnp.mean(z, axis=-1, keepdims=True)
    zc = z - mu
    var = jnp.mean(zc * zc, axis=-1, keepdims=True)
    return zc * lax.rsqrt(var + LN_EPS) * w + b


C_QK = 0
C_V = C_QK + 2 * ML_WIDTH
C_O = C_V + ML_WIDTH
C_AQ = C_O + ML_WIDTH
C_AK = C_AQ + ATT_WIDTH
C_AV = C_AK + ATT_KV_HEADS * LANES
C_G = C_AV + ATT_KV_HEADS * LANES
C_END = C_G + 2 * LANES


def _pack_w_in(w_in):
    sizes = (2 * ML_WIDTH, ML_WIDTH, ML_WIDTH, ML_HEADS, ML_HEADS, ATT_WIDTH, ATT_KV_WIDTH, ATT_KV_WIDTH)
    splits = np.cumsum(sizes)[:-1].tolist()
    w_qk, w_v, w_o, w_i, w_f, w_aq, w_ak, w_av = jnp.split(w_in, splits, axis=-1)
    half = ATT_HEAD_DIM // 2

    def head(w, h):
        return w[:, h * ATT_HEAD_DIM:(h + 1) * ATT_HEAD_DIM]

    def q_tile(a, b):
        return [a[:, :half], b[:, :half], a[:, half:], b[:, half:]]

    q_cols = [t for p in range(ATT_Q_HEADS // 2) for t in q_tile(head(w_aq, 2 * p), head(w_aq, 2 * p + 1))]
    k_cols = [t for h in range(ATT_KV_HEADS) for t in q_tile(head(w_ak, h), head(w_ak, h))]
    v_cols = [t for h in range(ATT_KV_HEADS) for t in (head(w_av, h), head(w_av, h))]
    lane_pad = jnp.zeros((w_in.shape[0], LANES - ML_HEADS), w_in.dtype)
    packed = jnp.concatenate([w_qk, w_v, w_o] + q_cols + k_cols + v_cols + [w_i, lane_pad, w_f, lane_pad], axis=-1)
    return packed.astype(BF16)


def _inproj_kernel(x_ref, w_ref, cw_ref, cb_ref, q_ref, k_ref, v_ref, og_ref, aq_ref, ak_ref, av_ref, g_ref,
                   *scratch, blocks_per_seq):
    *ext_refs, xb_ref = scratch
    tm = x_ref.shape[0]
    halo = SUBLANES
    cs = ext_refs[0].shape[1]
    xb_ref[...] = x_ref[...].astype(BF16)

    def mm(lo, hi):
        return _dot(xb_ref[...], w_ref[:, lo:hi])

    @pl.when(pl.program_id(0) % blocks_per_seq == 0)
    def _():
        for ext_ref in ext_refs:
            ext_ref[0:halo, :] = jnp.zeros((halo, cs), F32)

    scale = ML_HEAD_DIM ** -0.5

    def conv_slice(idx):
        ext_ref = ext_refs[idx]
        c0 = idx * cs
        dst_ref, off, mul = (q_ref, c0, 1.0) if c0 < ML_WIDTH else (k_ref, c0 - ML_WIDTH, scale)
        rt = ML_CHUNK
        for r0 in range(0, tm, rt):
            conv = cb_ref[:, c0:c0 + cs]
            for j in range(CONV_WIDTH):
                start = halo + r0 - (CONV_WIDTH - 1) + j
                conv = conv + cw_ref[j:j + 1, c0:c0 + cs] * ext_ref[start:start + rt, :]
            act = conv * _sigmoid(conv)
            dst_ref[r0:r0 + rt, off:off + cs] = (act * mul).astype(BF16)
        ext_ref[0:halo, :] = ext_ref[tm:tm + halo, :]

    for idx in range(len(ext_refs)):
        ext_refs[idx][halo:halo + tm, :] = mm(C_QK + idx * cs, C_QK + (idx + 1) * cs)
        if idx > 0:
            conv_slice(idx - 1)
    half_v = ML_WIDTH // 2
    v_ref[:, 0:half_v] = mm(C_V, C_V + half_v).astype(BF16)
    conv_slice(len(ext_refs) - 1)
    v_ref[:, half_v:] = mm(C_V + half_v, C_O).astype(BF16)
    og_ref[...] = _sigmoid(mm(C_O, C_AQ)).astype(BF16)
    aq_ref[...] = mm(C_AQ, C_AK)
    ak_ref[...] = mm(C_AK, C_AV)
    av_ref[...] = mm(C_AV, C_G).astype(BF16)
    g_ref[...] = mm(C_G, C_END)


def _inproj(x2d, w_packed, conv_w, conv_b, tm, blocks_per_seq):
    t, d = x2d.shape
    widths = (ML_WIDTH, ML_WIDTH, ML_WIDTH, ML_WIDTH, C_AK - C_AQ, C_AV - C_AK, C_G - C_AV, C_END - C_G)
    dtypes = (BF16, BF16, BF16, BF16, F32, F32, BF16, F32)
    kern = functools.partial(_inproj_kernel, blocks_per_seq=blocks_per_seq)
    return pl.pallas_call(
        kern,
        grid=(t // tm,),
        in_specs=[pl.BlockSpec((tm, d), lambda i: (i, 0)),
                  pl.BlockSpec((d, C_END), lambda i: (0, 0)),
                  pl.BlockSpec((CONV_WIDTH, 2 * ML_WIDTH), lambda i: (0, 0)),
                  pl.BlockSpec((1, 2 * ML_WIDTH), lambda i: (0, 0))],
        out_specs=[pl.BlockSpec((tm, w), lambda i: (i, 0)) for w in widths],
        out_shape=[jax.ShapeDtypeStruct((t, w), dt) for w, dt in zip(widths, dtypes)],
        scratch_shapes=[pltpu.VMEM((tm + SUBLANES, 2 * LANES), F32)] * (2 * ML_WIDTH // (2 * LANES))
        + [pltpu.VMEM((tm, d), BF16)],
        compiler_params=pltpu.CompilerParams(dimension_semantics=("arbitrary",),
                                             vmem_limit_bytes=VMEM_LIMIT),
        name="inproj",
    )(x2d, w_packed, conv_w, conv_b)


def _time_scan(x, combine, identity):
    row = _iota(x.shape, 0)
    sh = 1
    while sh < x.shape[0]:
        x = combine(x, jnp.where(row >= sh, pltpu.roll(x, sh, 0), identity))
        sh *= 2
    return x


def _mlstm_kernel(q_ref, k_ref, v_ref, og_ref, g_ref, gb_ref, nw_ref, out_ref, ct_ref, m_ref, *, tq):
    s_idx = pl.program_id(1)
    L = ML_CHUNK
    D = ML_HEAD_DIM
    H = ML_HEADS
    heads = range(H)

    @pl.when(s_idx == 0)
    def _():
        ct_ref[...] = jnp.zeros_like(ct_ref)
        m_ref[...] = jnp.zeros_like(m_ref)

    causal = _iota((L, L), 1) <= _iota((L, L), 0)
    ones_blk = jnp.ones((L, D), BF16)
    mean_blk = jnp.full((D, D), 1.0 / D, BF16)
    m_prev = m_ref[0:1, :]
    head_lanes = _iota((L, LANES), 1) < H
    tile_of_lane = jnp.right_shift(_iota((LANES, H * L), 1), L.bit_length() - 1)
    spread = jnp.where(_iota((LANES, H * L), 0) == tile_of_lane, 1.0, 0.0).astype(BF16)

    def spread_heads(x):
        x = jnp.where(head_lanes, x, 0.0)
        hi = x.astype(BF16)
        lo = (x - hi.astype(F32)).astype(BF16)
        return _dot(hi, spread) + _dot(lo, spread)

    for c in range(tq // L):
        r0 = c * L
        gi = g_ref[r0:r0 + L, 0:LANES] + gb_ref[:, 0:LANES]
        gf = g_ref[r0:r0 + L, LANES:2 * LANES] + gb_ref[:, LANES:2 * LANES]
        b_cum = _time_scan(_log_sigmoid(gf), jnp.add, 0.0)
        r = gi - b_cum
        g = jnp.maximum(m_prev, _time_scan(r, jnp.maximum, NEG_INF))
        g_rep = spread_heads(g)
        b_rep = spread_heads(b_cum)
        b_last = b_cum[L - 1:L, :]
        m_new = jnp.maximum(b_last + m_prev, jnp.max(b_last + r, axis=0, keepdims=True))
        decay = jnp.exp(b_last + m_prev - m_new)
        shift = b_last - m_new
        r_t = r.T

        q_b = [q_ref[r0:r0 + L, h * D:(h + 1) * D] for h in heads]
        kt_b = [k_ref[r0:r0 + L, h * D:(h + 1) * D].T for h in heads]
        v_aug = [jnp.concatenate([v_ref[r0:r0 + L, h * D:(h + 1) * D], ones_blk], axis=-1) for h in heads]
        g_col = [g_rep[:, h * L:(h + 1) * L] for h in heads]
        w_intra = [jnp.exp(jnp.where(causal, r_t[h:h + 1, :] - g_col[h], NEG_INF)) for h in heads]
        s_b = [(_dot(q_b[h], kt_b[h]) * w_intra[h]).astype(BF16) for h in heads]
        ct = [ct_ref[h] for h in heads]
        inter = [_dot(q_b[h], ct[h].astype(BF16)) for h in heads]
        intra = [_dot(s_b[h], v_aug[h]) for h in heads]
        for h in heads:
            wi_col = jnp.exp(m_prev[:, h:h + 1] - g_col[h])
            clamp = jnp.exp(-(b_rep[:, h * L:(h + 1) * L] + g_col[h]))
            num = wi_col * inter[h][:, 0:D] + intra[h][:, 0:D]
            den = wi_col * inter[h][:, D:] + intra[h][:, D:]
            hh = num / jnp.maximum(jnp.abs(den), clamp)
            mu = _dot(hh.astype(BF16), mean_blk)
            hc = hh - mu
            var = _dot((hc * hc).astype(BF16), mean_blk)
            hn = hc * lax.rsqrt(var + LN_EPS) * nw_ref[:, h * D:(h + 1) * D]
            gate_o = og_ref[r0:r0 + L, h * D:(h + 1) * D].astype(F32)
            out_ref[r0:r0 + L, h * D:(h + 1) * D] = (gate_o * hn).astype(out_ref.dtype)
        for h in heads:
            w_row = jnp.exp(r_t[h:h + 1, :] + shift[:, h:h + 1])
            ktw = (kt_b[h].astype(F32) * w_row).astype(BF16)
            ct_ref[h] = decay[:, h:h + 1] * ct[h] + _dot(ktw, v_aug[h])
        m_prev = m_new

    m_ref[...] = jnp.broadcast_to(m_prev, m_ref.shape)


def _mlstm(q, k, v, og, g, gate_bias_row, norm_w_row, tq):
    b, s, _ = q.shape
    kern = functools.partial(_mlstm_kernel, tq=tq)

    def seq_spec(width):
        return pl.BlockSpec((None, tq, width), lambda bi, si: (bi, si, 0))

    def const_spec(shape):
        return pl.BlockSpec(shape, lambda bi, si: (0,) * len(shape))

    return pl.pallas_call(
        kern,
        grid=(b, s // tq),
        in_specs=[seq_spec(ML_WIDTH), seq_spec(ML_WIDTH), seq_spec(ML_WIDTH), seq_spec(ML_WIDTH),
                  seq_spec(2 * LANES), const_spec((1, 2 * LANES)), const_spec((1, ML_WIDTH))],
        out_specs=seq_spec(ML_WIDTH),
        out_shape=jax.ShapeDtypeStruct((b, s, ML_WIDTH), BF16),
        scratch_shapes=[pltpu.VMEM((ML_HEADS, ML_HEAD_DIM, 2 * ML_HEAD_DIM), F32),
                        pltpu.VMEM((SUBLANES, LANES), F32)],
        compiler_params=pltpu.CompilerParams(dimension_semantics=("arbitrary", "arbitrary"),
                                             vmem_limit_bytes=VMEM_LIMIT),
        name="mlstm",
    )(q, k, v, og, g, gate_bias_row, norm_w_row)


def _rope_tables(seq_len):
    half = ATT_HEAD_DIM // 2
    inv_freq = ROPE_THETA ** (-jnp.arange(half, dtype=F32) / half)
    ang = jnp.arange(seq_len, dtype=F32)[:, None] * inv_freq[None, :]
    cos = jnp.cos(ang)
    sin = jnp.sin(ang)
    cos_t = jnp.concatenate([cos, cos, cos, cos], axis=-1)
    sin_t = jnp.concatenate([-sin, -sin, sin, sin], axis=-1)
    return cos_t, sin_t


def _swa_kernel(sink_ref, aq_ref, ak_ref, av_ref, cos_ref, sin_ref, out_ref, kprev_ref, vprev_ref):
    i = pl.program_id(1)
    Lb = WINDOW
    half = ATT_HEAD_DIM // 2
    pairs = ATT_Q_HEADS // 2
    pairs_per_kv = pairs // ATT_KV_HEADS

    @pl.when(i == 0)
    def _():
        kprev_ref[...] = jnp.zeros_like(kprev_ref)
        vprev_ref[...] = jnp.zeros_like(vprev_ref)

    cos = cos_ref[...]
    sin = sin_ref[...]

    def rope(x):
        tiles = []
        for c in range(x.shape[-1] // LANES):
            xt = x[:, c * LANES:(c + 1) * LANES]
            tiles.append(xt * cos + pltpu.roll(xt, LANES // 2, 1) * sin)
        return jnp.concatenate(tiles, axis=-1)

    q = rope(aq_ref[...]) * (ATT_HEAD_DIM ** -0.5)
    k_cur = rope(ak_ref[...]).astype(BF16)
    v_cur = av_ref[...]

    ql = _iota((Lb, 2 * Lb), 0)
    kj = _iota((Lb, 2 * Lb), 1)
    diff = Lb + ql - kj
    kpos = i * Lb + kj - Lb
    visible = jnp.where(diff >= 0, jnp.where(diff < WINDOW, jnp.where(kpos >= 0, 1, 0), 0), 0)
    bias = jnp.where(visible > 0, 0.0, NEG_INF).astype(F32)
    bias = jnp.concatenate([bias] * (2 * pairs_per_kv), axis=0)

    lane = _iota((Lb, LANES), 1)
    low_half = lane < ATT_HEAD_DIM
    first_head = (lane & half) == 0
    ones_blk = jnp.ones((2 * Lb, LANES), BF16)

    for g in range(ATT_KV_HEADS):
        kk = jnp.concatenate([kprev_ref[:, g * LANES:(g + 1) * LANES], k_cur[:, g * LANES:(g + 1) * LANES]], axis=0)
        vv = jnp.concatenate([vprev_ref[:, g * LANES:(g + 1) * LANES], v_cur[:, g * LANES:(g + 1) * LANES]], axis=0)
        vv_aug = jnp.concatenate([vv, ones_blk], axis=-1)
        rows = []
        sinks = []
        for p in range(pairs_per_kv):
            pair = g * pairs_per_kv + p
            q2 = q[:, pair * LANES:(pair + 1) * LANES]
            rows.append(jnp.where(first_head, q2, 0.0))
            rows.append(jnp.where(first_head, 0.0, q2))
            sinks.append(jnp.full((Lb, LANES), sink_ref[2 * pair], F32))
            sinks.append(jnp.full((Lb, LANES), sink_ref[2 * pair + 1], F32))
        qs = jnp.concatenate(rows, axis=0).astype(BF16)
        sink = jnp.concatenate(sinks, axis=0)

        sc = lax.dot_general(qs, kk, (((1,), (1,)), ((), ())), preferred_element_type=F32) + bias
        m = jnp.maximum(jnp.broadcast_to(jnp.max(sc, axis=-1, keepdims=True), sink.shape), sink)
        p_un = jnp.exp(sc - jnp.concatenate([m, m], axis=-1))
        acc = _dot(p_un.astype(BF16), vv_aug)
        o = acc[:, 0:LANES] / (acc[:, LANES:] + jnp.exp(sink - m))
        for p in range(pairs_per_kv):
            pair = g * pairs_per_kv + p
            even = o[(2 * p) * Lb:(2 * p + 1) * Lb, :]
            odd = o[(2 * p + 1) * Lb:(2 * p + 2) * Lb, :]
            out_ref[:, pair * LANES:(pair + 1) * LANES] = jnp.where(low_half, even, odd).astype(out_ref.dtype)

    kprev_ref[...] = k_cur
    vprev_ref[...] = v_cur


def _swa(aq, ak, av, cos_t, sin_t, sinks):
    b, s, _ = aq.shape
    kvw = ATT_KV_HEADS * LANES

    def seq_spec(width):
        return pl.BlockSpec((None, WINDOW, width), lambda bi, si: (bi, si, 0))

    tab_spec = pl.BlockSpec((WINDOW, LANES), lambda bi, si: (si, 0))
    return pl.pallas_call(
        _swa_kernel,
        grid=(b, s // WINDOW),
        in_specs=[pl.BlockSpec(memory_space=pltpu.SMEM),
                  seq_spec(ATT_WIDTH), seq_spec(kvw), seq_spec(kvw), tab_spec, tab_spec],
        out_specs=seq_spec(ATT_WIDTH),
        out_shape=jax.ShapeDtypeStruct((b, s, ATT_WIDTH), BF16),
        scratch_shapes=[pltpu.VMEM((WINDOW, kvw), BF16), pltpu.VMEM((WINDOW, kvw), BF16)],
        compiler_params=pltpu.CompilerParams(dimension_semantics=("arbitrary", "arbitrary"),
                                             vmem_limit_bytes=VMEM_LIMIT),
        name="swa",
    )(sinks, aq, ak, av, cos_t, sin_t)


def _store_token_tiles(ref, val, row0=0, rows_per_token=None):
    n, w = val.shape
    segs = w // LANES
    rpt = rows_per_token or segs
    for j in range(segs):
        ref[pl.ds(row0 + j, n, stride=rpt), :] = val[:, j * LANES:(j + 1) * LANES]


def _load_token_tiles(ref, n, segs, row0=0, rows_per_token=None):
    rpt = rows_per_token or segs
    return jnp.concatenate([ref[pl.ds(row0 + j, n, stride=rpt), :] for j in range(segs)], axis=-1)


ROUTER_ROWS = 48


def _pack_router(w_group, b_group, w_expert, b_expert):
    d = w_group.shape[0]
    wt = jnp.zeros((ROUTER_ROWS, d), F32)
    wt = wt.at[0:N_GROUPS].set(w_group.T).at[SUBLANES:SUBLANES + N_EXPERTS].set(w_expert.T)
    bias = jnp.zeros((ROUTER_ROWS,), F32)
    bias = bias.at[0:N_GROUPS].set(b_group).at[SUBLANES:SUBLANES + N_EXPERTS].set(b_expert)
    hi = wt.astype(BF16)
    lo = (wt - hi.astype(F32)).astype(BF16)
    return jnp.concatenate([hi, lo], axis=0), jnp.broadcast_to(bias[:, None], (ROUTER_ROWS, LANES))


def _outproj_kernel(x_ref, ml_ref, att_ref, wo_ref, lnw_ref, lnb_ref, wrt_ref, brt_ref,
                    x1t_ref, meta_ref, cnt_ref, *, alpha):
    step = pl.program_id(0)
    tm, d = x_ref.shape
    y = _dot(ml_ref[...], wo_ref[0:ML_WIDTH, :]) + _dot(att_ref[...], wo_ref[ML_WIDTH:, :])
    x1 = _layer_norm(alpha * x_ref[...] + y, lnw_ref[...], lnb_ref[...])
    _store_token_tiles(x1t_ref, x1)

    rr = wrt_ref.shape[0] // 2
    x1_hi = x1.astype(BF16)
    x1_lo = (x1 - x1_hi.astype(F32)).astype(BF16)
    nt = (((1,), (1,)), ((), ()))
    both = lax.dot_general(wrt_ref[...], x1_hi, nt, preferred_element_type=F32)
    cross = lax.dot_general(wrt_ref[0:rr, :], x1_lo, nt, preferred_element_type=F32)
    logits = both[0:rr] + both[rr:2 * rr] + cross + jnp.concatenate([brt_ref[...]] * (tm // LANES), axis=1)

    row = _iota((SUBLANES, tm), 0).astype(F32)

    def first_argmax(vals):
        top = jnp.max(vals, axis=0, keepdims=True)
        idx = jnp.min(jnp.where(vals == top, row, float(SUBLANES)), axis=0, keepdims=True)
        return top, idx

    g_logits = jnp.where(row < N_GROUPS, logits[0:SUBLANES], NEG_INF)
    g_top, g_idx = first_argmax(g_logits)
    g_p = 1.0 / jnp.sum(jnp.exp(g_logits - g_top), axis=0, keepdims=True)

    e_logits = logits[SUBLANES:2 * SUBLANES]
    for grp in range(1, N_GROUPS):
        e_logits = jnp.where(g_idx == grp, logits[(1 + grp) * SUBLANES:(2 + grp) * SUBLANES], e_logits)
    v1, a1 = first_argmax(e_logits)
    v2, a2 = first_argmax(jnp.where(row == a1, NEG_INF, e_logits))
    r = jnp.exp(v2 - v1)
    w1 = g_p / (1.0 + r)
    w2 = g_p * r / (1.0 + r)

    lo = jnp.minimum(a1, a2)
    hi = jnp.maximum(a1, a2)
    w_lo = jnp.where(a1 < a2, w1, w2)
    w_hi = jnp.where(a1 < a2, w2, w1)
    pair_idx = (EXPERTS_PER_GROUP - 1) * lo - lo * (lo - 1.0) * 0.5 + (hi - lo - 1.0)
    cls = g_idx * PAIRS_PER_GROUP + pair_idx

    meta_t = jnp.where(row == 0.0, cls, jnp.where(row == 1.0, w_lo, jnp.where(row == 2.0, w_hi, 0.0)))
    meta = jnp.concatenate([meta_t, jnp.zeros((LANES - SUBLANES, tm), F32)], axis=0).T
    meta_ref[...] = meta

    @pl.when(step == 0)
    def _():
        cnt_ref[...] = jnp.zeros_like(cnt_ref)

    lane = _iota((tm, LANES), 1).astype(F32)
    onehot = jnp.where(lane == meta[:, 0:1], 1.0, 0.0)
    cnt_ref[0:1, :] += jnp.sum(onehot, axis=0, keepdims=True)


def _outproj(x2d, ml2d, att2d, w_out_b, ln_w, ln_b, w_router, b_router, alpha, tm):
    t, d = x2d.shape
    kern = functools.partial(_outproj_kernel, alpha=alpha)

    def const_spec(shape):
        return pl.BlockSpec(shape, lambda i: (0,) * len(shape))

    return pl.pallas_call(
        kern,
        grid=(t // tm,),
        in_specs=[pl.BlockSpec((tm, d), lambda i: (i, 0)),
                  pl.BlockSpec((tm, ML_WIDTH), lambda i: (i, 0)),
                  pl.BlockSpec((tm, ATT_WIDTH), lambda i: (i, 0)),
                  const_spec((ML_WIDTH + ATT_WIDTH, d)), const_spec((1, d)), const_spec((1, d)),
                  const_spec((2 * ROUTER_ROWS, d)), const_spec((ROUTER_ROWS, LANES))],
        out_specs=[pl.BlockSpec((tm * (d // LANES), LANES), lambda i: (i, 0)),
                   pl.BlockSpec((tm, LANES), lambda i: (i, 0)), const_spec((SUBLANES, LANES))],
        out_shape=[jax.ShapeDtypeStruct((t * (d // LANES), LANES), F32),
                   jax.ShapeDtypeStruct((t, LANES), F32),
                   jax.ShapeDtypeStruct((SUBLANES, LANES), F32)],
        compiler_params=pltpu.CompilerParams(dimension_semantics=("arbitrary",),
                                             vmem_limit_bytes=VMEM_LIMIT),
        name="outproj",
    )(x2d, ml2d, att2d, w_out_b, ln_w, ln_b, w_router, b_router)


def _class_expert_table():
    tab = np.zeros((SUBLANES, LANES), np.float32)
    for g in range(N_GROUPS):
        idx = 0
        for lo in range(EXPERTS_PER_GROUP):
            for hi in range(lo + 1, EXPERTS_PER_GROUP):
                c = g * PAIRS_PER_GROUP + idx
                tab[0, c] = g * EXPERTS_PER_GROUP + lo
                tab[1, c] = g * EXPERTS_PER_GROUP + hi
                idx += 1
    return tab


def _rank_kernel(meta_ref, cnt_ref, tab_ref, pos_ref, tile_ref, base_ref, run_ref, *, tb, n_tiles_pad):
    step = pl.program_id(0)
    lane8 = _iota((SUBLANES, LANES), 1)

    @pl.when(step == 0)
    def _():
        cnt = jnp.broadcast_to(cnt_ref[0:1, :], (SUBLANES, LANES))
        tiles = jnp.floor((cnt + (MOE_TILE - 1.0)) * (1.0 / MOE_TILE))
        cum = tiles
        sh = 1
        while sh < LANES:
            cum = cum + jnp.where(lane8 >= sh, pltpu.roll(cum, sh, 1), 0.0)
            sh *= 2
        excl = cum - tiles
        base_ref[...] = excl * MOE_TILE
        run_ref[...] = jnp.zeros_like(run_ref)

        ti = _iota((n_tiles_pad, LANES), 0).astype(F32)
        lane = _iota((n_tiles_pad, LANES), 1)
        done = jnp.where(lane < N_CLASSES, jnp.where(cum[0:1, :] <= ti, 1.0, 0.0), 0.0)
        t_cls = jnp.sum(done, axis=-1, keepdims=True)
        sel = jnp.where(lane.astype(F32) == t_cls, 1.0, 0.0)
        cnt_i = jnp.sum(sel * cnt[0:1, :], axis=-1, keepdims=True)
        first_i = jnp.sum(sel * excl[0:1, :], axis=-1, keepdims=True)
        rows_i = jnp.clip(cnt_i - MOE_TILE * (ti[:, 0:1] - first_i), 0.0, float(MOE_TILE))
        e_lo = jnp.sum(sel * tab_ref[0:1, :], axis=-1, keepdims=True)
        e_hi = jnp.sum(sel * tab_ref[1:2, :], axis=-1, keepdims=True)
        n_tiles = jnp.sum(jnp.where(lane < N_CLASSES, jnp.broadcast_to(tiles[0:1, :], (n_tiles_pad, LANES)), 0.0),
                          axis=-1, keepdims=True)
        info = jnp.where(lane == 0, rows_i,
                         jnp.where(lane == 1, e_lo, jnp.where(lane == 2, e_hi, jnp.where(lane == 3, n_tiles, 0.0))))
        tile_ref[...] = info.astype(I32)

    cls = meta_ref[:, 0:1]
    lane = _iota((tb, LANES), 1).astype(F32)
    onehot = jnp.where(lane == cls, 1.0, 0.0)
    strict_lower = jnp.where(_iota((tb, tb), 1) < _iota((tb, tb), 0), 1.0, 0.0).astype(BF16)
    before = _dot(strict_lower, onehot.astype(BF16))
    slot = jnp.sum(onehot * (before + run_ref[0:1, :] + base_ref[0:1, :]), axis=-1, keepdims=True)
    run_ref[...] = run_ref[...] + jnp.sum(onehot, axis=0, keepdims=True)
    slot_t = jnp.broadcast_to(slot, (tb, LANES)).T
    pos_ref[...] = slot_t[0:SUBLANES, :].astype(I32)


def _rank(meta, counts, tb, n_tiles_pad):
    t = meta.shape[0]
    kern = functools.partial(_rank_kernel, tb=tb, n_tiles_pad=n_tiles_pad)
    tab = jnp.asarray(_class_expert_table())
    return pl.pallas_call(
        kern,
        grid=(t // tb,),
        in_specs=[pl.BlockSpec((tb, LANES), lambda i: (i, 0)),
                  pl.BlockSpec((SUBLANES, LANES), lambda i: (0, 0)),
                  pl.BlockSpec((SUBLANES, LANES), lambda i: (0, 0))],
        out_specs=[pl.BlockSpec((SUBLANES, tb), lambda i: (0, i)),
                   pl.BlockSpec((n_tiles_pad, LANES), lambda i: (0, 0))],
        out_shape=[jax.ShapeDtypeStruct((SUBLANES, t), I32),
                   jax.ShapeDtypeStruct((n_tiles_pad, LANES), I32)],
        scratch_shapes=[pltpu.VMEM((SUBLANES, LANES), F32), pltpu.VMEM((SUBLANES, LANES), F32)],
        compiler_params=pltpu.CompilerParams(dimension_semantics=("arbitrary",),
                                             vmem_limit_bytes=VMEM_LIMIT),
        name="rank",
    )(meta, counts, tab)


def _dispatch_kernel(pos_ref, src_ref, zeros_ref, dst_ref, sem, *, tb, rpt):
    del zeros_ref
    base = pl.program_id(0) * tb

    def issue(j, carry):
        src = src_ref.at[pl.ds(pl.multiple_of(j * rpt, rpt), rpt)]
        dst = dst_ref.at[pl.ds(pl.multiple_of(pos_ref[base + j] * rpt, rpt), rpt)]
        pltpu.make_async_copy(src, dst, sem).start()
        return carry

    lax.fori_loop(0, tb, issue, 0, unroll=8)
    pltpu.make_async_copy(src_ref, dst_ref.at[pl.ds(0, tb * rpt)], sem).wait()


def _dispatch(pos, src, n_slots, rpt, tb):
    t = pos.shape[0]
    kern = functools.partial(_dispatch_kernel, tb=tb, rpt=rpt)
    return pl.pallas_call(
        kern,
        grid_spec=pltpu.PrefetchScalarGridSpec(
            num_scalar_prefetch=1,
            grid=(t // tb,),
            in_specs=[pl.BlockSpec((tb * rpt, LANES), lambda i, pos_ref: (i, 0)),
                      pl.BlockSpec(memory_space=pl.ANY)],
            out_specs=pl.BlockSpec(memory_space=pl.ANY),
            scratch_shapes=[pltpu.SemaphoreType.DMA(())],
        ),
        out_shape=jax.ShapeDtypeStruct((n_slots * rpt, LANES), src.dtype),
        input_output_aliases={2: 0},
        compiler_params=pltpu.CompilerParams(dimension_semantics=("arbitrary",),
                                             has_side_effects=True, vmem_limit_bytes=VMEM_LIMIT),
        name="dispatch",
    )(pos, src, jnp.zeros((n_slots * rpt, LANES), src.dtype))


def _collect_kernel(pos_ref, ys_ref, x1t_ref, meta_ref, lnw_ref, lnb_ref, out_ref, buf_ref, sems,
                    *, tb, alpha, d):
    segs = d // LANES
    rpt = 2 * segs
    step = pl.program_id(0)
    n_steps = pl.num_programs(0)

    def gather(blk, slot):
        base = blk * tb

        def issue(j, carry):
            src = ys_ref.at[pl.ds(pos_ref[base + j], 1)]
            dst = buf_ref.at[slot, pl.ds(j, 1)]
            pltpu.make_async_copy(src, dst, sems.at[slot]).start()
            return carry

        lax.fori_loop(0, tb, issue, 0, unroll=8)

    @pl.when(step == 0)
    def _():
        gather(0, 0)

    @pl.when(step + 1 < n_steps)
    def _():
        gather(step + 1, (step + 1) % 2)

    slot = step % 2
    pltpu.make_async_copy(ys_ref.at[pl.ds(0, tb)], buf_ref.at[slot], sems.at[slot]).wait()

    x1 = _load_token_tiles(x1t_ref, tb, segs)
    y_lo = buf_ref[slot, :, 0:d]
    y_hi = buf_ref[slot, :, d:2 * d]
    meta = meta_ref[...]
    z = alpha * x1 + meta[:, 1:2] * y_lo + meta[:, 2:3] * y_hi
    out_ref[...] = _layer_norm(z, lnw_ref[...], lnb_ref[...])


def _collect(pos, ys, x1t, meta, ln_w, ln_b, alpha, d, tb):
    t = pos.shape[0]
    segs = d // LANES
    kern = functools.partial(_collect_kernel, tb=tb, alpha=alpha, d=d)
    return pl.pallas_call(
        kern,
        grid_spec=pltpu.PrefetchScalarGridSpec(
            num_scalar_prefetch=1,
            grid=(t // tb,),
            in_specs=[pl.BlockSpec(memory_space=pl.ANY),
                      pl.BlockSpec((tb * segs, LANES), lambda i, pos_ref: (i, 0)),
                      pl.BlockSpec((tb, LANES), lambda i, pos_ref: (i, 0)),
                      pl.BlockSpec((1, d), lambda i, pos_ref: (0, 0)),
                      pl.BlockSpec((1, d), lambda i, pos_ref: (0, 0))],
            out_specs=pl.BlockSpec((tb, d), lambda i, pos_ref: (i, 0)),
            scratch_shapes=[pltpu.VMEM((2, tb, 2 * d), F32), pltpu.SemaphoreType.DMA((2,))],
        ),
        out_shape=jax.ShapeDtypeStruct((t, d), F32),
        compiler_params=pltpu.CompilerParams(dimension_semantics=("arbitrary",),
                                             vmem_limit_bytes=VMEM_LIMIT),
        name="collect",
    )(pos, ys, x1t, meta, ln_w, ln_b)


MOE_TILES_PER_STEP = 1


def _moe_kernel(rows_ref, elo_ref, ehi_ref, nt_ref, xs_ref, *refs, d):
    *w_refs, ys_ref = refs
    i = pl.program_id(0)
    segs = d // LANES

    @pl.when(rows_ref[i * MOE_TILES_PER_STEP] > 0)
    def _():
        for k in range(MOE_TILES_PER_STEP):
            wgu_lo, wd_lo, wgu_hi, wd_hi = w_refs[4 * k:4 * k + 4]
            xs_k = xs_ref.at[pl.ds(k * MOE_TILE * segs, MOE_TILE * segs)]
            xb = _load_token_tiles(xs_k, MOE_TILE, segs).astype(BF16)

            def expert(wgu_ref, wd_ref):
                de = wd_ref.shape[0]
                gate_up = _dot(xb, wgu_ref[...])
                gate = gate_up[:, 0:de]
                hidden = gate * _sigmoid(gate) * gate_up[:, de:]
                return _dot(hidden.astype(BF16), wd_ref[...])

            ys_ref[k * MOE_TILE:(k + 1) * MOE_TILE, 0:d] = expert(wgu_lo, wd_lo)
            ys_ref[k * MOE_TILE:(k + 1) * MOE_TILE, d:2 * d] = expert(wgu_hi, wd_hi)

    @pl.when(rows_ref[i * MOE_TILES_PER_STEP] <= 0)
    def _():
        ys_ref[...] = jnp.zeros_like(ys_ref)


def _moe(info, xs, wgu, wd, d, n_tiles_pad):
    de = wd.shape[1]
    segs = d // LANES
    tps = MOE_TILES_PER_STEP
    kern = functools.partial(_moe_kernel, d=d)

    def last_live(tile, nt_ref):
        return jnp.minimum(tile, jnp.maximum(nt_ref[0] - 1, 0))

    def up_spec(k, which):
        return pl.BlockSpec((None, d, 2 * de),
                            lambda i, rows, elo, ehi, nt: ((elo, ehi)[which][last_live(i * tps + k, nt)], 0, 0))

    def down_spec(k, which):
        return pl.BlockSpec((None, de, d),
                            lambda i, rows, elo, ehi, nt: ((elo, ehi)[which][last_live(i * tps + k, nt)], 0, 0))

    w_specs = []
    for k in range(tps):
        w_specs += [up_spec(k, 0), down_spec(k, 0), up_spec(k, 1), down_spec(k, 1)]
    rows, elo, ehi, nt = info[:, 0], info[:, 1], info[:, 2], info[0:1, 3]
    return pl.pallas_call(
        kern,
        grid_spec=pltpu.PrefetchScalarGridSpec(
            num_scalar_prefetch=4,
            grid=(n_tiles_pad // tps,),
            in_specs=[pl.BlockSpec((tps * MOE_TILE * segs, LANES),
                                   lambda i, rows, elo, ehi, nt: (last_live(i * tps, nt) // tps, 0))] + w_specs,
            out_specs=pl.BlockSpec((tps * MOE_TILE, 2 * d), lambda i, rows, elo, ehi, nt: (i, 0)),
        ),
        out_shape=jax.ShapeDtypeStruct((n_tiles_pad * MOE_TILE, 2 * d), F32),
        compiler_params=pltpu.CompilerParams(dimension_semantics=("arbitrary",),
                                             vmem_limit_bytes=VMEM_LIMIT),
        name="moe",
    )(rows, elo, ehi, nt, xs, *([wgu, wd, wgu, wd] * tps))


def _pick_block(n, target):
    blk = min(n, target)
    while n % blk:
        blk //= 2
    return blk


def kernel(x, w_in, conv_w, conv_b, mlstm_gate_bias, mlstm_norm_w, attn_sinks, w_out, ln1_w, ln1_b,
           w_group_router, b_group_router, w_expert_router, b_expert_router,
           w_exp_gate, w_exp_up, w_exp_down, ln2_w, ln2_b):
    b, s, d = x.shape
    t = b * s
    depth = w_in.shape[0]
    alpha = (2.0 * depth) ** 0.25
    assert s % ML_CHUNK == 0 and s % WINDOW == 0 and d % LANES == 0

    tm = _pick_block(t, 512)
    tq = _pick_block(s, 4 * ML_CHUNK)
    tb_rank = _pick_block(t, 512)
    tb_dma = _pick_block(t, 2048)
    tb_col = _pick_block(t, 256)
    n_tiles_pad = -(-(t // MOE_TILE + N_CLASSES) // SUBLANES) * SUBLANES
    cos_t, sin_t = _rope_tables(s)

    for l in range(depth):
        x2d = x.reshape(t, d)
        q, k, v, og, aq, ak, av, g = _inproj(x2d, _pack_w_in(w_in[l]), conv_w[l], conv_b[l][None, :], tm, s // tm)
        bias_pad = jnp.zeros((LANES - ML_HEADS,), F32)
        gate_bias_row = jnp.concatenate(
            [mlstm_gate_bias[l, 0], bias_pad, mlstm_gate_bias[l, 1], bias_pad])[None, :]
        ml = _mlstm(q.reshape(b, s, -1), k.reshape(b, s, -1), v.reshape(b, s, -1), og.reshape(b, s, -1),
                    g.reshape(b, s, -1), gate_bias_row, mlstm_norm_w[l][None, :], tq)
        att = _swa(aq.reshape(b, s, -1), ak.reshape(b, s, -1), av.reshape(b, s, -1), cos_t, sin_t, attn_sinks[l])

        w_router, b_router = _pack_router(w_group_router[l], b_group_router[l],
                                          w_expert_router[l], b_expert_router[l])
        x1t, meta, counts = _outproj(x2d, ml.reshape(t, -1), att.reshape(t, -1), w_out[l].astype(BF16),
                                     ln1_w[l][None, :], ln1_b[l][None, :], w_router, b_router, alpha, tm)

        pos2d, info = _rank(meta, counts, tb_rank, n_tiles_pad)
        pos = pos2d[0]
        xs = _dispatch(pos, x1t, n_tiles_pad * MOE_TILE, d // LANES, tb_dma)
        w_gate_up = jnp.concatenate([w_exp_gate[l].astype(BF16), w_exp_up[l].astype(BF16)], axis=-1)
        ys = _moe(info, xs, w_gate_up, w_exp_down[l].astype(BF16), d, n_tiles_pad)
        out = _collect(pos, ys, x1t, meta, ln2_w[l][None, :], ln2_b[l][None, :], alpha, d, tb_col)
        x = out.reshape(b, s, d)
    return x
```

```python
import functools
import math

import numpy as np
import jax
import jax.numpy as jnp
from jax import lax
from jax.experimental import pallas as pl
from jax.experimental.pallas import tpu as pltpu

F32 = jnp.float32
BF16 = jnp.bfloat16
I32 = jnp.int32

ML_HEADS = 4
ML_HEAD_DIM = 128
ML_WIDTH = ML_HEADS * ML_HEAD_DIM
ML_CHUNK = 128
CONV_WIDTH = 4
ATT_Q_HEADS = 8
ATT_KV_HEADS = 2
ATT_HEAD_DIM = 64
ATT_WIDTH = ATT_Q_HEADS * ATT_HEAD_DIM
ATT_KV_WIDTH = ATT_KV_HEADS * ATT_HEAD_DIM
WINDOW = 128
ROPE_THETA = 10000.0
N_GROUPS = 4
EXPERTS_PER_GROUP = 8
N_EXPERTS = N_GROUPS * EXPERTS_PER_GROUP
PAIRS_PER_GROUP = EXPERTS_PER_GROUP * (EXPERTS_PER_GROUP - 1) // 2
N_CLASSES = N_GROUPS * PAIRS_PER_GROUP
LN_EPS = 1e-5

LANES = 128
SUBLANES = 8
MOE_TILE = 320
VMEM_LIMIT = 56 * 1024 * 1024

NEG_INF = float("-inf")


def _sigmoid(x):
    return 1.0 / (1.0 + jnp.exp(-x))


def _log_sigmoid(x):
    return jnp.minimum(x, 0.0) - jnp.log(1.0 + jnp.exp(-jnp.abs(x)))


def _iota(shape, dim):
    return lax.broadcasted_iota(I32, shape, dim)


def _dot(a, b):
    return jnp.dot(a, b, preferred_element_type=F32)


def _dot_exact(a, b):
    return jnp.dot(a, b, preferred_element_type=F32, precision=lax.Precision.HIGHEST)


def _layer_norm(z, w, b):
    mu = jnp.mean(z, axis=-1, keepdims=True)
    zc = z - mu
    var = jnp.mean(zc * zc, axis=-1, keepdims=True)
    return zc * lax.rsqrt(var + LN_EPS) * w + b


C_QK = 0
C_V = C_QK + 2 * ML_WIDTH
C_O = C_V + ML_WIDTH
C_AQ = C_O + ML_WIDTH
C_AK = C_AQ + ATT_WIDTH
C_AV = C_AK + ATT_KV_HEADS * LANES
C_G = C_AV + ATT_KV_HEADS * LANES
C_END = C_G + 2 * LANES


def _pack_w_in(w_in):
    sizes = (2 * ML_WIDTH, ML_WIDTH, ML_WIDTH, ML_HEADS, ML_HEADS, ATT_WIDTH, ATT_KV_WIDTH, ATT_KV_WIDTH)
    splits = np.cumsum(sizes)[:-1].tolist()
    w_qk, w_v, w_o, w_i, w_f, w_aq, w_ak, w_av = jnp.split(w_in, splits, axis=-1)
    half = ATT_HEAD_DIM // 2

    def head(w, h):
        return w[:, h * ATT_HEAD_DIM:(h + 1) * ATT_HEAD_DIM]

    def q_tile(a, b):
        return [a[:, :half], b[:, :half], a[:, half:], b[:, half:]]

    q_cols = [t for p in range(ATT_Q_HEADS // 2) for t in q_tile(head(w_aq, 2 * p), head(w_aq, 2 * p + 1))]
    k_cols = [t for h in range(ATT_KV_HEADS) for t in q_tile(head(w_ak, h), head(w_ak, h))]
    v_cols = [t for h in range(ATT_KV_HEADS) for t in (head(w_av, h), head(w_av, h))]
    lane_pad = jnp.zeros((w_in.shape[0], LANES - ML_HEADS), w_in.dtype)
    packed = jnp.concatenate([w_qk, w_v, w_o] + q_cols + k_cols + v_cols + [w_i, lane_pad, w_f, lane_pad], axis=-1)
    return packed.astype(BF16)


def _inproj_kernel(x_ref, w_ref, cw_ref, cb_ref, q_ref, k_ref, v_ref, og_ref, aq_ref, ak_ref, av_ref, g_ref,
                   *scratch, blocks_per_seq):
    *ext_refs, xb_ref = scratch
    tm = x_ref.shape[0]
    halo = SUBLANES
    cs = ext_refs[0].shape[1]
    xb_ref[...] = x_ref[...].astype(BF16)

    def mm(lo, hi):
        return _dot(xb_ref[...], w_ref[:, lo:hi])

    @pl.when(pl.program_id(0) % blocks_per_seq == 0)
    def _():
        for ext_ref in ext_refs:
            ext_ref[0:halo, :] = jnp.zeros((halo, cs), F32)

    scale = ML_HEAD_DIM ** -0.5

    def conv_slice(idx):
        ext_ref = ext_refs[idx]
        c0 = idx * cs
        dst_ref, off, mul = (q_ref, c0, 1.0) if c0 < ML_WIDTH else (k_ref, c0 - ML_WIDTH, scale)
        rt = ML_CHUNK
        for r0 in range(0, tm, rt):
            conv = cb_ref[:, c0:c0 + cs]
            for j in range(CONV_WIDTH):
                start = halo + r0 - (CONV_WIDTH - 1) + j
                conv = conv + cw_ref[j:j + 1, c0:c0 + cs] * ext_ref[start:start + rt, :]
            act = conv * _sigmoid(conv)
            dst_ref[r0:r0 + rt, off:off + cs] = (act * mul).astype(BF16)
        ext_ref[0:halo, :] = ext_ref[tm:tm + halo, :]

    for idx in range(len(ext_refs)):
        ext_refs[idx][halo:halo + tm, :] = mm(C_QK + idx * cs, C_QK + (idx + 1) * cs)
        if idx > 0:
            conv_slice(idx - 1)
    half_v = ML_WIDTH // 2
    v_ref[:, 0:half_v] = mm(C_V, C_V + half_v).astype(BF16)
    conv_slice(len(ext_refs) - 1)
    v_ref[:, half_v:] = mm(C_V + half_v, C_O).astype(BF16)
    og_ref[...] = _sigmoid(mm(C_O, C_AQ)).astype(BF16)
    aq_ref[...] = mm(C_AQ, C_AK)
    ak_ref[...] = mm(C_AK, C_AV)
    av_ref[...] = mm(C_AV, C_G).astype(BF16)
    g_ref[...] = mm(C_G, C_END)


def _inproj(x2d, w_packed, conv_w, conv_b, tm, blocks_per_seq):
    t, d = x2d.shape
    widths = (ML_WIDTH, ML_WIDTH, ML_WIDTH, ML_WIDTH, C_AK - C_AQ, C_AV - C_AK, C_G - C_AV, C_END - C_G)
    dtypes = (BF16, BF16, BF16, BF16, F32, F32, BF16, F32)
    kern = functools.partial(_inproj_kernel, blocks_per_seq=blocks_per_seq)
    return pl.pallas_call(
        kern,
        grid=(t // tm,),
        in_specs=[pl.BlockSpec((tm, d), lambda i: (i, 0)),
                  pl.BlockSpec((d, C_END), lambda i: (0, 0)),
                  pl.BlockSpec((CONV_WIDTH, 2 * ML_WIDTH), lambda i: (0, 0)),
                  pl.BlockSpec((1, 2 * ML_WIDTH), lambda i: (0, 0))],
        out_specs=[pl.BlockSpec((tm, w), lambda i: (i, 0)) for w in widths],
        out_shape=[jax.ShapeDtypeStruct((t, w), dt) for w, dt in zip(widths, dtypes)],
        scratch_shapes=[pltpu.VMEM((tm + SUBLANES, 2 * LANES), F32)] * (2 * ML_WIDTH // (2 * LANES))
        + [pltpu.VMEM((tm, d), BF16)],
        compiler_params=pltpu.CompilerParams(dimension_semantics=("arbitrary",),
                                             vmem_limit_bytes=VMEM_LIMIT),
        name="inproj",
    )(x2d, w_packed, conv_w, conv_b)


def _time_scan(x, combine, identity):
    row = _iota(x.shape, 0)
    sh = 1
    while sh < x.shape[0]:
        x = combine(x, jnp.where(row >= sh, pltpu.roll(x, sh, 0), identity))
        sh *= 2
    return x


def _mlstm_kernel(q_ref, k_ref, v_ref, og_ref, g_ref, gb_ref, nw_ref, out_ref, ct_ref, m_ref, *, tq):
    s_idx = pl.program_id(1)
    L = ML_CHUNK
    D = ML_HEAD_DIM
    H = ML_HEADS
    heads = range(H)

    @pl.when(s_idx == 0)
    def _():
        ct_ref[...] = jnp.zeros_like(ct_ref)
        m_ref[...] = jnp.zeros_like(m_ref)

    causal = _iota((L, L), 1) <= _iota((L, L), 0)
    ones_blk = jnp.ones((L, D), BF16)
    mean_blk = jnp.full((D, D), 1.0 / D, BF16)
    m_prev = m_ref[0:1, :]
    head_lanes = _iota((L, LANES), 1) < H
    tile_of_lane = jnp.right_shift(_iota((LANES, H * L), 1), L.bit_length() - 1)
    spread = jnp.where(_iota((LANES, H * L), 0) == tile_of_lane, 1.0, 0.0).astype(BF16)

    def spread_heads(x):
        x = jnp.where(head_lanes, x, 0.0)
        hi = x.astype(BF16)
        lo = (x - hi.astype(F32)).astype(BF16)
        return _dot(hi, spread) + _dot(lo, spread)

    for c in range(tq // L):
        r0 = c * L
        gi = g_ref[r0:r0 + L, 0:LANES] + gb_ref[:, 0:LANES]
        gf = g_ref[r0:r0 + L, LANES:2 * LANES] + gb_ref[:, LANES:2 * LANES]
        b_cum = _time_scan(_log_sigmoid(gf), jnp.add, 0.0)
        r = gi - b_cum
        g = jnp.maximum(m_prev, _time_scan(r, jnp.maximum, NEG_INF))
        g_rep = spread_heads(g)
        b_rep = spread_heads(b_cum)
        b_last = b_cum[L - 1:L, :]
        m_new = jnp.maximum(b_last + m_prev, jnp.max(b_last + r, axis=0, keepdims=True))
        decay = jnp.exp(b_last + m_prev - m_new)
        shift = b_last - m_new
        r_t = r.T

        q_b = [q_ref[r0:r0 + L, h * D:(h + 1) * D] for h in heads]
        kt_b = [k_ref[r0:r0 + L, h * D:(h + 1) * D].T for h in heads]
        v_aug = [jnp.concatenate([v_ref[r0:r0 + L, h * D:(h + 1) * D], ones_blk], axis=-1) for h in heads]
        g_col = [g_rep[:, h * L:(h + 1) * L] for h in heads]
        w_intra = [jnp.exp(jnp.where(causal, r_t[h:h + 1, :] - g_col[h], NEG_INF)) for h in heads]
        s_b = [(_dot(q_b[h], kt_b[h]) * w_intra[h]).astype(BF16) for h in heads]
        ct = [ct_ref[h] for h in heads]
        inter = [_dot(q_b[h], ct[h].astype(BF16)) for h in heads]
        intra = [_dot(s_b[h], v_aug[h]) for h in heads]
        for h in heads:
            wi_col = jnp.exp(m_prev[:, h:h + 1] - g_col[h])
            clamp = jnp.exp(-(b_rep[:, h * L:(h + 1) * L] + g_col[h]))
            num = wi_col * inter[h][:, 0:D] + intra[h][:, 0:D]
            den = wi_col * inter[h][:, D:] + intra[h][:, D:]
            hh = num / jnp.maximum(jnp.abs(den), clamp)
            mu = _dot(hh.astype(BF16), mean_blk)
            hc = hh - mu
            var = _dot((hc * hc).astype(BF16), mean_blk)
            hn = hc * lax.rsqrt(var + LN_EPS) * nw_ref[:, h * D:(h + 1) * D]
            gate_o = og_ref[r0:r0 + L, h * D:(h + 1) * D].astype(F32)
            out_ref[r0:r0 + L, h * D:(h + 1) * D] = (gate_o * hn).astype(out_ref.dtype)
        for h in heads:
            w_row = jnp.exp(r_t[h:h + 1, :] + shift[:, h:h + 1])
            ktw = (kt_b[h].astype(F32) * w_row).astype(BF16)
            ct_ref[h] = decay[:, h:h + 1] * ct[h] + _dot(ktw, v_aug[h])
        m_prev = m_new

    m_ref[...] = jnp.broadcast_to(m_prev, m_ref.shape)


def _mlstm(q, k, v, og, g, gate_bias_row, norm_w_row, tq):
    b, s, _ = q.shape
    kern = functools.partial(_mlstm_kernel, tq=tq)

    def seq_spec(width):
        return pl.BlockSpec((None, tq, width), lambda bi, si: (bi, si, 0))

    def const_spec(shape):
        return pl.BlockSpec(shape, lambda bi, si: (0,) * len(shape))

    return pl.pallas_call(
        kern,
        grid=(b, s // tq),
        in_specs=[seq_spec(ML_WIDTH), seq_spec(ML_WIDTH), seq_spec(ML_WIDTH), seq_spec(ML_WIDTH),
                  seq_spec(2 * LANES), const_spec((1, 2 * LANES)), const_spec((1, ML_WIDTH))],
        out_specs=seq_spec(ML_WIDTH),
        out_shape=jax.ShapeDtypeStruct((b, s, ML_WIDTH), BF16),
        scratch_shapes=[pltpu.VMEM((ML_HEADS, ML_HEAD_DIM, 2 * ML_HEAD_DIM), F32),
                        pltpu.VMEM((SUBLANES, LANES), F32)],
        compiler_params=pltpu.CompilerParams(dimension_semantics=("arbitrary", "arbitrary"),
                                             vmem_limit_bytes=VMEM_LIMIT),
        name="mlstm",
    )(q, k, v, og, g, gate_bias_row, norm_w_row)


def _rope_tables(seq_len):
    half = ATT_HEAD_DIM // 2
    inv_freq = ROPE_THETA ** (-jnp.arange(half, dtype=F32) / half)
    ang = jnp.arange(seq_len, dtype=F32)[:, None] * inv_freq[None, :]
    cos = jnp.cos(ang)
    sin = jnp.sin(ang)
    cos_t = jnp.concatenate([cos, cos, cos, cos], axis=-1)
    sin_t = jnp.concatenate([-sin, -sin, sin, sin], axis=-1)
    return cos_t, sin_t


def _swa_kernel(sink_ref, aq_ref, ak_ref, av_ref, cos_ref, sin_ref, out_ref, kprev_ref, vprev_ref, *, nsub):
    step = pl.program_id(1)
    Lb = WINDOW
    half = ATT_HEAD_DIM // 2
    pairs = ATT_Q_HEADS // 2
    pairs_per_kv = pairs // ATT_KV_HEADS

    @pl.when(step == 0)
    def _():
        kprev_ref[...] = jnp.zeros_like(kprev_ref)
        vprev_ref[...] = jnp.zeros_like(vprev_ref)

    def rope(x, cos, sin):
        tiles = []
        for c in range(x.shape[-1] // LANES):
            xt = x[:, c * LANES:(c + 1) * LANES]
            tiles.append(xt * cos + pltpu.roll(xt, LANES // 2, 1) * sin)
        return jnp.concatenate(tiles, axis=-1)

    ql = _iota((Lb, 2 * Lb), 0)
    kj = _iota((Lb, 2 * Lb), 1)
    diff = Lb + ql - kj
    lane = _iota((Lb, LANES), 1)
    low_half = lane < ATT_HEAD_DIM
    first_head = (lane & half) == 0
    ones_blk = jnp.ones((2 * Lb, LANES), BF16)

    k_prev = kprev_ref
    v_prev = vprev_ref
    for j in range(nsub):
        r0 = j * Lb
        cos = cos_ref[r0:r0 + Lb, :]
        sin = sin_ref[r0:r0 + Lb, :]
        q = rope(aq_ref[r0:r0 + Lb, :], cos, sin) * (ATT_HEAD_DIM ** -0.5)
        k_cur = rope(ak_ref[r0:r0 + Lb, :], cos, sin).astype(BF16)
        v_cur = av_ref[r0:r0 + Lb, :]

        kpos = (step * nsub + j) * Lb + kj - Lb
        visible = jnp.where(diff >= 0, jnp.where(diff < WINDOW, jnp.where(kpos >= 0, 1, 0), 0), 0)
        bias = jnp.where(visible > 0, 0.0, NEG_INF).astype(F32)
        bias = jnp.concatenate([bias] * (2 * pairs_per_kv), axis=0)

        for g in range(ATT_KV_HEADS):
            kk = jnp.concatenate([k_prev[:, g * LANES:(g + 1) * LANES], k_cur[:, g * LANES:(g + 1) * LANES]], axis=0)
            vv = jnp.concatenate([v_prev[:, g * LANES:(g + 1) * LANES], v_cur[:, g * LANES:(g + 1) * LANES]], axis=0)
            vv_aug = jnp.concatenate([vv, ones_blk], axis=-1)
            rows = []
            sinks = []
            for p in range(pairs_per_kv):
                pair = g * pairs_per_kv + p
                q2 = q[:, pair * LANES:(pair + 1) * LANES]
                rows.append(jnp.where(first_head, q2, 0.0))
                rows.append(jnp.where(first_head, 0.0, q2))
                sinks.append(jnp.full((Lb, LANES), sink_ref[2 * pair], F32))
                sinks.append(jnp.full((Lb, LANES), sink_ref[2 * pair + 1], F32))
            qs = jnp.concatenate(rows, axis=0).astype(BF16)
            sink = jnp.concatenate(sinks, axis=0)

            sc = lax.dot_general(qs, kk, (((1,), (1,)), ((), ())), preferred_element_type=F32) + bias
            m = jnp.maximum(jnp.broadcast_to(jnp.max(sc, axis=-1, keepdims=True), sink.shape), sink)
            p_un = jnp.exp(sc - jnp.concatenate([m, m], axis=-1))
            acc = _dot(p_un.astype(BF16), vv_aug)
            o = acc[:, 0:LANES] / (acc[:, LANES:] + jnp.exp(sink - m))
            for p in range(pairs_per_kv):
                pair = g * pairs_per_kv + p
                even = o[(2 * p) * Lb:(2 * p + 1) * Lb, :]
                odd = o[(2 * p + 1) * Lb:(2 * p + 2) * Lb, :]
                out_ref[r0:r0 + Lb, pair * LANES:(pair + 1) * LANES] = (
                    jnp.where(low_half, even, odd).astype(out_ref.dtype))
        k_prev = k_cur
        v_prev = v_cur

    kprev_ref[...] = k_prev
    vprev_ref[...] = v_prev


def _swa(aq, ak, av, cos_t, sin_t, sinks, nsub):
    b, s, _ = aq.shape
    kvw = ATT_KV_HEADS * LANES
    rows = nsub * WINDOW

    def seq_spec(width):
        return pl.BlockSpec((None, rows, width), lambda bi, si: (bi, si, 0))

    tab_spec = pl.BlockSpec((rows, LANES), lambda bi, si: (si, 0))
    return pl.pallas_call(
        functools.partial(_swa_kernel, nsub=nsub),
        grid=(b, s // rows),
        in_specs=[pl.BlockSpec(memory_space=pltpu.SMEM),
                  seq_spec(ATT_WIDTH), seq_spec(kvw), seq_spec(kvw), tab_spec, tab_spec],
        out_specs=seq_spec(ATT_WIDTH),
        out_shape=jax.ShapeDtypeStruct((b, s, ATT_WIDTH), BF16),
        scratch_shapes=[pltpu.VMEM((WINDOW, kvw), BF16), pltpu.VMEM((WINDOW, kvw), BF16)],
        compiler_params=pltpu.CompilerParams(dimension_semantics=("arbitrary", "arbitrary"),
                                             vmem_limit_bytes=VMEM_LIMIT),
        name="swa",
    )(sinks, aq, ak, av, cos_t, sin_t)


def _store_token_tiles(ref, val, row0=0, rows_per_token=None):
    n, w = val.shape
    segs = w // LANES
    rpt = rows_per_token or segs
    for j in range(segs):
        ref[pl.ds(row0 + j, n, stride=rpt), :] = val[:, j * LANES:(j + 1) * LANES]


def _load_token_tiles(ref, n, segs, row0=0, rows_per_token=None):
    rpt = rows_per_token or segs
    return jnp.concatenate([ref[pl.ds(row0 + j, n, stride=rpt), :] for j in range(segs)], axis=-1)


ROUTER_ROWS = 48


def _pack_router(w_group, b_group, w_expert, b_expert):
    d = w_group.shape[0]
    wt = jnp.zeros((ROUTER_ROWS, d), F32)
    wt = wt.at[0:N_GROUPS].set(w_group.T).at[SUBLANES:SUBLANES + N_EXPERTS].set(w_expert.T)
    bias = jnp.zeros((ROUTER_ROWS,), F32)
    bias = bias.at[0:N_GROUPS].set(b_group).at[SUBLANES:SUBLANES + N_EXPERTS].set(b_expert)
    hi = wt.astype(BF16)
    lo = (wt - hi.astype(F32)).astype(BF16)
    return jnp.concatenate([hi, lo], axis=0), jnp.broadcast_to(bias[:, None], (ROUTER_ROWS, LANES))


def _outproj_kernel(x_ref, ml_ref, att_ref, wo_ref, lnw_ref, lnb_ref, wrt_ref, brt_ref,
                    x1t_ref, meta_ref, cnt_ref, *, alpha):
    step = pl.program_id(0)
    tm, d = x_ref.shape
    y = _dot(ml_ref[...], wo_ref[0:ML_WIDTH, :]) + _dot(att_ref[...], wo_ref[ML_WIDTH:, :])
    x1 = _layer_norm(alpha * x_ref[...] + y, lnw_ref[...], lnb_ref[...])
    _store_token_tiles(x1t_ref, x1)

    rr = wrt_ref.shape[0] // 2
    x1_hi = x1.astype(BF16)
    x1_lo = (x1 - x1_hi.astype(F32)).astype(BF16)
    nt = (((1,), (1,)), ((), ()))
    both = lax.dot_general(wrt_ref[...], x1_hi, nt, preferred_element_type=F32)
    cross = lax.dot_general(wrt_ref[0:rr, :], x1_lo, nt, preferred_element_type=F32)
    logits = both[0:rr] + both[rr:2 * rr] + cross + jnp.concatenate([brt_ref[...]] * (tm // LANES), axis=1)

    row = _iota((SUBLANES, tm), 0).astype(F32)

    def first_argmax(vals):
        top = jnp.max(vals, axis=0, keepdims=True)
        idx = jnp.min(jnp.where(vals == top, row, float(SUBLANES)), axis=0, keepdims=True)
        return top, idx

    g_logits = jnp.where(row < N_GROUPS, logits[0:SUBLANES], NEG_INF)
    g_top, g_idx = first_argmax(g_logits)
    g_p = 1.0 / jnp.sum(jnp.exp(g_logits - g_top), axis=0, keepdims=True)

    e_logits = logits[SUBLANES:2 * SUBLANES]
    for grp in range(1, N_GROUPS):
        e_logits = jnp.where(g_idx == grp, logits[(1 + grp) * SUBLANES:(2 + grp) * SUBLANES], e_logits)
    v1, a1 = first_argmax(e_logits)
    v2, a2 = first_argmax(jnp.where(row == a1, NEG_INF, e_logits))
    r = jnp.exp(v2 - v1)
    w1 = g_p / (1.0 + r)
    w2 = g_p * r / (1.0 + r)

    lo = jnp.minimum(a1, a2)
    hi = jnp.maximum(a1, a2)
    w_lo = jnp.where(a1 < a2, w1, w2)
    w_hi = jnp.where(a1 < a2, w2, w1)
    pair_idx = (EXPERTS_PER_GROUP - 1) * lo - lo * (lo - 1.0) * 0.5 + (hi - lo - 1.0)
    cls = g_idx * PAIRS_PER_GROUP + pair_idx

    meta_t = jnp.where(row == 0.0, cls, jnp.where(row == 1.0, w_lo, jnp.where(row == 2.0, w_hi, 0.0)))
    meta = jnp.concatenate([meta_t, jnp.zeros((LANES - SUBLANES, tm), F32)], axis=0).T
    meta_ref[...] = meta

    @pl.when(step == 0)
    def _():
        cnt_ref[...] = jnp.zeros_like(cnt_ref)

    lane = _iota((tm, LANES), 1).astype(F32)
    onehot = jnp.where(lane == meta[:, 0:1], 1.0, 0.0)
    cnt_ref[0:1, :] += jnp.sum(onehot, axis=0, keepdims=True)


def _outproj(x2d, ml2d, att2d, w_out_b, ln_w, ln_b, w_router, b_router, alpha, tm):
    t, d = x2d.shape
    kern = functools.partial(_outproj_kernel, alpha=alpha)

    def const_spec(shape):
        return pl.BlockSpec(shape, lambda i: (0,) * len(shape))

    return pl.pallas_call(
        kern,
        grid=(t // tm,),
        in_specs=[pl.BlockSpec((tm, d), lambda i: (i, 0)),
                  pl.BlockSpec((tm, ML_WIDTH), lambda i: (i, 0)),
                  pl.BlockSpec((tm, ATT_WIDTH), lambda i: (i, 0)),
                  const_spec((ML_WIDTH + ATT_WIDTH, d)), const_spec((1, d)), const_spec((1, d)),
                  const_spec((2 * ROUTER_ROWS, d)), const_spec((ROUTER_ROWS, LANES))],
        out_specs=[pl.BlockSpec((tm * (d // LANES), LANES), lambda i: (i, 0)),
                   pl.BlockSpec((tm, LANES), lambda i: (i, 0)), const_spec((SUBLANES, LANES))],
        out_shape=[jax.ShapeDtypeStruct((t * (d // LANES), LANES), F32),
                   jax.ShapeDtypeStruct((t, LANES), F32),
                   jax.ShapeDtypeStruct((SUBLANES, LANES), F32)],
        compiler_params=pltpu.CompilerParams(dimension_semantics=("arbitrary",),
                                             vmem_limit_bytes=VMEM_LIMIT),
        name="outproj",
    )(x2d, ml2d, att2d, w_out_b, ln_w, ln_b, w_router, b_router)


def _class_expert_table():
    tab = np.zeros((SUBLANES, LANES), np.float32)
    for g in range(N_GROUPS):
        idx = 0
        for lo in range(EXPERTS_PER_GROUP):
            for hi in range(lo + 1, EXPERTS_PER_GROUP):
                c = g * PAIRS_PER_GROUP + idx
                tab[0, c] = g * EXPERTS_PER_GROUP + lo
                tab[1, c] = g * EXPERTS_PER_GROUP + hi
                idx += 1
    return tab


def _rank_kernel(meta_ref, cnt_ref, tab_ref, pos_ref, tile_ref, base_ref, run_ref, *, tb, n_tiles_pad):
    step = pl.program_id(0)
    lane8 = _iota((SUBLANES, LANES), 1)

    @pl.when(step == 0)
    def _():
        cnt = jnp.broadcast_to(cnt_ref[0:1, :], (SUBLANES, LANES))
        tiles = jnp.floor((cnt + (MOE_TILE - 1.0)) * (1.0 / MOE_TILE))
        cum = tiles
        sh = 1
        while sh < LANES:
            cum = cum + jnp.where(lane8 >= sh, pltpu.roll(cum, sh, 1), 0.0)
            sh *= 2
        excl = cum - tiles
        base_ref[...] = excl * MOE_TILE
        run_ref[...] = jnp.zeros_like(run_ref)

        ti = _iota((n_tiles_pad, LANES), 0).astype(F32)
        lane = _iota((n_tiles_pad, LANES), 1)
        done = jnp.where(lane < N_CLASSES, jnp.where(cum[0:1, :] <= ti, 1.0, 0.0), 0.0)
        t_cls = jnp.sum(done, axis=-1, keepdims=True)
        sel = jnp.where(lane.astype(F32) == t_cls, 1.0, 0.0)
        cnt_i = jnp.sum(sel * cnt[0:1, :], axis=-1, keepdims=True)
        first_i = jnp.sum(sel * excl[0:1, :], axis=-1, keepdims=True)
        rows_i = jnp.clip(cnt_i - MOE_TILE * (ti[:, 0:1] - first_i), 0.0, float(MOE_TILE))
        e_lo = jnp.sum(sel * tab_ref[0:1, :], axis=-1, keepdims=True)
        e_hi = jnp.sum(sel * tab_ref[1:2, :], axis=-1, keepdims=True)
        n_tiles = jnp.sum(jnp.where(lane < N_CLASSES, jnp.broadcast_to(tiles[0:1, :], (n_tiles_pad, LANES)), 0.0),
                          axis=-1, keepdims=True)
        info = jnp.where(lane == 0, rows_i,
                         jnp.where(lane == 1, e_lo, jnp.where(lane == 2, e_hi, jnp.where(lane == 3, n_tiles, 0.0))))
        tile_ref[...] = info.astype(I32)

    cls = meta_ref[:, 0:1]
    lane = _iota((tb, LANES), 1).astype(F32)
    onehot = jnp.where(lane == cls, 1.0, 0.0)
    strict_lower = jnp.where(_iota((tb, tb), 1) < _iota((tb, tb), 0), 1.0, 0.0).astype(BF16)
    before = _dot(strict_lower, onehot.astype(BF16))
    slot = jnp.sum(onehot * (before + run_ref[0:1, :] + base_ref[0:1, :]), axis=-1, keepdims=True)
    run_ref[...] = run_ref[...] + jnp.sum(onehot, axis=0, keepdims=True)
    slot_t = jnp.broadcast_to(slot, (tb, LANES)).T
    pos_ref[...] = slot_t[0:SUBLANES, :].astype(I32)


def _rank(meta, counts, tb, n_tiles_pad):
    t = meta.shape[0]
    kern = functools.partial(_rank_kernel, tb=tb, n_tiles_pad=n_tiles_pad)
    tab = jnp.asarray(_class_expert_table())
    return pl.pallas_call(
        kern,
        grid=(t // tb,),
        in_specs=[pl.BlockSpec((tb, LANES), lambda i: (i, 0)),
                  pl.BlockSpec((SUBLANES, LANES), lambda i: (0, 0)),
                  pl.BlockSpec((SUBLANES, LANES), lambda i: (0, 0))],
        out_specs=[pl.BlockSpec((SUBLANES, tb), lambda i: (0, i)),
                   pl.BlockSpec((n_tiles_pad, LANES), lambda i: (0, 0))],
        out_shape=[jax.ShapeDtypeStruct((SUBLANES, t), I32),
                   jax.ShapeDtypeStruct((n_tiles_pad, LANES), I32)],
        scratch_shapes=[pltpu.VMEM((SUBLANES, LANES), F32), pltpu.VMEM((SUBLANES, LANES), F32)],
        compiler_params=pltpu.CompilerParams(dimension_semantics=("arbitrary",),
                                             vmem_limit_bytes=VMEM_LIMIT),
        name="rank",
    )(meta, counts, tab)


DMA_UNROLL = 8


def _dispatch_kernel(pos_ref, src_ref, zeros_ref, dst_ref, sem, *, tb, rpt):
    del zeros_ref
    base = pl.program_id(0) * tb

    def issue(grp, carry):
        for u in range(DMA_UNROLL):
            j = grp * DMA_UNROLL + u
            src = src_ref.at[pl.ds(pl.multiple_of(j * rpt, rpt), rpt)]
            dst = dst_ref.at[pl.ds(pl.multiple_of(pos_ref[base + j] * rpt, rpt), rpt)]
            pltpu.make_async_copy(src, dst, sem).start(priority=u % 2)
        return carry

    lax.fori_loop(0, tb // DMA_UNROLL, issue, 0)
    pltpu.make_async_copy(src_ref, dst_ref.at[pl.ds(0, tb * rpt)], sem).wait()


def _dispatch(pos, src, n_slots, rpt, tb):
    t = pos.shape[0]
    kern = functools.partial(_dispatch_kernel, tb=tb, rpt=rpt)
    return pl.pallas_call(
        kern,
        grid_spec=pltpu.PrefetchScalarGridSpec(
            num_scalar_prefetch=1,
            grid=(t // tb,),
            in_specs=[pl.BlockSpec((tb * rpt, LANES), lambda i, pos_ref: (i, 0)),
                      pl.BlockSpec(memory_space=pl.ANY)],
            out_specs=pl.BlockSpec(memory_space=pl.ANY),
            scratch_shapes=[pltpu.SemaphoreType.DMA(())],
        ),
        out_shape=jax.ShapeDtypeStruct((n_slots * rpt, LANES), src.dtype),
        input_output_aliases={2: 0},
        compiler_params=pltpu.CompilerParams(dimension_semantics=("arbitrary",),
                                             has_side_effects=True, vmem_limit_bytes=VMEM_LIMIT),
        name="dispatch",
    )(pos, src, jnp.zeros((n_slots * rpt, LANES), src.dtype))


def _collect_kernel(pos_ref, ys_ref, x1t_ref, meta_ref, lnw_ref, lnb_ref, out_ref, buf_ref, sems,
                    *, tb, alpha, d):
    segs = d // LANES
    rpt = 2 * segs
    step = pl.program_id(0)
    n_steps = pl.num_programs(0)

    def gather(blk, slot):
        base = blk * tb

        def issue(grp, carry):
            for u in range(DMA_UNROLL):
                j = grp * DMA_UNROLL + u
                src = ys_ref.at[pl.ds(pos_ref[base + j], 1)]
                dst = buf_ref.at[slot, pl.ds(j, 1)]
                pltpu.make_async_copy(src, dst, sems.at[slot]).start(priority=u % 2)
            return carry

        lax.fori_loop(0, tb // DMA_UNROLL, issue, 0)

    @pl.when(step == 0)
    def _():
        gather(0, 0)

    @pl.when(step + 1 < n_steps)
    def _():
        gather(step + 1, (step + 1) % 2)

    slot = step % 2
    pltpu.make_async_copy(ys_ref.at[pl.ds(0, tb)], buf_ref.at[slot], sems.at[slot]).wait()

    x1 = _load_token_tiles(x1t_ref, tb, segs)
    y_lo = buf_ref[slot, :, 0:d]
    y_hi = buf_ref[slot, :, d:2 * d]
    meta = meta_ref[...]
    z = alpha * x1 + meta[:, 1:2] * y_lo + meta[:, 2:3] * y_hi
    out_ref[...] = _layer_norm(z, lnw_ref[...], lnb_ref[...])


def _collect(pos, ys, x1t, meta, ln_w, ln_b, alpha, d, tb):
    t = pos.shape[0]
    segs = d // LANES
    kern = functools.partial(_collect_kernel, tb=tb, alpha=alpha, d=d)
    return pl.pallas_call(
        kern,
        grid_spec=pltpu.PrefetchScalarGridSpec(
            num_scalar_prefetch=1,
            grid=(t // tb,),
            in_specs=[pl.BlockSpec(memory_space=pl.ANY),
                      pl.BlockSpec((tb * segs, LANES), lambda i, pos_ref: (i, 0)),
                      pl.BlockSpec((tb, LANES), lambda i, pos_ref: (i, 0)),
                      pl.BlockSpec((1, d), lambda i, pos_ref: (0, 0)),
                      pl.BlockSpec((1, d), lambda i, pos_ref: (0, 0))],
            out_specs=pl.BlockSpec((tb, d), lambda i, pos_ref: (i, 0)),
            scratch_shapes=[pltpu.VMEM((2, tb, 2 * d), F32), pltpu.SemaphoreType.DMA((2,))],
        ),
        out_shape=jax.ShapeDtypeStruct((t, d), F32),
        compiler_params=pltpu.CompilerParams(dimension_semantics=("arbitrary",),
                                             vmem_limit_bytes=VMEM_LIMIT),
        name="collect",
    )(pos, ys, x1t, meta, ln_w, ln_b)


MOE_TILES_PER_STEP = 1


def _moe_kernel(rows_ref, elo_ref, ehi_ref, nt_ref, xs_ref, *refs, d):
    *w_refs, ys_ref = refs
    i = pl.program_id(0)
    segs = d // LANES

    @pl.when(rows_ref[i * MOE_TILES_PER_STEP] > 0)
    def _():
        for k in range(MOE_TILES_PER_STEP):
            wgu_lo, wd_lo, wgu_hi, wd_hi = w_refs[4 * k:4 * k + 4]
            xs_k = xs_ref.at[pl.ds(k * MOE_TILE * segs, MOE_TILE * segs)]
            xb = _load_token_tiles(xs_k, MOE_TILE, segs).astype(BF16)

            def expert(wgu_ref, wd_ref):
                de = wd_ref.shape[0]
                gate_up = _dot(xb, wgu_ref[...])
                gate = gate_up[:, 0:de]
                hidden = gate * _sigmoid(gate) * gate_up[:, de:]
                return _dot(hidden.astype(BF16), wd_ref[...])

            ys_ref[k * MOE_TILE:(k + 1) * MOE_TILE, 0:d] = expert(wgu_lo, wd_lo)
            ys_ref[k * MOE_TILE:(k + 1) * MOE_TILE, d:2 * d] = expert(wgu_hi, wd_hi)

    @pl.when(rows_ref[i * MOE_TILES_PER_STEP] <= 0)
    def _():
        ys_ref[...] = jnp.zeros_like(ys_ref)


def _moe(info, xs, wgu, wd, d, n_tiles_pad):
    de = wd.shape[1]
    segs = d // LANES
    tps = MOE_TILES_PER_STEP
    kern = functools.partial(_moe_kernel, d=d)

    def last_live(tile, nt_ref):
        return jnp.minimum(tile, jnp.maximum(nt_ref[0] - 1, 0))

    def up_spec(k, which):
        return pl.BlockSpec((None, d, 2 * de),
                            lambda i, rows, elo, ehi, nt: ((elo, ehi)[which][last_live(i * tps + k, nt)], 0, 0))

    def down_spec(k, which):
        return pl.BlockSpec((None, de, d),
                            lambda i, rows, elo, ehi, nt: ((elo, ehi)[which][last_live(i * tps + k, nt)], 0, 0))

    w_specs = []
    for k in range(tps):
        w_specs += [up_spec(k, 0), down_spec(k, 0), up_spec(k, 1), down_spec(k, 1)]
    rows, elo, ehi, nt = info[:, 0], info[:, 1], info[:, 2], info[0:1, 3]
    return pl.pallas_call(
        kern,
        grid_spec=pltpu.PrefetchScalarGridSpec(
            num_scalar_prefetch=4,
            grid=(n_tiles_pad // tps,),
            in_specs=[pl.BlockSpec((tps * MOE_TILE * segs, LANES),
                                   lambda i, rows, elo, ehi, nt: (last_live(i * tps, nt) // tps, 0))] + w_specs,
            out_specs=pl.BlockSpec((tps * MOE_TILE, 2 * d), lambda i, rows, elo, ehi, nt: (i, 0)),
        ),
        out_shape=jax.ShapeDtypeStruct((n_tiles_pad * MOE_TILE, 2 * d), F32),
        compiler_params=pltpu.CompilerParams(dimension_semantics=("arbitrary",),
                                             vmem_limit_bytes=VMEM_LIMIT),
        name="moe",
    )(rows, elo, ehi, nt, xs, *([wgu, wd, wgu, wd] * tps))


def _pick_block(n, target):
    blk = min(n, target)
    while n % blk:
        blk //= 2
    return blk


def kernel(x, w_in, conv_w, conv_b, mlstm_gate_bias, mlstm_norm_w, attn_sinks, w_out, ln1_w, ln1_b,
           w_group_router, b_group_router, w_expert_router, b_expert_router,
           w_exp_gate, w_exp_up, w_exp_down, ln2_w, ln2_b):
    b, s, d = x.shape
    t = b * s
    depth = w_in.shape[0]
    alpha = (2.0 * depth) ** 0.25
    assert s % ML_CHUNK == 0 and s % WINDOW == 0 and d % LANES == 0

    tm = _pick_block(t, 512)
    tq = _pick_block(s, 4 * ML_CHUNK)
    tb_rank = _pick_block(t, 512)
    tb_dma = _pick_block(t, 2048)
    tb_col = _pick_block(t, 256)
    n_tiles_pad = -(-(t // MOE_TILE + N_CLASSES) // SUBLANES) * SUBLANES
    cos_t, sin_t = _rope_tables(s)

    for l in range(depth):
        x2d = x.reshape(t, d)
        q, k, v, og, aq, ak, av, g = _inproj(x2d, _pack_w_in(w_in[l]), conv_w[l], conv_b[l][None, :], tm, s // tm)
        bias_pad = jnp.zeros((LANES - ML_HEADS,), F32)
        gate_bias_row = jnp.concatenate(
            [mlstm_gate_bias[l, 0], bias_pad, mlstm_gate_bias[l, 1], bias_pad])[None, :]
        ml = _mlstm(q.reshape(b, s, -1), k.reshape(b, s, -1), v.reshape(b, s, -1), og.reshape(b, s, -1),
                    g.reshape(b, s, -1), gate_bias_row, mlstm_norm_w[l][None, :], tq)
        att = _swa(aq.reshape(b, s, -1), ak.reshape(b, s, -1), av.reshape(b, s, -1), cos_t, sin_t, attn_sinks[l],
                   1)

        w_router, b_router = _pack_router(w_group_router[l], b_group_router[l],
                                          w_expert_router[l], b_expert_router[l])
        x1t, meta, counts = _outproj(x2d, ml.reshape(t, -1), att.reshape(t, -1), w_out[l].astype(BF16),
                                     ln1_w[l][None, :], ln1_b[l][None, :], w_router, b_router, alpha, tm)

        pos2d, info = _rank(meta, counts, tb_rank, n_tiles_pad)
        pos = pos2d[0]
        xs = _dispatch(pos, x1t, n_tiles_pad * MOE_TILE, d // LANES, tb_dma)
        w_gate_up = jnp.concatenate([w_exp_gate[l].astype(BF16), w_exp_up[l].astype(BF16)], axis=-1)
        ys = _moe(info, xs, w_gate_up, w_exp_down[l].astype(BF16), d, n_tiles_pad)
        out = _collect(pos, ys, x1t, meta, ln2_w[l][None, :], ln2_b[l][None, :], alpha, d, tb_col)
        x = out.reshape(b, s, d)
    return x
```

```python
import functools
import math

import numpy as np
import jax
import jax.numpy as jnp
from jax import lax
from jax.experimental import pallas as pl
from jax.experimental.pallas import tpu as pltpu

F32 = jnp.float32
BF16 = jnp.bfloat16
I32 = jnp.int32

ML_HEADS = 4
ML_HEAD_DIM = 128
ML_WIDTH = ML_HEADS * ML_HEAD_DIM
ML_CHUNK = 128
CONV_WIDTH = 4
ATT_Q_HEADS = 8
ATT_KV_HEADS = 2
ATT_HEAD_DIM = 64
ATT_WIDTH = ATT_Q_HEADS * ATT_HEAD_DIM
ATT_KV_WIDTH = ATT_KV_HEADS * ATT_HEAD_DIM
WINDOW = 128
ROPE_THETA = 10000.0
N_GROUPS = 4
EXPERTS_PER_GROUP = 8
N_EXPERTS = N_GROUPS * EXPERTS_PER_GROUP
PAIRS_PER_GROUP = EXPERTS_PER_GROUP * (EXPERTS_PER_GROUP - 1) // 2
N_CLASSES = N_GROUPS * PAIRS_PER_GROUP
LN_EPS = 1e-5

LANES = 128
SUBLANES = 8
MOE_TILE = 320
VMEM_LIMIT = 56 * 1024 * 1024

NEG_INF = float("-inf")


def _sigmoid(x):
    return 1.0 / (1.0 + jnp.exp(-x))


def _log_sigmoid(x):
    return jnp.minimum(x, 0.0) - jnp.log(1.0 + jnp.exp(-jnp.abs(x)))


def _iota(shape, dim):
    return lax.broadcasted_iota(I32, shape, dim)


def _dot(a, b):
    return jnp.dot(a, b, preferred_element_type=F32)


def _dot_exact(a, b):
    return jnp.dot(a, b, preferred_element_type=F32, precision=lax.Precision.HIGHEST)


def _layer_norm(z, w, b):
    mu = jnp.mean(z, axis=-1, keepdims=True)
    zc = z - mu
    var = jnp.mean(zc * zc, axis=-1, keepdims=True)
    return zc * lax.rsqrt(var + LN_EPS) * w + b


C_QK = 0
C_V = C_QK + 2 * ML_WIDTH
C_O = C_V + ML_WIDTH
C_AQ = C_O + ML_WIDTH
C_AK = C_AQ + ATT_WIDTH
C_AV = C_AK + ATT_KV_HEADS * LANES
C_G = C_AV + ATT_KV_HEADS * LANES
C_END = C_G + 2 * LANES


def _pack_w_in(w_in):
    sizes = (2 * ML_WIDTH, ML_WIDTH, ML_WIDTH, ML_HEADS, ML_HEADS, ATT_WIDTH, ATT_KV_WIDTH, ATT_KV_WIDTH)
    splits = np.cumsum(sizes)[:-1].tolist()
    w_qk, w_v, w_o, w_i, w_f, w_aq, w_ak, w_av = jnp.split(w_in, splits, axis=-1)
    half = ATT_HEAD_DIM // 2

    def head(w, h):
        return w[:, h * ATT_HEAD_DIM:(h + 1) * ATT_HEAD_DIM]

    def q_tile(a, b):
        return [a[:, :half], b[:, :half], a[:, half:], b[:, half:]]

    q_cols = [t for p in range(ATT_Q_HEADS // 2) for t in q_tile(head(w_aq, 2 * p), head(w_aq, 2 * p + 1))]
    k_cols = [t for h in range(ATT_KV_HEADS) for t in q_tile(head(w_ak, h), head(w_ak, h))]
    v_cols = [t for h in range(ATT_KV_HEADS) for t in (head(w_av, h), head(w_av, h))]
    lane_pad = jnp.zeros((w_in.shape[0], LANES - ML_HEADS), w_in.dtype)
    packed = jnp.concatenate([w_qk, w_v, w_o] + q_cols + k_cols + v_cols + [w_i, lane_pad, w_f, lane_pad], axis=-1)
    return packed.astype(BF16)


def _inproj_kernel(x_ref, w_ref, cw_ref, cb_ref, q_ref, k_ref, v_ref, og_ref, aq_ref, ak_ref, av_ref, g_ref,
                   *scratch, blocks_per_seq):
    *ext_refs, xb_ref = scratch
    tm = x_ref.shape[0]
    halo = SUBLANES
    cs = ext_refs[0].shape[1]
    xb_ref[...] = x_ref[...].astype(BF16)

    @pl.when(pl.program_id(0) % blocks_per_seq == 0)
    def _():
        for ext_ref in ext_refs:
            ext_ref[0:halo, :] = jnp.zeros((halo, cs), F32)

    scale = ML_HEAD_DIM ** -0.5

    rt = ML_CHUNK
    row_pieces = range(0, tm, rt)

    def conv_piece(idx, r0):
        ext_ref = ext_refs[idx]
        c0 = idx * cs
        is_q = c0 < ML_WIDTH
        dst_ref, off = (q_ref, c0) if is_q else (k_ref, c0 - ML_WIDTH)
        conv = cb_ref[:, c0:c0 + cs]
        for j in range(CONV_WIDTH):
            start = halo + r0 - (CONV_WIDTH - 1) + j
            conv = conv + cw_ref[j:j + 1, c0:c0 + cs] * ext_ref[start:start + rt, :]
        act = conv * _sigmoid(conv)
        dst_ref[r0:r0 + rt, off:off + cs] = (act if is_q else act * scale).astype(BF16)
        if r0 + rt == tm:
            ext_ref[0:halo, :] = ext_ref[tm:tm + halo, :]

    def mm_piece(r0, lo, hi):
        return _dot(xb_ref[r0:r0 + rt, :], w_ref[:, lo:hi])

    matmuls = []
    for idx in range(len(ext_refs)):
        for r0 in row_pieces:
            def qk_piece(idx=idx, r0=r0):
                ext_refs[idx][halo + r0:halo + r0 + rt, :] = mm_piece(r0, C_QK + idx * cs, C_QK + (idx + 1) * cs)
                return (idx, r0)
            matmuls.append(qk_piece)

    def plain(dst_ref, lo, width, post=None):
        for c0 in range(0, width, cs):
            w = min(cs, width - c0)
            for r0 in row_pieces:
                def piece(c0=c0, w=w, r0=r0):
                    val = mm_piece(r0, lo + c0, lo + c0 + w)
                    dst_ref[r0:r0 + rt, c0:c0 + w] = (post(val) if post else val).astype(dst_ref.dtype)
                    return None
                matmuls.append(piece)

    plain(v_ref, C_V, ML_WIDTH)
    plain(og_ref, C_O, ML_WIDTH, _sigmoid)
    plain(aq_ref, C_AQ, C_AK - C_AQ)
    plain(ak_ref, C_AK, C_AV - C_AK)
    plain(av_ref, C_AV, C_G - C_AV)
    plain(g_ref, C_G, C_END - C_G)

    n_conv = len(ext_refs) * len(row_pieces)
    gap = max((len(matmuls) - 1) // n_conv, 1)
    ready = []
    for n, piece in enumerate(matmuls):
        done = piece()
        if done is not None:
            ready.append(done)
        if ready and n % gap == gap - 1:
            conv_piece(*ready.pop(0))
    while ready:
        conv_piece(*ready.pop(0))


def _inproj(x2d, w_packed, conv_w, conv_b, tm, blocks_per_seq):
    t, d = x2d.shape
    widths = (ML_WIDTH, ML_WIDTH, ML_WIDTH, ML_WIDTH, C_AK - C_AQ, C_AV - C_AK, C_G - C_AV, C_END - C_G)
    dtypes = (BF16, BF16, BF16, BF16, F32, F32, BF16, F32)
    kern = functools.partial(_inproj_kernel, blocks_per_seq=blocks_per_seq)
    return pl.pallas_call(
        kern,
        grid=(t // tm,),
        in_specs=[pl.BlockSpec((tm, d), lambda i: (i, 0)),
                  pl.BlockSpec((d, C_END), lambda i: (0, 0)),
                  pl.BlockSpec((CONV_WIDTH, 2 * ML_WIDTH), lambda i: (0, 0)),
                  pl.BlockSpec((1, 2 * ML_WIDTH), lambda i: (0, 0))],
        out_specs=[pl.BlockSpec((tm, w), lambda i: (i, 0)) for w in widths],
        out_shape=[jax.ShapeDtypeStruct((t, w), dt) for w, dt in zip(widths, dtypes)],
        scratch_shapes=[pltpu.VMEM((tm + SUBLANES, 2 * LANES), F32)] * (2 * ML_WIDTH // (2 * LANES))
        + [pltpu.VMEM((tm, d), BF16)],
        compiler_params=pltpu.CompilerParams(dimension_semantics=("arbitrary",),
                                             vmem_limit_bytes=VMEM_LIMIT),
        name="inproj",
    )(x2d, w_packed, conv_w, conv_b)


def _time_scan(x, combine, identity):
    row = _iota(x.shape, 0)
    sh = 1
    while sh < x.shape[0]:
        x = combine(x, jnp.where(row >= sh, pltpu.roll(x, sh, 0), identity))
        sh *= 2
    return x


def _mlstm_kernel(q_ref, k_ref, v_ref, og_ref, g_ref, gb_ref, nw_ref, out_ref, ct_ref, m_ref, *, tq):
    s_idx = pl.program_id(1)
    L = ML_CHUNK
    D = ML_HEAD_DIM
    H = ML_HEADS
    heads = range(H)

    @pl.when(s_idx == 0)
    def _():
        ct_ref[...] = jnp.zeros_like(ct_ref)
        m_ref[...] = jnp.zeros_like(m_ref)

    causal = _iota((L, L), 1) <= _iota((L, L), 0)
    ones_blk = jnp.ones((L, D), BF16)
    mean_blk = jnp.full((D, D), 1.0 / D, BF16)
    m_prev = m_ref[0:1, :]
    head_lanes = _iota((L, LANES), 1) < H
    tile_of_lane = jnp.right_shift(_iota((LANES, H * L), 1), L.bit_length() - 1)
    spread = jnp.where(_iota((LANES, H * L), 0) == tile_of_lane, 1.0, 0.0).astype(BF16)

    def spread_heads(x):
        x = jnp.where(head_lanes, x, 0.0)
        hi = x.astype(BF16)
        lo = (x - hi.astype(F32)).astype(BF16)
        return _dot(hi, spread) + _dot(lo, spread)

    for c in range(tq // L):
        r0 = c * L
        gi = g_ref[r0:r0 + L, 0:LANES] + gb_ref[:, 0:LANES]
        gf = g_ref[r0:r0 + L, LANES:2 * LANES] + gb_ref[:, LANES:2 * LANES]
        b_cum = _time_scan(_log_sigmoid(gf), jnp.add, 0.0)
        r = gi - b_cum
        g = jnp.maximum(m_prev, _time_scan(r, jnp.maximum, NEG_INF))
        g_rep = spread_heads(g)
        b_rep = spread_heads(b_cum)
        b_last = b_cum[L - 1:L, :]
        m_new = jnp.maximum(b_last + m_prev, jnp.max(b_last + r, axis=0, keepdims=True))
        decay = jnp.exp(b_last + m_prev - m_new)
        shift = b_last - m_new
        r_t = r.T

        q_b = [q_ref[r0:r0 + L, h * D:(h + 1) * D] for h in heads]
        kt_b = [k_ref[r0:r0 + L, h * D:(h + 1) * D].T for h in heads]
        v_aug = [jnp.concatenate([v_ref[r0:r0 + L, h * D:(h + 1) * D], ones_blk], axis=-1) for h in heads]
        g_col = [g_rep[:, h * L:(h + 1) * L] for h in heads]
        w_intra = [jnp.exp(jnp.where(causal, r_t[h:h + 1, :] - g_col[h], NEG_INF)) for h in heads]
        s_b = [(_dot(q_b[h], kt_b[h]) * w_intra[h]).astype(BF16) for h in heads]
        ct = [ct_ref[h] for h in heads]
        inter = [_dot(q_b[h], ct[h].astype(BF16)) for h in heads]
        intra = [_dot(s_b[h], v_aug[h]) for h in heads]
        for h in heads:
            wi_col = jnp.exp(m_prev[:, h:h + 1] - g_col[h])
            clamp = jnp.exp(-(b_rep[:, h * L:(h + 1) * L] + g_col[h]))
            num = wi_col * inter[h][:, 0:D] + intra[h][:, 0:D]
            den = wi_col * inter[h][:, D:] + intra[h][:, D:]
            hh = num / jnp.maximum(jnp.abs(den), clamp)
            mu = _dot(hh.astype(BF16), mean_blk)
            hc = hh - mu
            var = _dot((hc * hc).astype(BF16), mean_blk)
            hn = hc * lax.rsqrt(var + LN_EPS) * nw_ref[:, h * D:(h + 1) * D]
            gate_o = og_ref[r0:r0 + L, h * D:(h + 1) * D].astype(F32)
            out_ref[r0:r0 + L, h * D:(h + 1) * D] = (gate_o * hn).astype(out_ref.dtype)
        for h in heads:
            w_row = jnp.exp(r_t[h:h + 1, :] + shift[:, h:h + 1])
            ktw = (kt_b[h].astype(F32) * w_row).astype(BF16)
            ct_ref[h] = decay[:, h:h + 1] * ct[h] + _dot(ktw, v_aug[h])
        m_prev = m_new

    m_ref[...] = jnp.broadcast_to(m_prev, m_ref.shape)


def _mlstm(q, k, v, og, g, gate_bias_row, norm_w_row, tq):
    b, s, _ = q.shape
    kern = functools.partial(_mlstm_kernel, tq=tq)

    def seq_spec(width):
        return pl.BlockSpec((None, tq, width), lambda bi, si: (bi, si, 0))

    def const_spec(shape):
        return pl.BlockSpec(shape, lambda bi, si: (0,) * len(shape))

    return pl.pallas_call(
        kern,
        grid=(b, s // tq),
        in_specs=[seq_spec(ML_WIDTH), seq_spec(ML_WIDTH), seq_spec(ML_WIDTH), seq_spec(ML_WIDTH),
                  seq_spec(2 * LANES), const_spec((1, 2 * LANES)), const_spec((1, ML_WIDTH))],
        out_specs=seq_spec(ML_WIDTH),
        out_shape=jax.ShapeDtypeStruct((b, s, ML_WIDTH), BF16),
        scratch_shapes=[pltpu.VMEM((ML_HEADS, ML_HEAD_DIM, 2 * ML_HEAD_DIM), F32),
                        pltpu.VMEM((SUBLANES, LANES), F32)],
        compiler_params=pltpu.CompilerParams(dimension_semantics=("arbitrary", "arbitrary"),
                                             vmem_limit_bytes=VMEM_LIMIT),
        name="mlstm",
    )(q, k, v, og, g, gate_bias_row, norm_w_row)


def _rope_tables(seq_len):
    half = ATT_HEAD_DIM // 2
    inv_freq = ROPE_THETA ** (-jnp.arange(half, dtype=F32) / half)
    ang = jnp.arange(seq_len, dtype=F32)[:, None] * inv_freq[None, :]
    cos = jnp.cos(ang)
    sin = jnp.sin(ang)
    cos_t = jnp.concatenate([cos, cos, cos, cos], axis=-1)
    sin_t = jnp.concatenate([-sin, -sin, sin, sin], axis=-1)
    return cos_t, sin_t


def _swa_kernel(sink_ref, aq_ref, ak_ref, av_ref, cos_ref, sin_ref, out_ref, kprev_ref, vprev_ref, *, nsub):
    step = pl.program_id(1)
    Lb = WINDOW
    half = ATT_HEAD_DIM // 2
    pairs = ATT_Q_HEADS // 2
    pairs_per_kv = pairs // ATT_KV_HEADS

    @pl.when(step == 0)
    def _():
        kprev_ref[...] = jnp.zeros_like(kprev_ref)
        vprev_ref[...] = jnp.zeros_like(vprev_ref)

    def rope(x, cos, sin):
        tiles = []
        for c in range(x.shape[-1] // LANES):
            xt = x[:, c * LANES:(c + 1) * LANES]
            tiles.append(xt * cos + pltpu.roll(xt, LANES // 2, 1) * sin)
        return jnp.concatenate(tiles, axis=-1)

    ql = _iota((Lb, 2 * Lb), 0)
    kj = _iota((Lb, 2 * Lb), 1)
    diff = Lb + ql - kj
    lane = _iota((Lb, LANES), 1)
    low_half = lane < ATT_HEAD_DIM
    first_head = (lane & half) == 0
    ones_blk = jnp.ones((2 * Lb, LANES), BF16)

    k_prev = kprev_ref
    v_prev = vprev_ref
    for j in range(nsub):
        r0 = j * Lb
        cos = cos_ref[r0:r0 + Lb, :]
        sin = sin_ref[r0:r0 + Lb, :]
        q = rope(aq_ref[r0:r0 + Lb, :], cos, sin) * (ATT_HEAD_DIM ** -0.5)
        k_cur = rope(ak_ref[r0:r0 + Lb, :], cos, sin).astype(BF16)
        v_cur = av_ref[r0:r0 + Lb, :]

        kpos = (step * nsub + j) * Lb + kj - Lb
        visible = jnp.where(diff >= 0, jnp.where(diff < WINDOW, jnp.where(kpos >= 0, 1, 0), 0), 0)
        bias = jnp.where(visible > 0, 0.0, NEG_INF).astype(F32)
        bias = jnp.concatenate([bias] * (2 * pairs_per_kv), axis=0)

        for g in range(ATT_KV_HEADS):
            kk = jnp.concatenate([k_prev[:, g * LANES:(g + 1) * LANES], k_cur[:, g * LANES:(g + 1) * LANES]], axis=0)
            vv = jnp.concatenate([v_prev[:, g * LANES:(g + 1) * LANES], v_cur[:, g * LANES:(g + 1) * LANES]], axis=0)
            vv_aug = jnp.concatenate([vv, ones_blk], axis=-1)
            rows = []
            sinks = []
            for p in range(pairs_per_kv):
                pair = g * pairs_per_kv + p
                q2 = q[:, pair * LANES:(pair + 1) * LANES]
                rows.append(jnp.where(first_head, q2, 0.0))
                rows.append(jnp.where(first_head, 0.0, q2))
                sinks.append(jnp.full((Lb, LANES), sink_ref[2 * pair], F32))
                sinks.append(jnp.full((Lb, LANES), sink_ref[2 * pair + 1], F32))
            qs = jnp.concatenate(rows, axis=0).astype(BF16)
            sink = jnp.concatenate(sinks, axis=0)

            sc = lax.dot_general(qs, kk, (((1,), (1,)), ((), ())), preferred_element_type=F32) + bias
            m = jnp.maximum(jnp.broadcast_to(jnp.max(sc, axis=-1, keepdims=True), sink.shape), sink)
            p_un = jnp.exp(sc - jnp.concatenate([m, m], axis=-1))
            acc = _dot(p_un.astype(BF16), vv_aug)
            o = acc[:, 0:LANES] / (acc[:, LANES:] + jnp.exp(sink - m))
            for p in range(pairs_per_kv):
                pair = g * pairs_per_kv + p
                even = o[(2 * p) * Lb:(2 * p + 1) * Lb, :]
                odd = o[(2 * p + 1) * Lb:(2 * p + 2) * Lb, :]
                out_ref[r0:r0 + Lb, pair * LANES:(pair + 1) * LANES] = (
                    jnp.where(low_half, even, odd).astype(out_ref.dtype))
        k_prev = k_cur
        v_prev = v_cur

    kprev_ref[...] = k_prev
    vprev_ref[...] = v_prev


def _swa(aq, ak, av, cos_t, sin_t, sinks, nsub):
    b, s, _ = aq.shape
    kvw = ATT_KV_HEADS * LANES
    rows = nsub * WINDOW

    def seq_spec(width):
        return pl.BlockSpec((None, rows, width), lambda bi, si: (bi, si, 0))

    tab_spec = pl.BlockSpec((rows, LANES), lambda bi, si: (si, 0))
    return pl.pallas_call(
        functools.partial(_swa_kernel, nsub=nsub),
        grid=(b, s // rows),
        in_specs=[pl.BlockSpec(memory_space=pltpu.SMEM),
                  seq_spec(ATT_WIDTH), seq_spec(kvw), seq_spec(kvw), tab_spec, tab_spec],
        out_specs=seq_spec(ATT_WIDTH),
        out_shape=jax.ShapeDtypeStruct((b, s, ATT_WIDTH), BF16),
        scratch_shapes=[pltpu.VMEM((WINDOW, kvw), BF16), pltpu.VMEM((WINDOW, kvw), BF16)],
        compiler_params=pltpu.CompilerParams(dimension_semantics=("arbitrary", "arbitrary"),
                                             vmem_limit_bytes=VMEM_LIMIT),
        name="swa",
    )(sinks, aq, ak, av, cos_t, sin_t)


def _store_token_tiles(ref, val, row0=0, rows_per_token=None):
    n, w = val.shape
    segs = w // LANES
    rpt = rows_per_token or segs
    for j in range(segs):
        ref[pl.ds(row0 + j, n, stride=rpt), :] = val[:, j * LANES:(j + 1) * LANES]


def _load_token_tiles(ref, n, segs, row0=0, rows_per_token=None):
    rpt = rows_per_token or segs
    return jnp.concatenate([ref[pl.ds(row0 + j, n, stride=rpt), :] for j in range(segs)], axis=-1)


ROUTER_ROWS = 48


def _pack_router(w_group, b_group, w_expert, b_expert):
    d = w_group.shape[0]
    wt = jnp.zeros((ROUTER_ROWS, d), F32)
    wt = wt.at[0:N_GROUPS].set(w_group.T).at[SUBLANES:SUBLANES + N_EXPERTS].set(w_expert.T)
    bias = jnp.zeros((ROUTER_ROWS,), F32)
    bias = bias.at[0:N_GROUPS].set(b_group).at[SUBLANES:SUBLANES + N_EXPERTS].set(b_expert)
    hi = wt.astype(BF16)
    lo = (wt - hi.astype(F32)).astype(BF16)
    return jnp.concatenate([hi, lo], axis=0), jnp.broadcast_to(bias[:, None], (ROUTER_ROWS, LANES))


def _outproj_kernel(x_ref, ml_ref, att_ref, wo_ref, lnw_ref, lnb_ref, wrt_ref, brt_ref,
                    x1t_ref, meta_ref, cnt_ref, *, alpha):
    step = pl.program_id(0)
    tm, d = x_ref.shape
    segs = d // LANES
    pt = tm
    rr = wrt_ref.shape[0] // 2
    nt = (((1,), (1,)), ((), ()))
    row = _iota((SUBLANES, pt), 0).astype(F32)
    lane = _iota((pt, LANES), 1).astype(F32)

    def first_argmax(vals):
        top = jnp.max(vals, axis=0, keepdims=True)
        idx = jnp.min(jnp.where(vals == top, row, float(SUBLANES)), axis=0, keepdims=True)
        return top, idx

    @pl.when(step == 0)
    def _():
        cnt_ref[...] = jnp.zeros_like(cnt_ref)

    counts = jnp.zeros((1, LANES), F32)
    for r0 in range(0, tm, pt):
        y = (_dot(ml_ref[r0:r0 + pt, :], wo_ref[0:ML_WIDTH, :])
             + _dot(att_ref[r0:r0 + pt, :], wo_ref[ML_WIDTH:, :]))
        x1 = _layer_norm(alpha * x_ref[r0:r0 + pt, :] + y, lnw_ref[...], lnb_ref[...])
        _store_token_tiles(x1t_ref.at[pl.ds(r0 * segs, pt * segs)], x1)

        x1_hi = x1.astype(BF16)
        x1_lo = (x1 - x1_hi.astype(F32)).astype(BF16)
        both = lax.dot_general(wrt_ref[...], x1_hi, nt, preferred_element_type=F32)
        cross = lax.dot_general(wrt_ref[0:rr, :], x1_lo, nt, preferred_element_type=F32)
        logits = both[0:rr] + both[rr:2 * rr] + cross + jnp.concatenate([brt_ref[...]] * (pt // LANES), axis=1)

        g_logits = jnp.where(row < N_GROUPS, logits[0:SUBLANES], NEG_INF)
        g_top, g_idx = first_argmax(g_logits)
        g_p = 1.0 / jnp.sum(jnp.exp(g_logits - g_top), axis=0, keepdims=True)

        e_logits = logits[SUBLANES:2 * SUBLANES]
        for grp in range(1, N_GROUPS):
            e_logits = jnp.where(g_idx == grp, logits[(1 + grp) * SUBLANES:(2 + grp) * SUBLANES], e_logits)
        v1, a1 = first_argmax(e_logits)
        v2, a2 = first_argmax(jnp.where(row == a1, NEG_INF, e_logits))
        r = jnp.exp(v2 - v1)
        w1 = g_p / (1.0 + r)
        w2 = g_p * r / (1.0 + r)

        lo = jnp.minimum(a1, a2)
        hi = jnp.maximum(a1, a2)
        w_lo = jnp.where(a1 < a2, w1, w2)
        w_hi = jnp.where(a1 < a2, w2, w1)
        pair_idx = (EXPERTS_PER_GROUP - 1) * lo - lo * (lo - 1.0) * 0.5 + (hi - lo - 1.0)
        cls = g_idx * PAIRS_PER_GROUP + pair_idx

        meta_t = jnp.where(row == 0.0, cls, jnp.where(row == 1.0, w_lo, jnp.where(row == 2.0, w_hi, 0.0)))
        meta = jnp.concatenate([meta_t, jnp.zeros((LANES - SUBLANES, pt), F32)], axis=0).T
        meta_ref[r0:r0 + pt, :] = meta
        counts = counts + jnp.sum(jnp.where(lane == meta[:, 0:1], 1.0, 0.0), axis=0, keepdims=True)

    cnt_ref[0:1, :] += counts


def _outproj(x2d, ml2d, att2d, w_out_b, ln_w, ln_b, w_router, b_router, alpha, tm):
    t, d = x2d.shape
    kern = functools.partial(_outproj_kernel, alpha=alpha)

    def const_spec(shape):
        return pl.BlockSpec(shape, lambda i: (0,) * len(shape))

    return pl.pallas_call(
        kern,
        grid=(t // tm,),
        in_specs=[pl.BlockSpec((tm, d), lambda i: (i, 0)),
                  pl.BlockSpec((tm, ML_WIDTH), lambda i: (i, 0)),
                  pl.BlockSpec((tm, ATT_WIDTH), lambda i: (i, 0)),
                  const_spec((ML_WIDTH + ATT_WIDTH, d)), const_spec((1, d)), const_spec((1, d)),
                  const_spec((2 * ROUTER_ROWS, d)), const_spec((ROUTER_ROWS, LANES))],
        out_specs=[pl.BlockSpec((tm * (d // LANES), LANES), lambda i: (i, 0)),
                   pl.BlockSpec((tm, LANES), lambda i: (i, 0)), const_spec((SUBLANES, LANES))],
        out_shape=[jax.ShapeDtypeStruct((t * (d // LANES), LANES), F32),
                   jax.ShapeDtypeStruct((t, LANES), F32),
                   jax.ShapeDtypeStruct((SUBLANES, LANES), F32)],
        compiler_params=pltpu.CompilerParams(dimension_semantics=("arbitrary",),
                                             vmem_limit_bytes=VMEM_LIMIT),
        name="outproj",
    )(x2d, ml2d, att2d, w_out_b, ln_w, ln_b, w_router, b_router)


def _class_expert_table():
    tab = np.zeros((SUBLANES, LANES), np.float32)
    for g in range(N_GROUPS):
        idx = 0
        for lo in range(EXPERTS_PER_GROUP):
            for hi in range(lo + 1, EXPERTS_PER_GROUP):
                c = g * PAIRS_PER_GROUP + idx
                tab[0, c] = g * EXPERTS_PER_GROUP + lo
                tab[1, c] = g * EXPERTS_PER_GROUP + hi
                idx += 1
    return tab


def _rank_kernel(meta_ref, cnt_ref, tab_ref, pos_ref, tile_ref, base_ref, run_ref, *, tb, n_tiles_pad):
    step = pl.program_id(0)
    lane8 = _iota((SUBLANES, LANES), 1)

    @pl.when(step == 0)
    def _():
        cnt = jnp.broadcast_to(cnt_ref[0:1, :], (SUBLANES, LANES))
        tiles = jnp.floor((cnt + (MOE_TILE - 1.0)) * (1.0 / MOE_TILE))
        cum = tiles
        sh = 1
        while sh < LANES:
            cum = cum + jnp.where(lane8 >= sh, pltpu.roll(cum, sh, 1), 0.0)
            sh *= 2
        excl = cum - tiles
        base_ref[...] = excl * MOE_TILE
        run_ref[...] = jnp.zeros_like(run_ref)

        ti = _iota((n_tiles_pad, LANES), 0).astype(F32)
        lane = _iota((n_tiles_pad, LANES), 1)
        done = jnp.where(lane < N_CLASSES, jnp.where(cum[0:1, :] <= ti, 1.0, 0.0), 0.0)
        t_cls = jnp.sum(done, axis=-1, keepdims=True)
        sel = jnp.where(lane.astype(F32) == t_cls, 1.0, 0.0)
        cnt_i = jnp.sum(sel * cnt[0:1, :], axis=-1, keepdims=True)
        first_i = jnp.sum(sel * excl[0:1, :], axis=-1, keepdims=True)
        rows_i = jnp.clip(cnt_i - MOE_TILE * (ti[:, 0:1] - first_i), 0.0, float(MOE_TILE))
        e_lo = jnp.sum(sel * tab_ref[0:1, :], axis=-1, keepdims=True)
        e_hi = jnp.sum(sel * tab_ref[1:2, :], axis=-1, keepdims=True)
        n_tiles = jnp.sum(jnp.where(lane < N_CLASSES, jnp.broadcast_to(tiles[0:1, :], (n_tiles_pad, LANES)), 0.0),
                          axis=-1, keepdims=True)
        info = jnp.where(lane == 0, rows_i,
                         jnp.where(lane == 1, e_lo, jnp.where(lane == 2, e_hi, jnp.where(lane == 3, n_tiles, 0.0))))
        tile_ref[...] = info.astype(I32)

    cls = meta_ref[:, 0:1]
    lane = _iota((tb, LANES), 1).astype(F32)
    onehot = jnp.where(lane == cls, 1.0, 0.0)
    strict_lower = jnp.where(_iota((tb, tb), 1) < _iota((tb, tb), 0), 1.0, 0.0).astype(BF16)
    before = _dot(strict_lower, onehot.astype(BF16))
    slot = jnp.sum(onehot * (before + run_ref[0:1, :] + base_ref[0:1, :]), axis=-1, keepdims=True)
    run_ref[...] = run_ref[...] + jnp.sum(onehot, axis=0, keepdims=True)
    slot_t = jnp.broadcast_to(slot, (tb, LANES)).T
    pos_ref[...] = slot_t[0:SUBLANES, :].astype(I32)


def _rank(meta, counts, tb, n_tiles_pad):
    t = meta.shape[0]
    kern = functools.partial(_rank_kernel, tb=tb, n_tiles_pad=n_tiles_pad)
    tab = jnp.asarray(_class_expert_table())
    return pl.pallas_call(
        kern,
        grid=(t // tb,),
        in_specs=[pl.BlockSpec((tb, LANES), lambda i: (i, 0)),
                  pl.BlockSpec((SUBLANES, LANES), lambda i: (0, 0)),
                  pl.BlockSpec((SUBLANES, LANES), lambda i: (0, 0))],
        out_specs=[pl.BlockSpec((SUBLANES, tb), lambda i: (0, i)),
                   pl.BlockSpec((n_tiles_pad, LANES), lambda i: (0, 0))],
        out_shape=[jax.ShapeDtypeStruct((SUBLANES, t), I32),
                   jax.ShapeDtypeStruct((n_tiles_pad, LANES), I32)],
        scratch_shapes=[pltpu.VMEM((SUBLANES, LANES), F32), pltpu.VMEM((SUBLANES, LANES), F32)],
        compiler_params=pltpu.CompilerParams(dimension_semantics=("arbitrary",),
                                             vmem_limit_bytes=VMEM_LIMIT),
        name="rank",
    )(meta, counts, tab)


DMA_UNROLL = 8


def _dispatch_kernel(pos_ref, src_ref, zeros_ref, dst_ref, sem, *, tb, rpt):
    del zeros_ref
    base = pl.program_id(0) * tb

    def issue(grp, carry):
        for u in range(DMA_UNROLL):
            j = grp * DMA_UNROLL + u
            src = src_ref.at[pl.ds(pl.multiple_of(j * rpt, rpt), rpt)]
            dst = dst_ref.at[pl.ds(pl.multiple_of(pos_ref[base + j] * rpt, rpt), rpt)]
            pltpu.make_async_copy(src, dst, sem).start(priority=u % 2)
        return carry

    lax.fori_loop(0, tb // DMA_UNROLL, issue, 0)
    pltpu.make_async_copy(src_ref, dst_ref.at[pl.ds(0, tb * rpt)], sem).wait()


def _dispatch(pos, src, n_slots, rpt, tb):
    t = pos.shape[0]
    kern = functools.partial(_dispatch_kernel, tb=tb, rpt=rpt)
    return pl.pallas_call(
        kern,
        grid_spec=pltpu.PrefetchScalarGridSpec(
            num_scalar_prefetch=1,
            grid=(t // tb,),
            in_specs=[pl.BlockSpec((tb * rpt, LANES), lambda i, pos_ref: (i, 0)),
                      pl.BlockSpec(memory_space=pl.ANY)],
            out_specs=pl.BlockSpec(memory_space=pl.ANY),
            scratch_shapes=[pltpu.SemaphoreType.DMA(())],
        ),
        out_shape=jax.ShapeDtypeStruct((n_slots * rpt, LANES), src.dtype),
        input_output_aliases={2: 0},
        compiler_params=pltpu.CompilerParams(dimension_semantics=("arbitrary",),
                                             has_side_effects=True, vmem_limit_bytes=VMEM_LIMIT),
        name="dispatch",
    )(pos, src, jnp.zeros((n_slots * rpt, LANES), src.dtype))


def _collect_kernel(pos_ref, ys_ref, x1t_ref, meta_ref, lnw_ref, lnb_ref, out_ref, buf_ref, sems,
                    *, tb, alpha, d):
    segs = d // LANES
    rpt = 2 * segs
    step = pl.program_id(0)
    n_steps = pl.num_programs(0)

    def gather(blk, slot):
        base = blk * tb

        def issue(grp, carry):
            for u in range(DMA_UNROLL):
                j = grp * DMA_UNROLL + u
                src = ys_ref.at[pl.ds(pos_ref[base + j], 1)]
                dst = buf_ref.at[slot, pl.ds(j, 1)]
                pltpu.make_async_copy(src, dst, sems.at[slot]).start(priority=u % 2)
            return carry

        lax.fori_loop(0, tb // DMA_UNROLL, issue, 0)

    @pl.when(step == 0)
    def _():
        gather(0, 0)

    @pl.when(step + 1 < n_steps)
    def _():
        gather(step + 1, (step + 1) % 2)

    slot = step % 2
    pltpu.make_async_copy(ys_ref.at[pl.ds(0, tb)], buf_ref.at[slot], sems.at[slot]).wait()

    x1 = _load_token_tiles(x1t_ref, tb, segs)
    y_lo = buf_ref[slot, :, 0:d]
    y_hi = buf_ref[slot, :, d:2 * d]
    meta = meta_ref[...]
    z = alpha * x1 + meta[:, 1:2] * y_lo + meta[:, 2:3] * y_hi
    out_ref[...] = _layer_norm(z, lnw_ref[...], lnb_ref[...])


def _collect(pos, ys, x1t, meta, ln_w, ln_b, alpha, d, tb):
    t = pos.shape[0]
    segs = d // LANES
    kern = functools.partial(_collect_kernel, tb=tb, alpha=alpha, d=d)
    return pl.pallas_call(
        kern,
        grid_spec=pltpu.PrefetchScalarGridSpec(
            num_scalar_prefetch=1,
            grid=(t // tb,),
            in_specs=[pl.BlockSpec(memory_space=pl.ANY),
                      pl.BlockSpec((tb * segs, LANES), lambda i, pos_ref: (i, 0)),
                      pl.BlockSpec((tb, LANES), lambda i, pos_ref: (i, 0)),
                      pl.BlockSpec((1, d), lambda i, pos_ref: (0, 0)),
                      pl.BlockSpec((1, d), lambda i, pos_ref: (0, 0))],
            out_specs=pl.BlockSpec((tb, d), lambda i, pos_ref: (i, 0)),
            scratch_shapes=[pltpu.VMEM((2, tb, 2 * d), F32), pltpu.SemaphoreType.DMA((2,))],
        ),
        out_shape=jax.ShapeDtypeStruct((t, d), F32),
        compiler_params=pltpu.CompilerParams(dimension_semantics=("arbitrary",),
                                             vmem_limit_bytes=VMEM_LIMIT),
        name="collect",
    )(pos, ys, x1t, meta, ln_w, ln_b)


MOE_TILES_PER_STEP = 1


def _moe_kernel(rows_ref, elo_ref, ehi_ref, nt_ref, xs_ref, *refs, d):
    *w_refs, ys_ref = refs
    i = pl.program_id(0)
    segs = d // LANES

    @pl.when(rows_ref[i * MOE_TILES_PER_STEP] > 0)
    def _():
        for k in range(MOE_TILES_PER_STEP):
            wgu_lo, wd_lo, wgu_hi, wd_hi = w_refs[4 * k:4 * k + 4]
            xs_k = xs_ref.at[pl.ds(k * MOE_TILE * segs, MOE_TILE * segs)]
            xb = _load_token_tiles(xs_k, MOE_TILE, segs).astype(BF16)

            def expert(wgu_ref, wd_ref):
                de = wd_ref.shape[0]
                gate_up = _dot(xb, wgu_ref[...])
                gate = gate_up[:, 0:de]
                hidden = gate * _sigmoid(gate) * gate_up[:, de:]
                return _dot(hidden.astype(BF16), wd_ref[...])

            ys_ref[k * MOE_TILE:(k + 1) * MOE_TILE, 0:d] = expert(wgu_lo, wd_lo)
            ys_ref[k * MOE_TILE:(k + 1) * MOE_TILE, d:2 * d] = expert(wgu_hi, wd_hi)

    @pl.when(rows_ref[i * MOE_TILES_PER_STEP] <= 0)
    def _():
        ys_ref[...] = jnp.zeros_like(ys_ref)


def _moe(info, xs, wgu, wd, d, n_tiles_pad):
    de = wd.shape[1]
    segs = d // LANES
    tps = MOE_TILES_PER_STEP
    kern = functools.partial(_moe_kernel, d=d)

    def last_live(tile, nt_ref):
        return jnp.minimum(tile, jnp.maximum(nt_ref[0] - 1, 0))

    def up_spec(k, which):
        return pl.BlockSpec((None, d, 2 * de),
                            lambda i, rows, elo, ehi, nt: ((elo, ehi)[which][last_live(i * tps + k, nt)], 0, 0))

    def down_spec(k, which):
        return pl.BlockSpec((None, de, d),
                            lambda i, rows, elo, ehi, nt: ((elo, ehi)[which][last_live(i * tps + k, nt)], 0, 0))

    w_specs = []
    for k in range(tps):
        w_specs += [up_spec(k, 0), down_spec(k, 0), up_spec(k, 1), down_spec(k, 1)]
    rows, elo, ehi, nt = info[:, 0], info[:, 1], info[:, 2], info[0:1, 3]
    return pl.pallas_call(
        kern,
        grid_spec=pltpu.PrefetchScalarGridSpec(
            num_scalar_prefetch=4,
            grid=(n_tiles_pad // tps,),
            in_specs=[pl.BlockSpec((tps * MOE_TILE * segs, LANES),
                                   lambda i, rows, elo, ehi, nt: (last_live(i * tps, nt) // tps, 0))] + w_specs,
            out_specs=pl.BlockSpec((tps * MOE_TILE, 2 * d), lambda i, rows, elo, ehi, nt: (i, 0)),
        ),
        out_shape=jax.ShapeDtypeStruct((n_tiles_pad * MOE_TILE, 2 * d), F32),
        compiler_params=pltpu.CompilerParams(dimension_semantics=("arbitrary",),
                                             vmem_limit_bytes=VMEM_LIMIT),
        name="moe",
    )(rows, elo, ehi, nt, xs, *([wgu, wd, wgu, wd] * tps))


def _pick_block(n, target):
    blk = min(n, target)
    while n % blk:
        blk //= 2
    return blk


def kernel(x, w_in, conv_w, conv_b, mlstm_gate_bias, mlstm_norm_w, attn_sinks, w_out, ln1_w, ln1_b,
           w_group_router, b_group_router, w_expert_router, b_expert_router,
           w_exp_gate, w_exp_up, w_exp_down, ln2_w, ln2_b):
    b, s, d = x.shape
    t = b * s
    depth = w_in.shape[0]
    alpha = (2.0 * depth) ** 0.25
    assert s % ML_CHUNK == 0 and s % WINDOW == 0 and d % LANES == 0

    tm = _pick_block(t, 512)
    tq = _pick_block(s, 4 * ML_CHUNK)
    tb_rank = _pick_block(t, 512)
    tb_dma = _pick_block(t, 2048)
    tb_col = _pick_block(t, 256)
    n_tiles_pad = -(-(t // MOE_TILE + N_CLASSES) // SUBLANES) * SUBLANES
    cos_t, sin_t = _rope_tables(s)

    for l in range(depth):
        x2d = x.reshape(t, d)
        q, k, v, og, aq, ak, av, g = _inproj(x2d, _pack_w_in(w_in[l]), conv_w[l], conv_b[l][None, :], tm, s // tm)
        bias_pad = jnp.zeros((LANES - ML_HEADS,), F32)
        gate_bias_row = jnp.concatenate(
            [mlstm_gate_bias[l, 0], bias_pad, mlstm_gate_bias[l, 1], bias_pad])[None, :]
        ml = _mlstm(q.reshape(b, s, -1), k.reshape(b, s, -1), v.reshape(b, s, -1), og.reshape(b, s, -1),
                    g.reshape(b, s, -1), gate_bias_row, mlstm_norm_w[l][None, :], tq)
        att = _swa(aq.reshape(b, s, -1), ak.reshape(b, s, -1), av.reshape(b, s, -1), cos_t, sin_t, attn_sinks[l],
                   1)

        w_router, b_router = _pack_router(w_group_router[l], b_group_router[l],
                                          w_expert_router[l], b_expert_router[l])
        x1t, meta, counts = _outproj(x2d, ml.reshape(t, -1), att.reshape(t, -1), w_out[l].astype(BF16),
                                     ln1_w[l][None, :], ln1_b[l][None, :], w_router, b_router, alpha, tm)

        pos2d, info = _rank(meta, counts, tb_rank, n_tiles_pad)
        pos = pos2d[0]
        xs = _dispatch(pos, x1t, n_tiles_pad * MOE_TILE, d // LANES, tb_dma)
        w_gate_up = jnp.concatenate([w_exp_gate[l].astype(BF16), w_exp_up[l].astype(BF16)], axis=-1)
        ys = _moe(info, xs, w_gate_up, w_exp_down[l].astype(BF16), d, n_tiles_pad)
        out = _collect(pos, ys, x1t, meta, ln2_w[l][None, :], ln2_b[l][None, :], alpha, d, tb_col)
        x = out.reshape(b, s, d)
    return x
```

```python
import functools
import math

import numpy as np
import jax
import jax.numpy as jnp
from jax import lax
from jax.experimental import pallas as pl
from jax.experimental.pallas import tpu as pltpu

F32 = jnp.float32
BF16 = jnp.bfloat16
I32 = jnp.int32

ML_HEADS = 4
ML_HEAD_DIM = 128
ML_WIDTH = ML_HEADS * ML_HEAD_DIM
ML_CHUNK = 128
CONV_WIDTH = 4
ATT_Q_HEADS = 8
ATT_KV_HEADS = 2
ATT_HEAD_DIM = 64
ATT_WIDTH = ATT_Q_HEADS * ATT_HEAD_DIM
ATT_KV_WIDTH = ATT_KV_HEADS * ATT_HEAD_DIM
WINDOW = 128
ROPE_THETA = 10000.0
N_GROUPS = 4
EXPERTS_PER_GROUP = 8
N_EXPERTS = N_GROUPS * EXPERTS_PER_GROUP
PAIRS_PER_GROUP = EXPERTS_PER_GROUP * (EXPERTS_PER_GROUP - 1) // 2
N_CLASSES = N_GROUPS * PAIRS_PER_GROUP
LN_EPS = 1e-5

LANES = 128
SUBLANES = 8
MOE_TILE = 320
VMEM_LIMIT = 56 * 1024 * 1024

NEG_INF = float("-inf")


def _sigmoid(x):
    return 1.0 / (1.0 + jnp.exp(-x))


def _log_sigmoid(x):
    return jnp.minimum(x, 0.0) - jnp.log(1.0 + jnp.exp(-jnp.abs(x)))


def _iota(shape, dim):
    return lax.broadcasted_iota(I32, shape, dim)


def _dot(a, b):
    return jnp.dot(a, b, preferred_element_type=F32)


def _dot_exact(a, b):
    return jnp.dot(a, b, preferred_element_type=F32, precision=lax.Precision.HIGHEST)


def _layer_norm(z, w, b):
    mu = jnp.mean(z, axis=-1, keepdims=True)
    zc = z - mu
    var = jnp.mean(zc * zc, axis=-1, keepdims=True)
    return zc * lax.rsqrt(var + LN_EPS) * w + b


C_QK = 0
C_V = C_QK + 2 * ML_WIDTH
C_O = C_V + ML_WIDTH
C_AQ = C_O + ML_WIDTH
C_AK = C_AQ + ATT_WIDTH
C_AV = C_AK + ATT_KV_HEADS * LANES
C_G = C_AV + ATT_KV_HEADS * LANES
C_END = C_G + 2 * LANES


def _pack_w_in(w_in):
    sizes = (2 * ML_WIDTH, ML_WIDTH, ML_WIDTH, ML_HEADS, ML_HEADS, ATT_WIDTH, ATT_KV_WIDTH, ATT_KV_WIDTH)
    splits = np.cumsum(sizes)[:-1].tolist()
    w_qk, w_v, w_o, w_i, w_f, w_aq, w_ak, w_av = jnp.split(w_in, splits, axis=-1)
    half = ATT_HEAD_DIM // 2

    def head(w, h):
        return w[:, h * ATT_HEAD_DIM:(h + 1) * ATT_HEAD_DIM]

    def q_tile(a, b):
        return [a[:, :half], b[:, :half], a[:, half:], b[:, half:]]

    q_cols = [t for p in range(ATT_Q_HEADS // 2) for t in q_tile(head(w_aq, 2 * p), head(w_aq, 2 * p + 1))]
    k_cols = [t for h in range(ATT_KV_HEADS) for t in q_tile(head(w_ak, h), head(w_ak, h))]
    v_cols = [t for h in range(ATT_KV_HEADS) for t in (head(w_av, h), head(w_av, h))]
    lane_pad = jnp.zeros((w_in.shape[0], LANES - ML_HEADS), w_in.dtype)
    packed = jnp.concatenate([w_qk, w_v, w_o] + q_cols + k_cols + v_cols + [w_i, lane_pad, w_f, lane_pad], axis=-1)
    return packed.astype(BF16)


def _inproj_kernel(x_ref, w_ref, cw_ref, cb_ref, q_ref, k_ref, v_ref, og_ref, aq_ref, ak_ref, av_ref, g_ref,
                   *scratch, blocks_per_seq):
    *ext_refs, xb_ref = scratch
    tm = x_ref.shape[0]
    halo = SUBLANES
    cs = ext_refs[0].shape[1]
    xb_ref[...] = x_ref[...].astype(BF16)

    @pl.when(pl.program_id(0) % blocks_per_seq == 0)
    def _():
        for ext_ref in ext_refs:
            ext_ref[0:halo, :] = jnp.zeros((halo, cs), F32)

    scale = ML_HEAD_DIM ** -0.5

    def mm(lo, hi):
        return _dot(xb_ref[...], w_ref[:, lo:hi])

    def conv_slice(idx):
        ext_ref = ext_refs[idx]
        c0 = idx * cs
        is_q = c0 < ML_WIDTH
        dst_ref, off = (q_ref, c0) if is_q else (k_ref, c0 - ML_WIDTH)
        rt = ML_CHUNK
        for r0 in range(0, tm, rt):
            conv = cb_ref[:, c0:c0 + cs]
            for j in range(CONV_WIDTH):
                start = halo + r0 - (CONV_WIDTH - 1) + j
                conv = conv + cw_ref[j:j + 1, c0:c0 + cs] * ext_ref[start:start + rt, :]
            act = conv * _sigmoid(conv)
            dst_ref[r0:r0 + rt, off:off + cs] = (act if is_q else act * scale).astype(BF16)
        ext_ref[0:halo, :] = ext_ref[tm:tm + halo, :]

    for idx in range(len(ext_refs)):
        ext_refs[idx][halo:halo + tm, :] = mm(C_QK + idx * cs, C_QK + (idx + 1) * cs)
        if idx > 0:
            conv_slice(idx - 1)
    half_v = ML_WIDTH // 2
    v_ref[:, 0:half_v] = mm(C_V, C_V + half_v).astype(BF16)
    conv_slice(len(ext_refs) - 1)
    v_ref[:, half_v:] = mm(C_V + half_v, C_O).astype(BF16)
    og_ref[...] = _sigmoid(mm(C_O, C_AQ)).astype(BF16)
    aq_ref[...] = mm(C_AQ, C_AK)
    ak_ref[...] = mm(C_AK, C_AV)
    av_ref[...] = mm(C_AV, C_G).astype(BF16)
    g_ref[...] = mm(C_G, C_END)


def _inproj(x2d, w_packed, conv_w, conv_b, tm, blocks_per_seq):
    t, d = x2d.shape
    widths = (ML_WIDTH, ML_WIDTH, ML_WIDTH, ML_WIDTH, C_AK - C_AQ, C_AV - C_AK, C_G - C_AV, C_END - C_G)
    dtypes = (BF16, BF16, BF16, BF16, F32, F32, BF16, F32)
    kern = functools.partial(_inproj_kernel, blocks_per_seq=blocks_per_seq)
    return pl.pallas_call(
        kern,
        grid=(t // tm,),
        in_specs=[pl.BlockSpec((tm, d), lambda i: (i, 0)),
                  pl.BlockSpec((d, C_END), lambda i: (0, 0)),
                  pl.BlockSpec((CONV_WIDTH, 2 * ML_WIDTH), lambda i: (0, 0)),
                  pl.BlockSpec((1, 2 * ML_WIDTH), lambda i: (0, 0))],
        out_specs=[pl.BlockSpec((tm, w), lambda i: (i, 0)) for w in widths],
        out_shape=[jax.ShapeDtypeStruct((t, w), dt) for w, dt in zip(widths, dtypes)],
        scratch_shapes=[pltpu.VMEM((tm + SUBLANES, 2 * LANES), F32)] * (2 * ML_WIDTH // (2 * LANES))
        + [pltpu.VMEM((tm, d), BF16)],
        compiler_params=pltpu.CompilerParams(dimension_semantics=("arbitrary",),
                                             vmem_limit_bytes=VMEM_LIMIT),
        name="inproj",
    )(x2d, w_packed, conv_w, conv_b)


def _time_scan(x, combine, identity):
    row = _iota(x.shape, 0)
    sh = 1
    while sh < x.shape[0]:
        x = combine(x, jnp.where(row >= sh, pltpu.roll(x, sh, 0), identity))
        sh *= 2
    return x


def _mlstm_kernel(q_ref, k_ref, v_ref, og_ref, g_ref, gb_ref, nw_ref, out_ref, ct_ref, m_ref, *, tq):
    s_idx = pl.program_id(1)
    L = ML_CHUNK
    D = ML_HEAD_DIM
    H = ML_HEADS
    heads = range(H)

    @pl.when(s_idx == 0)
    def _():
        ct_ref[...] = jnp.zeros_like(ct_ref)
        m_ref[...] = jnp.zeros_like(m_ref)

    causal = _iota((L, L), 1) <= _iota((L, L), 0)
    ones_blk = jnp.ones((L, D), BF16)
    mean_blk = jnp.full((D, D), 1.0 / D, BF16)
    m_prev = m_ref[0:1, :]
    head_lanes = _iota((L, LANES), 1) < H
    tile_of_lane = jnp.right_shift(_iota((LANES, H * L), 1), L.bit_length() - 1)
    spread = jnp.where(_iota((LANES, H * L), 0) == tile_of_lane, 1.0, 0.0).astype(BF16)

    def spread_heads(x):
        x = jnp.where(head_lanes, x, 0.0)
        hi = x.astype(BF16)
        lo = (x - hi.astype(F32)).astype(BF16)
        return _dot(hi, spread) + _dot(lo, spread)

    for c in range(tq // L):
        r0 = c * L
        gi = g_ref[r0:r0 + L, 0:LANES] + gb_ref[:, 0:LANES]
        gf = g_ref[r0:r0 + L, LANES:2 * LANES] + gb_ref[:, LANES:2 * LANES]
        b_cum = _time_scan(_log_sigmoid(gf), jnp.add, 0.0)
        r = gi - b_cum
        g = jnp.maximum(m_prev, _time_scan(r, jnp.maximum, NEG_INF))
        g_rep = spread_heads(g)
        b_rep = spread_heads(b_cum)
        b_last = b_cum[L - 1:L, :]
        m_new = jnp.maximum(b_last + m_prev, jnp.max(b_last + r, axis=0, keepdims=True))
        decay = jnp.exp(b_last + m_prev - m_new)
        shift = b_last - m_new
        r_t = r.T

        q_b = [q_ref[r0:r0 + L, h * D:(h + 1) * D] for h in heads]
        kt_b = [k_ref[r0:r0 + L, h * D:(h + 1) * D].T for h in heads]
        v_aug = [jnp.concatenate([v_ref[r0:r0 + L, h * D:(h + 1) * D], ones_blk], axis=-1) for h in heads]
        g_col = [g_rep[:, h * L:(h + 1) * L] for h in heads]
        w_intra = [jnp.exp(jnp.where(causal, r_t[h:h + 1, :] - g_col[h], NEG_INF)) for h in heads]
        s_b = [(_dot(q_b[h], kt_b[h]) * w_intra[h]).astype(BF16) for h in heads]
        ct = [ct_ref[h] for h in heads]
        inter = [_dot(q_b[h], ct[h].astype(BF16)) for h in heads]
        intra = [_dot(s_b[h], v_aug[h]) for h in heads]
        for h in heads:
            wi_col = jnp.exp(m_prev[:, h:h + 1] - g_col[h])
            clamp = jnp.exp(-(b_rep[:, h * L:(h + 1) * L] + g_col[h]))
            num = wi_col * inter[h][:, 0:D] + intra[h][:, 0:D]
            den = wi_col * inter[h][:, D:] + intra[h][:, D:]
            hh = num / jnp.maximum(jnp.abs(den), clamp)
            mu = _dot(hh.astype(BF16), mean_blk)
            hc = hh - mu
            var = _dot((hc * hc).astype(BF16), mean_blk)
            hn = hc * lax.rsqrt(var + LN_EPS) * nw_ref[:, h * D:(h + 1) * D]
            gate_o = og_ref[r0:r0 + L, h * D:(h + 1) * D].astype(F32)
            out_ref[r0:r0 + L, h * D:(h + 1) * D] = (gate_o * hn).astype(out_ref.dtype)
        for h in heads:
            w_row = jnp.exp(r_t[h:h + 1, :] + shift[:, h:h + 1])
            ktw = (kt_b[h].astype(F32) * w_row).astype(BF16)
            ct_ref[h] = decay[:, h:h + 1] * ct[h] + _dot(ktw, v_aug[h])
        m_prev = m_new

    m_ref[...] = jnp.broadcast_to(m_prev, m_ref.shape)


def _mlstm(q, k, v, og, g, gate_bias_row, norm_w_row, tq):
    b, s, _ = q.shape
    kern = functools.partial(_mlstm_kernel, tq=tq)

    def seq_spec(width):
        return pl.BlockSpec((None, tq, width), lambda bi, si: (bi, si, 0))

    def const_spec(shape):
        return pl.BlockSpec(shape, lambda bi, si: (0,) * len(shape))

    return pl.pallas_call(
        kern,
        grid=(b, s // tq),
        in_specs=[seq_spec(ML_WIDTH), seq_spec(ML_WIDTH), seq_spec(ML_WIDTH), seq_spec(ML_WIDTH),
                  seq_spec(2 * LANES), const_spec((1, 2 * LANES)), const_spec((1, ML_WIDTH))],
        out_specs=seq_spec(ML_WIDTH),
        out_shape=jax.ShapeDtypeStruct((b, s, ML_WIDTH), BF16),
        scratch_shapes=[pltpu.VMEM((ML_HEADS, ML_HEAD_DIM, 2 * ML_HEAD_DIM), F32),
                        pltpu.VMEM((SUBLANES, LANES), F32)],
        compiler_params=pltpu.CompilerParams(dimension_semantics=("arbitrary", "arbitrary"),
                                             vmem_limit_bytes=VMEM_LIMIT),
        name="mlstm",
    )(q, k, v, og, g, gate_bias_row, norm_w_row)


def _rope_tables(seq_len):
    half = ATT_HEAD_DIM // 2
    inv_freq = ROPE_THETA ** (-jnp.arange(half, dtype=F32) / half)
    ang = jnp.arange(seq_len, dtype=F32)[:, None] * inv_freq[None, :]
    cos = jnp.cos(ang)
    sin = jnp.sin(ang)
    cos_t = jnp.concatenate([cos, cos, cos, cos], axis=-1)
    sin_t = jnp.concatenate([-sin, -sin, sin, sin], axis=-1)
    return cos_t, sin_t


def _swa_kernel(sink_ref, aq_ref, ak_ref, av_ref, cos_ref, sin_ref, out_ref, kprev_ref, vprev_ref, *, nsub):
    step = pl.program_id(1)
    Lb = WINDOW
    half = ATT_HEAD_DIM // 2
    pairs = ATT_Q_HEADS // 2
    pairs_per_kv = pairs // ATT_KV_HEADS

    @pl.when(step == 0)
    def _():
        kprev_ref[...] = jnp.zeros_like(kprev_ref)
        vprev_ref[...] = jnp.zeros_like(vprev_ref)

    def rope(x, cos, sin):
        tiles = []
        for c in range(x.shape[-1] // LANES):
            xt = x[:, c * LANES:(c + 1) * LANES]
            tiles.append(xt * cos + pltpu.roll(xt, LANES // 2, 1) * sin)
        return jnp.concatenate(tiles, axis=-1)

    ql = _iota((Lb, 2 * Lb), 0)
    kj = _iota((Lb, 2 * Lb), 1)
    diff = Lb + ql - kj
    lane = _iota((Lb, LANES), 1)
    low_half = lane < ATT_HEAD_DIM
    first_head = (lane & half) == 0
    ones_blk = jnp.ones((2 * Lb, LANES), BF16)

    k_prev = kprev_ref
    v_prev = vprev_ref
    for j in range(nsub):
        r0 = j * Lb
        cos = cos_ref[r0:r0 + Lb, :]
        sin = sin_ref[r0:r0 + Lb, :]
        q = rope(aq_ref[r0:r0 + Lb, :], cos, sin) * (ATT_HEAD_DIM ** -0.5)
        k_cur = rope(ak_ref[r0:r0 + Lb, :], cos, sin).astype(BF16)
        v_cur = av_ref[r0:r0 + Lb, :]

        kpos = (step * nsub + j) * Lb + kj - Lb
        visible = jnp.where(diff >= 0, jnp.where(diff < WINDOW, jnp.where(kpos >= 0, 1, 0), 0), 0)
        bias = jnp.where(visible > 0, 0.0, NEG_INF).astype(F32)
        bias = jnp.concatenate([bias] * (2 * pairs_per_kv), axis=0)

        for g in range(ATT_KV_HEADS):
            kk = jnp.concatenate([k_prev[:, g * LANES:(g + 1) * LANES], k_cur[:, g * LANES:(g + 1) * LANES]], axis=0)
            vv = jnp.concatenate([v_prev[:, g * LANES:(g + 1) * LANES], v_cur[:, g * LANES:(g + 1) * LANES]], axis=0)
            vv_aug = jnp.concatenate([vv, ones_blk], axis=-1)
            rows = []
            sinks = []
            for p in range(pairs_per_kv):
                pair = g * pairs_per_kv + p
                q2 = q[:, pair * LANES:(pair + 1) * LANES]
                rows.append(jnp.where(first_head, q2, 0.0))
                rows.append(jnp.where(first_head, 0.0, q2))
                sinks.append(jnp.full((Lb, LANES), sink_ref[2 * pair], F32))
                sinks.append(jnp.full((Lb, LANES), sink_ref[2 * pair + 1], F32))
            qs = jnp.concatenate(rows, axis=0).astype(BF16)
            sink = jnp.concatenate(sinks, axis=0)

            sc = lax.dot_general(qs, kk, (((1,), (1,)), ((), ())), preferred_element_type=F32) + bias
            m = jnp.maximum(jnp.broadcast_to(jnp.max(sc, axis=-1, keepdims=True), sink.shape), sink)
            p_un = jnp.exp(sc - jnp.concatenate([m, m], axis=-1))
            acc = _dot(p_un.astype(BF16), vv_aug)
            o = acc[:, 0:LANES] / (acc[:, LANES:] + jnp.exp(sink - m))
            for p in range(pairs_per_kv):
                pair = g * pairs_per_kv + p
                even = o[(2 * p) * Lb:(2 * p + 1) * Lb, :]
                odd = o[(2 * p + 1) * Lb:(2 * p + 2) * Lb, :]
                out_ref[r0:r0 + Lb, pair * LANES:(pair + 1) * LANES] = (
                    jnp.where(low_half, even, odd).astype(out_ref.dtype))
        k_prev = k_cur
        v_prev = v_cur

    kprev_ref[...] = k_prev
    vprev_ref[...] = v_prev


def _swa(aq, ak, av, cos_t, sin_t, sinks, nsub):
    b, s, _ = aq.shape
    kvw = ATT_KV_HEADS * LANES
    rows = nsub * WINDOW

    def seq_spec(width):
        return pl.BlockSpec((None, rows, width), lambda bi, si: (bi, si, 0))

    tab_spec = pl.BlockSpec((rows, LANES), lambda bi, si: (si, 0))
    return pl.pallas_call(
        functools.partial(_swa_kernel, nsub=nsub),
        grid=(b, s // rows),
        in_specs=[pl.BlockSpec(memory_space=pltpu.SMEM),
                  seq_spec(ATT_WIDTH), seq_spec(kvw), seq_spec(kvw), tab_spec, tab_spec],
        out_specs=seq_spec(ATT_WIDTH),
        out_shape=jax.ShapeDtypeStruct((b, s, ATT_WIDTH), BF16),
        scratch_shapes=[pltpu.VMEM((WINDOW, kvw), BF16), pltpu.VMEM((WINDOW, kvw), BF16)],
        compiler_params=pltpu.CompilerParams(dimension_semantics=("arbitrary", "arbitrary"),
                                             vmem_limit_bytes=VMEM_LIMIT),
        name="swa",
    )(sinks, aq, ak, av, cos_t, sin_t)


def _store_token_tiles(ref, val, row0=0, rows_per_token=None):
    n, w = val.shape
    segs = w // LANES
    rpt = rows_per_token or segs
    for j in range(segs):
        ref[pl.ds(row0 + j, n, stride=rpt), :] = val[:, j * LANES:(j + 1) * LANES]


def _load_token_tiles(ref, n, segs, row0=0, rows_per_token=None):
    rpt = rows_per_token or segs
    return jnp.concatenate([ref[pl.ds(row0 + j, n, stride=rpt), :] for j in range(segs)], axis=-1)


ROUTER_ROWS = 48


def _pack_router(w_group, b_group, w_expert, b_expert):
    d = w_group.shape[0]
    wt = jnp.zeros((ROUTER_ROWS, d), F32)
    wt = wt.at[0:N_GROUPS].set(w_group.T).at[SUBLANES:SUBLANES + N_EXPERTS].set(w_expert.T)
    bias = jnp.zeros((ROUTER_ROWS,), F32)
    bias = bias.at[0:N_GROUPS].set(b_group).at[SUBLANES:SUBLANES + N_EXPERTS].set(b_expert)
    hi = wt.astype(BF16)
    lo = (wt - hi.astype(F32)).astype(BF16)
    return jnp.concatenate([hi, lo], axis=0), jnp.broadcast_to(bias[:, None], (ROUTER_ROWS, LANES))


def _outproj_kernel(x_ref, ml_ref, att_ref, wo_ref, lnw_ref, lnb_ref, wrt_ref, brt_ref,
                    x1t_ref, meta_ref, cnt_ref, *, alpha):
    step = pl.program_id(0)
    tm, d = x_ref.shape
    segs = d // LANES
    pt = tm
    rr = wrt_ref.shape[0] // 2
    nt = (((1,), (1,)), ((), ()))
    row = _iota((SUBLANES, pt), 0).astype(F32)
    lane = _iota((pt, LANES), 1).astype(F32)

    def first_argmax(vals):
        top = jnp.max(vals, axis=0, keepdims=True)
        idx = jnp.min(jnp.where(vals == top, row, float(SUBLANES)), axis=0, keepdims=True)
        return top, idx

    @pl.when(step == 0)
    def _():
        cnt_ref[...] = jnp.zeros_like(cnt_ref)

    counts = jnp.zeros((1, LANES), F32)
    for r0 in range(0, tm, pt):
        y = (_dot(ml_ref[r0:r0 + pt, :], wo_ref[0:ML_WIDTH, :])
             + _dot(att_ref[r0:r0 + pt, :], wo_ref[ML_WIDTH:, :]))
        x1 = _layer_norm(alpha * x_ref[r0:r0 + pt, :] + y, lnw_ref[...], lnb_ref[...])
        _store_token_tiles(x1t_ref.at[pl.ds(r0 * segs, pt * segs)], x1)

        x1_hi = x1.astype(BF16)
        x1_lo = (x1 - x1_hi.astype(F32)).astype(BF16)
        both = lax.dot_general(wrt_ref[...], x1_hi, nt, preferred_element_type=F32)
        cross = lax.dot_general(wrt_ref[0:rr, :], x1_lo, nt, preferred_element_type=F32)
        logits = both[0:rr] + both[rr:2 * rr] + cross + jnp.concatenate([brt_ref[...]] * (pt // LANES), axis=1)

        g_logits = jnp.where(row < N_GROUPS, logits[0:SUBLANES], NEG_INF)
        g_top, g_idx = first_argmax(g_logits)
        g_p = 1.0 / jnp.sum(jnp.exp(g_logits - g_top), axis=0, keepdims=True)

        e_logits = logits[SUBLANES:2 * SUBLANES]
        for grp in range(1, N_GROUPS):
            e_logits = jnp.where(g_idx == grp, logits[(1 + grp) * SUBLANES:(2 + grp) * SUBLANES], e_logits)
        v1, a1 = first_argmax(e_logits)
        v2, a2 = first_argmax(jnp.where(row == a1, NEG_INF, e_logits))
        r = jnp.exp(v2 - v1)
        w1 = g_p / (1.0 + r)
        w2 = g_p * r / (1.0 + r)

        lo = jnp.minimum(a1, a2)
        hi = jnp.maximum(a1, a2)
        w_lo = jnp.where(a1 < a2, w1, w2)
        w_hi = jnp.where(a1 < a2, w2, w1)
        pair_idx = (EXPERTS_PER_GROUP - 1) * lo - lo * (lo - 1.0) * 0.5 + (hi - lo - 1.0)
        cls = g_idx * PAIRS_PER_GROUP + pair_idx

        meta_t = jnp.where(row == 0.0, cls, jnp.where(row == 1.0, w_lo, jnp.where(row == 2.0, w_hi, 0.0)))
        meta = jnp.concatenate([meta_t, jnp.zeros((LANES - SUBLANES, pt), F32)], axis=0).T
        meta_ref[r0:r0 + pt, :] = meta
        counts = counts + jnp.sum(jnp.where(lane == meta[:, 0:1], 1.0, 0.0), axis=0, keepdims=True)

    cnt_ref[0:1, :] += counts


def _outproj(x2d, ml2d, att2d, w_out_b, ln_w, ln_b, w_router, b_router, alpha, tm):
    t, d = x2d.shape
    kern = functools.partial(_outproj_kernel, alpha=alpha)

    def const_spec(shape):
        return pl.BlockSpec(shape, lambda i: (0,) * len(shape))

    return pl.pallas_call(
        kern,
        grid=(t // tm,),
        in_specs=[pl.BlockSpec((tm, d), lambda i: (i, 0)),
                  pl.BlockSpec((tm, ML_WIDTH), lambda i: (i, 0)),
                  pl.BlockSpec((tm, ATT_WIDTH), lambda i: (i, 0)),
                  const_spec((ML_WIDTH + ATT_WIDTH, d)), const_spec((1, d)), const_spec((1, d)),
                  const_spec((2 * ROUTER_ROWS, d)), const_spec((ROUTER_ROWS, LANES))],
        out_specs=[pl.BlockSpec((tm * (d // LANES), LANES), lambda i: (i, 0)),
                   pl.BlockSpec((tm, LANES), lambda i: (i, 0)), const_spec((SUBLANES, LANES))],
        out_shape=[jax.ShapeDtypeStruct((t * (d // LANES), LANES), F32),
                   jax.ShapeDtypeStruct((t, LANES), F32),
                   jax.ShapeDtypeStruct((SUBLANES, LANES), F32)],
        compiler_params=pltpu.CompilerParams(dimension_semantics=("arbitrary",),
                                             vmem_limit_bytes=VMEM_LIMIT),
        name="outproj",
    )(x2d, ml2d, att2d, w_out_b, ln_w, ln_b, w_router, b_router)


def _class_expert_table():
    tab = np.zeros((SUBLANES, LANES), np.float32)
    for g in range(N_GROUPS):
        idx = 0
        for lo in range(EXPERTS_PER_GROUP):
            for hi in range(lo + 1, EXPERTS_PER_GROUP):
                c = g * PAIRS_PER_GROUP + idx
                tab[0, c] = g * EXPERTS_PER_GROUP + lo
                tab[1, c] = g * EXPERTS_PER_GROUP + hi
                idx += 1
    return tab


def _rank_kernel(meta_ref, cnt_ref, tab_ref, pos_ref, tile_ref, base_ref, run_ref, *, tb, n_tiles_pad):
    step = pl.program_id(0)
    lane8 = _iota((SUBLANES, LANES), 1)

    @pl.when(step == 0)
    def _():
        cnt = jnp.broadcast_to(cnt_ref[0:1, :], (SUBLANES, LANES))
        tiles = jnp.floor((cnt + (MOE_TILE - 1.0)) * (1.0 / MOE_TILE))
        cum = tiles
        sh = 1
        while sh < LANES:
            cum = cum + jnp.where(lane8 >= sh, pltpu.roll(cum, sh, 1), 0.0)
            sh *= 2
        excl = cum - tiles
        base_ref[...] = excl * MOE_TILE
        run_ref[...] = jnp.zeros_like(run_ref)

        ti = _iota((n_tiles_pad, LANES), 0).astype(F32)
        lane = _iota((n_tiles_pad, LANES), 1)
        done = jnp.where(lane < N_CLASSES, jnp.where(cum[0:1, :] <= ti, 1.0, 0.0), 0.0)
        t_cls = jnp.sum(done, axis=-1, keepdims=True)
        sel = jnp.where(lane.astype(F32) == t_cls, 1.0, 0.0)
        cnt_i = jnp.sum(sel * cnt[0:1, :], axis=-1, keepdims=True)
        first_i = jnp.sum(sel * excl[0:1, :], axis=-1, keepdims=True)
        rows_i = jnp.clip(cnt_i - MOE_TILE * (ti[:, 0:1] - first_i), 0.0, float(MOE_TILE))
        e_lo = jnp.sum(sel * tab_ref[0:1, :], axis=-1, keepdims=True)
        e_hi = jnp.sum(sel * tab_ref[1:2, :], axis=-1, keepdims=True)
        n_tiles = jnp.sum(jnp.where(lane < N_CLASSES, jnp.broadcast_to(tiles[0:1, :], (n_tiles_pad, LANES)), 0.0),
                          axis=-1, keepdims=True)
        info = jnp.where(lane == 0, rows_i,
                         jnp.where(lane == 1, e_lo, jnp.where(lane == 2, e_hi, jnp.where(lane == 3, n_tiles, 0.0))))
        tile_ref[...] = info.astype(I32)

    cls = meta_ref[:, 0:1]
    lane = _iota((tb, LANES), 1).astype(F32)
    onehot = jnp.where(lane == cls, 1.0, 0.0)
    strict_lower = jnp.where(_iota((tb, tb), 1) < _iota((tb, tb), 0), 1.0, 0.0).astype(BF16)
    before = _dot(strict_lower, onehot.astype(BF16))
    slot = jnp.sum(onehot * (before + run_ref[0:1, :] + base_ref[0:1, :]), axis=-1, keepdims=True)
    run_ref[...] = run_ref[...] + jnp.sum(onehot, axis=0, keepdims=True)
    slot_t = jnp.broadcast_to(slot, (tb, LANES)).T
    pos_ref[...] = slot_t[0:SUBLANES, :].astype(I32)


def _rank(meta, counts, tb, n_tiles_pad):
    t = meta.shape[0]
    kern = functools.partial(_rank_kernel, tb=tb, n_tiles_pad=n_tiles_pad)
    tab = jnp.asarray(_class_expert_table())
    return pl.pallas_call(
        kern,
        grid=(t // tb,),
        in_specs=[pl.BlockSpec((tb, LANES), lambda i: (i, 0)),
                  pl.BlockSpec((SUBLANES, LANES), lambda i: (0, 0)),
                  pl.BlockSpec((SUBLANES, LANES), lambda i: (0, 0))],
        out_specs=[pl.BlockSpec((SUBLANES, tb), lambda i: (0, i)),
                   pl.BlockSpec((n_tiles_pad, LANES), lambda i: (0, 0))],
        out_shape=[jax.ShapeDtypeStruct((SUBLANES, t), I32),
                   jax.ShapeDtypeStruct((n_tiles_pad, LANES), I32)],
        scratch_shapes=[pltpu.VMEM((SUBLANES, LANES), F32), pltpu.VMEM((SUBLANES, LANES), F32)],
        compiler_params=pltpu.CompilerParams(dimension_semantics=("arbitrary",),
                                             vmem_limit_bytes=VMEM_LIMIT),
        name="rank",
    )(meta, counts, tab)


DMA_UNROLL = 8


def _dispatch_kernel(pos_ref, src_ref, zeros_ref, dst_ref, sem, *, tb, rpt):
    del zeros_ref
    base = pl.program_id(0) * tb

    def issue(grp, carry):
        for u in range(DMA_UNROLL):
            j = grp * DMA_UNROLL + u
            src = src_ref.at[pl.ds(pl.multiple_of(j * rpt, rpt), rpt)]
            dst = dst_ref.at[pl.ds(pl.multiple_of(pos_ref[base + j] * rpt, rpt), rpt)]
            pltpu.make_async_copy(src, dst, sem).start(priority=u % 2)
        return carry

    lax.fori_loop(0, tb // DMA_UNROLL, issue, 0)
    pltpu.make_async_copy(src_ref, dst_ref.at[pl.ds(0, tb * rpt)], sem).wait()


def _dispatch(pos, src, n_slots, rpt, tb):
    t = pos.shape[0]
    kern = functools.partial(_dispatch_kernel, tb=tb, rpt=rpt)
    return pl.pallas_call(
        kern,
        grid_spec=pltpu.PrefetchScalarGridSpec(
            num_scalar_prefetch=1,
            grid=(t // tb,),
            in_specs=[pl.BlockSpec((tb * rpt, LANES), lambda i, pos_ref: (i, 0)),
                      pl.BlockSpec(memory_space=pl.ANY)],
            out_specs=pl.BlockSpec(memory_space=pl.ANY),
            scratch_shapes=[pltpu.SemaphoreType.DMA(())],
        ),
        out_shape=jax.ShapeDtypeStruct((n_slots * rpt, LANES), src.dtype),
        input_output_aliases={2: 0},
        compiler_params=pltpu.CompilerParams(dimension_semantics=("arbitrary",),
                                             has_side_effects=True, vmem_limit_bytes=VMEM_LIMIT),
        name="dispatch",
    )(pos, src, jnp.zeros((n_slots * rpt, LANES), src.dtype))


def _collect_kernel(pos_ref, ys_ref, x1t_ref, meta_ref, lnw_ref, lnb_ref, out_ref, buf_ref, sems,
                    *, tb, alpha, d):
    segs = d // LANES
    rpt = 2 * segs
    step = pl.program_id(0)
    n_steps = pl.num_programs(0)

    def gather(blk, slot):
        base = blk * tb

        def issue(grp, carry):
            for u in range(DMA_UNROLL):
                j = grp * DMA_UNROLL + u
                src = ys_ref.at[pl.ds(pos_ref[base + j], 1)]
                dst = buf_ref.at[slot, pl.ds(j, 1)]
                pltpu.make_async_copy(src, dst, sems.at[slot]).start(priority=u % 2)
            return carry

        lax.fori_loop(0, tb // DMA_UNROLL, issue, 0)

    @pl.when(step == 0)
    def _():
        gather(0, 0)

    @pl.when(step + 1 < n_steps)
    def _():
        gather(step + 1, (step + 1) % 2)

    slot = step % 2
    pltpu.make_async_copy(ys_ref.at[pl.ds(0, tb)], buf_ref.at[slot], sems.at[slot]).wait()

    x1 = _load_token_tiles(x1t_ref, tb, segs)
    y_lo = buf_ref[slot, :, 0:d]
    y_hi = buf_ref[slot, :, d:2 * d]
    meta = meta_ref[...]
    z = alpha * x1 + meta[:, 1:2] * y_lo + meta[:, 2:3] * y_hi
    out_ref[...] = _layer_norm(z, lnw_ref[...], lnb_ref[...])


def _collect(pos, ys, x1t, meta, ln_w, ln_b, alpha, d, tb):
    t = pos.shape[0]
    segs = d // LANES
    kern = functools.partial(_collect_kernel, tb=tb, alpha=alpha, d=d)
    return pl.pallas_call(
        kern,
        grid_spec=pltpu.PrefetchScalarGridSpec(
            num_scalar_prefetch=1,
            grid=(t // tb,),
            in_specs=[pl.BlockSpec(memory_space=pl.ANY),
                      pl.BlockSpec((tb * segs, LANES), lambda i, pos_ref: (i, 0)),
                      pl.BlockSpec((tb, LANES), lambda i, pos_ref: (i, 0)),
                      pl.BlockSpec((1, d), lambda i, pos_ref: (0, 0)),
                      pl.BlockSpec((1, d), lambda i, pos_ref: (0, 0))],
            out_specs=pl.BlockSpec((tb, d), lambda i, pos_ref: (i, 0)),
            scratch_shapes=[pltpu.VMEM((2, tb, 2 * d), F32), pltpu.SemaphoreType.DMA((2,))],
        ),
        out_shape=jax.ShapeDtypeStruct((t, d), F32),
        compiler_params=pltpu.CompilerParams(dimension_semantics=("arbitrary",),
                                             vmem_limit_bytes=VMEM_LIMIT),
        name="collect",
    )(pos, ys, x1t, meta, ln_w, ln_b)


MOE_TILES_PER_STEP = 1


def _moe_kernel(rows_ref, elo_ref, ehi_ref, nt_ref, xs_ref, *refs, d):
    *w_refs, ys_ref = refs
    i = pl.program_id(0)
    segs = d // LANES

    @pl.when(rows_ref[i * MOE_TILES_PER_STEP] > 0)
    def _():
        for k in range(MOE_TILES_PER_STEP):
            wgu_lo, wd_lo, wgu_hi, wd_hi = w_refs[4 * k:4 * k + 4]
            xs_k = xs_ref.at[pl.ds(k * MOE_TILE * segs, MOE_TILE * segs)]
            xb = _load_token_tiles(xs_k, MOE_TILE, segs).astype(BF16)

            def expert(wgu_ref, wd_ref):
                de = wd_ref.shape[0]
                gate_up = _dot(xb, wgu_ref[...])
                gate = gate_up[:, 0:de]
                hidden = gate * _sigmoid(gate) * gate_up[:, de:]
                return _dot(hidden.astype(BF16), wd_ref[...])

            ys_ref[k * MOE_TILE:(k + 1) * MOE_TILE, 0:d] = expert(wgu_lo, wd_lo)
            ys_ref[k * MOE_TILE:(k + 1) * MOE_TILE, d:2 * d] = expert(wgu_hi, wd_hi)

    @pl.when(rows_ref[i * MOE_TILES_PER_STEP] <= 0)
    def _():
        ys_ref[...] = jnp.zeros_like(ys_ref)


def _moe(info, xs, wgu, wd, d, n_tiles_pad):
    de = wd.shape[1]
    segs = d // LANES
    tps = MOE_TILES_PER_STEP
    kern = functools.partial(_moe_kernel, d=d)

    def last_live(tile, nt_ref):
        return jnp.minimum(tile, jnp.maximum(nt_ref[0] - 1, 0))

    def up_spec(k, which):
        return pl.BlockSpec((None, d, 2 * de),
                            lambda i, rows, elo, ehi, nt: ((elo, ehi)[which][last_live(i * tps + k, nt)], 0, 0))

    def down_spec(k, which):
        return pl.BlockSpec((None, de, d),
                            lambda i, rows, elo, ehi, nt: ((elo, ehi)[which][last_live(i * tps + k, nt)], 0, 0))

    w_specs = []
    for k in range(tps):
        w_specs += [up_spec(k, 0), down_spec(k, 0), up_spec(k, 1), down_spec(k, 1)]
    rows, elo, ehi, nt = info[:, 0], info[:, 1], info[:, 2], info[0:1, 3]
    return pl.pallas_call(
        kern,
        grid_spec=pltpu.PrefetchScalarGridSpec(
            num_scalar_prefetch=4,
            grid=(n_tiles_pad // tps,),
            in_specs=[pl.BlockSpec((tps * MOE_TILE * segs, LANES),
                                   lambda i, rows, elo, ehi, nt: (last_live(i * tps, nt) // tps, 0))] + w_specs,
            out_specs=pl.BlockSpec((tps * MOE_TILE, 2 * d), lambda i, rows, elo, ehi, nt: (i, 0)),
        ),
        out_shape=jax.ShapeDtypeStruct((n_tiles_pad * MOE_TILE, 2 * d), F32),
        compiler_params=pltpu.CompilerParams(dimension_semantics=("arbitrary",),
                                             vmem_limit_bytes=VMEM_LIMIT),
        name="moe",
    )(rows, elo, ehi, nt, xs, *([wgu, wd, wgu, wd] * tps))


def _pick_block(n, target):
    blk = min(n, target)
    while n % blk:
        blk //= 2
    return blk


def kernel(x, w_in, conv_w, conv_b, mlstm_gate_bias, mlstm_norm_w, attn_sinks, w_out, ln1_w, ln1_b,
           w_group_router, b_group_router, w_expert_router, b_expert_router,
           w_exp_gate, w_exp_up, w_exp_down, ln2_w, ln2_b):
    b, s, d = x.shape
    t = b * s
    depth = w_in.shape[0]
    alpha = (2.0 * depth) ** 0.25
    assert s % ML_CHUNK == 0 and s % WINDOW == 0 and d % LANES == 0

    tm = _pick_block(t, 512)
    tq = _pick_block(s, 4 * ML_CHUNK)
    tb_rank = _pick_block(t, 512)
    tb_dma = _pick_block(t, 2048)
    tb_col = _pick_block(t, 512)
    n_tiles_pad = -(-(t // MOE_TILE + N_CLASSES) // SUBLANES) * SUBLANES
    cos_t, sin_t = _rope_tables(s)

    for l in range(depth):
        x2d = x.reshape(t, d)
        q, k, v, og, aq, ak, av, g = _inproj(x2d, _pack_w_in(w_in[l]), conv_w[l], conv_b[l][None, :], tm, s // tm)
        bias_pad = jnp.zeros((LANES - ML_HEADS,), F32)
        gate_bias_row = jnp.concatenate(
            [mlstm_gate_bias[l, 0], bias_pad, mlstm_gate_bias[l, 1], bias_pad])[None, :]
        ml = _mlstm(q.reshape(b, s, -1), k.reshape(b, s, -1), v.reshape(b, s, -1), og.reshape(b, s, -1),
                    g.reshape(b, s, -1), gate_bias_row, mlstm_norm_w[l][None, :], tq)
        att = _swa(aq.reshape(b, s, -1), ak.reshape(b, s, -1), av.reshape(b, s, -1), cos_t, sin_t, attn_sinks[l],
                   1)

        w_router, b_router = _pack_router(w_group_router[l], b_group_router[l],
                                          w_expert_router[l], b_expert_router[l])
        x1t, meta, counts = _outproj(x2d, ml.reshape(t, -1), att.reshape(t, -1), w_out[l].astype(BF16),
                                     ln1_w[l][None, :], ln1_b[l][None, :], w_router, b_router, alpha, tm)

        pos2d, info = _rank(meta, counts, tb_rank, n_tiles_pad)
        pos = pos2d[0]
        xs = _dispatch(pos, x1t, n_tiles_pad * MOE_TILE, d // LANES, tb_dma)
        w_gate_up = jnp.concatenate([w_exp_gate[l].astype(BF16), w_exp_up[l].astype(BF16)], axis=-1)
        ys = _moe(info, xs, w_gate_up, w_exp_down[l].astype(BF16), d, n_tiles_pad)
        out = _collect(pos, ys, x1t, meta, ln2_w[l][None, :], ln2_b[l][None, :], alpha, d, tb_col)
        x = out.reshape(b, s, d)
    return x
```

```python
import functools
import math

import numpy as np
import jax
import jax.numpy as jnp
from jax import lax
from jax.experimental import pallas as pl
from jax.experimental.pallas import tpu as pltpu

F32 = jnp.float32
BF16 = jnp.bfloat16
I32 = jnp.int32

ML_HEADS = 4
ML_HEAD_DIM = 128
ML_WIDTH = ML_HEADS * ML_HEAD_DIM
ML_CHUNK = 128
CONV_WIDTH = 4
ATT_Q_HEADS = 8
ATT_KV_HEADS = 2
ATT_HEAD_DIM = 64
ATT_WIDTH = ATT_Q_HEADS * ATT_HEAD_DIM
ATT_KV_WIDTH = ATT_KV_HEADS * ATT_HEAD_DIM
WINDOW = 128
ROPE_THETA = 10000.0
N_GROUPS = 4
EXPERTS_PER_GROUP = 8
N_EXPERTS = N_GROUPS * EXPERTS_PER_GROUP
PAIRS_PER_GROUP = EXPERTS_PER_GROUP * (EXPERTS_PER_GROUP - 1) // 2
N_CLASSES = N_GROUPS * PAIRS_PER_GROUP
LN_EPS = 1e-5

LANES = 128
SUBLANES = 8
MOE_TILE = 320
VMEM_LIMIT = 56 * 1024 * 1024

NEG_INF = float("-inf")


def _sigmoid(x):
    return 1.0 / (1.0 + jnp.exp(-x))


def _log_sigmoid(x):
    return jnp.minimum(x, 0.0) - jnp.log(1.0 + jnp.exp(-jnp.abs(x)))


def _iota(shape, dim):
    return lax.broadcasted_iota(I32, shape, dim)


def _dot(a, b):
    return jnp.dot(a, b, preferred_element_type=F32)


def _dot_exact(a, b):
    return jnp.dot(a, b, preferred_element_type=F32, precision=lax.Precision.HIGHEST)


def _layer_norm(z, w, b):
    mu = jnp.mean(z, axis=-1, keepdims=True)
    zc = z - mu
    var = jnp.mean(zc * zc, axis=-1, keepdims=True)
    return zc * lax.rsqrt(var + LN_EPS) * w + b


C_QK = 0
C_V = C_QK + 2 * ML_WIDTH
C_O = C_V + ML_WIDTH
C_AQ = C_O + ML_WIDTH
C_AK = C_AQ + ATT_WIDTH
C_AV = C_AK + ATT_KV_HEADS * LANES
C_G = C_AV + ATT_KV_HEADS * LANES
C_END = C_G + 2 * LANES


def _pack_w_in(w_in):
    sizes = (2 * ML_WIDTH, ML_WIDTH, ML_WIDTH, ML_HEADS, ML_HEADS, ATT_WIDTH, ATT_KV_WIDTH, ATT_KV_WIDTH)
    splits = np.cumsum(sizes)[:-1].tolist()
    w_qk, w_v, w_o, w_i, w_f, w_aq, w_ak, w_av = jnp.split(w_in, splits, axis=-1)
    half = ATT_HEAD_DIM // 2

    def head(w, h):
        return w[:, h * ATT_HEAD_DIM:(h + 1) * ATT_HEAD_DIM]

    def q_tile(a, b):
        return [a[:, :half], b[:, :half], a[:, half:], b[:, half:]]

    q_cols = [t for p in range(ATT_Q_HEADS // 2) for t in q_tile(head(w_aq, 2 * p), head(w_aq, 2 * p + 1))]
    k_cols = [t for h in range(ATT_KV_HEADS) for t in q_tile(head(w_ak, h), head(w_ak, h))]
    v_cols = [t for h in range(ATT_KV_HEADS) for t in (head(w_av, h), head(w_av, h))]
    lane_pad = jnp.zeros((w_in.shape[0], LANES - ML_HEADS), w_in.dtype)
    packed = jnp.concatenate([w_qk, w_v, w_o] + q_cols + k_cols + v_cols + [w_i, lane_pad, w_f, lane_pad], axis=-1)
    return packed.astype(BF16)


def _inproj_kernel(x_ref, w_ref, cw_ref, cb_ref, q_ref, k_ref, v_ref, og_ref, aq_ref, ak_ref, av_ref, g_ref,
                   *scratch, blocks_per_seq):
    *ext_refs, xb_ref = scratch
    tm = x_ref.shape[0]
    halo = SUBLANES
    cs = ext_refs[0].shape[1]
    xb_ref[...] = x_ref[...].astype(BF16)

    @pl.when(pl.program_id(0) % blocks_per_seq == 0)
    def _():
        for ext_ref in ext_refs:
            ext_ref[0:halo, :] = jnp.zeros((halo, cs), F32)

    scale = ML_HEAD_DIM ** -0.5

    def mm(lo, hi):
        return _dot(xb_ref[...], w_ref[:, lo:hi])

    def conv_slice(idx):
        ext_ref = ext_refs[idx]
        c0 = idx * cs
        is_q = c0 < ML_WIDTH
        dst_ref, off = (q_ref, c0) if is_q else (k_ref, c0 - ML_WIDTH)
        rt = ML_CHUNK
        for r0 in range(0, tm, rt):
            conv = cb_ref[:, c0:c0 + cs]
            for j in range(CONV_WIDTH):
                start = halo + r0 - (CONV_WIDTH - 1) + j
                conv = conv + cw_ref[j:j + 1, c0:c0 + cs] * ext_ref[start:start + rt, :]
            act = conv * _sigmoid(conv)
            dst_ref[r0:r0 + rt, off:off + cs] = (act if is_q else act * scale).astype(BF16)
        ext_ref[0:halo, :] = ext_ref[tm:tm + halo, :]

    for idx in range(len(ext_refs)):
        ext_refs[idx][halo:halo + tm, :] = mm(C_QK + idx * cs, C_QK + (idx + 1) * cs)
        if idx > 0:
            conv_slice(idx - 1)
    half_v = ML_WIDTH // 2
    v_ref[:, 0:half_v] = mm(C_V, C_V + half_v).astype(BF16)
    conv_slice(len(ext_refs) - 1)
    v_ref[:, half_v:] = mm(C_V + half_v, C_O).astype(BF16)
    og_ref[...] = _sigmoid(mm(C_O, C_AQ)).astype(BF16)
    aq_ref[...] = mm(C_AQ, C_AK)
    ak_ref[...] = mm(C_AK, C_AV)
    av_ref[...] = mm(C_AV, C_G).astype(BF16)
    g_ref[...] = mm(C_G, C_END)


def _inproj(x2d, w_packed, conv_w, conv_b, tm, blocks_per_seq):
    t, d = x2d.shape
    widths = (ML_WIDTH, ML_WIDTH, ML_WIDTH, ML_WIDTH, C_AK - C_AQ, C_AV - C_AK, C_G - C_AV, C_END - C_G)
    dtypes = (BF16, BF16, BF16, BF16, F32, F32, BF16, F32)
    kern = functools.partial(_inproj_kernel, blocks_per_seq=blocks_per_seq)
    return pl.pallas_call(
        kern,
        grid=(t // tm,),
        in_specs=[pl.BlockSpec((tm, d), lambda i: (i, 0)),
                  pl.BlockSpec((d, C_END), lambda i: (0, 0)),
                  pl.BlockSpec((CONV_WIDTH, 2 * ML_WIDTH), lambda i: (0, 0)),
                  pl.BlockSpec((1, 2 * ML_WIDTH), lambda i: (0, 0))],
        out_specs=[pl.BlockSpec((tm, w), lambda i: (i, 0)) for w in widths],
        out_shape=[jax.ShapeDtypeStruct((t, w), dt) for w, dt in zip(widths, dtypes)],
        scratch_shapes=[pltpu.VMEM((tm + SUBLANES, 2 * LANES), F32)] * (2 * ML_WIDTH // (2 * LANES))
        + [pltpu.VMEM((tm, d), BF16)],
        compiler_params=pltpu.CompilerParams(dimension_semantics=("arbitrary",),
                                             vmem_limit_bytes=VMEM_LIMIT),
        name="inproj",
    )(x2d, w_packed, conv_w, conv_b)


def _time_scan(x, combine, identity):
    row = _iota(x.shape, 0)
    sh = 1
    while sh < x.shape[0]:
        x = combine(x, jnp.where(row >= sh, pltpu.roll(x, sh, 0), identity))
        sh *= 2
    return x


def _mlstm_kernel(q_ref, k_ref, v_ref, og_ref, g_ref, gb_ref, nw_ref, out_ref, ct_ref, m_ref, *, tq):
    s_idx = pl.program_id(1)
    L = ML_CHUNK
    D = ML_HEAD_DIM
    H = ML_HEADS
    heads = range(H)

    @pl.when(s_idx == 0)
    def _():
        ct_ref[...] = jnp.zeros_like(ct_ref)
        m_ref[...] = jnp.zeros_like(m_ref)

    causal = _iota((L, L), 1) <= _iota((L, L), 0)
    ones_blk = jnp.ones((L, D), BF16)
    mean_blk = jnp.full((D, D), 1.0 / D, BF16)
    m_prev = m_ref[0:1, :]
    head_lanes = _iota((L, LANES), 1) < H
    tile_of_lane = jnp.right_shift(_iota((LANES, H * L), 1), L.bit_length() - 1)
    spread = jnp.where(_iota((LANES, H * L), 0) == tile_of_lane, 1.0, 0.0).astype(BF16)

    def spread_heads(x):
        x = jnp.where(head_lanes, x, 0.0)
        hi = x.astype(BF16)
        lo = (x - hi.astype(F32)).astype(BF16)
        return _dot(hi, spread) + _dot(lo, spread)

    for c in range(tq // L):
        r0 = c * L
        gi = g_ref[r0:r0 + L, 0:LANES] + gb_ref[:, 0:LANES]
        gf = g_ref[r0:r0 + L, LANES:2 * LANES] + gb_ref[:, LANES:2 * LANES]
        b_cum = _time_scan(_log_sigmoid(gf), jnp.add, 0.0)
        r = gi - b_cum
        g = jnp.maximum(m_prev, _time_scan(r, jnp.maximum, NEG_INF))
        g_rep = spread_heads(g)
        b_rep = spread_heads(b_cum)
        b_last = b_cum[L - 1:L, :]
        m_new = jnp.maximum(b_last + m_prev, jnp.max(b_last + r, axis=0, keepdims=True))
        decay = jnp.exp(b_last + m_prev - m_new)
        shift = b_last - m_new
        r_t = r.T

        q_b = [q_ref[r0:r0 + L, h * D:(h + 1) * D] for h in heads]
        kt_b = [k_ref[r0:r0 + L, h * D:(h + 1) * D].T for h in heads]
        v_aug = [jnp.concatenate([v_ref[r0:r0 + L, h * D:(h + 1) * D], ones_blk], axis=-1) for h in heads]
        g_col = [g_rep[:, h * L:(h + 1) * L] for h in heads]
        w_intra = [jnp.exp(jnp.where(causal, r_t[h:h + 1, :] - g_col[h], NEG_INF)) for h in heads]
        s_b = [(_dot(q_b[h], kt_b[h]) * w_intra[h]).astype(BF16) for h in heads]
        ct = [ct_ref[h] for h in heads]
        inter = [_dot(q_b[h], ct[h].astype(BF16)) for h in heads]
        intra = [_dot(s_b[h], v_aug[h]) for h in heads]
        for h in heads:
            wi_col = jnp.exp(m_prev[:, h:h + 1] - g_col[h])
            clamp = jnp.exp(-(b_rep[:, h * L:(h + 1) * L] + g_col[h]))
            num = wi_col * inter[h][:, 0:D] + intra[h][:, 0:D]
            den = wi_col * inter[h][:, D:] + intra[h][:, D:]
            hh = num / jnp.maximum(jnp.abs(den), clamp)
            mu = _dot(hh.astype(BF16), mean_blk)
            hc = hh - mu
            var = _dot((hc * hc).astype(BF16), mean_blk)
            hn = hc * lax.rsqrt(var + LN_EPS) * nw_ref[:, h * D:(h + 1) * D]
            gate_o = og_ref[r0:r0 + L, h * D:(h + 1) * D].astype(F32)
            out_ref[r0:r0 + L, h * D:(h + 1) * D] = (gate_o * hn).astype(out_ref.dtype)
        for h in heads:
            w_row = jnp.exp(r_t[h:h + 1, :] + shift[:, h:h + 1])
            ktw = (kt_b[h].astype(F32) * w_row).astype(BF16)
            ct_ref[h] = decay[:, h:h + 1] * ct[h] + _dot(ktw, v_aug[h])
        m_prev = m_new

    m_ref[...] = jnp.broadcast_to(m_prev, m_ref.shape)


def _mlstm(q, k, v, og, g, gate_bias_row, norm_w_row, tq):
    b, s, _ = q.shape
    kern = functools.partial(_mlstm_kernel, tq=tq)

    def seq_spec(width):
        return pl.BlockSpec((None, tq, width), lambda bi, si: (bi, si, 0))

    def const_spec(shape):
        return pl.BlockSpec(shape, lambda bi, si: (0,) * len(shape))

    return pl.pallas_call(
        kern,
        grid=(b, s // tq),
        in_specs=[seq_spec(ML_WIDTH), seq_spec(ML_WIDTH), seq_spec(ML_WIDTH), seq_spec(ML_WIDTH),
                  seq_spec(2 * LANES), const_spec((1, 2 * LANES)), const_spec((1, ML_WIDTH))],
        out_specs=seq_spec(ML_WIDTH),
        out_shape=jax.ShapeDtypeStruct((b, s, ML_WIDTH), BF16),
        scratch_shapes=[pltpu.VMEM((ML_HEADS, ML_HEAD_DIM, 2 * ML_HEAD_DIM), F32),
                        pltpu.VMEM((SUBLANES, LANES), F32)],
        compiler_params=pltpu.CompilerParams(dimension_semantics=("arbitrary", "arbitrary"),
                                             vmem_limit_bytes=VMEM_LIMIT),
        name="mlstm",
    )(q, k, v, og, g, gate_bias_row, norm_w_row)


def _rope_tables(seq_len):
    half = ATT_HEAD_DIM // 2
    inv_freq = ROPE_THETA ** (-jnp.arange(half, dtype=F32) / half)
    ang = jnp.arange(seq_len, dtype=F32)[:, None] * inv_freq[None, :]
    cos = jnp.cos(ang)
    sin = jnp.sin(ang)
    cos_t = jnp.concatenate([cos, cos, cos, cos], axis=-1)
    sin_t = jnp.concatenate([-sin, -sin, sin, sin], axis=-1)
    return cos_t, sin_t


def _swa_kernel(sink_ref, aq_ref, ak_ref, av_ref, cos_ref, sin_ref, out_ref, kprev_ref, vprev_ref, *, nsub):
    step = pl.program_id(1)
    Lb = WINDOW
    half = ATT_HEAD_DIM // 2
    pairs = ATT_Q_HEADS // 2
    pairs_per_kv = pairs // ATT_KV_HEADS

    @pl.when(step == 0)
    def _():
        kprev_ref[...] = jnp.zeros_like(kprev_ref)
        vprev_ref[...] = jnp.zeros_like(vprev_ref)

    def rope(x, cos, sin):
        tiles = []
        for c in range(x.shape[-1] // LANES):
            xt = x[:, c * LANES:(c + 1) * LANES]
            tiles.append(xt * cos + pltpu.roll(xt, LANES // 2, 1) * sin)
        return jnp.concatenate(tiles, axis=-1)

    ql = _iota((Lb, 2 * Lb), 0)
    kj = _iota((Lb, 2 * Lb), 1)
    diff = Lb + ql - kj
    lane = _iota((Lb, LANES), 1)
    low_half = lane < ATT_HEAD_DIM
    first_head = (lane & half) == 0
    ones_blk = jnp.ones((2 * Lb, LANES), BF16)

    k_prev = kprev_ref
    v_prev = vprev_ref
    for j in range(nsub):
        r0 = j * Lb
        cos = cos_ref[r0:r0 + Lb, :]
        sin = sin_ref[r0:r0 + Lb, :]
        q = rope(aq_ref[r0:r0 + Lb, :], cos, sin) * (ATT_HEAD_DIM ** -0.5)
        k_cur = rope(ak_ref[r0:r0 + Lb, :], cos, sin).astype(BF16)
        v_cur = av_ref[r0:r0 + Lb, :]

        kpos = (step * nsub + j) * Lb + kj - Lb
        visible = jnp.where(diff >= 0, jnp.where(diff < WINDOW, jnp.where(kpos >= 0, 1, 0), 0), 0)
        bias = jnp.where(visible > 0, 0.0, NEG_INF).astype(F32)
        bias = jnp.concatenate([bias] * (2 * pairs_per_kv), axis=0)

        for g in range(ATT_KV_HEADS):
            kk = jnp.concatenate([k_prev[:, g * LANES:(g + 1) * LANES], k_cur[:, g * LANES:(g + 1) * LANES]], axis=0)
            vv = jnp.concatenate([v_prev[:, g * LANES:(g + 1) * LANES], v_cur[:, g * LANES:(g + 1) * LANES]], axis=0)
            vv_aug = jnp.concatenate([vv, ones_blk], axis=-1)
            rows = []
            sinks = []
            for p in range(pairs_per_kv):
                pair = g * pairs_per_kv + p
                q2 = q[:, pair * LANES:(pair + 1) * LANES]
                rows.append(jnp.where(first_head, q2, 0.0))
                rows.append(jnp.where(first_head, 0.0, q2))
                sinks.append(jnp.full((Lb, LANES), sink_ref[2 * pair], F32))
                sinks.append(jnp.full((Lb, LANES), sink_ref[2 * pair + 1], F32))
            qs = jnp.concatenate(rows, axis=0).astype(BF16)
            sink = jnp.concatenate(sinks, axis=0)

            sc = lax.dot_general(qs, kk, (((1,), (1,)), ((), ())), preferred_element_type=F32) + bias
            m = jnp.maximum(jnp.broadcast_to(jnp.max(sc, axis=-1, keepdims=True), sink.shape), sink)
            p_un = jnp.exp(sc - jnp.concatenate([m, m], axis=-1))
            acc = _dot(p_un.astype(BF16), vv_aug)
            o = acc[:, 0:LANES] / (acc[:, LANES:] + jnp.exp(sink - m))
            for p in range(pairs_per_kv):
                pair = g * pairs_per_kv + p
                even = o[(2 * p) * Lb:(2 * p + 1) * Lb, :]
                odd = o[(2 * p + 1) * Lb:(2 * p + 2) * Lb, :]
                out_ref[r0:r0 + Lb, pair * LANES:(pair + 1) * LANES] = (
                    jnp.where(low_half, even, odd).astype(out_ref.dtype))
        k_prev = k_cur
        v_prev = v_cur

    kprev_ref[...] = k_prev
    vprev_ref[...] = v_prev


def _swa(aq, ak, av, cos_t, sin_t, sinks, nsub):
    b, s, _ = aq.shape
    kvw = ATT_KV_HEADS * LANES
    rows = nsub * WINDOW

    def seq_spec(width):
        return pl.BlockSpec((None, rows, width), lambda bi, si: (bi, si, 0))

    tab_spec = pl.BlockSpec((rows, LANES), lambda bi, si: (si, 0))
    return pl.pallas_call(
        functools.partial(_swa_kernel, nsub=nsub),
        grid=(b, s // rows),
        in_specs=[pl.BlockSpec(memory_space=pltpu.SMEM),
                  seq_spec(ATT_WIDTH), seq_spec(kvw), seq_spec(kvw), tab_spec, tab_spec],
        out_specs=seq_spec(ATT_WIDTH),
        out_shape=jax.ShapeDtypeStruct((b, s, ATT_WIDTH), BF16),
        scratch_shapes=[pltpu.VMEM((WINDOW, kvw), BF16), pltpu.VMEM((WINDOW, kvw), BF16)],
        compiler_params=pltpu.CompilerParams(dimension_semantics=("arbitrary", "arbitrary"),
                                             vmem_limit_bytes=VMEM_LIMIT),
        name="swa",
    )(sinks, aq, ak, av, cos_t, sin_t)


def _store_token_tiles(ref, val, row0=0, rows_per_token=None):
    n, w = val.shape
    segs = w // LANES
    rpt = rows_per_token or segs
    for j in range(segs):
        ref[pl.ds(row0 + j, n, stride=rpt), :] = val[:, j * LANES:(j + 1) * LANES]


def _load_token_tiles(ref, n, segs, row0=0, rows_per_token=None):
    rpt = rows_per_token or segs
    return jnp.concatenate([ref[pl.ds(row0 + j, n, stride=rpt), :] for j in range(segs)], axis=-1)


ROUTER_ROWS = 48


def _pack_router(w_group, b_group, w_expert, b_expert):
    d = w_group.shape[0]
    wt = jnp.zeros((ROUTER_ROWS, d), F32)
    wt = wt.at[0:N_GROUPS].set(w_group.T).at[SUBLANES:SUBLANES + N_EXPERTS].set(w_expert.T)
    bias = jnp.zeros((ROUTER_ROWS,), F32)
    bias = bias.at[0:N_GROUPS].set(b_group).at[SUBLANES:SUBLANES + N_EXPERTS].set(b_expert)
    hi = wt.astype(BF16)
    lo = (wt - hi.astype(F32)).astype(BF16)
    return jnp.concatenate([hi, lo], axis=0), jnp.broadcast_to(bias[:, None], (ROUTER_ROWS, LANES))


def _outproj_kernel(x_ref, ml_ref, att_ref, wo_ref, lnw_ref, lnb_ref, wrt_ref, brt_ref,
                    x1t_ref, meta_ref, cnt_ref, *, alpha):
    step = pl.program_id(0)
    tm, d = x_ref.shape
    segs = d // LANES
    pt = tm
    rr = wrt_ref.shape[0] // 2
    nt = (((1,), (1,)), ((), ()))
    row = _iota((SUBLANES, pt), 0).astype(F32)
    lane = _iota((pt, LANES), 1).astype(F32)

    def first_argmax(vals):
        top = jnp.max(vals, axis=0, keepdims=True)
        idx = jnp.min(jnp.where(vals == top, row, float(SUBLANES)), axis=0, keepdims=True)
        return top, idx

    @pl.when(step == 0)
    def _():
        cnt_ref[...] = jnp.zeros_like(cnt_ref)

    counts = jnp.zeros((1, LANES), F32)
    for r0 in range(0, tm, pt):
        y = (_dot(ml_ref[r0:r0 + pt, :], wo_ref[0:ML_WIDTH, :])
             + _dot(att_ref[r0:r0 + pt, :], wo_ref[ML_WIDTH:, :]))
        x1 = _layer_norm(alpha * x_ref[r0:r0 + pt, :] + y, lnw_ref[...], lnb_ref[...])
        _store_token_tiles(x1t_ref.at[pl.ds(r0 * segs, pt * segs)], x1)

        x1_hi = x1.astype(BF16)
        x1_lo = (x1 - x1_hi.astype(F32)).astype(BF16)
        both = lax.dot_general(wrt_ref[...], x1_hi, nt, preferred_element_type=F32)
        cross = lax.dot_general(wrt_ref[0:rr, :], x1_lo, nt, preferred_element_type=F32)
        logits = both[0:rr] + both[rr:2 * rr] + cross + jnp.concatenate([brt_ref[...]] * (pt // LANES), axis=1)

        g_logits = jnp.where(row < N_GROUPS, logits[0:SUBLANES], NEG_INF)
        g_top, g_idx = first_argmax(g_logits)
        g_p = 1.0 / jnp.sum(jnp.exp(g_logits - g_top), axis=0, keepdims=True)

        e_logits = logits[SUBLANES:2 * SUBLANES]
        for grp in range(1, N_GROUPS):
            e_logits = jnp.where(g_idx == grp, logits[(1 + grp) * SUBLANES:(2 + grp) * SUBLANES], e_logits)
        v1, a1 = first_argmax(e_logits)
        v2, a2 = first_argmax(jnp.where(row == a1, NEG_INF, e_logits))
        r = jnp.exp(v2 - v1)
        w1 = g_p / (1.0 + r)
        w2 = g_p * r / (1.0 + r)

        lo = jnp.minimum(a1, a2)
        hi = jnp.maximum(a1, a2)
        w_lo = jnp.where(a1 < a2, w1, w2)
        w_hi = jnp.where(a1 < a2, w2, w1)
        pair_idx = (EXPERTS_PER_GROUP - 1) * lo - lo * (lo - 1.0) * 0.5 + (hi - lo - 1.0)
        cls = g_idx * PAIRS_PER_GROUP + pair_idx

        meta_t = jnp.where(row == 0.0, cls, jnp.where(row == 1.0, w_lo, jnp.where(row == 2.0, w_hi, 0.0)))
        meta = jnp.concatenate([meta_t, jnp.zeros((LANES - SUBLANES, pt), F32)], axis=0).T
        meta_ref[r0:r0 + pt, :] = meta
        counts = counts + jnp.sum(jnp.where(lane == meta[:, 0:1], 1.0, 0.0), axis=0, keepdims=True)

    cnt_ref[0:1, :] += counts


def _outproj(x2d, ml2d, att2d, w_out_b, ln_w, ln_b, w_router, b_router, alpha, tm):
    t, d = x2d.shape
    kern = functools.partial(_outproj_kernel, alpha=alpha)

    def const_spec(shape):
        return pl.BlockSpec(shape, lambda i: (0,) * len(shape))

    return pl.pallas_call(
        kern,
        grid=(t // tm,),
        in_specs=[pl.BlockSpec((tm, d), lambda i: (i, 0)),
                  pl.BlockSpec((tm, ML_WIDTH), lambda i: (i, 0)),
                  pl.BlockSpec((tm, ATT_WIDTH), lambda i: (i, 0)),
                  const_spec((ML_WIDTH + ATT_WIDTH, d)), const_spec((1, d)), const_spec((1, d)),
                  const_spec((2 * ROUTER_ROWS, d)), const_spec((ROUTER_ROWS, LANES))],
        out_specs=[pl.BlockSpec((tm * (d // LANES), LANES), lambda i: (i, 0)),
                   pl.BlockSpec((tm, LANES), lambda i: (i, 0)), const_spec((SUBLANES, LANES))],
        out_shape=[jax.ShapeDtypeStruct((t * (d // LANES), LANES), F32),
                   jax.ShapeDtypeStruct((t, LANES), F32),
                   jax.ShapeDtypeStruct((SUBLANES, LANES), F32)],
        compiler_params=pltpu.CompilerParams(dimension_semantics=("arbitrary",),
                                             vmem_limit_bytes=VMEM_LIMIT),
        name="outproj",
    )(x2d, ml2d, att2d, w_out_b, ln_w, ln_b, w_router, b_router)


def _class_expert_table():
    tab = np.zeros((SUBLANES, LANES), np.float32)
    for g in range(N_GROUPS):
        idx = 0
        for lo in range(EXPERTS_PER_GROUP):
            for hi in range(lo + 1, EXPERTS_PER_GROUP):
                c = g * PAIRS_PER_GROUP + idx
                tab[0, c] = g * EXPERTS_PER_GROUP + lo
                tab[1, c] = g * EXPERTS_PER_GROUP + hi
                idx += 1
    return tab


def _rank_kernel(meta_ref, cnt_ref, tab_ref, pos_ref, tile_ref, base_ref, run_ref, *, tb, n_tiles_pad):
    step = pl.program_id(0)
    lane8 = _iota((SUBLANES, LANES), 1)

    @pl.when(step == 0)
    def _():
        cnt = jnp.broadcast_to(cnt_ref[0:1, :], (SUBLANES, LANES))
        tiles = jnp.floor((cnt + (MOE_TILE - 1.0)) * (1.0 / MOE_TILE))
        cum = tiles
        sh = 1
        while sh < LANES:
            cum = cum + jnp.where(lane8 >= sh, pltpu.roll(cum, sh, 1), 0.0)
            sh *= 2
        excl = cum - tiles
        base_ref[...] = excl * MOE_TILE
        run_ref[...] = jnp.zeros_like(run_ref)

        ti = _iota((n_tiles_pad, LANES), 0).astype(F32)
        lane = _iota((n_tiles_pad, LANES), 1)
        done = jnp.where(lane < N_CLASSES, jnp.where(cum[0:1, :] <= ti, 1.0, 0.0), 0.0)
        t_cls = jnp.sum(done, axis=-1, keepdims=True)
        sel = jnp.where(lane.astype(F32) == t_cls, 1.0, 0.0)
        cnt_i = jnp.sum(sel * cnt[0:1, :], axis=-1, keepdims=True)
        first_i = jnp.sum(sel * excl[0:1, :], axis=-1, keepdims=True)
        rows_i = jnp.clip(cnt_i - MOE_TILE * (ti[:, 0:1] - first_i), 0.0, float(MOE_TILE))
        e_lo = jnp.sum(sel * tab_ref[0:1, :], axis=-1, keepdims=True)
        e_hi = jnp.sum(sel * tab_ref[1:2, :], axis=-1, keepdims=True)
        n_tiles = jnp.sum(jnp.where(lane < N_CLASSES, jnp.broadcast_to(tiles[0:1, :], (n_tiles_pad, LANES)), 0.0),
                          axis=-1, keepdims=True)
        info = jnp.where(lane == 0, rows_i,
                         jnp.where(lane == 1, e_lo, jnp.where(lane == 2, e_hi, jnp.where(lane == 3, n_tiles, 0.0))))
        tile_ref[...] = info.astype(I32)

    cls = meta_ref[:, 0:1]
    lane = _iota((tb, LANES), 1).astype(F32)
    onehot = jnp.where(lane == cls, 1.0, 0.0)
    strict_lower = jnp.where(_iota((tb, tb), 1) < _iota((tb, tb), 0), 1.0, 0.0).astype(BF16)
    before = _dot(strict_lower, onehot.astype(BF16))
    slot = jnp.sum(onehot * (before + run_ref[0:1, :] + base_ref[0:1, :]), axis=-1, keepdims=True)
    run_ref[...] = run_ref[...] + jnp.sum(onehot, axis=0, keepdims=True)
    slot_t = jnp.broadcast_to(slot, (tb, LANES)).T
    pos_ref[...] = slot_t[0:SUBLANES, :].astype(I32)


def _rank(meta, counts, tb, n_tiles_pad):
    t = meta.shape[0]
    kern = functools.partial(_rank_kernel, tb=tb, n_tiles_pad=n_tiles_pad)
    tab = jnp.asarray(_class_expert_table())
    return pl.pallas_call(
        kern,
        grid=(t // tb,),
        in_specs=[pl.BlockSpec((tb, LANES), lambda i: (i, 0)),
                  pl.BlockSpec((SUBLANES, LANES), lambda i: (0, 0)),
                  pl.BlockSpec((SUBLANES, LANES), lambda i: (0, 0))],
        out_specs=[pl.BlockSpec((SUBLANES, tb), lambda i: (0, i)),
                   pl.BlockSpec((n_tiles_pad, LANES), lambda i: (0, 0))],
        out_shape=[jax.ShapeDtypeStruct((SUBLANES, t), I32),
                   jax.ShapeDtypeStruct((n_tiles_pad, LANES), I32)],
        scratch_shapes=[pltpu.VMEM((SUBLANES, LANES), F32), pltpu.VMEM((SUBLANES, LANES), F32)],
        compiler_params=pltpu.CompilerParams(dimension_semantics=("arbitrary",),
                                             vmem_limit_bytes=VMEM_LIMIT),
        name="rank",
    )(meta, counts, tab)


DMA_UNROLL = 8


def _dispatch_kernel(pos_ref, src_ref, meta_ref, zeros_x_ref, zeros_m_ref, dst_ref, dmeta_ref, sems, *, tb, rpt):
    del zeros_x_ref, zeros_m_ref
    base = pl.program_id(0) * tb

    def issue(grp, carry):
        for u in range(DMA_UNROLL):
            j = grp * DMA_UNROLL + u
            slot = pos_ref[base + j]
            src = src_ref.at[pl.ds(pl.multiple_of(j * rpt, rpt), rpt)]
            dst = dst_ref.at[pl.ds(pl.multiple_of(slot * rpt, rpt), rpt)]
            pltpu.make_async_copy(src, dst, sems.at[0]).start(priority=u % 2)
            pltpu.make_async_copy(meta_ref.at[pl.ds(j, 1)], dmeta_ref.at[pl.ds(slot, 1)],
                                  sems.at[1]).start(priority=(u + 1) % 2)
        return carry

    lax.fori_loop(0, tb // DMA_UNROLL, issue, 0)
    pltpu.make_async_copy(src_ref, dst_ref.at[pl.ds(0, tb * rpt)], sems.at[0]).wait()
    pltpu.make_async_copy(meta_ref, dmeta_ref.at[pl.ds(0, tb)], sems.at[1]).wait()


def _dispatch(pos, src, meta, n_slots, rpt, tb):
    t = pos.shape[0]
    kern = functools.partial(_dispatch_kernel, tb=tb, rpt=rpt)
    return pl.pallas_call(
        kern,
        grid_spec=pltpu.PrefetchScalarGridSpec(
            num_scalar_prefetch=1,
            grid=(t // tb,),
            in_specs=[pl.BlockSpec((tb * rpt, LANES), lambda i, pos_ref: (i, 0)),
                      pl.BlockSpec((tb, LANES), lambda i, pos_ref: (i, 0)),
                      pl.BlockSpec(memory_space=pl.ANY), pl.BlockSpec(memory_space=pl.ANY)],
            out_specs=[pl.BlockSpec(memory_space=pl.ANY), pl.BlockSpec(memory_space=pl.ANY)],
            scratch_shapes=[pltpu.SemaphoreType.DMA((2,))],
        ),
        out_shape=[jax.ShapeDtypeStruct((n_slots * rpt, LANES), src.dtype),
                   jax.ShapeDtypeStruct((n_slots, LANES), meta.dtype)],
        input_output_aliases={3: 0, 4: 1},
        compiler_params=pltpu.CompilerParams(dimension_semantics=("arbitrary",),
                                             has_side_effects=True, vmem_limit_bytes=VMEM_LIMIT),
        name="dispatch",
    )(pos, src, meta, jnp.zeros((n_slots * rpt, LANES), src.dtype), jnp.zeros((n_slots, LANES), meta.dtype))


def _collect_kernel(pos_ref, ys_ref, x1t_ref, lnw_ref, lnb_ref, out_ref, buf_ref, sems, *, tb, alpha, d):
    segs = d // LANES
    step = pl.program_id(0)
    n_steps = pl.num_programs(0)

    def gather(blk, slot):
        base = blk * tb

        def issue(grp, carry):
            for u in range(DMA_UNROLL):
                j = grp * DMA_UNROLL + u
                src = ys_ref.at[pl.ds(pos_ref[base + j], 1)]
                dst = buf_ref.at[slot, pl.ds(j, 1)]
                pltpu.make_async_copy(src, dst, sems.at[slot]).start(priority=u % 2)
            return carry

        lax.fori_loop(0, tb // DMA_UNROLL, issue, 0)

    @pl.when(step == 0)
    def _():
        gather(0, 0)

    @pl.when(step + 1 < n_steps)
    def _():
        gather(step + 1, (step + 1) % 2)

    slot = step % 2
    pltpu.make_async_copy(ys_ref.at[pl.ds(0, tb)], buf_ref.at[slot], sems.at[slot]).wait()

    x1 = _load_token_tiles(x1t_ref, tb, segs)
    out_ref[...] = _layer_norm(alpha * x1 + buf_ref[slot], lnw_ref[...], lnb_ref[...])


def _collect(pos, ys, x1t, ln_w, ln_b, alpha, d, tb):
    t = pos.shape[0]
    segs = d // LANES
    kern = functools.partial(_collect_kernel, tb=tb, alpha=alpha, d=d)
    return pl.pallas_call(
        kern,
        grid_spec=pltpu.PrefetchScalarGridSpec(
            num_scalar_prefetch=1,
            grid=(t // tb,),
            in_specs=[pl.BlockSpec(memory_space=pl.ANY),
                      pl.BlockSpec((tb * segs, LANES), lambda i, pos_ref: (i, 0)),
                      pl.BlockSpec((1, d), lambda i, pos_ref: (0, 0)),
                      pl.BlockSpec((1, d), lambda i, pos_ref: (0, 0))],
            out_specs=pl.BlockSpec((tb, d), lambda i, pos_ref: (i, 0)),
            scratch_shapes=[pltpu.VMEM((2, tb, d), F32), pltpu.SemaphoreType.DMA((2,))],
        ),
        out_shape=jax.ShapeDtypeStruct((t, d), F32),
        compiler_params=pltpu.CompilerParams(dimension_semantics=("arbitrary",),
                                             vmem_limit_bytes=VMEM_LIMIT),
        name="collect",
    )(pos, ys, x1t, ln_w, ln_b)


MOE_TILES_PER_STEP = 1


def _moe_kernel(rows_ref, elo_ref, ehi_ref, nt_ref, xs_ref, ms_ref, *refs, d):
    *w_refs, ys_ref = refs
    i = pl.program_id(0)
    segs = d // LANES

    @pl.when(rows_ref[i * MOE_TILES_PER_STEP] > 0)
    def _():
        for k in range(MOE_TILES_PER_STEP):
            wgu_lo, wd_lo, wgu_hi, wd_hi = w_refs[4 * k:4 * k + 4]
            xs_k = xs_ref.at[pl.ds(k * MOE_TILE * segs, MOE_TILE * segs)]
            xb = _load_token_tiles(xs_k, MOE_TILE, segs).astype(BF16)

            def expert(wgu_ref, wd_ref):
                de = wd_ref.shape[0]
                gate_up = _dot(xb, wgu_ref[...])
                gate = gate_up[:, 0:de]
                hidden = gate * _sigmoid(gate) * gate_up[:, de:]
                return _dot(hidden.astype(BF16), wd_ref[...])

            meta = ms_ref[k * MOE_TILE:(k + 1) * MOE_TILE, :]
            ys_ref[k * MOE_TILE:(k + 1) * MOE_TILE, :] = (meta[:, 1:2] * expert(wgu_lo, wd_lo)
                                                          + meta[:, 2:3] * expert(wgu_hi, wd_hi))

    @pl.when(rows_ref[i * MOE_TILES_PER_STEP] <= 0)
    def _():
        ys_ref[...] = jnp.zeros_like(ys_ref)


def _moe(info, xs, ms, wgu, wd, d, n_tiles_pad):
    de = wd.shape[1]
    segs = d // LANES
    tps = MOE_TILES_PER_STEP
    kern = functools.partial(_moe_kernel, d=d)

    def last_live(tile, nt_ref):
        return jnp.minimum(tile, jnp.maximum(nt_ref[0] - 1, 0))

    def up_spec(k, which):
        return pl.BlockSpec((None, d, 2 * de),
                            lambda i, rows, elo, ehi, nt: ((elo, ehi)[which][last_live(i * tps + k, nt)], 0, 0))

    def down_spec(k, which):
        return pl.BlockSpec((None, de, d),
                            lambda i, rows, elo, ehi, nt: ((elo, ehi)[which][last_live(i * tps + k, nt)], 0, 0))

    w_specs = []
    for k in range(tps):
        w_specs += [up_spec(k, 0), down_spec(k, 0), up_spec(k, 1), down_spec(k, 1)]
    rows, elo, ehi, nt = info[:, 0], info[:, 1], info[:, 2], info[0:1, 3]
    return pl.pallas_call(
        kern,
        grid_spec=pltpu.PrefetchScalarGridSpec(
            num_scalar_prefetch=4,
            grid=(n_tiles_pad // tps,),
            in_specs=[pl.BlockSpec((tps * MOE_TILE * segs, LANES),
                                   lambda i, rows, elo, ehi, nt: (last_live(i * tps, nt) // tps, 0)),
                      pl.BlockSpec((tps * MOE_TILE, LANES),
                                   lambda i, rows, elo, ehi, nt: (last_live(i * tps, nt) // tps, 0))] + w_specs,
            out_specs=pl.BlockSpec((tps * MOE_TILE, d), lambda i, rows, elo, ehi, nt: (i, 0)),
        ),
        out_shape=jax.ShapeDtypeStruct((n_tiles_pad * MOE_TILE, d), F32),
        compiler_params=pltpu.CompilerParams(dimension_semantics=("arbitrary",),
                                             vmem_limit_bytes=VMEM_LIMIT),
        name="moe",
    )(rows, elo, ehi, nt, xs, ms, *([wgu, wd, wgu, wd] * tps))


def _pick_block(n, target):
    blk = min(n, target)
    while n % blk:
        blk //= 2
    return blk


def kernel(x, w_in, conv_w, conv_b, mlstm_gate_bias, mlstm_norm_w, attn_sinks, w_out, ln1_w, ln1_b,
           w_group_router, b_group_router, w_expert_router, b_expert_router,
           w_exp_gate, w_exp_up, w_exp_down, ln2_w, ln2_b):
    b, s, d = x.shape
    t = b * s
    depth = w_in.shape[0]
    alpha = (2.0 * depth) ** 0.25
    assert s % ML_CHUNK == 0 and s % WINDOW == 0 and d % LANES == 0

    tm = _pick_block(t, 512)
    tq = _pick_block(s, 4 * ML_CHUNK)
    tb_rank = _pick_block(t, 512)
    tb_dma = _pick_block(t, 2048)
    tb_col = _pick_block(t, 512)
    n_tiles_pad = -(-(t // MOE_TILE + N_CLASSES) // SUBLANES) * SUBLANES
    cos_t, sin_t = _rope_tables(s)

    for l in range(depth):
        x2d = x.reshape(t, d)
        q, k, v, og, aq, ak, av, g = _inproj(x2d, _pack_w_in(w_in[l]), conv_w[l], conv_b[l][None, :], tm, s // tm)
        bias_pad = jnp.zeros((LANES - ML_HEADS,), F32)
        gate_bias_row = jnp.concatenate(
            [mlstm_gate_bias[l, 0], bias_pad, mlstm_gate_bias[l, 1], bias_pad])[None, :]
        ml = _mlstm(q.reshape(b, s, -1), k.reshape(b, s, -1), v.reshape(b, s, -1), og.reshape(b, s, -1),
                    g.reshape(b, s, -1), gate_bias_row, mlstm_norm_w[l][None, :], tq)
        att = _swa(aq.reshape(b, s, -1), ak.reshape(b, s, -1), av.reshape(b, s, -1), cos_t, sin_t, attn_sinks[l],
                   1)

        w_router, b_router = _pack_router(w_group_router[l], b_group_router[l],
                                          w_expert_router[l], b_expert_router[l])
        x1t, meta, counts = _outproj(x2d, ml.reshape(t, -1), att.reshape(t, -1), w_out[l].astype(BF16),
                                     ln1_w[l][None, :], ln1_b[l][None, :], w_router, b_router, alpha, tm)

        pos2d, info = _rank(meta, counts, tb_rank, n_tiles_pad)
        pos = pos2d[0]
        xs, ms = _dispatch(pos, x1t, meta, n_tiles_pad * MOE_TILE, d // LANES, tb_dma)
        w_gate_up = jnp.concatenate([w_exp_gate[l].astype(BF16), w_exp_up[l].astype(BF16)], axis=-1)
        ys = _moe(info, xs, ms, w_gate_up, w_exp_down[l].astype(BF16), d, n_tiles_pad)
        out = _collect(pos, ys, x1t, ln2_w[l][None, :], ln2_b[l][None, :], alpha, d, tb_col)
        x = out.reshape(b, s, d)
    return x
```

```python
import functools
import math

import numpy as np
import jax
import jax.numpy as jnp
from jax import lax
from jax.experimental import pallas as pl
from jax.experimental.pallas import tpu as pltpu

F32 = jnp.float32
BF16 = jnp.bfloat16
I32 = jnp.int32

ML_HEADS = 4
ML_HEAD_DIM = 128
ML_WIDTH = ML_HEADS * ML_HEAD_DIM
ML_CHUNK = 128
CONV_WIDTH = 4
ATT_Q_HEADS = 8
ATT_KV_HEADS = 2
ATT_HEAD_DIM = 64
ATT_WIDTH = ATT_Q_HEADS * ATT_HEAD_DIM
ATT_KV_WIDTH = ATT_KV_HEADS * ATT_HEAD_DIM
WINDOW = 128
ROPE_THETA = 10000.0
N_GROUPS = 4
EXPERTS_PER_GROUP = 8
N_EXPERTS = N_GROUPS * EXPERTS_PER_GROUP
PAIRS_PER_GROUP = EXPERTS_PER_GROUP * (EXPERTS_PER_GROUP - 1) // 2
N_CLASSES = N_GROUPS * PAIRS_PER_GROUP
LN_EPS = 1e-5

LANES = 128
SUBLANES = 8
MOE_TILE = 320
VMEM_LIMIT = 56 * 1024 * 1024

NEG_INF = float("-inf")


def _sigmoid(x):
    return 1.0 / (1.0 + jnp.exp(-x))


def _log_sigmoid(x):
    return jnp.minimum(x, 0.0) - jnp.log(1.0 + jnp.exp(-jnp.abs(x)))


def _iota(shape, dim):
    return lax.broadcasted_iota(I32, shape, dim)


def _dot(a, b):
    return jnp.dot(a, b, preferred_element_type=F32)


def _dot_exact(a, b):
    return jnp.dot(a, b, preferred_element_type=F32, precision=lax.Precision.HIGHEST)


def _layer_norm(z, w, b):
    mu = jnp.mean(z, axis=-1, keepdims=True)
    zc = z - mu
    var = jnp.mean(zc * zc, axis=-1, keepdims=True)
    return zc * lax.rsqrt(var + LN_EPS) * w + b


C_QK = 0
C_V = C_QK + 2 * ML_WIDTH
C_O = C_V + ML_WIDTH
C_AQ = C_O + ML_WIDTH
C_AK = C_AQ + ATT_WIDTH
C_AV = C_AK + ATT_KV_HEADS * LANES
C_G = C_AV + ATT_KV_HEADS * LANES
C_END = C_G + 2 * LANES


def _pack_w_in(w_in):
    sizes = (2 * ML_WIDTH, ML_WIDTH, ML_WIDTH, ML_HEADS, ML_HEADS, ATT_WIDTH, ATT_KV_WIDTH, ATT_KV_WIDTH)
    splits = np.cumsum(sizes)[:-1].tolist()
    w_qk, w_v, w_o, w_i, w_f, w_aq, w_ak, w_av = jnp.split(w_in, splits, axis=-1)
    half = ATT_HEAD_DIM // 2

    def head(w, h):
        return w[:, h * ATT_HEAD_DIM:(h + 1) * ATT_HEAD_DIM]

    def q_tile(a, b):
        return [a[:, :half], b[:, :half], a[:, half:], b[:, half:]]

    q_cols = [t for p in range(ATT_Q_HEADS // 2) for t in q_tile(head(w_aq, 2 * p), head(w_aq, 2 * p + 1))]
    k_cols = [t for h in range(ATT_KV_HEADS) for t in q_tile(head(w_ak, h), head(w_ak, h))]
    v_cols = [t for h in range(ATT_KV_HEADS) for t in (head(w_av, h), head(w_av, h))]
    lane_pad = jnp.zeros((w_in.shape[0], LANES - ML_HEADS), w_in.dtype)
    packed = jnp.concatenate([w_qk, w_v, w_o] + q_cols + k_cols + v_cols + [w_i, lane_pad, w_f, lane_pad], axis=-1)
    return packed.astype(BF16)


def _inproj_kernel(x_ref, w_ref, cw_ref, cb_ref, q_ref, k_ref, v_ref, og_ref, aq_ref, ak_ref, av_ref, g_ref,
                   *scratch, blocks_per_seq):
    *ext_refs, xb_ref = scratch
    tm = x_ref.shape[0]
    halo = SUBLANES
    cs = ext_refs[0].shape[1]
    xb_ref[...] = x_ref[...].astype(BF16)

    @pl.when(pl.program_id(0) % blocks_per_seq == 0)
    def _():
        for ext_ref in ext_refs:
            ext_ref[0:halo, :] = jnp.zeros((halo, cs), F32)

    scale = ML_HEAD_DIM ** -0.5

    def mm(lo, hi):
        return _dot(xb_ref[...], w_ref[:, lo:hi])

    def conv_slice(idx):
        ext_ref = ext_refs[idx]
        c0 = idx * cs
        is_q = c0 < ML_WIDTH
        dst_ref, off = (q_ref, c0) if is_q else (k_ref, c0 - ML_WIDTH)
        rt = ML_CHUNK
        for r0 in range(0, tm, rt):
            conv = cb_ref[:, c0:c0 + cs]
            for j in range(CONV_WIDTH):
                start = halo + r0 - (CONV_WIDTH - 1) + j
                conv = conv + cw_ref[j:j + 1, c0:c0 + cs] * ext_ref[start:start + rt, :]
            act = conv * _sigmoid(conv)
            dst_ref[r0:r0 + rt, off:off + cs] = (act if is_q else act * scale).astype(BF16)
        ext_ref[0:halo, :] = ext_ref[tm:tm + halo, :]

    for idx in range(len(ext_refs)):
        ext_refs[idx][halo:halo + tm, :] = mm(C_QK + idx * cs, C_QK + (idx + 1) * cs)
        if idx > 0:
            conv_slice(idx - 1)
    half_v = ML_WIDTH // 2
    v_ref[:, 0:half_v] = mm(C_V, C_V + half_v).astype(BF16)
    conv_slice(len(ext_refs) - 1)
    v_ref[:, half_v:] = mm(C_V + half_v, C_O).astype(BF16)
    og_ref[...] = _sigmoid(mm(C_O, C_AQ)).astype(BF16)
    aq_ref[...] = mm(C_AQ, C_AK)
    ak_ref[...] = mm(C_AK, C_AV)
    av_ref[...] = mm(C_AV, C_G).astype(BF16)
    g_ref[...] = mm(C_G, C_END)


def _inproj(x2d, w_packed, conv_w, conv_b, tm, blocks_per_seq):
    t, d = x2d.shape
    widths = (ML_WIDTH, ML_WIDTH, ML_WIDTH, ML_WIDTH, C_AK - C_AQ, C_AV - C_AK, C_G - C_AV, C_END - C_G)
    dtypes = (BF16, BF16, BF16, BF16, F32, F32, BF16, F32)
    kern = functools.partial(_inproj_kernel, blocks_per_seq=blocks_per_seq)
    return pl.pallas_call(
        kern,
        grid=(t // tm,),
        in_specs=[pl.BlockSpec((tm, d), lambda i: (i, 0)),
                  pl.BlockSpec((d, C_END), lambda i: (0, 0)),
                  pl.BlockSpec((CONV_WIDTH, 2 * ML_WIDTH), lambda i: (0, 0)),
                  pl.BlockSpec((1, 2 * ML_WIDTH), lambda i: (0, 0))],
        out_specs=[pl.BlockSpec((tm, w), lambda i: (i, 0)) for w in widths],
        out_shape=[jax.ShapeDtypeStruct((t, w), dt) for w, dt in zip(widths, dtypes)],
        scratch_shapes=[pltpu.VMEM((tm + SUBLANES, 2 * LANES), F32)] * (2 * ML_WIDTH // (2 * LANES))
        + [pltpu.VMEM((tm, d), BF16)],
        compiler_params=pltpu.CompilerParams(dimension_semantics=("arbitrary",),
                                             vmem_limit_bytes=VMEM_LIMIT),
        name="inproj",
    )(x2d, w_packed, conv_w, conv_b)


def _time_scan(x, combine, identity):
    row = _iota(x.shape, 0)
    sh = 1
    while sh < x.shape[0]:
        x = combine(x, jnp.where(row >= sh, pltpu.roll(x, sh, 0), identity))
        sh *= 2
    return x


def _mlstm_kernel(q_ref, k_ref, v_ref, og_ref, g_ref, gb_ref, nw_ref, out_ref, ct_ref, m_ref, *, tq):
    s_idx = pl.program_id(1)
    L = ML_CHUNK
    D = ML_HEAD_DIM
    H = ML_HEADS
    heads = range(H)

    @pl.when(s_idx == 0)
    def _():
        ct_ref[...] = jnp.zeros_like(ct_ref)
        m_ref[...] = jnp.zeros_like(m_ref)

    causal = _iota((L, L), 1) <= _iota((L, L), 0)
    ones_blk = jnp.ones((L, D), BF16)
    mean_blk = jnp.full((D, D), 1.0 / D, BF16)
    m_prev = m_ref[0:1, :]
    head_lanes = _iota((L, LANES), 1) < H
    tile_of_lane = jnp.right_shift(_iota((LANES, H * L), 1), L.bit_length() - 1)
    spread = jnp.where(_iota((LANES, H * L), 0) == tile_of_lane, 1.0, 0.0).astype(BF16)

    def spread_heads(x):
        x = jnp.where(head_lanes, x, 0.0)
        hi = x.astype(BF16)
        lo = (x - hi.astype(F32)).astype(BF16)
        return _dot(hi, spread) + _dot(lo, spread)

    for c in range(tq // L):
        r0 = c * L
        gi = g_ref[r0:r0 + L, 0:LANES] + gb_ref[:, 0:LANES]
        gf = g_ref[r0:r0 + L, LANES:2 * LANES] + gb_ref[:, LANES:2 * LANES]
        b_cum = _time_scan(_log_sigmoid(gf), jnp.add, 0.0)
        r = gi - b_cum
        g = jnp.maximum(m_prev, _time_scan(r, jnp.maximum, NEG_INF))
        g_rep = spread_heads(g)
        b_rep = spread_heads(b_cum)
        b_last = b_cum[L - 1:L, :]
        m_new = jnp.maximum(b_last + m_prev, jnp.max(b_last + r, axis=0, keepdims=True))
        decay = jnp.exp(b_last + m_prev - m_new)
        shift = b_last - m_new
        r_t = r.T

        q_b = [q_ref[r0:r0 + L, h * D:(h + 1) * D] for h in heads]
        kt_b = [k_ref[r0:r0 + L, h * D:(h + 1) * D].T for h in heads]
        v_aug = [jnp.concatenate([v_ref[r0:r0 + L, h * D:(h + 1) * D], ones_blk], axis=-1) for h in heads]
        g_col = [g_rep[:, h * L:(h + 1) * L] for h in heads]
        w_intra = [jnp.exp(jnp.where(causal, r_t[h:h + 1, :] - g_col[h], NEG_INF)) for h in heads]
        s_b = [(_dot(q_b[h], kt_b[h]) * w_intra[h]).astype(BF16) for h in heads]
        ct = [ct_ref[h] for h in heads]
        inter = [_dot(q_b[h], ct[h].astype(BF16)) for h in heads]
        intra = [_dot(s_b[h], v_aug[h]) for h in heads]
        for h in heads:
            wi_col = jnp.exp(m_prev[:, h:h + 1] - g_col[h])
            clamp = jnp.exp(-(b_rep[:, h * L:(h + 1) * L] + g_col[h]))
            num = wi_col * inter[h][:, 0:D] + intra[h][:, 0:D]
            den = wi_col * inter[h][:, D:] + intra[h][:, D:]
            hh = num / jnp.maximum(jnp.abs(den), clamp)
            mu = _dot(hh.astype(BF16), mean_blk)
            hc = hh - mu
            var = _dot((hc * hc).astype(BF16), mean_blk)
            hn = hc * lax.rsqrt(var + LN_EPS) * nw_ref[:, h * D:(h + 1) * D]
            gate_o = og_ref[r0:r0 + L, h * D:(h + 1) * D].astype(F32)
            out_ref[r0:r0 + L, h * D:(h + 1) * D] = (gate_o * hn).astype(out_ref.dtype)
        for h in heads:
            w_row = jnp.exp(r_t[h:h + 1, :] + shift[:, h:h + 1])
            ktw = (kt_b[h].astype(F32) * w_row).astype(BF16)
            ct_ref[h] = decay[:, h:h + 1] * ct[h] + _dot(ktw, v_aug[h])
        m_prev = m_new

    m_ref[...] = jnp.broadcast_to(m_prev, m_ref.shape)


def _mlstm(q, k, v, og, g, gate_bias_row, norm_w_row, tq):
    b, s, _ = q.shape
    kern = functools.partial(_mlstm_kernel, tq=tq)

    def seq_spec(width):
        return pl.BlockSpec((None, tq, width), lambda bi, si: (bi, si, 0))

    def const_spec(shape):
        return pl.BlockSpec(shape, lambda bi, si: (0,) * len(shape))

    return pl.pallas_call(
        kern,
        grid=(b, s // tq),
        in_specs=[seq_spec(ML_WIDTH), seq_spec(ML_WIDTH), seq_spec(ML_WIDTH), seq_spec(ML_WIDTH),
                  seq_spec(2 * LANES), const_spec((1, 2 * LANES)), const_spec((1, ML_WIDTH))],
        out_specs=seq_spec(ML_WIDTH),
        out_shape=jax.ShapeDtypeStruct((b, s, ML_WIDTH), BF16),
        scratch_shapes=[pltpu.VMEM((ML_HEADS, ML_HEAD_DIM, 2 * ML_HEAD_DIM), F32),
                        pltpu.VMEM((SUBLANES, LANES), F32)],
        compiler_params=pltpu.CompilerParams(dimension_semantics=("arbitrary", "arbitrary"),
                                             vmem_limit_bytes=VMEM_LIMIT),
        name="mlstm",
    )(q, k, v, og, g, gate_bias_row, norm_w_row)


def _rope_tables(seq_len):
    half = ATT_HEAD_DIM // 2
    inv_freq = ROPE_THETA ** (-jnp.arange(half, dtype=F32) / half)
    ang = jnp.arange(seq_len, dtype=F32)[:, None] * inv_freq[None, :]
    cos = jnp.cos(ang)
    sin = jnp.sin(ang)
    cos_t = jnp.concatenate([cos, cos, cos, cos], axis=-1)
    sin_t = jnp.concatenate([-sin, -sin, sin, sin], axis=-1)
    return cos_t, sin_t


def _swa_kernel(sink_ref, aq_ref, ak_ref, av_ref, cos_ref, sin_ref, out_ref, kprev_ref, vprev_ref, *, nsub):
    step = pl.program_id(1)
    Lb = WINDOW
    half = ATT_HEAD_DIM // 2
    pairs = ATT_Q_HEADS // 2
    pairs_per_kv = pairs // ATT_KV_HEADS

    @pl.when(step == 0)
    def _():
        kprev_ref[...] = jnp.zeros_like(kprev_ref)
        vprev_ref[...] = jnp.zeros_like(vprev_ref)

    def rope(x, cos, sin):
        tiles = []
        for c in range(x.shape[-1] // LANES):
            xt = x[:, c * LANES:(c + 1) * LANES]
            tiles.append(xt * cos + pltpu.roll(xt, LANES // 2, 1) * sin)
        return jnp.concatenate(tiles, axis=-1)

    ql = _iota((Lb, 2 * Lb), 0)
    kj = _iota((Lb, 2 * Lb), 1)
    diff = Lb + ql - kj
    lane = _iota((Lb, LANES), 1)
    low_half = lane < ATT_HEAD_DIM
    first_head = (lane & half) == 0
    ones_blk = jnp.ones((2 * Lb, LANES), BF16)

    k_prev = kprev_ref
    v_prev = vprev_ref
    for j in range(nsub):
        r0 = j * Lb
        cos = cos_ref[r0:r0 + Lb, :]
        sin = sin_ref[r0:r0 + Lb, :]
        q = rope(aq_ref[r0:r0 + Lb, :], cos, sin) * (ATT_HEAD_DIM ** -0.5)
        k_cur = rope(ak_ref[r0:r0 + Lb, :], cos, sin).astype(BF16)
        v_cur = av_ref[r0:r0 + Lb, :]

        kpos = (step * nsub + j) * Lb + kj - Lb
        visible = jnp.where(diff >= 0, jnp.where(diff < WINDOW, jnp.where(kpos >= 0, 1, 0), 0), 0)
        bias = jnp.where(visible > 0, 0.0, NEG_INF).astype(F32)
        bias = jnp.concatenate([bias] * (2 * pairs_per_kv), axis=0)

        for g in range(ATT_KV_HEADS):
            kk = jnp.concatenate([k_prev[:, g * LANES:(g + 1) * LANES], k_cur[:, g * LANES:(g + 1) * LANES]], axis=0)
            vv = jnp.concatenate([v_prev[:, g * LANES:(g + 1) * LANES], v_cur[:, g * LANES:(g + 1) * LANES]], axis=0)
            vv_aug = jnp.concatenate([vv, ones_blk], axis=-1)
            rows = []
            sinks = []
            for p in range(pairs_per_kv):
                pair = g * pairs_per_kv + p
                q2 = q[:, pair * LANES:(pair + 1) * LANES]
                rows.append(jnp.where(first_head, q2, 0.0))
                rows.append(jnp.where(first_head, 0.0, q2))
                sinks.append(jnp.full((Lb, LANES), sink_ref[2 * pair], F32))
                sinks.append(jnp.full((Lb, LANES), sink_ref[2 * pair + 1], F32))
            qs = jnp.concatenate(rows, axis=0).astype(BF16)
            sink = jnp.concatenate(sinks, axis=0)

            sc = lax.dot_general(qs, kk, (((1,), (1,)), ((), ())), preferred_element_type=F32) + bias
            m = jnp.maximum(jnp.broadcast_to(jnp.max(sc, axis=-1, keepdims=True), sink.shape), sink)
            p_un = jnp.exp(sc - jnp.concatenate([m, m], axis=-1))
            acc = _dot(p_un.astype(BF16), vv_aug)
            o = acc[:, 0:LANES] / (acc[:, LANES:] + jnp.exp(sink - m))
            for p in range(pairs_per_kv):
                pair = g * pairs_per_kv + p
                even = o[(2 * p) * Lb:(2 * p + 1) * Lb, :]
                odd = o[(2 * p + 1) * Lb:(2 * p + 2) * Lb, :]
                out_ref[r0:r0 + Lb, pair * LANES:(pair + 1) * LANES] = (
                    jnp.where(low_half, even, odd).astype(out_ref.dtype))
        k_prev = k_cur
        v_prev = v_cur

    kprev_ref[...] = k_prev
    vprev_ref[...] = v_prev


def _swa(aq, ak, av, cos_t, sin_t, sinks, nsub):
    b, s, _ = aq.shape
    kvw = ATT_KV_HEADS * LANES
    rows = nsub * WINDOW

    def seq_spec(width):
        return pl.BlockSpec((None, rows, width), lambda bi, si: (bi, si, 0))

    tab_spec = pl.BlockSpec((rows, LANES), lambda bi, si: (si, 0))
    return pl.pallas_call(
        functools.partial(_swa_kernel, nsub=nsub),
        grid=(b, s // rows),
        in_specs=[pl.BlockSpec(memory_space=pltpu.SMEM),
                  seq_spec(ATT_WIDTH), seq_spec(kvw), seq_spec(kvw), tab_spec, tab_spec],
        out_specs=seq_spec(ATT_WIDTH),
        out_shape=jax.ShapeDtypeStruct((b, s, ATT_WIDTH), BF16),
        scratch_shapes=[pltpu.VMEM((WINDOW, kvw), BF16), pltpu.VMEM((WINDOW, kvw), BF16)],
        compiler_params=pltpu.CompilerParams(dimension_semantics=("arbitrary", "arbitrary"),
                                             vmem_limit_bytes=VMEM_LIMIT),
        name="swa",
    )(sinks, aq, ak, av, cos_t, sin_t)


def _store_token_tiles(ref, val, row0=0, rows_per_token=None):
    n, w = val.shape
    segs = w // LANES
    rpt = rows_per_token or segs
    for j in range(segs):
        ref[pl.ds(row0 + j, n, stride=rpt), :] = val[:, j * LANES:(j + 1) * LANES]


def _load_token_tiles(ref, n, segs, row0=0, rows_per_token=None):
    rpt = rows_per_token or segs
    return jnp.concatenate([ref[pl.ds(row0 + j, n, stride=rpt), :] for j in range(segs)], axis=-1)


ROUTER_ROWS = 48


def _pack_router(w_group, b_group, w_expert, b_expert):
    d = w_group.shape[0]
    wt = jnp.zeros((ROUTER_ROWS, d), F32)
    wt = wt.at[0:N_GROUPS].set(w_group.T).at[SUBLANES:SUBLANES + N_EXPERTS].set(w_expert.T)
    bias = jnp.zeros((ROUTER_ROWS,), F32)
    bias = bias.at[0:N_GROUPS].set(b_group).at[SUBLANES:SUBLANES + N_EXPERTS].set(b_expert)
    hi = wt.astype(BF16)
    lo = (wt - hi.astype(F32)).astype(BF16)
    return jnp.concatenate([hi, lo], axis=0), jnp.broadcast_to(bias[:, None], (ROUTER_ROWS, LANES))


def _outproj_kernel(x_ref, ml_ref, att_ref, wo_ref, lnw_ref, lnb_ref, wrt_ref, brt_ref,
                    x1t_ref, meta_ref, cnt_ref, *, alpha):
    step = pl.program_id(0)
    tm, d = x_ref.shape
    segs = d // LANES
    pt = tm
    rr = wrt_ref.shape[0] // 2
    nt = (((1,), (1,)), ((), ()))
    row = _iota((SUBLANES, pt), 0).astype(F32)
    lane = _iota((pt, LANES), 1).astype(F32)

    def first_argmax(vals):
        top = jnp.max(vals, axis=0, keepdims=True)
        idx = jnp.min(jnp.where(vals == top, row, float(SUBLANES)), axis=0, keepdims=True)
        return top, idx

    @pl.when(step == 0)
    def _():
        cnt_ref[...] = jnp.zeros_like(cnt_ref)

    counts = jnp.zeros((1, LANES), F32)
    for r0 in range(0, tm, pt):
        y = (_dot(ml_ref[r0:r0 + pt, :], wo_ref[0:ML_WIDTH, :])
             + _dot(att_ref[r0:r0 + pt, :], wo_ref[ML_WIDTH:, :]))
        x1 = _layer_norm(alpha * x_ref[r0:r0 + pt, :] + y, lnw_ref[...], lnb_ref[...])
        _store_token_tiles(x1t_ref.at[pl.ds(r0 * segs, pt * segs)], x1)

        x1_hi = x1.astype(BF16)
        x1_lo = (x1 - x1_hi.astype(F32)).astype(BF16)
        both = lax.dot_general(wrt_ref[...], x1_hi, nt, preferred_element_type=F32)
        cross = lax.dot_general(wrt_ref[0:rr, :], x1_lo, nt, preferred_element_type=F32)
        logits = both[0:rr] + both[rr:2 * rr] + cross + jnp.concatenate([brt_ref[...]] * (pt // LANES), axis=1)

        g_logits = jnp.where(row < N_GROUPS, logits[0:SUBLANES], NEG_INF)
        g_top, g_idx = first_argmax(g_logits)
        g_p = 1.0 / jnp.sum(jnp.exp(g_logits - g_top), axis=0, keepdims=True)

        e_logits = logits[SUBLANES:2 * SUBLANES]
        for grp in range(1, N_GROUPS):
            e_logits = jnp.where(g_idx == grp, logits[(1 + grp) * SUBLANES:(2 + grp) * SUBLANES], e_logits)
        v1, a1 = first_argmax(e_logits)
        v2, a2 = first_argmax(jnp.where(row == a1, NEG_INF, e_logits))
        r = jnp.exp(v2 - v1)
        w1 = g_p / (1.0 + r)
        w2 = g_p * r / (1.0 + r)

        lo = jnp.minimum(a1, a2)
        hi = jnp.maximum(a1, a2)
        w_lo = jnp.where(a1 < a2, w1, w2)
        w_hi = jnp.where(a1 < a2, w2, w1)
        pair_idx = (EXPERTS_PER_GROUP - 1) * lo - lo * (lo - 1.0) * 0.5 + (hi - lo - 1.0)
        cls = g_idx * PAIRS_PER_GROUP + pair_idx

        token = (_iota((SUBLANES, pt), 1) + (step * tm + r0)).astype(F32)
        meta_t = jnp.where(row == 0.0, cls, jnp.where(row == 1.0, w_lo, jnp.where(row == 2.0, w_hi,
                           jnp.where(row == 3.0, token, 0.0))))
        meta = jnp.concatenate([meta_t, jnp.zeros((LANES - SUBLANES, pt), F32)], axis=0).T
        meta_ref[r0:r0 + pt, :] = meta
        counts = counts + jnp.sum(jnp.where(lane == meta[:, 0:1], 1.0, 0.0), axis=0, keepdims=True)

    cnt_ref[0:1, :] += counts


def _outproj(x2d, ml2d, att2d, w_out_b, ln_w, ln_b, w_router, b_router, alpha, tm):
    t, d = x2d.shape
    kern = functools.partial(_outproj_kernel, alpha=alpha)

    def const_spec(shape):
        return pl.BlockSpec(shape, lambda i: (0,) * len(shape))

    return pl.pallas_call(
        kern,
        grid=(t // tm,),
        in_specs=[pl.BlockSpec((tm, d), lambda i: (i, 0)),
                  pl.BlockSpec((tm, ML_WIDTH), lambda i: (i, 0)),
                  pl.BlockSpec((tm, ATT_WIDTH), lambda i: (i, 0)),
                  const_spec((ML_WIDTH + ATT_WIDTH, d)), const_spec((1, d)), const_spec((1, d)),
                  const_spec((2 * ROUTER_ROWS, d)), const_spec((ROUTER_ROWS, LANES))],
        out_specs=[pl.BlockSpec((tm * (d // LANES), LANES), lambda i: (i, 0)),
                   pl.BlockSpec((tm, LANES), lambda i: (i, 0)), const_spec((SUBLANES, LANES))],
        out_shape=[jax.ShapeDtypeStruct((t * (d // LANES), LANES), F32),
                   jax.ShapeDtypeStruct((t, LANES), F32),
                   jax.ShapeDtypeStruct((SUBLANES, LANES), F32)],
        compiler_params=pltpu.CompilerParams(dimension_semantics=("arbitrary",),
                                             vmem_limit_bytes=VMEM_LIMIT),
        name="outproj",
    )(x2d, ml2d, att2d, w_out_b, ln_w, ln_b, w_router, b_router)


def _class_expert_table():
    tab = np.zeros((SUBLANES, LANES), np.float32)
    for g in range(N_GROUPS):
        idx = 0
        for lo in range(EXPERTS_PER_GROUP):
            for hi in range(lo + 1, EXPERTS_PER_GROUP):
                c = g * PAIRS_PER_GROUP + idx
                tab[0, c] = g * EXPERTS_PER_GROUP + lo
                tab[1, c] = g * EXPERTS_PER_GROUP + hi
                idx += 1
    return tab


def _rank_kernel(meta_ref, cnt_ref, tab_ref, pos_ref, tile_ref, base_ref, run_ref, *, tb, n_tiles_pad):
    step = pl.program_id(0)
    lane8 = _iota((SUBLANES, LANES), 1)

    @pl.when(step == 0)
    def _():
        cnt = jnp.broadcast_to(cnt_ref[0:1, :], (SUBLANES, LANES))
        tiles = jnp.floor((cnt + (MOE_TILE - 1.0)) * (1.0 / MOE_TILE))
        cum = tiles
        sh = 1
        while sh < LANES:
            cum = cum + jnp.where(lane8 >= sh, pltpu.roll(cum, sh, 1), 0.0)
            sh *= 2
        excl = cum - tiles
        base_ref[...] = excl * MOE_TILE
        run_ref[...] = jnp.zeros_like(run_ref)

        ti = _iota((n_tiles_pad, LANES), 0).astype(F32)
        lane = _iota((n_tiles_pad, LANES), 1)
        done = jnp.where(lane < N_CLASSES, jnp.where(cum[0:1, :] <= ti, 1.0, 0.0), 0.0)
        t_cls = jnp.sum(done, axis=-1, keepdims=True)
        sel = jnp.where(lane.astype(F32) == t_cls, 1.0, 0.0)
        cnt_i = jnp.sum(sel * cnt[0:1, :], axis=-1, keepdims=True)
        first_i = jnp.sum(sel * excl[0:1, :], axis=-1, keepdims=True)
        rows_i = jnp.clip(cnt_i - MOE_TILE * (ti[:, 0:1] - first_i), 0.0, float(MOE_TILE))
        e_lo = jnp.sum(sel * tab_ref[0:1, :], axis=-1, keepdims=True)
        e_hi = jnp.sum(sel * tab_ref[1:2, :], axis=-1, keepdims=True)
        n_tiles = jnp.sum(jnp.where(lane < N_CLASSES, jnp.broadcast_to(tiles[0:1, :], (n_tiles_pad, LANES)), 0.0),
                          axis=-1, keepdims=True)
        info = jnp.where(lane == 0, rows_i,
                         jnp.where(lane == 1, e_lo, jnp.where(lane == 2, e_hi, jnp.where(lane == 3, n_tiles, 0.0))))
        tile_ref[...] = info.astype(I32)

    cls = meta_ref[:, 0:1]
    lane = _iota((tb, LANES), 1).astype(F32)
    onehot = jnp.where(lane == cls, 1.0, 0.0)
    strict_lower = jnp.where(_iota((tb, tb), 1) < _iota((tb, tb), 0), 1.0, 0.0).astype(BF16)
    before = _dot(strict_lower, onehot.astype(BF16))
    slot = jnp.sum(onehot * (before + run_ref[0:1, :] + base_ref[0:1, :]), axis=-1, keepdims=True)
    run_ref[...] = run_ref[...] + jnp.sum(onehot, axis=0, keepdims=True)
    slot_t = jnp.broadcast_to(slot, (tb, LANES)).T
    pos_ref[...] = slot_t[0:SUBLANES, :].astype(I32)


def _rank(meta, counts, tb, n_tiles_pad):
    t = meta.shape[0]
    kern = functools.partial(_rank_kernel, tb=tb, n_tiles_pad=n_tiles_pad)
    tab = jnp.asarray(_class_expert_table())
    return pl.pallas_call(
        kern,
        grid=(t // tb,),
        in_specs=[pl.BlockSpec((tb, LANES), lambda i: (i, 0)),
                  pl.BlockSpec((SUBLANES, LANES), lambda i: (0, 0)),
                  pl.BlockSpec((SUBLANES, LANES), lambda i: (0, 0))],
        out_specs=[pl.BlockSpec((SUBLANES, tb), lambda i: (0, i)),
                   pl.BlockSpec((n_tiles_pad, LANES), lambda i: (0, 0))],
        out_shape=[jax.ShapeDtypeStruct((SUBLANES, t), I32),
                   jax.ShapeDtypeStruct((n_tiles_pad, LANES), I32)],
        scratch_shapes=[pltpu.VMEM((SUBLANES, LANES), F32), pltpu.VMEM((SUBLANES, LANES), F32)],
        compiler_params=pltpu.CompilerParams(dimension_semantics=("arbitrary",),
                                             vmem_limit_bytes=VMEM_LIMIT),
        name="rank",
    )(meta, counts, tab)


DMA_UNROLL = 8


def _dispatch_kernel(pos_ref, src_ref, meta_ref, zeros_x_ref, zeros_m_ref, dst_ref, dmeta_ref, sems, *, tb, rpt):
    del zeros_x_ref, zeros_m_ref
    base = pl.program_id(0) * tb

    def issue(grp, carry):
        for u in range(DMA_UNROLL):
            j = grp * DMA_UNROLL + u
            slot = pos_ref[base + j]
            src = src_ref.at[pl.ds(pl.multiple_of(j * rpt, rpt), rpt)]
            dst = dst_ref.at[pl.ds(pl.multiple_of(slot * rpt, rpt), rpt)]
            pltpu.make_async_copy(src, dst, sems.at[0]).start(priority=u % 2)
            pltpu.make_async_copy(meta_ref.at[pl.ds(j, 1)], dmeta_ref.at[pl.ds(slot, 1)],
                                  sems.at[1]).start(priority=(u + 1) % 2)
        return carry

    lax.fori_loop(0, tb // DMA_UNROLL, issue, 0)
    pltpu.make_async_copy(src_ref, dst_ref.at[pl.ds(0, tb * rpt)], sems.at[0]).wait()
    pltpu.make_async_copy(meta_ref, dmeta_ref.at[pl.ds(0, tb)], sems.at[1]).wait()


def _dispatch(pos, src, meta, n_slots, rpt, tb):
    t = pos.shape[0]
    kern = functools.partial(_dispatch_kernel, tb=tb, rpt=rpt)
    return pl.pallas_call(
        kern,
        grid_spec=pltpu.PrefetchScalarGridSpec(
            num_scalar_prefetch=1,
            grid=(t // tb,),
            in_specs=[pl.BlockSpec((tb * rpt, LANES), lambda i, pos_ref: (i, 0)),
                      pl.BlockSpec((tb, LANES), lambda i, pos_ref: (i, 0)),
                      pl.BlockSpec(memory_space=pl.ANY), pl.BlockSpec(memory_space=pl.ANY)],
            out_specs=[pl.BlockSpec(memory_space=pl.ANY), pl.BlockSpec(memory_space=pl.ANY)],
            scratch_shapes=[pltpu.SemaphoreType.DMA((2,))],
        ),
        out_shape=[jax.ShapeDtypeStruct((n_slots * rpt, LANES), src.dtype),
                   jax.ShapeDtypeStruct((n_slots, LANES), meta.dtype)],
        input_output_aliases={3: 0, 4: 1},
        compiler_params=pltpu.CompilerParams(dimension_semantics=("arbitrary",),
                                             has_side_effects=True, vmem_limit_bytes=VMEM_LIMIT),
        name="dispatch",
    )(pos, src, meta, jnp.zeros((n_slots * rpt, LANES), src.dtype), jnp.zeros((n_slots, LANES), meta.dtype))


def _finish_kernel(x1t_ref, yt_ref, lnw_ref, lnb_ref, out_ref, *, tb, alpha, d):
    segs = d // LANES
    z = alpha * _load_token_tiles(x1t_ref, tb, segs) + _load_token_tiles(yt_ref, tb, segs)
    out_ref[...] = _layer_norm(z, lnw_ref[...], lnb_ref[...])


def _finish(x1t, yt, ln_w, ln_b, alpha, d, tb):
    segs = d // LANES
    t = x1t.shape[0] // segs
    kern = functools.partial(_finish_kernel, tb=tb, alpha=alpha, d=d)
    tile_spec = pl.BlockSpec((tb * segs, LANES), lambda i: (i, 0))
    return pl.pallas_call(
        kern,
        grid=(t // tb,),
        in_specs=[tile_spec, tile_spec, pl.BlockSpec((1, d), lambda i: (0, 0)), pl.BlockSpec((1, d), lambda i: (0, 0))],
        out_specs=pl.BlockSpec((tb, d), lambda i: (i, 0)),
        out_shape=jax.ShapeDtypeStruct((t, d), F32),
        compiler_params=pltpu.CompilerParams(dimension_semantics=("arbitrary",),
                                             vmem_limit_bytes=VMEM_LIMIT),
        name="finish",
    )(x1t, yt, ln_w, ln_b)


MOE_ID_LANES = -(-MOE_TILE // LANES) * LANES


def _moe_kernel(rows_ref, elo_ref, ehi_ref, nt_ref, xs_ref, ms_ref, wgu_lo_ref, wd_lo_ref, wgu_hi_ref, wd_hi_ref,
                yt_ref, stage_ref, idv_ref, ids_ref, sems, *, d):
    i = pl.program_id(0)
    segs = d // LANES
    rows = rows_ref[i]
    prev_rows = rows_ref[jnp.maximum(i - 1, 0)]

    def row_copy(j, token):
        return pltpu.make_async_copy(stage_ref.at[pl.ds(pl.multiple_of(j * segs, segs), segs)],
                                     yt_ref.at[pl.ds(pl.multiple_of(token * segs, segs), segs)], sems.at[0])

    def drain(n):
        def wait_one(j, carry):
            row_copy(0, 0).wait()
            return carry
        lax.fori_loop(0, n, wait_one, 0)

    @pl.when(rows > 0)
    def _():
        xb = _load_token_tiles(xs_ref, MOE_TILE, segs).astype(BF16)

        def expert(wgu_ref, wd_ref):
            de = wd_ref.shape[0]
            gate_up = _dot(xb, wgu_ref[...])
            gate = gate_up[:, 0:de]
            hidden = gate * _sigmoid(gate) * gate_up[:, de:]
            return _dot(hidden.astype(BF16), wd_ref[...])

        meta = ms_ref[...]
        y = meta[:, 1:2] * expert(wgu_lo_ref, wd_lo_ref) + meta[:, 2:3] * expert(wgu_hi_ref, wd_hi_ref)

        meta_t = jnp.concatenate([meta, jnp.zeros((MOE_ID_LANES - MOE_TILE, LANES), F32)], axis=0).T
        idv_ref[...] = meta_t[0:SUBLANES, :].astype(I32)
        to_smem = pltpu.make_async_copy(idv_ref, ids_ref, sems.at[1])
        to_smem.start()

        @pl.when(i > 0)
        def _():
            drain(prev_rows)
        _store_token_tiles(stage_ref, y)
        to_smem.wait()

        def issue(j, carry):
            row_copy(j, ids_ref[3, j]).start()
            return carry
        lax.fori_loop(0, rows, issue, 0)

        @pl.when(i == pl.num_programs(0) - 1)
        def _():
            drain(rows)

    @pl.when(jnp.logical_and(rows <= 0, i > 0))
    def _():
        drain(prev_rows)


def _moe(info, xs, ms, wgu, wd, n_tokens, d, n_tiles_pad):
    de = wd.shape[1]
    segs = d // LANES
    kern = functools.partial(_moe_kernel, d=d)

    def last_live(i, nt_ref):
        return jnp.minimum(i, jnp.maximum(nt_ref[0] - 1, 0))

    def up_spec(which):
        return pl.BlockSpec((None, d, 2 * de),
                            lambda i, rows, elo, ehi, nt: ((elo, ehi)[which][last_live(i, nt)], 0, 0))

    def down_spec(which):
        return pl.BlockSpec((None, de, d),
                            lambda i, rows, elo, ehi, nt: ((elo, ehi)[which][last_live(i, nt)], 0, 0))

    rows, elo, ehi, nt = info[:, 0], info[:, 1], info[:, 2], info[0:1, 3]
    return pl.pallas_call(
        kern,
        grid_spec=pltpu.PrefetchScalarGridSpec(
            num_scalar_prefetch=4,
            grid=(n_tiles_pad,),
            in_specs=[pl.BlockSpec((MOE_TILE * segs, LANES), lambda i, rows, elo, ehi, nt: (last_live(i, nt), 0)),
                      pl.BlockSpec((MOE_TILE, LANES), lambda i, rows, elo, ehi, nt: (last_live(i, nt), 0)),
                      up_spec(0), down_spec(0), up_spec(1), down_spec(1)],
            out_specs=pl.BlockSpec(memory_space=pl.ANY),
            scratch_shapes=[pltpu.VMEM((MOE_TILE * segs, LANES), F32),
                            pltpu.VMEM((SUBLANES, MOE_ID_LANES), I32),
                            pltpu.SMEM((SUBLANES, MOE_ID_LANES), I32),
                            pltpu.SemaphoreType.DMA((2,))],
        ),
        out_shape=jax.ShapeDtypeStruct((n_tokens * segs, LANES), F32),
        compiler_params=pltpu.CompilerParams(dimension_semantics=("arbitrary",),
                                             has_side_effects=True, vmem_limit_bytes=VMEM_LIMIT),
        name="moe",
    )(rows, elo, ehi, nt, xs, ms, wgu, wd, wgu, wd)


def _pick_block(n, target):
    blk = min(n, target)
    while n % blk:
        blk //= 2
    return blk


def kernel(x, w_in, conv_w, conv_b, mlstm_gate_bias, mlstm_norm_w, attn_sinks, w_out, ln1_w, ln1_b,
           w_group_router, b_group_router, w_expert_router, b_expert_router,
           w_exp_gate, w_exp_up, w_exp_down, ln2_w, ln2_b):
    b, s, d = x.shape
    t = b * s
    depth = w_in.shape[0]
    alpha = (2.0 * depth) ** 0.25
    assert s % ML_CHUNK == 0 and s % WINDOW == 0 and d % LANES == 0

    tm = _pick_block(t, 512)
    tq = _pick_block(s, 4 * ML_CHUNK)
    tb_rank = _pick_block(t, 512)
    tb_dma = _pick_block(t, 2048)
    tb_col = _pick_block(t, 512)
    n_tiles_pad = -(-(t // MOE_TILE + N_CLASSES) // SUBLANES) * SUBLANES
    cos_t, sin_t = _rope_tables(s)

    for l in range(depth):
        x2d = x.reshape(t, d)
        q, k, v, og, aq, ak, av, g = _inproj(x2d, _pack_w_in(w_in[l]), conv_w[l], conv_b[l][None, :], tm, s // tm)
        bias_pad = jnp.zeros((LANES - ML_HEADS,), F32)
        gate_bias_row = jnp.concatenate(
            [mlstm_gate_bias[l, 0], bias_pad, mlstm_gate_bias[l, 1], bias_pad])[None, :]
        ml = _mlstm(q.reshape(b, s, -1), k.reshape(b, s, -1), v.reshape(b, s, -1), og.reshape(b, s, -1),
                    g.reshape(b, s, -1), gate_bias_row, mlstm_norm_w[l][None, :], tq)
        att = _swa(aq.reshape(b, s, -1), ak.reshape(b, s, -1), av.reshape(b, s, -1), cos_t, sin_t, attn_sinks[l],
                   1)

        w_router, b_router = _pack_router(w_group_router[l], b_group_router[l],
                                          w_expert_router[l], b_expert_router[l])
        x1t, meta, counts = _outproj(x2d, ml.reshape(t, -1), att.reshape(t, -1), w_out[l].astype(BF16),
                                     ln1_w[l][None, :], ln1_b[l][None, :], w_router, b_router, alpha, tm)

        pos2d, info = _rank(meta, counts, tb_rank, n_tiles_pad)
        pos = pos2d[0]
        xs, ms = _dispatch(pos, x1t, meta, n_tiles_pad * MOE_TILE, d // LANES, tb_dma)
        w_gate_up = jnp.concatenate([w_exp_gate[l].astype(BF16), w_exp_up[l].astype(BF16)], axis=-1)
        yt = _moe(info, xs, ms, w_gate_up, w_exp_down[l].astype(BF16), t, d, n_tiles_pad)
        out = _finish(x1t, yt, ln2_w[l][None, :], ln2_b[l][None, :], alpha, d, tb_col)
        x = out.reshape(b, s, d)
    return x
```

```python
import functools
import math

import numpy as np
import jax
import jax.numpy as jnp
from jax import lax
from jax.experimental import pallas as pl
from jax.experimental.pallas import tpu as pltpu

F32 = jnp.float32
BF16 = jnp.bfloat16
I32 = jnp.int32

ML_HEADS = 4
ML_HEAD_DIM = 128
ML_WIDTH = ML_HEADS * ML_HEAD_DIM
ML_CHUNK = 128
CONV_WIDTH = 4
ATT_Q_HEADS = 8
ATT_KV_HEADS = 2
ATT_HEAD_DIM = 64
ATT_WIDTH = ATT_Q_HEADS * ATT_HEAD_DIM
ATT_KV_WIDTH = ATT_KV_HEADS * ATT_HEAD_DIM
WINDOW = 128
ROPE_THETA = 10000.0
N_GROUPS = 4
EXPERTS_PER_GROUP = 8
N_EXPERTS = N_GROUPS * EXPERTS_PER_GROUP
PAIRS_PER_GROUP = EXPERTS_PER_GROUP * (EXPERTS_PER_GROUP - 1) // 2
N_CLASSES = N_GROUPS * PAIRS_PER_GROUP
LN_EPS = 1e-5

LANES = 128
SUBLANES = 8
MOE_TILE = 320
VMEM_LIMIT = 56 * 1024 * 1024

NEG_INF = float("-inf")


def _sigmoid(x):
    return 1.0 / (1.0 + jnp.exp(-x))


def _log_sigmoid(x):
    return jnp.minimum(x, 0.0) - jnp.log(1.0 + jnp.exp(-jnp.abs(x)))


def _iota(shape, dim):
    return lax.broadcasted_iota(I32, shape, dim)


def _dot(a, b):
    return jnp.dot(a, b, preferred_element_type=F32)


def _dot_exact(a, b):
    return jnp.dot(a, b, preferred_element_type=F32, precision=lax.Precision.HIGHEST)


def _layer_norm(z, w, b):
    mu = jnp.mean(z, axis=-1, keepdims=True)
    zc = z - mu
    var = jnp.mean(zc * zc, axis=-1, keepdims=True)
    return zc * lax.rsqrt(var + LN_EPS) * w + b


C_QK = 0
C_V = C_QK + 2 * ML_WIDTH
C_O = C_V + ML_WIDTH
C_AQ = C_O + ML_WIDTH
C_AK = C_AQ + ATT_WIDTH
C_AV = C_AK + ATT_KV_HEADS * LANES
C_G = C_AV + ATT_KV_HEADS * LANES
C_END = C_G + 2 * LANES


def _pack_w_in(w_in):
    sizes = (2 * ML_WIDTH, ML_WIDTH, ML_WIDTH, ML_HEADS, ML_HEADS, ATT_WIDTH, ATT_KV_WIDTH, ATT_KV_WIDTH)
    splits = np.cumsum(sizes)[:-1].tolist()
    w_qk, w_v, w_o, w_i, w_f, w_aq, w_ak, w_av = jnp.split(w_in, splits, axis=-1)
    half = ATT_HEAD_DIM // 2

    def head(w, h):
        return w[:, h * ATT_HEAD_DIM:(h + 1) * ATT_HEAD_DIM]

    def q_tile(a, b):
        return [a[:, :half], b[:, :half], a[:, half:], b[:, half:]]

    q_cols = [t for p in range(ATT_Q_HEADS // 2) for t in q_tile(head(w_aq, 2 * p), head(w_aq, 2 * p + 1))]
    k_cols = [t for h in range(ATT_KV_HEADS) for t in q_tile(head(w_ak, h), head(w_ak, h))]
    v_cols = [t for h in range(ATT_KV_HEADS) for t in (head(w_av, h), head(w_av, h))]
    lane_pad = jnp.zeros((w_in.shape[0], LANES - ML_HEADS), w_in.dtype)
    packed = jnp.concatenate([w_qk, w_v, w_o] + q_cols + k_cols + v_cols + [w_i, lane_pad, w_f, lane_pad], axis=-1)
    return packed.astype(BF16)


def _inproj_kernel(x_ref, w_ref, cw_ref, cb_ref, q_ref, k_ref, v_ref, og_ref, aq_ref, ak_ref, av_ref, g_ref,
                   *scratch, blocks_per_seq):
    *ext_refs, xb_ref = scratch
    tm = x_ref.shape[0]
    halo = SUBLANES
    cs = ext_refs[0].shape[1]
    xb_ref[...] = x_ref[...].astype(BF16)

    @pl.when(pl.program_id(0) % blocks_per_seq == 0)
    def _():
        for ext_ref in ext_refs:
            ext_ref[0:halo, :] = jnp.zeros((halo, cs), F32)

    scale = ML_HEAD_DIM ** -0.5

    def mm(lo, hi):
        return _dot(xb_ref[...], w_ref[:, lo:hi])

    def conv_slice(idx):
        ext_ref = ext_refs[idx]
        c0 = idx * cs
        is_q = c0 < ML_WIDTH
        dst_ref, off = (q_ref, c0) if is_q else (k_ref, c0 - ML_WIDTH)
        rt = ML_CHUNK
        for r0 in range(0, tm, rt):
            conv = cb_ref[:, c0:c0 + cs]
            for j in range(CONV_WIDTH):
                start = halo + r0 - (CONV_WIDTH - 1) + j
                conv = conv + cw_ref[j:j + 1, c0:c0 + cs] * ext_ref[start:start + rt, :]
            act = conv * _sigmoid(conv)
            dst_ref[r0:r0 + rt, off:off + cs] = (act if is_q else act * scale).astype(BF16)
        ext_ref[0:halo, :] = ext_ref[tm:tm + halo, :]

    for idx in range(len(ext_refs)):
        ext_refs[idx][halo:halo + tm, :] = mm(C_QK + idx * cs, C_QK + (idx + 1) * cs)
        if idx > 0:
            conv_slice(idx - 1)
    half_v = ML_WIDTH // 2
    v_ref[:, 0:half_v] = mm(C_V, C_V + half_v).astype(BF16)
    conv_slice(len(ext_refs) - 1)
    v_ref[:, half_v:] = mm(C_V + half_v, C_O).astype(BF16)
    og_ref[...] = _sigmoid(mm(C_O, C_AQ)).astype(BF16)
    aq_ref[...] = mm(C_AQ, C_AK)
    ak_ref[...] = mm(C_AK, C_AV)
    av_ref[...] = mm(C_AV, C_G).astype(BF16)
    g_ref[...] = mm(C_G, C_END)


def _inproj(x2d, w_packed, conv_w, conv_b, tm, blocks_per_seq):
    t, d = x2d.shape
    widths = (ML_WIDTH, ML_WIDTH, ML_WIDTH, ML_WIDTH, C_AK - C_AQ, C_AV - C_AK, C_G - C_AV, C_END - C_G)
    dtypes = (BF16, BF16, BF16, BF16, F32, F32, BF16, F32)
    kern = functools.partial(_inproj_kernel, blocks_per_seq=blocks_per_seq)
    return pl.pallas_call(
        kern,
        grid=(t // tm,),
        in_specs=[pl.BlockSpec((tm, d), lambda i: (i, 0)),
                  pl.BlockSpec((d, C_END), lambda i: (0, 0)),
                  pl.BlockSpec((CONV_WIDTH, 2 * ML_WIDTH), lambda i: (0, 0)),
                  pl.BlockSpec((1, 2 * ML_WIDTH), lambda i: (0, 0))],
        out_specs=[pl.BlockSpec((tm, w), lambda i: (i, 0)) for w in widths],
        out_shape=[jax.ShapeDtypeStruct((t, w), dt) for w, dt in zip(widths, dtypes)],
        scratch_shapes=[pltpu.VMEM((tm + SUBLANES, 2 * LANES), F32)] * (2 * ML_WIDTH // (2 * LANES))
        + [pltpu.VMEM((tm, d), BF16)],
        compiler_params=pltpu.CompilerParams(dimension_semantics=("arbitrary",),
                                             vmem_limit_bytes=VMEM_LIMIT),
        name="inproj",
    )(x2d, w_packed, conv_w, conv_b)


def _time_scan(x, combine, identity):
    row = _iota(x.shape, 0)
    sh = 1
    while sh < x.shape[0]:
        x = combine(x, jnp.where(row >= sh, pltpu.roll(x, sh, 0), identity))
        sh *= 2
    return x


def _mlstm_kernel(q_ref, k_ref, v_ref, og_ref, g_ref, gb_ref, nw_ref, out_ref, ct_ref, m_ref, *, tq):
    s_idx = pl.program_id(1)
    L = ML_CHUNK
    D = ML_HEAD_DIM
    H = ML_HEADS
    heads = range(H)

    @pl.when(s_idx == 0)
    def _():
        ct_ref[...] = jnp.zeros_like(ct_ref)
        m_ref[...] = jnp.zeros_like(m_ref)

    causal = _iota((L, L), 1) <= _iota((L, L), 0)
    ones_blk = jnp.ones((L, D), BF16)
    mean_blk = jnp.full((D, D), 1.0 / D, BF16)
    m_prev = m_ref[0:1, :]
    head_lanes = _iota((L, LANES), 1) < H
    tile_of_lane = jnp.right_shift(_iota((LANES, H * L), 1), L.bit_length() - 1)
    spread = jnp.where(_iota((LANES, H * L), 0) == tile_of_lane, 1.0, 0.0).astype(BF16)

    def spread_heads(x):
        x = jnp.where(head_lanes, x, 0.0)
        hi = x.astype(BF16)
        lo = (x - hi.astype(F32)).astype(BF16)
        return _dot(hi, spread) + _dot(lo, spread)

    for c in range(tq // L):
        r0 = c * L
        gi = g_ref[r0:r0 + L, 0:LANES] + gb_ref[:, 0:LANES]
        gf = g_ref[r0:r0 + L, LANES:2 * LANES] + gb_ref[:, LANES:2 * LANES]
        b_cum = _time_scan(_log_sigmoid(gf), jnp.add, 0.0)
        r = gi - b_cum
        g = jnp.maximum(m_prev, _time_scan(r, jnp.maximum, NEG_INF))
        g_rep = spread_heads(g)
        b_rep = spread_heads(b_cum)
        b_last = b_cum[L - 1:L, :]
        m_new = jnp.maximum(b_last + m_prev, jnp.max(b_last + r, axis=0, keepdims=True))
        decay = jnp.exp(b_last + m_prev - m_new)
        shift = b_last - m_new
        r_t = r.T

        q_b = [q_ref[r0:r0 + L, h * D:(h + 1) * D] for h in heads]
        kt_b = [k_ref[r0:r0 + L, h * D:(h + 1) * D].T for h in heads]
        v_aug = [jnp.concatenate([v_ref[r0:r0 + L, h * D:(h + 1) * D], ones_blk], axis=-1) for h in heads]
        g_col = [g_rep[:, h * L:(h + 1) * L] for h in heads]
        w_intra = [jnp.exp(jnp.where(causal, r_t[h:h + 1, :] - g_col[h], NEG_INF)) for h in heads]
        s_b = [(_dot(q_b[h], kt_b[h]) * w_intra[h]).astype(BF16) for h in heads]
        ct = [ct_ref[h] for h in heads]
        inter = [_dot(q_b[h], ct[h].astype(BF16)) for h in heads]
        intra = [_dot(s_b[h], v_aug[h]) for h in heads]
        for h in heads:
            wi_col = jnp.exp(m_prev[:, h:h + 1] - g_col[h])
            clamp = jnp.exp(-(b_rep[:, h * L:(h + 1) * L] + g_col[h]))
            num = wi_col * inter[h][:, 0:D] + intra[h][:, 0:D]
            den = wi_col * inter[h][:, D:] + intra[h][:, D:]
            hh = num / jnp.maximum(jnp.abs(den), clamp)
            mu = _dot(hh.astype(BF16), mean_blk)
            hc = hh - mu
            var = _dot((hc * hc).astype(BF16), mean_blk)
            hn = hc * lax.rsqrt(var + LN_EPS) * nw_ref[:, h * D:(h + 1) * D]
            gate_o = og_ref[r0:r0 + L, h * D:(h + 1) * D].astype(F32)
            out_ref[r0:r0 + L, h * D:(h + 1) * D] = (gate_o * hn).astype(out_ref.dtype)
        for h in heads:
            w_row = jnp.exp(r_t[h:h + 1, :] + shift[:, h:h + 1])
            ktw = (kt_b[h].astype(F32) * w_row).astype(BF16)
            ct_ref[h] = decay[:, h:h + 1] * ct[h] + _dot(ktw, v_aug[h])
        m_prev = m_new

    m_ref[...] = jnp.broadcast_to(m_prev, m_ref.shape)


def _mlstm(q, k, v, og, g, gate_bias_row, norm_w_row, tq):
    b, s, _ = q.shape
    kern = functools.partial(_mlstm_kernel, tq=tq)

    def seq_spec(width):
        return pl.BlockSpec((None, tq, width), lambda bi, si: (bi, si, 0))

    def const_spec(shape):
        return pl.BlockSpec(shape, lambda bi, si: (0,) * len(shape))

    return pl.pallas_call(
        kern,
        grid=(b, s // tq),
        in_specs=[seq_spec(ML_WIDTH), seq_spec(ML_WIDTH), seq_spec(ML_WIDTH), seq_spec(ML_WIDTH),
                  seq_spec(2 * LANES), const_spec((1, 2 * LANES)), const_spec((1, ML_WIDTH))],
        out_specs=seq_spec(ML_WIDTH),
        out_shape=jax.ShapeDtypeStruct((b, s, ML_WIDTH), BF16),
        scratch_shapes=[pltpu.VMEM((ML_HEADS, ML_HEAD_DIM, 2 * ML_HEAD_DIM), F32),
                        pltpu.VMEM((SUBLANES, LANES), F32)],
        compiler_params=pltpu.CompilerParams(dimension_semantics=("arbitrary", "arbitrary"),
                                             vmem_limit_bytes=VMEM_LIMIT),
        name="mlstm",
    )(q, k, v, og, g, gate_bias_row, norm_w_row)


def _rope_tables(seq_len):
    half = ATT_HEAD_DIM // 2
    inv_freq = ROPE_THETA ** (-jnp.arange(half, dtype=F32) / half)
    ang = jnp.arange(seq_len, dtype=F32)[:, None] * inv_freq[None, :]
    cos = jnp.cos(ang)
    sin = jnp.sin(ang)
    cos_t = jnp.concatenate([cos, cos, cos, cos], axis=-1)
    sin_t = jnp.concatenate([-sin, -sin, sin, sin], axis=-1)
    return cos_t, sin_t


def _swa_kernel(sink_ref, aq_ref, ak_ref, av_ref, cos_ref, sin_ref, out_ref, kprev_ref, vprev_ref, *, nsub):
    step = pl.program_id(1)
    Lb = WINDOW
    half = ATT_HEAD_DIM // 2
    pairs = ATT_Q_HEADS // 2
    pairs_per_kv = pairs // ATT_KV_HEADS

    @pl.when(step == 0)
    def _():
        kprev_ref[...] = jnp.zeros_like(kprev_ref)
        vprev_ref[...] = jnp.zeros_like(vprev_ref)

    def rope(x, cos, sin):
        tiles = []
        for c in range(x.shape[-1] // LANES):
            xt = x[:, c * LANES:(c + 1) * LANES]
            tiles.append(xt * cos + pltpu.roll(xt, LANES // 2, 1) * sin)
        return jnp.concatenate(tiles, axis=-1)

    ql = _iota((Lb, 2 * Lb), 0)
    kj = _iota((Lb, 2 * Lb), 1)
    diff = Lb + ql - kj
    lane = _iota((Lb, LANES), 1)
    low_half = lane < ATT_HEAD_DIM
    first_head = (lane & half) == 0
    ones_blk = jnp.ones((2 * Lb, LANES), BF16)

    k_prev = kprev_ref
    v_prev = vprev_ref
    for j in range(nsub):
        r0 = j * Lb
        cos = cos_ref[r0:r0 + Lb, :]
        sin = sin_ref[r0:r0 + Lb, :]
        q = rope(aq_ref[r0:r0 + Lb, :], cos, sin) * (ATT_HEAD_DIM ** -0.5)
        k_cur = rope(ak_ref[r0:r0 + Lb, :], cos, sin).astype(BF16)
        v_cur = av_ref[r0:r0 + Lb, :]

        kpos = (step * nsub + j) * Lb + kj - Lb
        visible = jnp.where(diff >= 0, jnp.where(diff < WINDOW, jnp.where(kpos >= 0, 1, 0), 0), 0)
        bias = jnp.where(visible > 0, 0.0, NEG_INF).astype(F32)
        bias = jnp.concatenate([bias] * (2 * pairs_per_kv), axis=0)

        for g in range(ATT_KV_HEADS):
            kk = jnp.concatenate([k_prev[:, g * LANES:(g + 1) * LANES], k_cur[:, g * LANES:(g + 1) * LANES]], axis=0)
            vv = jnp.concatenate([v_prev[:, g * LANES:(g + 1) * LANES], v_cur[:, g * LANES:(g + 1) * LANES]], axis=0)
            vv_aug = jnp.concatenate([vv, ones_blk], axis=-1)
            rows = []
            sinks = []
            for p in range(pairs_per_kv):
                pair = g * pairs_per_kv + p
                q2 = q[:, pair * LANES:(pair + 1) * LANES]
                rows.append(jnp.where(first_head, q2, 0.0))
                rows.append(jnp.where(first_head, 0.0, q2))
                sinks.append(jnp.full((Lb, LANES), sink_ref[2 * pair], F32))
                sinks.append(jnp.full((Lb, LANES), sink_ref[2 * pair + 1], F32))
            qs = jnp.concatenate(rows, axis=0).astype(BF16)
            sink = jnp.concatenate(sinks, axis=0)

            sc = lax.dot_general(qs, kk, (((1,), (1,)), ((), ())), preferred_element_type=F32) + bias
            m = jnp.maximum(jnp.broadcast_to(jnp.max(sc, axis=-1, keepdims=True), sink.shape), sink)
            p_un = jnp.exp(sc - jnp.concatenate([m, m], axis=-1))
            acc = _dot(p_un.astype(BF16), vv_aug)
            o = acc[:, 0:LANES] / (acc[:, LANES:] + jnp.exp(sink - m))
            for p in range(pairs_per_kv):
                pair = g * pairs_per_kv + p
                even = o[(2 * p) * Lb:(2 * p + 1) * Lb, :]
                odd = o[(2 * p + 1) * Lb:(2 * p + 2) * Lb, :]
                out_ref[r0:r0 + Lb, pair * LANES:(pair + 1) * LANES] = (
                    jnp.where(low_half, even, odd).astype(out_ref.dtype))
        k_prev = k_cur
        v_prev = v_cur

    kprev_ref[...] = k_prev
    vprev_ref[...] = v_prev


def _swa(aq, ak, av, cos_t, sin_t, sinks, nsub):
    b, s, _ = aq.shape
    kvw = ATT_KV_HEADS * LANES
    rows = nsub * WINDOW

    def seq_spec(width):
        return pl.BlockSpec((None, rows, width), lambda bi, si: (bi, si, 0))

    tab_spec = pl.BlockSpec((rows, LANES), lambda bi, si: (si, 0))
    return pl.pallas_call(
        functools.partial(_swa_kernel, nsub=nsub),
        grid=(b, s // rows),
        in_specs=[pl.BlockSpec(memory_space=pltpu.SMEM),
                  seq_spec(ATT_WIDTH), seq_spec(kvw), seq_spec(kvw), tab_spec, tab_spec],
        out_specs=seq_spec(ATT_WIDTH),
        out_shape=jax.ShapeDtypeStruct((b, s, ATT_WIDTH), BF16),
        scratch_shapes=[pltpu.VMEM((WINDOW, kvw), BF16), pltpu.VMEM((WINDOW, kvw), BF16)],
        compiler_params=pltpu.CompilerParams(dimension_semantics=("arbitrary", "arbitrary"),
                                             vmem_limit_bytes=VMEM_LIMIT),
        name="swa",
    )(sinks, aq, ak, av, cos_t, sin_t)


def _store_token_tiles(ref, val, row0=0, rows_per_token=None):
    n, w = val.shape
    segs = w // LANES
    rpt = rows_per_token or segs
    for j in range(segs):
        ref[pl.ds(row0 + j, n, stride=rpt), :] = val[:, j * LANES:(j + 1) * LANES]


def _load_token_tiles(ref, n, segs, row0=0, rows_per_token=None):
    rpt = rows_per_token or segs
    return jnp.concatenate([ref[pl.ds(row0 + j, n, stride=rpt), :] for j in range(segs)], axis=-1)


ROUTER_ROWS = 48


def _pack_router(w_group, b_group, w_expert, b_expert):
    d = w_group.shape[0]
    wt = jnp.zeros((ROUTER_ROWS, d), F32)
    wt = wt.at[0:N_GROUPS].set(w_group.T).at[SUBLANES:SUBLANES + N_EXPERTS].set(w_expert.T)
    bias = jnp.zeros((ROUTER_ROWS,), F32)
    bias = bias.at[0:N_GROUPS].set(b_group).at[SUBLANES:SUBLANES + N_EXPERTS].set(b_expert)
    hi = wt.astype(BF16)
    lo = (wt - hi.astype(F32)).astype(BF16)
    return jnp.concatenate([hi, lo], axis=0), jnp.broadcast_to(bias[:, None], (ROUTER_ROWS, LANES))


def _outproj_kernel(x_ref, ml_ref, att_ref, wo_ref, lnw_ref, lnb_ref, wrt_ref, brt_ref,
                    x1t_ref, meta_ref, cnt_ref, *, alpha):
    step = pl.program_id(0)
    tm, d = x_ref.shape
    segs = d // LANES
    pt = tm
    rr = wrt_ref.shape[0] // 2
    nt = (((1,), (1,)), ((), ()))
    row = _iota((SUBLANES, pt), 0).astype(F32)
    lane = _iota((pt, LANES), 1).astype(F32)

    def first_argmax(vals):
        top = jnp.max(vals, axis=0, keepdims=True)
        idx = jnp.min(jnp.where(vals == top, row, float(SUBLANES)), axis=0, keepdims=True)
        return top, idx

    @pl.when(step == 0)
    def _():
        cnt_ref[...] = jnp.zeros_like(cnt_ref)

    counts = jnp.zeros((1, LANES), F32)
    for r0 in range(0, tm, pt):
        y = (_dot(ml_ref[r0:r0 + pt, :], wo_ref[0:ML_WIDTH, :])
             + _dot(att_ref[r0:r0 + pt, :], wo_ref[ML_WIDTH:, :]))
        x1 = _layer_norm(alpha * x_ref[r0:r0 + pt, :] + y, lnw_ref[...], lnb_ref[...])
        _store_token_tiles(x1t_ref.at[pl.ds(r0 * segs, pt * segs)], x1)

        x1_hi = x1.astype(BF16)
        x1_lo = (x1 - x1_hi.astype(F32)).astype(BF16)
        both = lax.dot_general(wrt_ref[...], x1_hi, nt, preferred_element_type=F32)
        cross = lax.dot_general(wrt_ref[0:rr, :], x1_lo, nt, preferred_element_type=F32)
        logits = both[0:rr] + both[rr:2 * rr] + cross + jnp.concatenate([brt_ref[...]] * (pt // LANES), axis=1)

        g_logits = jnp.where(row < N_GROUPS, logits[0:SUBLANES], NEG_INF)
        g_top, g_idx = first_argmax(g_logits)
        g_p = 1.0 / jnp.sum(jnp.exp(g_logits - g_top), axis=0, keepdims=True)

        e_logits = logits[SUBLANES:2 * SUBLANES]
        for grp in range(1, N_GROUPS):
            e_logits = jnp.where(g_idx == grp, logits[(1 + grp) * SUBLANES:(2 + grp) * SUBLANES], e_logits)
        v1, a1 = first_argmax(e_logits)
        v2, a2 = first_argmax(jnp.where(row == a1, NEG_INF, e_logits))
        r = jnp.exp(v2 - v1)
        w1 = g_p / (1.0 + r)
        w2 = g_p * r / (1.0 + r)

        lo = jnp.minimum(a1, a2)
        hi = jnp.maximum(a1, a2)
        w_lo = jnp.where(a1 < a2, w1, w2)
        w_hi = jnp.where(a1 < a2, w2, w1)
        pair_idx = (EXPERTS_PER_GROUP - 1) * lo - lo * (lo - 1.0) * 0.5 + (hi - lo - 1.0)
        cls = g_idx * PAIRS_PER_GROUP + pair_idx

        meta_t = jnp.where(row == 0.0, cls, jnp.where(row == 1.0, w_lo, jnp.where(row == 2.0, w_hi, 0.0)))
        meta = jnp.concatenate([meta_t, jnp.zeros((LANES - SUBLANES, pt), F32)], axis=0).T
        meta_ref[r0:r0 + pt, :] = meta
        counts = counts + jnp.sum(jnp.where(lane == meta[:, 0:1], 1.0, 0.0), axis=0, keepdims=True)

    cnt_ref[0:1, :] += counts


def _outproj(x2d, ml2d, att2d, w_out_b, ln_w, ln_b, w_router, b_router, alpha, tm):
    t, d = x2d.shape
    kern = functools.partial(_outproj_kernel, alpha=alpha)

    def const_spec(shape):
        return pl.BlockSpec(shape, lambda i: (0,) * len(shape))

    return pl.pallas_call(
        kern,
        grid=(t // tm,),
        in_specs=[pl.BlockSpec((tm, d), lambda i: (i, 0)),
                  pl.BlockSpec((tm, ML_WIDTH), lambda i: (i, 0)),
                  pl.BlockSpec((tm, ATT_WIDTH), lambda i: (i, 0)),
                  const_spec((ML_WIDTH + ATT_WIDTH, d)), const_spec((1, d)), const_spec((1, d)),
                  const_spec((2 * ROUTER_ROWS, d)), const_spec((ROUTER_ROWS, LANES))],
        out_specs=[pl.BlockSpec((tm * (d // LANES), LANES), lambda i: (i, 0)),
                   pl.BlockSpec((tm, LANES), lambda i: (i, 0)), const_spec((SUBLANES, LANES))],
        out_shape=[jax.ShapeDtypeStruct((t * (d // LANES), LANES), F32),
                   jax.ShapeDtypeStruct((t, LANES), F32),
                   jax.ShapeDtypeStruct((SUBLANES, LANES), F32)],
        compiler_params=pltpu.CompilerParams(dimension_semantics=("arbitrary",),
                                             vmem_limit_bytes=VMEM_LIMIT),
        name="outproj",
    )(x2d, ml2d, att2d, w_out_b, ln_w, ln_b, w_router, b_router)


def _class_expert_table():
    tab = np.zeros((SUBLANES, LANES), np.float32)
    for g in range(N_GROUPS):
        idx = 0
        for lo in range(EXPERTS_PER_GROUP):
            for hi in range(lo + 1, EXPERTS_PER_GROUP):
                c = g * PAIRS_PER_GROUP + idx
                tab[0, c] = g * EXPERTS_PER_GROUP + lo
                tab[1, c] = g * EXPERTS_PER_GROUP + hi
                idx += 1
    return tab


def _rank_kernel(meta_ref, cnt_ref, tab_ref, pos_ref, tile_ref, base_ref, run_ref, *, tb, n_tiles_pad):
    step = pl.program_id(0)
    lane8 = _iota((SUBLANES, LANES), 1)

    @pl.when(step == 0)
    def _():
        cnt = jnp.broadcast_to(cnt_ref[0:1, :], (SUBLANES, LANES))
        tiles = jnp.floor((cnt + (MOE_TILE - 1.0)) * (1.0 / MOE_TILE))
        cum = tiles
        sh = 1
        while sh < LANES:
            cum = cum + jnp.where(lane8 >= sh, pltpu.roll(cum, sh, 1), 0.0)
            sh *= 2
        excl = cum - tiles
        base_ref[...] = excl * MOE_TILE
        run_ref[...] = jnp.zeros_like(run_ref)

        ti = _iota((n_tiles_pad, LANES), 0).astype(F32)
        lane = _iota((n_tiles_pad, LANES), 1)
        done = jnp.where(lane < N_CLASSES, jnp.where(cum[0:1, :] <= ti, 1.0, 0.0), 0.0)
        t_cls = jnp.sum(done, axis=-1, keepdims=True)
        sel = jnp.where(lane.astype(F32) == t_cls, 1.0, 0.0)
        cnt_i = jnp.sum(sel * cnt[0:1, :], axis=-1, keepdims=True)
        first_i = jnp.sum(sel * excl[0:1, :], axis=-1, keepdims=True)
        rows_i = jnp.clip(cnt_i - MOE_TILE * (ti[:, 0:1] - first_i), 0.0, float(MOE_TILE))
        e_lo = jnp.sum(sel * tab_ref[0:1, :], axis=-1, keepdims=True)
        e_hi = jnp.sum(sel * tab_ref[1:2, :], axis=-1, keepdims=True)
        n_tiles = jnp.sum(jnp.where(lane < N_CLASSES, jnp.broadcast_to(tiles[0:1, :], (n_tiles_pad, LANES)), 0.0),
                          axis=-1, keepdims=True)
        info = jnp.where(lane == 0, rows_i,
                         jnp.where(lane == 1, e_lo, jnp.where(lane == 2, e_hi, jnp.where(lane == 3, n_tiles, 0.0))))
        tile_ref[...] = info.astype(I32)

    cls = meta_ref[:, 0:1]
    lane = _iota((tb, LANES), 1).astype(F32)
    onehot = jnp.where(lane == cls, 1.0, 0.0)
    strict_lower = jnp.where(_iota((tb, tb), 1) < _iota((tb, tb), 0), 1.0, 0.0).astype(BF16)
    before = _dot(strict_lower, onehot.astype(BF16))
    slot = jnp.sum(onehot * (before + run_ref[0:1, :] + base_ref[0:1, :]), axis=-1, keepdims=True)
    run_ref[...] = run_ref[...] + jnp.sum(onehot, axis=0, keepdims=True)
    slot_t = jnp.broadcast_to(slot, (tb, LANES)).T
    pos_ref[...] = slot_t[0:SUBLANES, :].astype(I32)


def _rank(meta, counts, tb, n_tiles_pad):
    t = meta.shape[0]
    kern = functools.partial(_rank_kernel, tb=tb, n_tiles_pad=n_tiles_pad)
    tab = jnp.asarray(_class_expert_table())
    return pl.pallas_call(
        kern,
        grid=(t // tb,),
        in_specs=[pl.BlockSpec((tb, LANES), lambda i: (i, 0)),
                  pl.BlockSpec((SUBLANES, LANES), lambda i: (0, 0)),
                  pl.BlockSpec((SUBLANES, LANES), lambda i: (0, 0))],
        out_specs=[pl.BlockSpec((SUBLANES, tb), lambda i: (0, i)),
                   pl.BlockSpec((n_tiles_pad, LANES), lambda i: (0, 0))],
        out_shape=[jax.ShapeDtypeStruct((SUBLANES, t), I32),
                   jax.ShapeDtypeStruct((n_tiles_pad, LANES), I32)],
        scratch_shapes=[pltpu.VMEM((SUBLANES, LANES), F32), pltpu.VMEM((SUBLANES, LANES), F32)],
        compiler_params=pltpu.CompilerParams(dimension_semantics=("arbitrary",),
                                             vmem_limit_bytes=VMEM_LIMIT),
        name="rank",
    )(meta, counts, tab)


DMA_UNROLL = 8


def _pad_chunks():
    size, out = 1, []
    while size <= MOE_TILE:
        out.append(size)
        size *= 2
    return out[::-1]


def _dispatch_kernel(pos_ref, rows_ref, src_ref, dst_ref, zero_ref, sems, *, tb, rpt, n_tiles):
    step = pl.program_id(0)
    base = step * tb

    def pad_copies(act):
        def one_tile(i, carry):
            valid = rows_ref[i]
            pad = MOE_TILE - valid
            off = i * MOE_TILE + valid
            for size in _pad_chunks():
                hit = (pad & size) != 0

                @pl.when(hit)
                def _():
                    act(pltpu.make_async_copy(zero_ref.at[pl.ds(0, size * rpt)],
                                              dst_ref.at[pl.ds(pl.multiple_of(off * rpt, rpt), size * rpt)],
                                              sems.at[1]))
                off = off + jnp.where(hit, size, 0)
            return carry
        lax.fori_loop(0, n_tiles, one_tile, 0)

    @pl.when(step == 0)
    def _():
        zero_ref[...] = jnp.zeros_like(zero_ref)
        pad_copies(lambda cp: cp.start())

    def issue(grp, carry):
        for u in range(DMA_UNROLL):
            j = grp * DMA_UNROLL + u
            src = src_ref.at[pl.ds(pl.multiple_of(j * rpt, rpt), rpt)]
            dst = dst_ref.at[pl.ds(pl.multiple_of(pos_ref[base + j] * rpt, rpt), rpt)]
            pltpu.make_async_copy(src, dst, sems.at[0]).start(priority=u % 2)
        return carry

    lax.fori_loop(0, tb // DMA_UNROLL, issue, 0)
    pltpu.make_async_copy(src_ref, dst_ref.at[pl.ds(0, tb * rpt)], sems.at[0]).wait()

    @pl.when(step == pl.num_programs(0) - 1)
    def _():
        pad_copies(lambda cp: cp.wait())


def _dispatch(pos, tile_rows, src, n_slots, rpt, tb):
    t = pos.shape[0]
    n_tiles = tile_rows.shape[0]
    kern = functools.partial(_dispatch_kernel, tb=tb, rpt=rpt, n_tiles=n_tiles)
    pad_rows = _pad_chunks()[0] * rpt
    return pl.pallas_call(
        kern,
        grid_spec=pltpu.PrefetchScalarGridSpec(
            num_scalar_prefetch=2,
            grid=(t // tb,),
            in_specs=[pl.BlockSpec((tb * rpt, LANES), lambda i, pos_ref, rows_ref: (i, 0))],
            out_specs=pl.BlockSpec(memory_space=pl.ANY),
            scratch_shapes=[pltpu.VMEM((pad_rows, LANES), src.dtype), pltpu.SemaphoreType.DMA((2,))],
        ),
        out_shape=jax.ShapeDtypeStruct((n_slots * rpt, LANES), src.dtype),
        compiler_params=pltpu.CompilerParams(dimension_semantics=("arbitrary",),
                                             has_side_effects=True, vmem_limit_bytes=VMEM_LIMIT),
        name="dispatch",
    )(pos, tile_rows, src)


def _collect_kernel(pos_ref, ys_ref, x1t_ref, meta_ref, lnw_ref, lnb_ref, out_ref, buf_ref, sems,
                    *, tb, alpha, d):
    segs = d // LANES
    rpt = 2 * segs
    step = pl.program_id(0)
    n_steps = pl.num_programs(0)

    def gather(blk, slot):
        base = blk * tb

        def issue(grp, carry):
            for u in range(DMA_UNROLL):
                j = grp * DMA_UNROLL + u
                src = ys_ref.at[pl.ds(pos_ref[base + j], 1)]
                dst = buf_ref.at[slot, pl.ds(j, 1)]
                pltpu.make_async_copy(src, dst, sems.at[slot]).start(priority=u % 2)
            return carry

        lax.fori_loop(0, tb // DMA_UNROLL, issue, 0)

    @pl.when(step == 0)
    def _():
        gather(0, 0)

    @pl.when(step + 1 < n_steps)
    def _():
        gather(step + 1, (step + 1) % 2)

    slot = step % 2
    pltpu.make_async_copy(ys_ref.at[pl.ds(0, tb)], buf_ref.at[slot], sems.at[slot]).wait()

    x1 = _load_token_tiles(x1t_ref, tb, segs)
    y_lo = buf_ref[slot, :, 0:d]
    y_hi = buf_ref[slot, :, d:2 * d]
    meta = meta_ref[...]
    z = alpha * x1 + meta[:, 1:2] * y_lo + meta[:, 2:3] * y_hi
    out_ref[...] = _layer_norm(z, lnw_ref[...], lnb_ref[...])


def _collect(pos, ys, x1t, meta, ln_w, ln_b, alpha, d, tb):
    t = pos.shape[0]
    segs = d // LANES
    kern = functools.partial(_collect_kernel, tb=tb, alpha=alpha, d=d)
    return pl.pallas_call(
        kern,
        grid_spec=pltpu.PrefetchScalarGridSpec(
            num_scalar_prefetch=1,
            grid=(t // tb,),
            in_specs=[pl.BlockSpec(memory_space=pl.ANY),
                      pl.BlockSpec((tb * segs, LANES), lambda i, pos_ref: (i, 0)),
                      pl.BlockSpec((tb, LANES), lambda i, pos_ref: (i, 0)),
                      pl.BlockSpec((1, d), lambda i, pos_ref: (0, 0)),
                      pl.BlockSpec((1, d), lambda i, pos_ref: (0, 0))],
            out_specs=pl.BlockSpec((tb, d), lambda i, pos_ref: (i, 0)),
            scratch_shapes=[pltpu.VMEM((2, tb, 2 * d), F32), pltpu.SemaphoreType.DMA((2,))],
        ),
        out_shape=jax.ShapeDtypeStruct((t, d), F32),
        compiler_params=pltpu.CompilerParams(dimension_semantics=("arbitrary",),
                                             vmem_limit_bytes=VMEM_LIMIT),
        name="collect",
    )(pos, ys, x1t, meta, ln_w, ln_b)


MOE_TILES_PER_STEP = 1


def _moe_kernel(rows_ref, elo_ref, ehi_ref, nt_ref, xs_ref, *refs, d):
    *w_refs, ys_ref = refs
    i = pl.program_id(0)
    segs = d // LANES

    @pl.when(rows_ref[i * MOE_TILES_PER_STEP] > 0)
    def _():
        for k in range(MOE_TILES_PER_STEP):
            wgu_lo, wd_lo, wgu_hi, wd_hi = w_refs[4 * k:4 * k + 4]
            xs_k = xs_ref.at[pl.ds(k * MOE_TILE * segs, MOE_TILE * segs)]
            xb = _load_token_tiles(xs_k, MOE_TILE, segs).astype(BF16)

            def expert(wgu_ref, wd_ref):
                de = wd_ref.shape[0]
                gate_up = _dot(xb, wgu_ref[...])
                gate = gate_up[:, 0:de]
                hidden = gate * _sigmoid(gate) * gate_up[:, de:]
                return _dot(hidden.astype(BF16), wd_ref[...])

            ys_ref[k * MOE_TILE:(k + 1) * MOE_TILE, 0:d] = expert(wgu_lo, wd_lo)
            ys_ref[k * MOE_TILE:(k + 1) * MOE_TILE, d:2 * d] = expert(wgu_hi, wd_hi)

    @pl.when(rows_ref[i * MOE_TILES_PER_STEP] <= 0)
    def _():
        ys_ref[...] = jnp.zeros_like(ys_ref)


def _moe(info, xs, wgu, wd, d, n_tiles_pad):
    de = wd.shape[1]
    segs = d // LANES
    tps = MOE_TILES_PER_STEP
    kern = functools.partial(_moe_kernel, d=d)

    def last_live(tile, nt_ref):
        return jnp.minimum(tile, jnp.maximum(nt_ref[0] - 1, 0))

    def up_spec(k, which):
        return pl.BlockSpec((None, d, 2 * de),
                            lambda i, rows, elo, ehi, nt: ((elo, ehi)[which][last_live(i * tps + k, nt)], 0, 0))

    def down_spec(k, which):
        return pl.BlockSpec((None, de, d),
                            lambda i, rows, elo, ehi, nt: ((elo, ehi)[which][last_live(i * tps + k, nt)], 0, 0))

    w_specs = []
    for k in range(tps):
        w_specs += [up_spec(k, 0), down_spec(k, 0), up_spec(k, 1), down_spec(k, 1)]
    rows, elo, ehi, nt = info[:, 0], info[:, 1], info[:, 2], info[0:1, 3]
    return pl.pallas_call(
        kern,
        grid_spec=pltpu.PrefetchScalarGridSpec(
            num_scalar_prefetch=4,
            grid=(n_tiles_pad // tps,),
            in_specs=[pl.BlockSpec((tps * MOE_TILE * segs, LANES),
                                   lambda i, rows, elo, ehi, nt: (last_live(i * tps, nt) // tps, 0))] + w_specs,
            out_specs=pl.BlockSpec((tps * MOE_TILE, 2 * d), lambda i, rows, elo, ehi, nt: (i, 0)),
        ),
        out_shape=jax.ShapeDtypeStruct((n_tiles_pad * MOE_TILE, 2 * d), F32),
        compiler_params=pltpu.CompilerParams(dimension_semantics=("arbitrary",),
                                             vmem_limit_bytes=VMEM_LIMIT),
        name="moe",
    )(rows, elo, ehi, nt, xs, *([wgu, wd, wgu, wd] * tps))


def _pick_block(n, target):
    blk = min(n, target)
    while n % blk:
        blk //= 2
    return blk


def kernel(x, w_in, conv_w, conv_b, mlstm_gate_bias, mlstm_norm_w, attn_sinks, w_out, ln1_w, ln1_b,
           w_group_router, b_group_router, w_expert_router, b_expert_router,
           w_exp_gate, w_exp_up, w_exp_down, ln2_w, ln2_b):
    b, s, d = x.shape
    t = b * s
    depth = w_in.shape[0]
    alpha = (2.0 * depth) ** 0.25
    assert s % ML_CHUNK == 0 and s % WINDOW == 0 and d % LANES == 0

    tm = _pick_block(t, 512)
    tq = _pick_block(s, 4 * ML_CHUNK)
    tb_rank = _pick_block(t, 512)
    tb_dma = _pick_block(t, 2048)
    tb_col = _pick_block(t, 512)
    n_tiles_pad = -(-(t // MOE_TILE + N_CLASSES) // SUBLANES) * SUBLANES
    cos_t, sin_t = _rope_tables(s)

    for l in range(depth):
        x2d = x.reshape(t, d)
        q, k, v, og, aq, ak, av, g = _inproj(x2d, _pack_w_in(w_in[l]), conv_w[l], conv_b[l][None, :], tm, s // tm)
        bias_pad = jnp.zeros((LANES - ML_HEADS,), F32)
        gate_bias_row = jnp.concatenate(
            [mlstm_gate_bias[l, 0], bias_pad, mlstm_gate_bias[l, 1], bias_pad])[None, :]
        ml = _mlstm(q.reshape(b, s, -1), k.reshape(b, s, -1), v.reshape(b, s, -1), og.reshape(b, s, -1),
                    g.reshape(b, s, -1), gate_bias_row, mlstm_norm_w[l][None, :], tq)
        att = _swa(aq.reshape(b, s, -1), ak.reshape(b, s, -1), av.reshape(b, s, -1), cos_t, sin_t, attn_sinks[l],
                   1)

        w_router, b_router = _pack_router(w_group_router[l], b_group_router[l],
                                          w_expert_router[l], b_expert_router[l])
        x1t, meta, counts = _outproj(x2d, ml.reshape(t, -1), att.reshape(t, -1), w_out[l].astype(BF16),
                                     ln1_w[l][None, :], ln1_b[l][None, :], w_router, b_router, alpha, tm)

        pos2d, info = _rank(meta, counts, tb_rank, n_tiles_pad)
        pos = pos2d[0]
        xs = _dispatch(pos, info[:, 0], x1t, n_tiles_pad * MOE_TILE, d // LANES, tb_dma)
        w_gate_up = jnp.concatenate([w_exp_gate[l].astype(BF16), w_exp_up[l].astype(BF16)], axis=-1)
        ys = _moe(info, xs, w_gate_up, w_exp_down[l].astype(BF16), d, n_tiles_pad)
        out = _collect(pos, ys, x1t, meta, ln2_w[l][None, :], ln2_b[l][None, :], alpha, d, tb_col)
        x = out.reshape(b, s, d)
    return x
```

```python
import functools
import math

import numpy as np
import jax
import jax.numpy as jnp
from jax import lax
from jax.experimental import pallas as pl
from jax.experimental.pallas import tpu as pltpu

F32 = jnp.float32
BF16 = jnp.bfloat16
I32 = jnp.int32

ML_HEADS = 4
ML_HEAD_DIM = 128
ML_WIDTH = ML_HEADS * ML_HEAD_DIM
ML_CHUNK = 128
CONV_WIDTH = 4
ATT_Q_HEADS = 8
ATT_KV_HEADS = 2
ATT_HEAD_DIM = 64
ATT_WIDTH = ATT_Q_HEADS * ATT_HEAD_DIM
ATT_KV_WIDTH = ATT_KV_HEADS * ATT_HEAD_DIM
WINDOW = 128
ROPE_THETA = 10000.0
N_GROUPS = 4
EXPERTS_PER_GROUP = 8
N_EXPERTS = N_GROUPS * EXPERTS_PER_GROUP
PAIRS_PER_GROUP = EXPERTS_PER_GROUP * (EXPERTS_PER_GROUP - 1) // 2
N_CLASSES = N_GROUPS * PAIRS_PER_GROUP
LN_EPS = 1e-5

LANES = 128
SUBLANES = 8
MOE_TILE = 320
VMEM_LIMIT = 56 * 1024 * 1024

NEG_INF = float("-inf")


def _sigmoid(x):
    return 1.0 / (1.0 + jnp.exp(-x))


def _log_sigmoid(x):
    return jnp.minimum(x, 0.0) - jnp.log(1.0 + jnp.exp(-jnp.abs(x)))


def _iota(shape, dim):
    return lax.broadcasted_iota(I32, shape, dim)


def _dot(a, b):
    return jnp.dot(a, b, preferred_element_type=F32)


def _dot_exact(a, b):
    return jnp.dot(a, b, preferred_element_type=F32, precision=lax.Precision.HIGHEST)


def _layer_norm(z, w, b):
    mu = jnp.mean(z, axis=-1, keepdims=True)
    zc = z - mu
    var = jnp.mean(zc * zc, axis=-1, keepdims=True)
    return zc * lax.rsqrt(var + LN_EPS) * w + b


C_QK = 0
C_V = C_QK + 2 * ML_WIDTH
C_O = C_V + ML_WIDTH
C_AQ = C_O + ML_WIDTH
C_AK = C_AQ + ATT_WIDTH
C_AV = C_AK + ATT_KV_HEADS * LANES
C_G = C_AV + ATT_KV_HEADS * LANES
C_END = C_G + 2 * LANES


def _pack_w_in(w_in):
    sizes = (2 * ML_WIDTH, ML_WIDTH, ML_WIDTH, ML_HEADS, ML_HEADS, ATT_WIDTH, ATT_KV_WIDTH, ATT_KV_WIDTH)
    splits = np.cumsum(sizes)[:-1].tolist()
    w_qk, w_v, w_o, w_i, w_f, w_aq, w_ak, w_av = jnp.split(w_in, splits, axis=-1)
    half = ATT_HEAD_DIM // 2

    def head(w, h):
        return w[:, h * ATT_HEAD_DIM:(h + 1) * ATT_HEAD_DIM]

    def q_tile(a, b):
        return [a[:, :half], b[:, :half], a[:, half:], b[:, half:]]

    q_cols = [t for p in range(ATT_Q_HEADS // 2) for t in q_tile(head(w_aq, 2 * p), head(w_aq, 2 * p + 1))]
    k_cols = [t for h in range(ATT_KV_HEADS) for t in q_tile(head(w_ak, h), head(w_ak, h))]
    v_cols = [t for h in range(ATT_KV_HEADS) for t in (head(w_av, h), head(w_av, h))]
    lane_pad = jnp.zeros((w_in.shape[0], LANES - ML_HEADS), w_in.dtype)
    packed = jnp.concatenate([w_qk, w_v, w_o] + q_cols + k_cols + v_cols + [w_i, lane_pad, w_f, lane_pad], axis=-1)
    return packed.astype(BF16)


def _inproj_kernel(x_ref, w_ref, cw_ref, cb_ref, q_ref, k_ref, v_ref, og_ref, aq_ref, ak_ref, av_ref, g_ref,
                   *scratch, blocks_per_seq):
    *ext_refs, xb_ref = scratch
    tm = x_ref.shape[0]
    halo = SUBLANES
    cs = ext_refs[0].shape[1]
    xb_ref[...] = x_ref[...].astype(BF16)

    @pl.when(pl.program_id(0) % blocks_per_seq == 0)
    def _():
        for ext_ref in ext_refs:
            ext_ref[0:halo, :] = jnp.zeros((halo, cs), F32)

    scale = ML_HEAD_DIM ** -0.5

    def mm(lo, hi):
        return _dot(xb_ref[...], w_ref[:, lo:hi])

    def conv_slice(idx):
        ext_ref = ext_refs[idx]
        c0 = idx * cs
        is_q = c0 < ML_WIDTH
        dst_ref, off = (q_ref, c0) if is_q else (k_ref, c0 - ML_WIDTH)
        rt = ML_CHUNK
        for r0 in range(0, tm, rt):
            conv = cb_ref[:, c0:c0 + cs]
            for j in range(CONV_WIDTH):
                start = halo + r0 - (CONV_WIDTH - 1) + j
                conv = conv + cw_ref[j:j + 1, c0:c0 + cs] * ext_ref[start:start + rt, :]
            act = conv * _sigmoid(conv)
            dst_ref[r0:r0 + rt, off:off + cs] = (act if is_q else act * scale).astype(BF16)
        ext_ref[0:halo, :] = ext_ref[tm:tm + halo, :]

    for idx in range(len(ext_refs)):
        ext_refs[idx][halo:halo + tm, :] = mm(C_QK + idx * cs, C_QK + (idx + 1) * cs)
        if idx > 0:
            conv_slice(idx - 1)
    half_v = ML_WIDTH // 2
    v_ref[:, 0:half_v] = mm(C_V, C_V + half_v).astype(BF16)
    conv_slice(len(ext_refs) - 1)
    v_ref[:, half_v:] = mm(C_V + half_v, C_O).astype(BF16)
    og_ref[...] = _sigmoid(mm(C_O, C_AQ)).astype(BF16)
    aq_ref[...] = mm(C_AQ, C_AK)
    ak_ref[...] = mm(C_AK, C_AV)
    av_ref[...] = mm(C_AV, C_G).astype(BF16)
    g_ref[...] = mm(C_G, C_END)


def _inproj(x2d, w_packed, conv_w, conv_b, tm, blocks_per_seq):
    t, d = x2d.shape
    widths = (ML_WIDTH, ML_WIDTH, ML_WIDTH, ML_WIDTH, C_AK - C_AQ, C_AV - C_AK, C_G - C_AV, C_END - C_G)
    dtypes = (BF16, BF16, BF16, BF16, F32, F32, BF16, F32)
    kern = functools.partial(_inproj_kernel, blocks_per_seq=blocks_per_seq)
    return pl.pallas_call(
        kern,
        grid=(t // tm,),
        in_specs=[pl.BlockSpec((tm, d), lambda i: (i, 0)),
                  pl.BlockSpec((d, C_END), lambda i: (0, 0)),
                  pl.BlockSpec((CONV_WIDTH, 2 * ML_WIDTH), lambda i: (0, 0)),
                  pl.BlockSpec((1, 2 * ML_WIDTH), lambda i: (0, 0))],
        out_specs=[pl.BlockSpec((tm, w), lambda i: (i, 0)) for w in widths],
        out_shape=[jax.ShapeDtypeStruct((t, w), dt) for w, dt in zip(widths, dtypes)],
        scratch_shapes=[pltpu.VMEM((tm + SUBLANES, 2 * LANES), F32)] * (2 * ML_WIDTH // (2 * LANES))
        + [pltpu.VMEM((tm, d), BF16)],
        compiler_params=pltpu.CompilerParams(dimension_semantics=("arbitrary",),
                                             vmem_limit_bytes=VMEM_LIMIT),
        name="inproj",
    )(x2d, w_packed, conv_w, conv_b)


def _time_scan(x, combine, identity):
    row = _iota(x.shape, 0)
    sh = 1
    while sh < x.shape[0]:
        x = combine(x, jnp.where(row >= sh, pltpu.roll(x, sh, 0), identity))
        sh *= 2
    return x


def _mlstm_kernel(q_ref, k_ref, v_ref, og_ref, g_ref, gb_ref, nw_ref, out_ref, ct_ref, m_ref, *, tq):
    s_idx = pl.program_id(1)
    L = ML_CHUNK
    D = ML_HEAD_DIM
    H = ML_HEADS
    heads = range(H)

    @pl.when(s_idx == 0)
    def _():
        ct_ref[...] = jnp.zeros_like(ct_ref)
        m_ref[...] = jnp.zeros_like(m_ref)

    causal = _iota((L, L), 1) <= _iota((L, L), 0)
    ones_blk = jnp.ones((L, D), BF16)
    mean_blk = jnp.full((D, D), 1.0 / D, BF16)
    m_prev = m_ref[0:1, :]
    head_lanes = _iota((L, LANES), 1) < H
    tile_of_lane = jnp.right_shift(_iota((LANES, H * L), 1), L.bit_length() - 1)
    spread = jnp.where(_iota((LANES, H * L), 0) == tile_of_lane, 1.0, 0.0).astype(BF16)

    def spread_heads(x):
        x = jnp.where(head_lanes, x, 0.0)
        hi = x.astype(BF16)
        lo = (x - hi.astype(F32)).astype(BF16)
        return _dot(hi, spread) + _dot(lo, spread)

    for c in range(tq // L):
        r0 = c * L
        gi = g_ref[r0:r0 + L, 0:LANES] + gb_ref[:, 0:LANES]
        gf = g_ref[r0:r0 + L, LANES:2 * LANES] + gb_ref[:, LANES:2 * LANES]
        b_cum = _time_scan(_log_sigmoid(gf), jnp.add, 0.0)
        r = gi - b_cum
        g = jnp.maximum(m_prev, _time_scan(r, jnp.maximum, NEG_INF))
        g_rep = spread_heads(g)
        b_rep = spread_heads(b_cum)
        b_last = b_cum[L - 1:L, :]
        m_new = jnp.maximum(b_last + m_prev, jnp.max(b_last + r, axis=0, keepdims=True))
        decay = jnp.exp(b_last + m_prev - m_new)
        shift = b_last - m_new
        r_t = r.T

        q_b = [q_ref[r0:r0 + L, h * D:(h + 1) * D] for h in heads]
        kt_b = [k_ref[r0:r0 + L, h * D:(h + 1) * D].T for h in heads]
        v_aug = [jnp.concatenate([v_ref[r0:r0 + L, h * D:(h + 1) * D], ones_blk], axis=-1) for h in heads]
        g_col = [g_rep[:, h * L:(h + 1) * L] for h in heads]
        w_intra = [jnp.exp(jnp.where(causal, r_t[h:h + 1, :] - g_col[h], NEG_INF)) for h in heads]
        s_b = [(_dot(q_b[h], kt_b[h]) * w_intra[h]).astype(BF16) for h in heads]
        ct = [ct_ref[h] for h in heads]
        inter = [_dot(q_b[h], ct[h].astype(BF16)) for h in heads]
        intra = [_dot(s_b[h], v_aug[h]) for h in heads]
        for h in heads:
            wi_col = jnp.exp(m_prev[:, h:h + 1] - g_col[h])
            clamp = jnp.exp(-(b_rep[:, h * L:(h + 1) * L] + g_col[h]))
            num = wi_col * inter[h][:, 0:D] + intra[h][:, 0:D]
            den = wi_col * inter[h][:, D:] + intra[h][:, D:]
            hh = num / jnp.maximum(jnp.abs(den), clamp)
            mu = _dot(hh.astype(BF16), mean_blk)
            hc = hh - mu
            var = _dot((hc * hc).astype(BF16), mean_blk)
            hn = hc * lax.rsqrt(var + LN_EPS) * nw_ref[:, h * D:(h + 1) * D]
            gate_o = og_ref[r0:r0 + L, h * D:(h + 1) * D].astype(F32)
            out_ref[r0:r0 + L, h * D:(h + 1) * D] = (gate_o * hn).astype(out_ref.dtype)
        for h in heads:
            w_row = jnp.exp(r_t[h:h + 1, :] + shift[:, h:h + 1])
            ktw = (kt_b[h].astype(F32) * w_row).astype(BF16)
            ct_ref[h] = decay[:, h:h + 1] * ct[h] + _dot(ktw, v_aug[h])
        m_prev = m_new

    m_ref[...] = jnp.broadcast_to(m_prev, m_ref.shape)


def _mlstm(q, k, v, og, g, gate_bias_row, norm_w_row, tq):
    b, s, _ = q.shape
    kern = functools.partial(_mlstm_kernel, tq=tq)

    def seq_spec(width):
        return pl.BlockSpec((None, tq, width), lambda bi, si: (bi, si, 0))

    def const_spec(shape):
        return pl.BlockSpec(shape, lambda bi, si: (0,) * len(shape))

    return pl.pallas_call(
        kern,
        grid=(b, s // tq),
        in_specs=[seq_spec(ML_WIDTH), seq_spec(ML_WIDTH), seq_spec(ML_WIDTH), seq_spec(ML_WIDTH),
                  seq_spec(2 * LANES), const_spec((1, 2 * LANES)), const_spec((1, ML_WIDTH))],
        out_specs=seq_spec(ML_WIDTH),
        out_shape=jax.ShapeDtypeStruct((b, s, ML_WIDTH), BF16),
        scratch_shapes=[pltpu.VMEM((ML_HEADS, ML_HEAD_DIM, 2 * ML_HEAD_DIM), F32),
                        pltpu.VMEM((SUBLANES, LANES), F32)],
        compiler_params=pltpu.CompilerParams(dimension_semantics=("arbitrary", "arbitrary"),
                                             vmem_limit_bytes=VMEM_LIMIT),
        name="mlstm",
    )(q, k, v, og, g, gate_bias_row, norm_w_row)


def _rope_tables(seq_len):
    half = ATT_HEAD_DIM // 2
    inv_freq = ROPE_THETA ** (-jnp.arange(half, dtype=F32) / half)
    ang = jnp.arange(seq_len, dtype=F32)[:, None] * inv_freq[None, :]
    cos = jnp.cos(ang)
    sin = jnp.sin(ang)
    cos_t = jnp.concatenate([cos, cos, cos, cos], axis=-1)
    sin_t = jnp.concatenate([-sin, -sin, sin, sin], axis=-1)
    return cos_t, sin_t


def _swa_kernel(sink_ref, aq_ref, ak_ref, av_ref, cos_ref, sin_ref, out_ref, kprev_ref, vprev_ref, *, nsub):
    step = pl.program_id(1)
    Lb = WINDOW
    half = ATT_HEAD_DIM // 2
    pairs = ATT_Q_HEADS // 2
    pairs_per_kv = pairs // ATT_KV_HEADS

    @pl.when(step == 0)
    def _():
        kprev_ref[...] = jnp.zeros_like(kprev_ref)
        vprev_ref[...] = jnp.zeros_like(vprev_ref)

    def rope(x, cos, sin):
        tiles = []
        for c in range(x.shape[-1] // LANES):
            xt = x[:, c * LANES:(c + 1) * LANES]
            tiles.append(xt * cos + pltpu.roll(xt, LANES // 2, 1) * sin)
        return jnp.concatenate(tiles, axis=-1)

    ql = _iota((Lb, 2 * Lb), 0)
    kj = _iota((Lb, 2 * Lb), 1)
    diff = Lb + ql - kj
    lane = _iota((Lb, LANES), 1)
    low_half = lane < ATT_HEAD_DIM
    first_head = (lane & half) == 0
    ones_blk = jnp.ones((2 * Lb, LANES), BF16)

    k_prev = kprev_ref
    v_prev = vprev_ref
    for j in range(nsub):
        r0 = j * Lb
        cos = cos_ref[r0:r0 + Lb, :]
        sin = sin_ref[r0:r0 + Lb, :]
        q = rope(aq_ref[r0:r0 + Lb, :], cos, sin) * (ATT_HEAD_DIM ** -0.5)
        k_cur = rope(ak_ref[r0:r0 + Lb, :], cos, sin).astype(BF16)
        v_cur = av_ref[r0:r0 + Lb, :]

        kpos = (step * nsub + j) * Lb + kj - Lb
        visible = jnp.where(diff >= 0, jnp.where(diff < WINDOW, jnp.where(kpos >= 0, 1, 0), 0), 0)
        bias = jnp.where(visible > 0, 0.0, NEG_INF).astype(F32)
        bias = jnp.concatenate([bias] * (2 * pairs_per_kv), axis=0)

        for g in range(ATT_KV_HEADS):
            kk = jnp.concatenate([k_prev[:, g * LANES:(g + 1) * LANES], k_cur[:, g * LANES:(g + 1) * LANES]], axis=0)
            vv = jnp.concatenate([v_prev[:, g * LANES:(g + 1) * LANES], v_cur[:, g * LANES:(g + 1) * LANES]], axis=0)
            vv_aug = jnp.concatenate([vv, ones_blk], axis=-1)
            rows = []
            sinks = []
            for p in range(pairs_per_kv):
                pair = g * pairs_per_kv + p
                q2 = q[:, pair * LANES:(pair + 1) * LANES]
                rows.append(jnp.where(first_head, q2, 0.0))
                rows.append(jnp.where(first_head, 0.0, q2))
                sinks.append(jnp.full((Lb, LANES), sink_ref[2 * pair], F32))
                sinks.append(jnp.full((Lb, LANES), sink_ref[2 * pair + 1], F32))
            qs = jnp.concatenate(rows, axis=0).astype(BF16)
            sink = jnp.concatenate(sinks, axis=0)

            sc = lax.dot_general(qs, kk, (((1,), (1,)), ((), ())), preferred_element_type=F32) + bias
            m = jnp.maximum(jnp.broadcast_to(jnp.max(sc, axis=-1, keepdims=True), sink.shape), sink)
            p_un = jnp.exp(sc - jnp.concatenate([m, m], axis=-1))
            acc = _dot(p_un.astype(BF16), vv_aug)
            o = acc[:, 0:LANES] / (acc[:, LANES:] + jnp.exp(sink - m))
            for p in range(pairs_per_kv):
                pair = g * pairs_per_kv + p
                even = o[(2 * p) * Lb:(2 * p + 1) * Lb, :]
                odd = o[(2 * p + 1) * Lb:(2 * p + 2) * Lb, :]
                out_ref[r0:r0 + Lb, pair * LANES:(pair + 1) * LANES] = (
                    jnp.where(low_half, even, odd).astype(out_ref.dtype))
        k_prev = k_cur
        v_prev = v_cur

    kprev_ref[...] = k_prev
    vprev_ref[...] = v_prev


def _swa(aq, ak, av, cos_t, sin_t, sinks, nsub):
    b, s, _ = aq.shape
    kvw = ATT_KV_HEADS * LANES
    rows = nsub * WINDOW

    def seq_spec(width):
        return pl.BlockSpec((None, rows, width), lambda bi, si: (bi, si, 0))

    tab_spec = pl.BlockSpec((rows, LANES), lambda bi, si: (si, 0))
    return pl.pallas_call(
        functools.partial(_swa_kernel, nsub=nsub),
        grid=(b, s // rows),
        in_specs=[pl.BlockSpec(memory_space=pltpu.SMEM),
                  seq_spec(ATT_WIDTH), seq_spec(kvw), seq_spec(kvw), tab_spec, tab_spec],
        out_specs=seq_spec(ATT_WIDTH),
        out_shape=jax.ShapeDtypeStruct((b, s, ATT_WIDTH), BF16),
        scratch_shapes=[pltpu.VMEM((WINDOW, kvw), BF16), pltpu.VMEM((WINDOW, kvw), BF16)],
        compiler_params=pltpu.CompilerParams(dimension_semantics=("arbitrary", "arbitrary"),
                                             vmem_limit_bytes=VMEM_LIMIT),
        name="swa",
    )(sinks, aq, ak, av, cos_t, sin_t)


def _store_token_tiles(ref, val, row0=0, rows_per_token=None):
    n, w = val.shape
    segs = w // LANES
    rpt = rows_per_token or segs
    for j in range(segs):
        ref[pl.ds(row0 + j, n, stride=rpt), :] = val[:, j * LANES:(j + 1) * LANES]


def _load_token_tiles(ref, n, segs, row0=0, rows_per_token=None):
    rpt = rows_per_token or segs
    return jnp.concatenate([ref[pl.ds(row0 + j, n, stride=rpt), :] for j in range(segs)], axis=-1)


ROUTER_ROWS = 48


def _pack_router(w_group, b_group, w_expert, b_expert):
    d = w_group.shape[0]
    wt = jnp.zeros((ROUTER_ROWS, d), F32)
    wt = wt.at[0:N_GROUPS].set(w_group.T).at[SUBLANES:SUBLANES + N_EXPERTS].set(w_expert.T)
    bias = jnp.zeros((ROUTER_ROWS,), F32)
    bias = bias.at[0:N_GROUPS].set(b_group).at[SUBLANES:SUBLANES + N_EXPERTS].set(b_expert)
    hi = wt.astype(BF16)
    lo = (wt - hi.astype(F32)).astype(BF16)
    return jnp.concatenate([hi, lo], axis=0), jnp.broadcast_to(bias[:, None], (ROUTER_ROWS, LANES))


def _outproj_kernel(x_ref, ml_ref, att_ref, wo_ref, lnw_ref, lnb_ref, wrt_ref, brt_ref,
                    x1t_ref, meta_ref, cnt_ref, *, alpha):
    step = pl.program_id(0)
    tm, d = x_ref.shape
    segs = d // LANES
    pt = tm
    rr = wrt_ref.shape[0] // 2
    nt = (((1,), (1,)), ((), ()))
    row = _iota((SUBLANES, pt), 0).astype(F32)
    lane = _iota((pt, LANES), 1).astype(F32)

    def first_argmax(vals):
        top = jnp.max(vals, axis=0, keepdims=True)
        idx = jnp.min(jnp.where(vals == top, row, float(SUBLANES)), axis=0, keepdims=True)
        return top, idx

    @pl.when(step == 0)
    def _():
        cnt_ref[...] = jnp.zeros_like(cnt_ref)

    counts = jnp.zeros((1, LANES), F32)
    for r0 in range(0, tm, pt):
        y = (_dot(ml_ref[r0:r0 + pt, :], wo_ref[0:ML_WIDTH, :])
             + _dot(att_ref[r0:r0 + pt, :], wo_ref[ML_WIDTH:, :]))
        x1 = _layer_norm(alpha * x_ref[r0:r0 + pt, :] + y, lnw_ref[...], lnb_ref[...])
        _store_token_tiles(x1t_ref.at[pl.ds(r0 * segs, pt * segs)], x1)

        x1_hi = x1.astype(BF16)
        x1_lo = (x1 - x1_hi.astype(F32)).astype(BF16)
        both = lax.dot_general(wrt_ref[...], x1_hi, nt, preferred_element_type=F32)
        cross = lax.dot_general(wrt_ref[0:rr, :], x1_lo, nt, preferred_element_type=F32)
        logits = both[0:rr] + both[rr:2 * rr] + cross + jnp.concatenate([brt_ref[...]] * (pt // LANES), axis=1)

        g_logits = jnp.where(row < N_GROUPS, logits[0:SUBLANES], NEG_INF)
        g_top, g_idx = first_argmax(g_logits)
        g_p = 1.0 / jnp.sum(jnp.exp(g_logits - g_top), axis=0, keepdims=True)

        e_logits = logits[SUBLANES:2 * SUBLANES]
        for grp in range(1, N_GROUPS):
            e_logits = jnp.where(g_idx == grp, logits[(1 + grp) * SUBLANES:(2 + grp) * SUBLANES], e_logits)
        v1, a1 = first_argmax(e_logits)
        v2, a2 = first_argmax(jnp.where(row == a1, NEG_INF, e_logits))
        r = jnp.exp(v2 - v1)
        w1 = g_p / (1.0 + r)
        w2 = g_p * r / (1.0 + r)

        lo = jnp.minimum(a1, a2)
        hi = jnp.maximum(a1, a2)
        w_lo = jnp.where(a1 < a2, w1, w2)
        w_hi = jnp.where(a1 < a2, w2, w1)
        pair_idx = (EXPERTS_PER_GROUP - 1) * lo - lo * (lo - 1.0) * 0.5 + (hi - lo - 1.0)
        cls = g_idx * PAIRS_PER_GROUP + pair_idx

        meta_t = jnp.where(row == 0.0, cls, jnp.where(row == 1.0, w_lo, jnp.where(row == 2.0, w_hi, 0.0)))
        meta = jnp.concatenate([meta_t, jnp.zeros((LANES - SUBLANES, pt), F32)], axis=0).T
        meta_ref[r0:r0 + pt, :] = meta
        counts = counts + jnp.sum(jnp.where(lane == meta[:, 0:1], 1.0, 0.0), axis=0, keepdims=True)

    cnt_ref[0:1, :] += counts


def _outproj(x2d, ml2d, att2d, w_out_b, ln_w, ln_b, w_router, b_router, alpha, tm):
    t, d = x2d.shape
    kern = functools.partial(_outproj_kernel, alpha=alpha)

    def const_spec(shape):
        return pl.BlockSpec(shape, lambda i: (0,) * len(shape))

    return pl.pallas_call(
        kern,
        grid=(t // tm,),
        in_specs=[pl.BlockSpec((tm, d), lambda i: (i, 0)),
                  pl.BlockSpec((tm, ML_WIDTH), lambda i: (i, 0)),
                  pl.BlockSpec((tm, ATT_WIDTH), lambda i: (i, 0)),
                  const_spec((ML_WIDTH + ATT_WIDTH, d)), const_spec((1, d)), const_spec((1, d)),
                  const_spec((2 * ROUTER_ROWS, d)), const_spec((ROUTER_ROWS, LANES))],
        out_specs=[pl.BlockSpec((tm * (d // LANES), LANES), lambda i: (i, 0)),
                   pl.BlockSpec((tm, LANES), lambda i: (i, 0)), const_spec((SUBLANES, LANES))],
        out_shape=[jax.ShapeDtypeStruct((t * (d // LANES), LANES), F32),
                   jax.ShapeDtypeStruct((t, LANES), F32),
                   jax.ShapeDtypeStruct((SUBLANES, LANES), F32)],
        compiler_params=pltpu.CompilerParams(dimension_semantics=("arbitrary",),
                                             vmem_limit_bytes=VMEM_LIMIT),
        name="outproj",
    )(x2d, ml2d, att2d, w_out_b, ln_w, ln_b, w_router, b_router)


def _class_expert_table():
    tab = np.zeros((SUBLANES, LANES), np.float32)
    for g in range(N_GROUPS):
        idx = 0
        for lo in range(EXPERTS_PER_GROUP):
            for hi in range(lo + 1, EXPERTS_PER_GROUP):
                c = g * PAIRS_PER_GROUP + idx
                tab[0, c] = g * EXPERTS_PER_GROUP + lo
                tab[1, c] = g * EXPERTS_PER_GROUP + hi
                idx += 1
    return tab


def _rank_kernel(meta_ref, cnt_ref, tab_ref, pos_ref, tile_ref, base_ref, run_ref, *, tb, n_tiles_pad):
    step = pl.program_id(0)
    lane8 = _iota((SUBLANES, LANES), 1)

    @pl.when(step == 0)
    def _():
        cnt = jnp.broadcast_to(cnt_ref[0:1, :], (SUBLANES, LANES))
        tiles = jnp.floor((cnt + (MOE_TILE - 1.0)) * (1.0 / MOE_TILE))
        cum = tiles
        sh = 1
        while sh < LANES:
            cum = cum + jnp.where(lane8 >= sh, pltpu.roll(cum, sh, 1), 0.0)
            sh *= 2
        excl = cum - tiles
        base_ref[...] = excl * MOE_TILE
        run_ref[...] = jnp.zeros_like(run_ref)

        ti = _iota((n_tiles_pad, LANES), 0).astype(F32)
        lane = _iota((n_tiles_pad, LANES), 1)
        done = jnp.where(lane < N_CLASSES, jnp.where(cum[0:1, :] <= ti, 1.0, 0.0), 0.0)
        t_cls = jnp.sum(done, axis=-1, keepdims=True)
        sel = jnp.where(lane.astype(F32) == t_cls, 1.0, 0.0)
        cnt_i = jnp.sum(sel * cnt[0:1, :], axis=-1, keepdims=True)
        first_i = jnp.sum(sel * excl[0:1, :], axis=-1, keepdims=True)
        rows_i = jnp.clip(cnt_i - MOE_TILE * (ti[:, 0:1] - first_i), 0.0, float(MOE_TILE))
        e_lo = jnp.sum(sel * tab_ref[0:1, :], axis=-1, keepdims=True)
        e_hi = jnp.sum(sel * tab_ref[1:2, :], axis=-1, keepdims=True)
        n_tiles = jnp.sum(jnp.where(lane < N_CLASSES, jnp.broadcast_to(tiles[0:1, :], (n_tiles_pad, LANES)), 0.0),
                          axis=-1, keepdims=True)
        info = jnp.where(lane == 0, rows_i,
                         jnp.where(lane == 1, e_lo, jnp.where(lane == 2, e_hi, jnp.where(lane == 3, n_tiles, 0.0))))
        tile_ref[...] = info.astype(I32)

    cls = meta_ref[:, 0:1]
    lane = _iota((tb, LANES), 1).astype(F32)
    onehot = jnp.where(lane == cls, 1.0, 0.0)
    strict_lower = jnp.where(_iota((tb, tb), 1) < _iota((tb, tb), 0), 1.0, 0.0).astype(BF16)
    before = _dot(strict_lower, onehot.astype(BF16))
    slot = jnp.sum(onehot * (before + run_ref[0:1, :] + base_ref[0:1, :]), axis=-1, keepdims=True)
    run_ref[...] = run_ref[...] + jnp.sum(onehot, axis=0, keepdims=True)
    slot_t = jnp.broadcast_to(slot, (tb, LANES)).T
    pos_ref[...] = slot_t[0:SUBLANES, :].astype(I32)


def _rank(meta, counts, tb, n_tiles_pad):
    t = meta.shape[0]
    kern = functools.partial(_rank_kernel, tb=tb, n_tiles_pad=n_tiles_pad)
    tab = jnp.asarray(_class_expert_table())
    return pl.pallas_call(
        kern,
        grid=(t // tb,),
        in_specs=[pl.BlockSpec((tb, LANES), lambda i: (i, 0)),
                  pl.BlockSpec((SUBLANES, LANES), lambda i: (0, 0)),
                  pl.BlockSpec((SUBLANES, LANES), lambda i: (0, 0))],
        out_specs=[pl.BlockSpec((SUBLANES, tb), lambda i: (0, i)),
                   pl.BlockSpec((n_tiles_pad, LANES), lambda i: (0, 0))],
        out_shape=[jax.ShapeDtypeStruct((SUBLANES, t), I32),
                   jax.ShapeDtypeStruct((n_tiles_pad, LANES), I32)],
        scratch_shapes=[pltpu.VMEM((SUBLANES, LANES), F32), pltpu.VMEM((SUBLANES, LANES), F32)],
        compiler_params=pltpu.CompilerParams(dimension_semantics=("arbitrary",),
                                             vmem_limit_bytes=VMEM_LIMIT),
        name="rank",
    )(meta, counts, tab)


DMA_UNROLL = 8


def _pad_chunks():
    size, out = 1, []
    while size <= MOE_TILE:
        out.append(size)
        size *= 2
    return out[::-1]


def _dispatch_kernel(pos_ref, rows_ref, src_ref, dst_ref, zero_ref, sems, *, tb, rpt, n_tiles):
    step = pl.program_id(0)
    base = step * tb

    def pad_copies(act):
        def one_tile(i, carry):
            valid = rows_ref[i]
            pad = MOE_TILE - valid
            off = i * MOE_TILE + valid
            for size in _pad_chunks():
                hit = (pad & size) != 0

                @pl.when(hit)
                def _():
                    act(pltpu.make_async_copy(zero_ref.at[pl.ds(0, size * rpt)],
                                              dst_ref.at[pl.ds(pl.multiple_of(off * rpt, rpt), size * rpt)],
                                              sems.at[1]))
                off = off + jnp.where(hit, size, 0)
            return carry
        lax.fori_loop(0, n_tiles, one_tile, 0)

    @pl.when(step == 0)
    def _():
        zero_ref[...] = jnp.zeros_like(zero_ref)
        pad_copies(lambda cp: cp.start())

    def issue(grp, carry):
        for u in range(DMA_UNROLL):
            j = grp * DMA_UNROLL + u
            src = src_ref.at[pl.ds(pl.multiple_of(j * rpt, rpt), rpt)]
            dst = dst_ref.at[pl.ds(pl.multiple_of(pos_ref[base + j] * rpt, rpt), rpt)]
            pltpu.make_async_copy(src, dst, sems.at[0]).start(priority=u % 2)
        return carry

    lax.fori_loop(0, tb // DMA_UNROLL, issue, 0)
    pltpu.make_async_copy(src_ref, dst_ref.at[pl.ds(0, tb * rpt)], sems.at[0]).wait()

    @pl.when(step == pl.num_programs(0) - 1)
    def _():
        pad_copies(lambda cp: cp.wait())


def _dispatch(pos, tile_rows, src, n_slots, rpt, tb):
    t = pos.shape[0]
    n_tiles = tile_rows.shape[0]
    kern = functools.partial(_dispatch_kernel, tb=tb, rpt=rpt, n_tiles=n_tiles)
    pad_rows = _pad_chunks()[0] * rpt
    return pl.pallas_call(
        kern,
        grid_spec=pltpu.PrefetchScalarGridSpec(
            num_scalar_prefetch=2,
            grid=(t // tb,),
            in_specs=[pl.BlockSpec((tb * rpt, LANES), lambda i, pos_ref, rows_ref: (i, 0))],
            out_specs=pl.BlockSpec(memory_space=pl.ANY),
            scratch_shapes=[pltpu.VMEM((pad_rows, LANES), src.dtype), pltpu.SemaphoreType.DMA((2,))],
        ),
        out_shape=jax.ShapeDtypeStruct((n_slots * rpt, LANES), src.dtype),
        compiler_params=pltpu.CompilerParams(dimension_semantics=("arbitrary",),
                                             has_side_effects=True, vmem_limit_bytes=VMEM_LIMIT),
        name="dispatch",
    )(pos, tile_rows, src)


def _collect_kernel(pos_ref, ys_ref, x1t_ref, meta_ref, lnw_ref, lnb_ref, out_ref, buf_ref, sems,
                    *, tb, alpha, d):
    segs = d // LANES
    rpt = 2 * segs
    step = pl.program_id(0)
    n_steps = pl.num_programs(0)

    def gather(blk, slot):
        base = blk * tb

        def issue(grp, carry):
            for u in range(DMA_UNROLL):
                j = grp * DMA_UNROLL + u
                src = ys_ref.at[pl.ds(pos_ref[base + j], 1)]
                dst = buf_ref.at[slot, pl.ds(j, 1)]
                pltpu.make_async_copy(src, dst, sems.at[slot]).start(priority=u % 2)
            return carry

        lax.fori_loop(0, tb // DMA_UNROLL, issue, 0)

    @pl.when(step == 0)
    def _():
        gather(0, 0)

    @pl.when(step + 1 < n_steps)
    def _():
        gather(step + 1, (step + 1) % 2)

    slot = step % 2
    pltpu.make_async_copy(ys_ref.at[pl.ds(0, tb)], buf_ref.at[slot], sems.at[slot]).wait()

    x1 = _load_token_tiles(x1t_ref, tb, segs)
    y_lo = buf_ref[slot, :, 0:d]
    y_hi = buf_ref[slot, :, d:2 * d]
    meta = meta_ref[...]
    z = alpha * x1 + meta[:, 1:2] * y_lo + meta[:, 2:3] * y_hi
    out_ref[...] = _layer_norm(z, lnw_ref[...], lnb_ref[...])


def _collect(pos, ys, x1t, meta, ln_w, ln_b, alpha, d, tb):
    t = pos.shape[0]
    segs = d // LANES
    kern = functools.partial(_collect_kernel, tb=tb, alpha=alpha, d=d)
    return pl.pallas_call(
        kern,
        grid_spec=pltpu.PrefetchScalarGridSpec(
            num_scalar_prefetch=1,
            grid=(t // tb,),
            in_specs=[pl.BlockSpec(memory_space=pl.ANY),
                      pl.BlockSpec((tb * segs, LANES), lambda i, pos_ref: (i, 0)),
                      pl.BlockSpec((tb, LANES), lambda i, pos_ref: (i, 0)),
                      pl.BlockSpec((1, d), lambda i, pos_ref: (0, 0)),
                      pl.BlockSpec((1, d), lambda i, pos_ref: (0, 0))],
            out_specs=pl.BlockSpec((tb, d), lambda i, pos_ref: (i, 0)),
            scratch_shapes=[pltpu.VMEM((2, tb, 2 * d), F32), pltpu.SemaphoreType.DMA((2,))],
        ),
        out_shape=jax.ShapeDtypeStruct((t, d), F32),
        compiler_params=pltpu.CompilerParams(dimension_semantics=("arbitrary",),
                                             vmem_limit_bytes=VMEM_LIMIT),
        name="collect",
    )(pos, ys, x1t, meta, ln_w, ln_b)


MOE_TILES_PER_STEP = 1


def _moe_kernel(rows_ref, elo_ref, ehi_ref, nt_ref, xs_ref, *refs, d):
    *w_refs, ys_ref = refs
    i = pl.program_id(0)
    segs = d // LANES

    @pl.when(rows_ref[i * MOE_TILES_PER_STEP] > 0)
    def _():
        for k in range(MOE_TILES_PER_STEP):
            wgu_lo, wd_lo, wgu_hi, wd_hi = w_refs[4 * k:4 * k + 4]
            xs_k = xs_ref.at[pl.ds(k * MOE_TILE * segs, MOE_TILE * segs)]
            xb = _load_token_tiles(xs_k, MOE_TILE, segs).astype(BF16)

            def expert(wgu_ref, wd_ref):
                de = wd_ref.shape[0]
                gate_up = _dot(xb, wgu_ref[...])
                gate = gate_up[:, 0:de]
                hidden = gate * _sigmoid(gate) * gate_up[:, de:]
                return _dot(hidden.astype(BF16), wd_ref[...])

            ys_ref[k * MOE_TILE:(k + 1) * MOE_TILE, 0:d] = expert(wgu_lo, wd_lo)
            ys_ref[k * MOE_TILE:(k + 1) * MOE_TILE, d:2 * d] = expert(wgu_hi, wd_hi)

    @pl.when(rows_ref[i * MOE_TILES_PER_STEP] <= 0)
    def _():
        ys_ref[...] = jnp.zeros_like(ys_ref)


def _moe(info, xs, wgu, wd, d, n_tiles_pad):
    de = wd.shape[1]
    segs = d // LANES
    tps = MOE_TILES_PER_STEP
    kern = functools.partial(_moe_kernel, d=d)

    def last_live(tile, nt_ref):
        return jnp.minimum(tile, jnp.maximum(nt_ref[0] - 1, 0))

    def up_spec(k, which):
        return pl.BlockSpec((None, d, 2 * de),
                            lambda i, rows, elo, ehi, nt: ((elo, ehi)[which][last_live(i * tps + k, nt)], 0, 0))

    def down_spec(k, which):
        return pl.BlockSpec((None, de, d),
                            lambda i, rows, elo, ehi, nt: ((elo, ehi)[which][last_live(i * tps + k, nt)], 0, 0))

    w_specs = []
    for k in range(tps):
        w_specs += [up_spec(k, 0), down_spec(k, 0), up_spec(k, 1), down_spec(k, 1)]
    rows, elo, ehi, nt = info[:, 0], info[:, 1], info[:, 2], info[0:1, 3]
    return pl.pallas_call(
        kern,
        grid_spec=pltpu.PrefetchScalarGridSpec(
            num_scalar_prefetch=4,
            grid=(n_tiles_pad // tps,),
            in_specs=[pl.BlockSpec((tps * MOE_TILE * segs, LANES),
                                   lambda i, rows, elo, ehi, nt: (last_live(i * tps, nt) // tps, 0))] + w_specs,
            out_specs=pl.BlockSpec((tps * MOE_TILE, 2 * d), lambda i, rows, elo, ehi, nt: (i, 0)),
        ),
        out_shape=jax.ShapeDtypeStruct((n_tiles_pad * MOE_TILE, 2 * d), F32),
        compiler_params=pltpu.CompilerParams(dimension_semantics=("arbitrary",),
                                             vmem_limit_bytes=VMEM_LIMIT),
        name="moe",
    )(rows, elo, ehi, nt, xs, *([wgu, wd, wgu, wd] * tps))


def _pick_block(n, target):
    blk = min(n, target)
    while n % blk:
        blk //= 2
    return blk


def kernel(x, w_in, conv_w, conv_b, mlstm_gate_bias, mlstm_norm_w, attn_sinks, w_out, ln1_w, ln1_b,
           w_group_router, b_group_router, w_expert_router, b_expert_router,
           w_exp_gate, w_exp_up, w_exp_down, ln2_w, ln2_b):
    b, s, d = x.shape
    t = b * s
    depth = w_in.shape[0]
    alpha = (2.0 * depth) ** 0.25
    assert s % ML_CHUNK == 0 and s % WINDOW == 0 and d % LANES == 0

    tm = _pick_block(s, 1024)
    tq = _pick_block(s, 8 * ML_CHUNK)
    tb_rank = _pick_block(t, 512)
    tb_dma = _pick_block(t, 2048)
    tb_col = _pick_block(t, 512)
    n_tiles_pad = -(-(t // MOE_TILE + N_CLASSES) // SUBLANES) * SUBLANES
    cos_t, sin_t = _rope_tables(s)

    for l in range(depth):
        x2d = x.reshape(t, d)
        q, k, v, og, aq, ak, av, g = _inproj(x2d, _pack_w_in(w_in[l]), conv_w[l], conv_b[l][None, :], tm, s // tm)
        bias_pad = jnp.zeros((LANES - ML_HEADS,), F32)
        gate_bias_row = jnp.concatenate(
            [mlstm_gate_bias[l, 0], bias_pad, mlstm_gate_bias[l, 1], bias_pad])[None, :]
        ml = _mlstm(q.reshape(b, s, -1), k.reshape(b, s, -1), v.reshape(b, s, -1), og.reshape(b, s, -1),
                    g.reshape(b, s, -1), gate_bias_row, mlstm_norm_w[l][None, :], tq)
        att = _swa(aq.reshape(b, s, -1), ak.reshape(b, s, -1), av.reshape(b, s, -1), cos_t, sin_t, attn_sinks[l],
                   1)

        w_router, b_router = _pack_router(w_group_router[l], b_group_router[l],
                                          w_expert_router[l], b_expert_router[l])
        x1t, meta, counts = _outproj(x2d, ml.reshape(t, -1), att.reshape(t, -1), w_out[l].astype(BF16),
                                     ln1_w[l][None, :], ln1_b[l][None, :], w_router, b_router, alpha, tm)

        pos2d, info = _rank(meta, counts, tb_rank, n_tiles_pad)
        pos = pos2d[0]
        xs = _dispatch(pos, info[:, 0], x1t, n_tiles_pad * MOE_TILE, d // LANES, tb_dma)
        w_gate_up = jnp.concatenate([w_exp_gate[l].astype(BF16), w_exp_up[l].astype(BF16)], axis=-1)
        ys = _moe(info, xs, w_gate_up, w_exp_down[l].astype(BF16), d, n_tiles_pad)
        out = _collect(pos, ys, x1t, meta, ln2_w[l][None, :], ln2_b[l][None, :], alpha, d, tb_col)
        x = out.reshape(b, s, d)
    return x
```

```python
import functools
import math

import numpy as np
import jax
import jax.numpy as jnp
from jax import lax
from jax.experimental import pallas as pl
from jax.experimental.pallas import tpu as pltpu

F32 = jnp.float32
BF16 = jnp.bfloat16
I32 = jnp.int32

ML_HEADS = 4
ML_HEAD_DIM = 128
ML_WIDTH = ML_HEADS * ML_HEAD_DIM
ML_CHUNK = 128
CONV_WIDTH = 4
ATT_Q_HEADS = 8
ATT_KV_HEADS = 2
ATT_HEAD_DIM = 64
ATT_WIDTH = ATT_Q_HEADS * ATT_HEAD_DIM
ATT_KV_WIDTH = ATT_KV_HEADS * ATT_HEAD_DIM
WINDOW = 128
ROPE_THETA = 10000.0
N_GROUPS = 4
EXPERTS_PER_GROUP = 8
N_EXPERTS = N_GROUPS * EXPERTS_PER_GROUP
PAIRS_PER_GROUP = EXPERTS_PER_GROUP * (EXPERTS_PER_GROUP - 1) // 2
N_CLASSES = N_GROUPS * PAIRS_PER_GROUP
LN_EPS = 1e-5

LANES = 128
SUBLANES = 8
MOE_TILE = 320
VMEM_LIMIT = 56 * 1024 * 1024

NEG_INF = float("-inf")


def _sigmoid(x):
    return 1.0 / (1.0 + jnp.exp(-x))


def _log_sigmoid(x):
    return jnp.minimum(x, 0.0) - jnp.log(1.0 + jnp.exp(-jnp.abs(x)))


def _iota(shape, dim):
    return lax.broadcasted_iota(I32, shape, dim)


def _dot(a, b):
    return jnp.dot(a, b, preferred_element_type=F32)


def _dot_exact(a, b):
    return jnp.dot(a, b, preferred_element_type=F32, precision=lax.Precision.HIGHEST)


def _layer_norm(z, w, b):
    mu = jnp.mean(z, axis=-1, keepdims=True)
    zc = z - mu
    var = jnp.mean(zc * zc, axis=-1, keepdims=True)
    return zc * lax.rsqrt(var + LN_EPS) * w + b


C_QK = 0
C_V = C_QK + 2 * ML_WIDTH
C_O = C_V + ML_WIDTH
C_AQ = C_O + ML_WIDTH
C_AK = C_AQ + ATT_WIDTH
C_AV = C_AK + ATT_KV_HEADS * LANES
C_G = C_AV + ATT_KV_HEADS * LANES
C_END = C_G + 2 * LANES


def _pack_w_in(w_in):
    sizes = (2 * ML_WIDTH, ML_WIDTH, ML_WIDTH, ML_HEADS, ML_HEADS, ATT_WIDTH, ATT_KV_WIDTH, ATT_KV_WIDTH)
    splits = np.cumsum(sizes)[:-1].tolist()
    w_qk, w_v, w_o, w_i, w_f, w_aq, w_ak, w_av = jnp.split(w_in, splits, axis=-1)
    half = ATT_HEAD_DIM // 2

    def head(w, h):
        return w[:, h * ATT_HEAD_DIM:(h + 1) * ATT_HEAD_DIM]

    def q_tile(a, b):
        return [a[:, :half], b[:, :half], a[:, half:], b[:, half:]]

    q_cols = [t for p in range(ATT_Q_HEADS // 2) for t in q_tile(head(w_aq, 2 * p), head(w_aq, 2 * p + 1))]
    k_cols = [t for h in range(ATT_KV_HEADS) for t in q_tile(head(w_ak, h), head(w_ak, h))]
    v_cols = [t for h in range(ATT_KV_HEADS) for t in (head(w_av, h), head(w_av, h))]
    lane_pad = jnp.zeros((w_in.shape[0], LANES - ML_HEADS), w_in.dtype)
    packed = jnp.concatenate([w_qk, w_v, w_o] + q_cols + k_cols + v_cols + [w_i, lane_pad, w_f, lane_pad], axis=-1)
    return packed.astype(BF16)


def _inproj_kernel(x_ref, w_ref, cw_ref, cb_ref, q_ref, k_ref, v_ref, og_ref, aq_ref, ak_ref, av_ref, g_ref,
                   *scratch, blocks_per_seq):
    *ext_refs, xb_ref = scratch
    tm = x_ref.shape[0]
    halo = SUBLANES
    cs = ext_refs[0].shape[1]
    xb_ref[...] = x_ref[...].astype(BF16)

    @pl.when(pl.program_id(0) % blocks_per_seq == 0)
    def _():
        for ext_ref in ext_refs:
            ext_ref[0:halo, :] = jnp.zeros((halo, cs), F32)

    scale = ML_HEAD_DIM ** -0.5

    def mm(lo, hi):
        return _dot(xb_ref[...], w_ref[:, lo:hi])

    def conv_slice(idx):
        ext_ref = ext_refs[idx]
        c0 = idx * cs
        is_q = c0 < ML_WIDTH
        dst_ref, off = (q_ref, c0) if is_q else (k_ref, c0 - ML_WIDTH)
        rt = ML_CHUNK
        for r0 in range(0, tm, rt):
            conv = cb_ref[:, c0:c0 + cs]
            for j in range(CONV_WIDTH):
                start = halo + r0 - (CONV_WIDTH - 1) + j
                conv = conv + cw_ref[j:j + 1, c0:c0 + cs] * ext_ref[start:start + rt, :]
            act = conv * _sigmoid(conv)
            dst_ref[r0:r0 + rt, off:off + cs] = (act if is_q else act * scale).astype(BF16)
        ext_ref[0:halo, :] = ext_ref[tm:tm + halo, :]

    for idx in range(len(ext_refs)):
        ext_refs[idx][halo:halo + tm, :] = mm(C_QK + idx * cs, C_QK + (idx + 1) * cs)
        if idx > 0:
            conv_slice(idx - 1)
    half_v = ML_WIDTH // 2
    v_ref[:, 0:half_v] = mm(C_V, C_V + half_v).astype(BF16)
    conv_slice(len(ext_refs) - 1)
    v_ref[:, half_v:] = mm(C_V + half_v, C_O).astype(BF16)
    og_ref[...] = _sigmoid(mm(C_O, C_AQ)).astype(BF16)
    aq_ref[...] = mm(C_AQ, C_AK)
    ak_ref[...] = mm(C_AK, C_AV)
    av_ref[...] = mm(C_AV, C_G).astype(BF16)
    g_ref[...] = mm(C_G, C_END)


def _inproj(x2d, w_packed, conv_w, conv_b, tm, blocks_per_seq):
    t, d = x2d.shape
    widths = (ML_WIDTH, ML_WIDTH, ML_WIDTH, ML_WIDTH, C_AK - C_AQ, C_AV - C_AK, C_G - C_AV, C_END - C_G)
    dtypes = (BF16, BF16, BF16, BF16, F32, F32, BF16, F32)
    kern = functools.partial(_inproj_kernel, blocks_per_seq=blocks_per_seq)
    return pl.pallas_call(
        kern,
        grid=(t // tm,),
        in_specs=[pl.BlockSpec((tm, d), lambda i: (i, 0)),
                  pl.BlockSpec((d, C_END), lambda i: (0, 0)),
                  pl.BlockSpec((CONV_WIDTH, 2 * ML_WIDTH), lambda i: (0, 0)),
                  pl.BlockSpec((1, 2 * ML_WIDTH), lambda i: (0, 0))],
        out_specs=[pl.BlockSpec((tm, w), lambda i: (i, 0)) for w in widths],
        out_shape=[jax.ShapeDtypeStruct((t, w), dt) for w, dt in zip(widths, dtypes)],
        scratch_shapes=[pltpu.VMEM((tm + SUBLANES, 2 * LANES), F32)] * (2 * ML_WIDTH // (2 * LANES))
        + [pltpu.VMEM((tm, d), BF16)],
        compiler_params=pltpu.CompilerParams(dimension_semantics=("arbitrary",),
                                             vmem_limit_bytes=VMEM_LIMIT),
        name="inproj",
    )(x2d, w_packed, conv_w, conv_b)


def _time_scan(x, combine, identity):
    row = _iota(x.shape, 0)
    sh = 1
    while sh < x.shape[0]:
        x = combine(x, jnp.where(row >= sh, pltpu.roll(x, sh, 0), identity))
        sh *= 2
    return x


def _mlstm_kernel(q_ref, k_ref, v_ref, og_ref, g_ref, gb_ref, nw_ref, out_ref, ct_ref, m_ref, *, tq):
    s_idx = pl.program_id(1)
    L = ML_CHUNK
    D = ML_HEAD_DIM
    H = ML_HEADS
    heads = range(H)

    @pl.when(s_idx == 0)
    def _():
        ct_ref[...] = jnp.zeros_like(ct_ref)
        m_ref[...] = jnp.zeros_like(m_ref)

    causal = _iota((L, L), 1) <= _iota((L, L), 0)
    ones_blk = jnp.ones((L, D), BF16)
    mean_blk = jnp.full((D, D), 1.0 / D, BF16)
    m_prev = m_ref[0:1, :]
    head_lanes = _iota((L, LANES), 1) < H
    tile_of_lane = jnp.right_shift(_iota((LANES, H * L), 1), L.bit_length() - 1)
    spread = jnp.where(_iota((LANES, H * L), 0) == tile_of_lane, 1.0, 0.0).astype(BF16)

    def spread_heads(x):
        x = jnp.where(head_lanes, x, 0.0)
        hi = x.astype(BF16)
        lo = (x - hi.astype(F32)).astype(BF16)
        return _dot(hi, spread) + _dot(lo, spread)

    for c in range(tq // L):
        r0 = c * L
        gi = g_ref[r0:r0 + L, 0:LANES] + gb_ref[:, 0:LANES]
        gf = g_ref[r0:r0 + L, LANES:2 * LANES] + gb_ref[:, LANES:2 * LANES]
        b_cum = _time_scan(_log_sigmoid(gf), jnp.add, 0.0)
        r = gi - b_cum
        g = jnp.maximum(m_prev, _time_scan(r, jnp.maximum, NEG_INF))
        g_rep = spread_heads(g)
        b_rep = spread_heads(b_cum)
        b_last = b_cum[L - 1:L, :]
        m_new = jnp.maximum(b_last + m_prev, jnp.max(b_last + r, axis=0, keepdims=True))
        decay = jnp.exp(b_last + m_prev - m_new)
        shift = b_last - m_new
        r_t = r.T

        q_b = [q_ref[r0:r0 + L, h * D:(h + 1) * D] for h in heads]
        kt_b = [k_ref[r0:r0 + L, h * D:(h + 1) * D].T for h in heads]
        v_aug = [jnp.concatenate([v_ref[r0:r0 + L, h * D:(h + 1) * D], ones_blk], axis=-1) for h in heads]
        g_col = [g_rep[:, h * L:(h + 1) * L] for h in heads]
        w_intra = [jnp.exp(jnp.where(causal, r_t[h:h + 1, :] - g_col[h], NEG_INF)) for h in heads]
        s_b = [(_dot(q_b[h], kt_b[h]) * w_intra[h]).astype(BF16) for h in heads]
        ct = [ct_ref[h] for h in heads]
        inter = [_dot(q_b[h], ct[h].astype(BF16)) for h in heads]
        intra = [_dot(s_b[h], v_aug[h]) for h in heads]
        for h in heads:
            wi_col = jnp.exp(m_prev[:, h:h + 1] - g_col[h])
            clamp = jnp.exp(-(b_rep[:, h * L:(h + 1) * L] + g_col[h]))
            num = wi_col * inter[h][:, 0:D] + intra[h][:, 0:D]
            den = wi_col * inter[h][:, D:] + intra[h][:, D:]
            hh = num / jnp.maximum(jnp.abs(den), clamp)
            mu = _dot(hh.astype(BF16), mean_blk)
            hc = hh - mu
            var = _dot((hc * hc).astype(BF16), mean_blk)
            hn = hc * lax.rsqrt(var + LN_EPS) * nw_ref[:, h * D:(h + 1) * D]
            gate_o = og_ref[r0:r0 + L, h * D:(h + 1) * D].astype(F32)
            out_ref[r0:r0 + L, h * D:(h + 1) * D] = (gate_o * hn).astype(out_ref.dtype)
        for h in heads:
            w_row = jnp.exp(r_t[h:h + 1, :] + shift[:, h:h + 1])
            ktw = (kt_b[h].astype(F32) * w_row).astype(BF16)
            ct_ref[h] = decay[:, h:h + 1] * ct[h] + _dot(ktw, v_aug[h])
        m_prev = m_new

    m_ref[...] = jnp.broadcast_to(m_prev, m_ref.shape)


def _mlstm(q, k, v, og, g, gate_bias_row, norm_w_row, tq):
    b, s, _ = q.shape
    kern = functools.partial(_mlstm_kernel, tq=tq)

    def seq_spec(width):
        return pl.BlockSpec((None, tq, width), lambda bi, si: (bi, si, 0))

    def const_spec(shape):
        return pl.BlockSpec(shape, lambda bi, si: (0,) * len(shape))

    return pl.pallas_call(
        kern,
        grid=(b, s // tq),
        in_specs=[seq_spec(ML_WIDTH), seq_spec(ML_WIDTH), seq_spec(ML_WIDTH), seq_spec(ML_WIDTH),
                  seq_spec(2 * LANES), const_spec((1, 2 * LANES)), const_spec((1, ML_WIDTH))],
        out_specs=seq_spec(ML_WIDTH),
        out_shape=jax.ShapeDtypeStruct((b, s, ML_WIDTH), BF16),
        scratch_shapes=[pltpu.VMEM((ML_HEADS, ML_HEAD_DIM, 2 * ML_HEAD_DIM), F32),
                        pltpu.VMEM((SUBLANES, LANES), F32)],
        compiler_params=pltpu.CompilerParams(dimension_semantics=("arbitrary", "arbitrary"),
                                             vmem_limit_bytes=VMEM_LIMIT),
        name="mlstm",
    )(q, k, v, og, g, gate_bias_row, norm_w_row)


def _rope_tables(seq_len):
    half = ATT_HEAD_DIM // 2
    inv_freq = ROPE_THETA ** (-jnp.arange(half, dtype=F32) / half)
    ang = jnp.arange(seq_len, dtype=F32)[:, None] * inv_freq[None, :]
    cos = jnp.cos(ang)
    sin = jnp.sin(ang)
    cos_t = jnp.concatenate([cos, cos, cos, cos], axis=-1)
    sin_t = jnp.concatenate([-sin, -sin, sin, sin], axis=-1)
    return cos_t, sin_t


def _swa_kernel(sink_ref, aq_ref, ak_ref, av_ref, cos_ref, sin_ref, out_ref, kprev_ref, vprev_ref, *, nsub):
    step = pl.program_id(1)
    Lb = WINDOW
    half = ATT_HEAD_DIM // 2
    pairs = ATT_Q_HEADS // 2
    pairs_per_kv = pairs // ATT_KV_HEADS

    @pl.when(step == 0)
    def _():
        kprev_ref[...] = jnp.zeros_like(kprev_ref)
        vprev_ref[...] = jnp.zeros_like(vprev_ref)

    def rope(x, cos, sin):
        tiles = []
        for c in range(x.shape[-1] // LANES):
            xt = x[:, c * LANES:(c + 1) * LANES]
            tiles.append(xt * cos + pltpu.roll(xt, LANES // 2, 1) * sin)
        return jnp.concatenate(tiles, axis=-1)

    ql = _iota((Lb, 2 * Lb), 0)
    kj = _iota((Lb, 2 * Lb), 1)
    diff = Lb + ql - kj
    lane = _iota((Lb, LANES), 1)
    low_half = lane < ATT_HEAD_DIM
    first_head = (lane & half) == 0
    ones_blk = jnp.ones((2 * Lb, LANES), BF16)

    k_prev = kprev_ref
    v_prev = vprev_ref

    def block(j, carry):
        rows_j = pl.ds(pl.multiple_of(j * Lb, Lb), Lb)
        cos = cos_ref[rows_j, :]
        sin = sin_ref[rows_j, :]
        q = rope(aq_ref[rows_j, :], cos, sin) * (ATT_HEAD_DIM ** -0.5)
        k_cur = rope(ak_ref[rows_j, :], cos, sin).astype(BF16)
        v_cur = av_ref[rows_j, :]

        kpos = (step * nsub + j) * Lb + kj - Lb
        visible = jnp.where(diff >= 0, jnp.where(diff < WINDOW, jnp.where(kpos >= 0, 1, 0), 0), 0)
        bias = jnp.where(visible > 0, 0.0, NEG_INF).astype(F32)
        bias = jnp.concatenate([bias] * (2 * pairs_per_kv), axis=0)

        for g in range(ATT_KV_HEADS):
            kk = jnp.concatenate([k_prev[:, g * LANES:(g + 1) * LANES], k_cur[:, g * LANES:(g + 1) * LANES]], axis=0)
            vv = jnp.concatenate([v_prev[:, g * LANES:(g + 1) * LANES], v_cur[:, g * LANES:(g + 1) * LANES]], axis=0)
            vv_aug = jnp.concatenate([vv, ones_blk], axis=-1)
            rows = []
            sinks = []
            for p in range(pairs_per_kv):
                pair = g * pairs_per_kv + p
                q2 = q[:, pair * LANES:(pair + 1) * LANES]
                rows.append(jnp.where(first_head, q2, 0.0))
                rows.append(jnp.where(first_head, 0.0, q2))
                sinks.append(jnp.full((Lb, LANES), sink_ref[2 * pair], F32))
                sinks.append(jnp.full((Lb, LANES), sink_ref[2 * pair + 1], F32))
            qs = jnp.concatenate(rows, axis=0).astype(BF16)
            sink = jnp.concatenate(sinks, axis=0)

            sc = lax.dot_general(qs, kk, (((1,), (1,)), ((), ())), preferred_element_type=F32) + bias
            m = jnp.maximum(jnp.broadcast_to(jnp.max(sc, axis=-1, keepdims=True), sink.shape), sink)
            p_un = jnp.exp(sc - jnp.concatenate([m, m], axis=-1))
            acc = _dot(p_un.astype(BF16), vv_aug)
            o = acc[:, 0:LANES] / (acc[:, LANES:] + jnp.exp(sink - m))
            for p in range(pairs_per_kv):
                pair = g * pairs_per_kv + p
                even = o[(2 * p) * Lb:(2 * p + 1) * Lb, :]
                odd = o[(2 * p + 1) * Lb:(2 * p + 2) * Lb, :]
                out_ref[rows_j, pair * LANES:(pair + 1) * LANES] = (
                    jnp.where(low_half, even, odd).astype(out_ref.dtype))
        kprev_ref[...] = k_cur
        vprev_ref[...] = v_cur
        return carry

    lax.fori_loop(0, nsub, block, 0)


def _swa(aq, ak, av, cos_t, sin_t, sinks, nsub):
    b, s, _ = aq.shape
    kvw = ATT_KV_HEADS * LANES
    rows = nsub * WINDOW

    def seq_spec(width):
        return pl.BlockSpec((None, rows, width), lambda bi, si: (bi, si, 0))

    tab_spec = pl.BlockSpec((rows, LANES), lambda bi, si: (si, 0))
    return pl.pallas_call(
        functools.partial(_swa_kernel, nsub=nsub),
        grid=(b, s // rows),
        in_specs=[pl.BlockSpec(memory_space=pltpu.SMEM),
                  seq_spec(ATT_WIDTH), seq_spec(kvw), seq_spec(kvw), tab_spec, tab_spec],
        out_specs=seq_spec(ATT_WIDTH),
        out_shape=jax.ShapeDtypeStruct((b, s, ATT_WIDTH), BF16),
        scratch_shapes=[pltpu.VMEM((WINDOW, kvw), BF16), pltpu.VMEM((WINDOW, kvw), BF16)],
        compiler_params=pltpu.CompilerParams(dimension_semantics=("arbitrary", "arbitrary"),
                                             vmem_limit_bytes=VMEM_LIMIT),
        name="swa",
    )(sinks, aq, ak, av, cos_t, sin_t)


def _store_token_tiles(ref, val, row0=0, rows_per_token=None):
    n, w = val.shape
    segs = w // LANES
    rpt = rows_per_token or segs
    for j in range(segs):
        ref[pl.ds(row0 + j, n, stride=rpt), :] = val[:, j * LANES:(j + 1) * LANES]


def _load_token_tiles(ref, n, segs, row0=0, rows_per_token=None):
    rpt = rows_per_token or segs
    return jnp.concatenate([ref[pl.ds(row0 + j, n, stride=rpt), :] for j in range(segs)], axis=-1)


ROUTER_ROWS = 48


def _pack_router(w_group, b_group, w_expert, b_expert):
    d = w_group.shape[0]
    wt = jnp.zeros((ROUTER_ROWS, d), F32)
    wt = wt.at[0:N_GROUPS].set(w_group.T).at[SUBLANES:SUBLANES + N_EXPERTS].set(w_expert.T)
    bias = jnp.zeros((ROUTER_ROWS,), F32)
    bias = bias.at[0:N_GROUPS].set(b_group).at[SUBLANES:SUBLANES + N_EXPERTS].set(b_expert)
    hi = wt.astype(BF16)
    lo = (wt - hi.astype(F32)).astype(BF16)
    return jnp.concatenate([hi, lo], axis=0), jnp.broadcast_to(bias[:, None], (ROUTER_ROWS, LANES))


def _outproj_kernel(x_ref, ml_ref, att_ref, wo_ref, lnw_ref, lnb_ref, wrt_ref, brt_ref,
                    x1t_ref, meta_ref, cnt_ref, *, alpha):
    step = pl.program_id(0)
    tm, d = x_ref.shape
    segs = d // LANES
    pt = tm
    rr = wrt_ref.shape[0] // 2
    nt = (((1,), (1,)), ((), ()))
    row = _iota((SUBLANES, pt), 0).astype(F32)
    lane = _iota((pt, LANES), 1).astype(F32)

    def first_argmax(vals):
        top = jnp.max(vals, axis=0, keepdims=True)
        idx = jnp.min(jnp.where(vals == top, row, float(SUBLANES)), axis=0, keepdims=True)
        return top, idx

    @pl.when(step == 0)
    def _():
        cnt_ref[...] = jnp.zeros_like(cnt_ref)

    counts = jnp.zeros((1, LANES), F32)
    for r0 in range(0, tm, pt):
        y = (_dot(ml_ref[r0:r0 + pt, :], wo_ref[0:ML_WIDTH, :])
             + _dot(att_ref[r0:r0 + pt, :], wo_ref[ML_WIDTH:, :]))
        x1 = _layer_norm(alpha * x_ref[r0:r0 + pt, :] + y, lnw_ref[...], lnb_ref[...])
        _store_token_tiles(x1t_ref.at[pl.ds(r0 * segs, pt * segs)], x1)

        x1_hi = x1.astype(BF16)
        x1_lo = (x1 - x1_hi.astype(F32)).astype(BF16)
        both = lax.dot_general(wrt_ref[...], x1_hi, nt, preferred_element_type=F32)
        cross = lax.dot_general(wrt_ref[0:rr, :], x1_lo, nt, preferred_element_type=F32)
        logits = both[0:rr] + both[rr:2 * rr] + cross + jnp.concatenate([brt_ref[...]] * (pt // LANES), axis=1)

        g_logits = jnp.where(row < N_GROUPS, logits[0:SUBLANES], NEG_INF)
        g_top, g_idx = first_argmax(g_logits)
        g_p = 1.0 / jnp.sum(jnp.exp(g_logits - g_top), axis=0, keepdims=True)

        e_logits = logits[SUBLANES:2 * SUBLANES]
        for grp in range(1, N_GROUPS):
            e_logits = jnp.where(g_idx == grp, logits[(1 + grp) * SUBLANES:(2 + grp) * SUBLANES], e_logits)
        v1, a1 = first_argmax(e_logits)
        v2, a2 = first_argmax(jnp.where(row == a1, NEG_INF, e_logits))
        r = jnp.exp(v2 - v1)
        w1 = g_p / (1.0 + r)
        w2 = g_p * r / (1.0 + r)

        lo = jnp.minimum(a1, a2)
        hi = jnp.maximum(a1, a2)
        w_lo = jnp.where(a1 < a2, w1, w2)
        w_hi = jnp.where(a1 < a2, w2, w1)
        pair_idx = (EXPERTS_PER_GROUP - 1) * lo - lo * (lo - 1.0) * 0.5 + (hi - lo - 1.0)
        cls = g_idx * PAIRS_PER_GROUP + pair_idx

        meta_t = jnp.where(row == 0.0, cls, jnp.where(row == 1.0, w_lo, jnp.where(row == 2.0, w_hi, 0.0)))
        meta = jnp.concatenate([meta_t, jnp.zeros((LANES - SUBLANES, pt), F32)], axis=0).T
        meta_ref[r0:r0 + pt, :] = meta
        counts = counts + jnp.sum(jnp.where(lane == meta[:, 0:1], 1.0, 0.0), axis=0, keepdims=True)

    cnt_ref[0:1, :] += counts


def _outproj(x2d, ml2d, att2d, w_out_b, ln_w, ln_b, w_router, b_router, alpha, tm):
    t, d = x2d.shape
    kern = functools.partial(_outproj_kernel, alpha=alpha)

    def const_spec(shape):
        return pl.BlockSpec(shape, lambda i: (0,) * len(shape))

    return pl.pallas_call(
        kern,
        grid=(t // tm,),
        in_specs=[pl.BlockSpec((tm, d), lambda i: (i, 0)),
                  pl.BlockSpec((tm, ML_WIDTH), lambda i: (i, 0)),
                  pl.BlockSpec((tm, ATT_WIDTH), lambda i: (i, 0)),
                  const_spec((ML_WIDTH + ATT_WIDTH, d)), const_spec((1, d)), const_spec((1, d)),
                  const_spec((2 * ROUTER_ROWS, d)), const_spec((ROUTER_ROWS, LANES))],
        out_specs=[pl.BlockSpec((tm * (d // LANES), LANES), lambda i: (i, 0)),
                   pl.BlockSpec((tm, LANES), lambda i: (i, 0)), const_spec((SUBLANES, LANES))],
        out_shape=[jax.ShapeDtypeStruct((t * (d // LANES), LANES), F32),
                   jax.ShapeDtypeStruct((t, LANES), F32),
                   jax.ShapeDtypeStruct((SUBLANES, LANES), F32)],
        compiler_params=pltpu.CompilerParams(dimension_semantics=("arbitrary",),
                                             vmem_limit_bytes=VMEM_LIMIT),
        name="outproj",
    )(x2d, ml2d, att2d, w_out_b, ln_w, ln_b, w_router, b_router)


def _class_expert_table():
    tab = np.zeros((SUBLANES, LANES), np.float32)
    for g in range(N_GROUPS):
        idx = 0
        for lo in range(EXPERTS_PER_GROUP):
            for hi in range(lo + 1, EXPERTS_PER_GROUP):
                c = g * PAIRS_PER_GROUP + idx
                tab[0, c] = g * EXPERTS_PER_GROUP + lo
                tab[1, c] = g * EXPERTS_PER_GROUP + hi
                idx += 1
    return tab


def _rank_kernel(meta_ref, cnt_ref, tab_ref, pos_ref, tile_ref, base_ref, run_ref, *, tb, n_tiles_pad):
    step = pl.program_id(0)
    lane8 = _iota((SUBLANES, LANES), 1)

    @pl.when(step == 0)
    def _():
        cnt = jnp.broadcast_to(cnt_ref[0:1, :], (SUBLANES, LANES))
        tiles = jnp.floor((cnt + (MOE_TILE - 1.0)) * (1.0 / MOE_TILE))
        cum = tiles
        sh = 1
        while sh < LANES:
            cum = cum + jnp.where(lane8 >= sh, pltpu.roll(cum, sh, 1), 0.0)
            sh *= 2
        excl = cum - tiles
        base_ref[...] = excl * MOE_TILE
        run_ref[...] = jnp.zeros_like(run_ref)

        ti = _iota((n_tiles_pad, LANES), 0).astype(F32)
        lane = _iota((n_tiles_pad, LANES), 1)
        done = jnp.where(lane < N_CLASSES, jnp.where(cum[0:1, :] <= ti, 1.0, 0.0), 0.0)
        t_cls = jnp.sum(done, axis=-1, keepdims=True)
        sel = jnp.where(lane.astype(F32) == t_cls, 1.0, 0.0)
        cnt_i = jnp.sum(sel * cnt[0:1, :], axis=-1, keepdims=True)
        first_i = jnp.sum(sel * excl[0:1, :], axis=-1, keepdims=True)
        rows_i = jnp.clip(cnt_i - MOE_TILE * (ti[:, 0:1] - first_i), 0.0, float(MOE_TILE))
        e_lo = jnp.sum(sel * tab_ref[0:1, :], axis=-1, keepdims=True)
        e_hi = jnp.sum(sel * tab_ref[1:2, :], axis=-1, keepdims=True)
        n_tiles = jnp.sum(jnp.where(lane < N_CLASSES, jnp.broadcast_to(tiles[0:1, :], (n_tiles_pad, LANES)), 0.0),
                          axis=-1, keepdims=True)
        info = jnp.where(lane == 0, rows_i,
                         jnp.where(lane == 1, e_lo, jnp.where(lane == 2, e_hi, jnp.where(lane == 3, n_tiles, 0.0))))
        tile_ref[...] = info.astype(I32)

    cls = meta_ref[:, 0:1]
    lane = _iota((tb, LANES), 1).astype(F32)
    onehot = jnp.where(lane == cls, 1.0, 0.0)
    strict_lower = jnp.where(_iota((tb, tb), 1) < _iota((tb, tb), 0), 1.0, 0.0).astype(BF16)
    before = _dot(strict_lower, onehot.astype(BF16))
    slot = jnp.sum(onehot * (before + run_ref[0:1, :] + base_ref[0:1, :]), axis=-1, keepdims=True)
    run_ref[...] = run_ref[...] + jnp.sum(onehot, axis=0, keepdims=True)
    slot_t = jnp.broadcast_to(slot, (tb, LANES)).T
    pos_ref[...] = slot_t[0:SUBLANES, :].astype(I32)


def _rank(meta, counts, tb, n_tiles_pad):
    t = meta.shape[0]
    kern = functools.partial(_rank_kernel, tb=tb, n_tiles_pad=n_tiles_pad)
    tab = jnp.asarray(_class_expert_table())
    return pl.pallas_call(
        kern,
        grid=(t // tb,),
        in_specs=[pl.BlockSpec((tb, LANES), lambda i: (i, 0)),
                  pl.BlockSpec((SUBLANES, LANES), lambda i: (0, 0)),
                  pl.BlockSpec((SUBLANES, LANES), lambda i: (0, 0))],
        out_specs=[pl.BlockSpec((SUBLANES, tb), lambda i: (0, i)),
                   pl.BlockSpec((n_tiles_pad, LANES), lambda i: (0, 0))],
        out_shape=[jax.ShapeDtypeStruct((SUBLANES, t), I32),
                   jax.ShapeDtypeStruct((n_tiles_pad, LANES), I32)],
        scratch_shapes=[pltpu.VMEM((SUBLANES, LANES), F32), pltpu.VMEM((SUBLANES, LANES), F32)],
        compiler_params=pltpu.CompilerParams(dimension_semantics=("arbitrary",),
                                             vmem_limit_bytes=VMEM_LIMIT),
        name="rank",
    )(meta, counts, tab)


DMA_UNROLL = 8


def _pad_chunks():
    size, out = 1, []
    while size <= MOE_TILE:
        out.append(size)
        size *= 2
    return out[::-1]


def _dispatch_kernel(pos_ref, rows_ref, src_ref, dst_ref, zero_ref, sems, *, tb, rpt, n_tiles):
    step = pl.program_id(0)
    base = step * tb

    def pad_copies(act):
        def one_tile(i, carry):
            valid = rows_ref[i]
            pad = MOE_TILE - valid
            off = i * MOE_TILE + valid
            for size in _pad_chunks():
                hit = (pad & size) != 0

                @pl.when(hit)
                def _():
                    act(pltpu.make_async_copy(zero_ref.at[pl.ds(0, size * rpt)],
                                              dst_ref.at[pl.ds(pl.multiple_of(off * rpt, rpt), size * rpt)],
                                              sems.at[1]))
                off = off + jnp.where(hit, size, 0)
            return carry
        lax.fori_loop(0, n_tiles, one_tile, 0)

    @pl.when(step == 0)
    def _():
        zero_ref[...] = jnp.zeros_like(zero_ref)
        pad_copies(lambda cp: cp.start())

    def issue(grp, carry):
        for u in range(DMA_UNROLL):
            j = grp * DMA_UNROLL + u
            src = src_ref.at[pl.ds(pl.multiple_of(j * rpt, rpt), rpt)]
            dst = dst_ref.at[pl.ds(pl.multiple_of(pos_ref[base + j] * rpt, rpt), rpt)]
            pltpu.make_async_copy(src, dst, sems.at[0]).start(priority=u % 2)
        return carry

    lax.fori_loop(0, tb // DMA_UNROLL, issue, 0)
    pltpu.make_async_copy(src_ref, dst_ref.at[pl.ds(0, tb * rpt)], sems.at[0]).wait()

    @pl.when(step == pl.num_programs(0) - 1)
    def _():
        pad_copies(lambda cp: cp.wait())


def _dispatch(pos, tile_rows, src, n_slots, rpt, tb):
    t = pos.shape[0]
    n_tiles = tile_rows.shape[0]
    kern = functools.partial(_dispatch_kernel, tb=tb, rpt=rpt, n_tiles=n_tiles)
    pad_rows = _pad_chunks()[0] * rpt
    return pl.pallas_call(
        kern,
        grid_spec=pltpu.PrefetchScalarGridSpec(
            num_scalar_prefetch=2,
            grid=(t // tb,),
            in_specs=[pl.BlockSpec((tb * rpt, LANES), lambda i, pos_ref, rows_ref: (i, 0))],
            out_specs=pl.BlockSpec(memory_space=pl.ANY),
            scratch_shapes=[pltpu.VMEM((pad_rows, LANES), src.dtype), pltpu.SemaphoreType.DMA((2,))],
        ),
        out_shape=jax.ShapeDtypeStruct((n_slots * rpt, LANES), src.dtype),
        compiler_params=pltpu.CompilerParams(dimension_semantics=("arbitrary",),
                                             has_side_effects=True, vmem_limit_bytes=VMEM_LIMIT),
        name="dispatch",
    )(pos, tile_rows, src)


def _collect_kernel(pos_ref, ys_ref, x1t_ref, meta_ref, lnw_ref, lnb_ref, out_ref, buf_ref, sems,
                    *, tb, alpha, d):
    segs = d // LANES
    rpt = 2 * segs
    step = pl.program_id(0)
    n_steps = pl.num_programs(0)

    def gather(blk, slot):
        base = blk * tb

        def issue(grp, carry):
            for u in range(DMA_UNROLL):
                j = grp * DMA_UNROLL + u
                src = ys_ref.at[pl.ds(pos_ref[base + j], 1)]
                dst = buf_ref.at[slot, pl.ds(j, 1)]
                pltpu.make_async_copy(src, dst, sems.at[slot]).start(priority=u % 2)
            return carry

        lax.fori_loop(0, tb // DMA_UNROLL, issue, 0)

    @pl.when(step == 0)
    def _():
        gather(0, 0)

    @pl.when(step + 1 < n_steps)
    def _():
        gather(step + 1, (step + 1) % 2)

    slot = step % 2
    pltpu.make_async_copy(ys_ref.at[pl.ds(0, tb)], buf_ref.at[slot], sems.at[slot]).wait()

    x1 = _load_token_tiles(x1t_ref, tb, segs)
    y_lo = buf_ref[slot, :, 0:d]
    y_hi = buf_ref[slot, :, d:2 * d]
    meta = meta_ref[...]
    z = alpha * x1 + meta[:, 1:2] * y_lo + meta[:, 2:3] * y_hi
    out_ref[...] = _layer_norm(z, lnw_ref[...], lnb_ref[...])


def _collect(pos, ys, x1t, meta, ln_w, ln_b, alpha, d, tb):
    t = pos.shape[0]
    segs = d // LANES
    kern = functools.partial(_collect_kernel, tb=tb, alpha=alpha, d=d)
    return pl.pallas_call(
        kern,
        grid_spec=pltpu.PrefetchScalarGridSpec(
            num_scalar_prefetch=1,
            grid=(t // tb,),
            in_specs=[pl.BlockSpec(memory_space=pl.ANY),
                      pl.BlockSpec((tb * segs, LANES), lambda i, pos_ref: (i, 0)),
                      pl.BlockSpec((tb, LANES), lambda i, pos_ref: (i, 0)),
                      pl.BlockSpec((1, d), lambda i, pos_ref: (0, 0)),
                      pl.BlockSpec((1, d), lambda i, pos_ref: (0, 0))],
            out_specs=pl.BlockSpec((tb, d), lambda i, pos_ref: (i, 0)),
            scratch_shapes=[pltpu.VMEM((2, tb, 2 * d), F32), pltpu.SemaphoreType.DMA((2,))],
        ),
        out_shape=jax.ShapeDtypeStruct((t, d), F32),
        compiler_params=pltpu.CompilerParams(dimension_semantics=("arbitrary",),
                                             vmem_limit_bytes=VMEM_LIMIT),
        name="collect",
    )(pos, ys, x1t, meta, ln_w, ln_b)


MOE_TILES_PER_STEP = 1


def _moe_kernel(rows_ref, elo_ref, ehi_ref, nt_ref, xs_ref, *refs, d):
    *w_refs, ys_ref = refs
    i = pl.program_id(0)
    segs = d // LANES

    @pl.when(rows_ref[i * MOE_TILES_PER_STEP] > 0)
    def _():
        for k in range(MOE_TILES_PER_STEP):
            wgu_lo, wd_lo, wgu_hi, wd_hi = w_refs[4 * k:4 * k + 4]
            xs_k = xs_ref.at[pl.ds(k * MOE_TILE * segs, MOE_TILE * segs)]
            xb = _load_token_tiles(xs_k, MOE_TILE, segs).astype(BF16)

            def expert(wgu_ref, wd_ref):
                de = wd_ref.shape[0]
                gate_up = _dot(xb, wgu_ref[...])
                gate = gate_up[:, 0:de]
                hidden = gate * _sigmoid(gate) * gate_up[:, de:]
                return _dot(hidden.astype(BF16), wd_ref[...])

            ys_ref[k * MOE_TILE:(k + 1) * MOE_TILE, 0:d] = expert(wgu_lo, wd_lo)
            ys_ref[k * MOE_TILE:(k + 1) * MOE_TILE, d:2 * d] = expert(wgu_hi, wd_hi)

    @pl.when(rows_ref[i * MOE_TILES_PER_STEP] <= 0)
    def _():
        ys_ref[...] = jnp.zeros_like(ys_ref)


def _moe(info, xs, wgu, wd, d, n_tiles_pad):
    de = wd.shape[1]
    segs = d // LANES
    tps = MOE_TILES_PER_STEP
    kern = functools.partial(_moe_kernel, d=d)

    def last_live(tile, nt_ref):
        return jnp.minimum(tile, jnp.maximum(nt_ref[0] - 1, 0))

    def up_spec(k, which):
        return pl.BlockSpec((None, d, 2 * de),
                            lambda i, rows, elo, ehi, nt: ((elo, ehi)[which][last_live(i * tps + k, nt)], 0, 0))

    def down_spec(k, which):
        return pl.BlockSpec((None, de, d),
                            lambda i, rows, elo, ehi, nt: ((elo, ehi)[which][last_live(i * tps + k, nt)], 0, 0))

    w_specs = []
    for k in range(tps):
        w_specs += [up_spec(k, 0), down_spec(k, 0), up_spec(k, 1), down_spec(k, 1)]
    rows, elo, ehi, nt = info[:, 0], info[:, 1], info[:, 2], info[0:1, 3]
    return pl.pallas_call(
        kern,
        grid_spec=pltpu.PrefetchScalarGridSpec(
            num_scalar_prefetch=4,
            grid=(n_tiles_pad // tps,),
            in_specs=[pl.BlockSpec((tps * MOE_TILE * segs, LANES),
                                   lambda i, rows, elo, ehi, nt: (last_live(i * tps, nt) // tps, 0))] + w_specs,
            out_specs=pl.BlockSpec((tps * MOE_TILE, 2 * d), lambda i, rows, elo, ehi, nt: (i, 0)),
        ),
        out_shape=jax.ShapeDtypeStruct((n_tiles_pad * MOE_TILE, 2 * d), F32),
        compiler_params=pltpu.CompilerParams(dimension_semantics=("arbitrary",),
                                             vmem_limit_bytes=VMEM_LIMIT),
        name="moe",
    )(rows, elo, ehi, nt, xs, *([wgu, wd, wgu, wd] * tps))


def _pick_block(n, target):
    blk = min(n, target)
    while n % blk:
        blk //= 2
    return blk


def kernel(x, w_in, conv_w, conv_b, mlstm_gate_bias, mlstm_norm_w, attn_sinks, w_out, ln1_w, ln1_b,
           w_group_router, b_group_router, w_expert_router, b_expert_router,
           w_exp_gate, w_exp_up, w_exp_down, ln2_w, ln2_b):
    b, s, d = x.shape
    t = b * s
    depth = w_in.shape[0]
    alpha = (2.0 * depth) ** 0.25
    assert s % ML_CHUNK == 0 and s % WINDOW == 0 and d % LANES == 0

    tm = _pick_block(s, 1024)
    tq = _pick_block(s, 8 * ML_CHUNK)
    tb_rank = _pick_block(t, 512)
    tb_dma = _pick_block(t, 2048)
    tb_col = _pick_block(t, 512)
    n_tiles_pad = -(-(t // MOE_TILE + N_CLASSES) // SUBLANES) * SUBLANES
    cos_t, sin_t = _rope_tables(s)

    for l in range(depth):
        x2d = x.reshape(t, d)
        q, k, v, og, aq, ak, av, g = _inproj(x2d, _pack_w_in(w_in[l]), conv_w[l], conv_b[l][None, :], tm, s // tm)
        bias_pad = jnp.zeros((LANES - ML_HEADS,), F32)
        gate_bias_row = jnp.concatenate(
            [mlstm_gate_bias[l, 0], bias_pad, mlstm_gate_bias[l, 1], bias_pad])[None, :]
        ml = _mlstm(q.reshape(b, s, -1), k.reshape(b, s, -1), v.reshape(b, s, -1), og.reshape(b, s, -1),
                    g.reshape(b, s, -1), gate_bias_row, mlstm_norm_w[l][None, :], tq)
        att = _swa(aq.reshape(b, s, -1), ak.reshape(b, s, -1), av.reshape(b, s, -1), cos_t, sin_t, attn_sinks[l],
                   _pick_block(s // WINDOW, 4))

        w_router, b_router = _pack_router(w_group_router[l], b_group_router[l],
                                          w_expert_router[l], b_expert_router[l])
        x1t, meta, counts = _outproj(x2d, ml.reshape(t, -1), att.reshape(t, -1), w_out[l].astype(BF16),
                                     ln1_w[l][None, :], ln1_b[l][None, :], w_router, b_router, alpha, tm)

        pos2d, info = _rank(meta, counts, tb_rank, n_tiles_pad)
        pos = pos2d[0]
        xs = _dispatch(pos, info[:, 0], x1t, n_tiles_pad * MOE_TILE, d // LANES, tb_dma)
        w_gate_up = jnp.concatenate([w_exp_gate[l].astype(BF16), w_exp_up[l].astype(BF16)], axis=-1)
        ys = _moe(info, xs, w_gate_up, w_exp_down[l].astype(BF16), d, n_tiles_pad)
        out = _collect(pos, ys, x1t, meta, ln2_w[l][None, :], ln2_b[l][None, :], alpha, d, tb_col)
        x = out.reshape(b, s, d)
    return x
```

```python
import functools
import math

import numpy as np
import jax
import jax.numpy as jnp
from jax import lax
from jax.experimental import pallas as pl
from jax.experimental.pallas import tpu as pltpu

F32 = jnp.float32
BF16 = jnp.bfloat16
I32 = jnp.int32

ML_HEADS = 4
ML_HEAD_DIM = 128
ML_WIDTH = ML_HEADS * ML_HEAD_DIM
ML_CHUNK = 128
CONV_WIDTH = 4
ATT_Q_HEADS = 8
ATT_KV_HEADS = 2
ATT_HEAD_DIM = 64
ATT_WIDTH = ATT_Q_HEADS * ATT_HEAD_DIM
ATT_KV_WIDTH = ATT_KV_HEADS * ATT_HEAD_DIM
WINDOW = 128
ROPE_THETA = 10000.0
N_GROUPS = 4
EXPERTS_PER_GROUP = 8
N_EXPERTS = N_GROUPS * EXPERTS_PER_GROUP
PAIRS_PER_GROUP = EXPERTS_PER_GROUP * (EXPERTS_PER_GROUP - 1) // 2
N_CLASSES = N_GROUPS * PAIRS_PER_GROUP
LN_EPS = 1e-5

LANES = 128
SUBLANES = 8
MOE_TILE = 320
VMEM_LIMIT = 56 * 1024 * 1024

NEG_INF = float("-inf")


def _sigmoid(x):
    return 1.0 / (1.0 + jnp.exp(-x))


def _log_sigmoid(x):
    return jnp.minimum(x, 0.0) - jnp.log(1.0 + jnp.exp(-jnp.abs(x)))


def _iota(shape, dim):
    return lax.broadcasted_iota(I32, shape, dim)


def _dot(a, b):
    return jnp.dot(a, b, preferred_element_type=F32)


def _dot_exact(a, b):
    return jnp.dot(a, b, preferred_element_type=F32, precision=lax.Precision.HIGHEST)


def _layer_norm(z, w, b):
    mu = jnp.mean(z, axis=-1, keepdims=True)
    zc = z - mu
    var = jnp.mean(zc * zc, axis=-1, keepdims=True)
    return zc * lax.rsqrt(var + LN_EPS) * w + b


C_QK = 0
C_V = C_QK + 2 * ML_WIDTH
C_O = C_V + ML_WIDTH
C_AQ = C_O + ML_WIDTH
C_AK = C_AQ + ATT_WIDTH
C_AV = C_AK + ATT_KV_HEADS * LANES
C_G = C_AV + ATT_KV_HEADS * LANES
C_END = C_G + 2 * LANES


def _pack_w_in(w_in):
    sizes = (2 * ML_WIDTH, ML_WIDTH, ML_WIDTH, ML_HEADS, ML_HEADS, ATT_WIDTH, ATT_KV_WIDTH, ATT_KV_WIDTH)
    splits = np.cumsum(sizes)[:-1].tolist()
    w_qk, w_v, w_o, w_i, w_f, w_aq, w_ak, w_av = jnp.split(w_in, splits, axis=-1)
    half = ATT_HEAD_DIM // 2

    def head(w, h):
        return w[:, h * ATT_HEAD_DIM:(h + 1) * ATT_HEAD_DIM]

    def q_tile(a, b):
        return [a[:, :half], b[:, :half], a[:, half:], b[:, half:]]

    q_cols = [t for p in range(ATT_Q_HEADS // 2) for t in q_tile(head(w_aq, 2 * p), head(w_aq, 2 * p + 1))]
    k_cols = [t for h in range(ATT_KV_HEADS) for t in q_tile(head(w_ak, h), head(w_ak, h))]
    v_cols = [t for h in range(ATT_KV_HEADS) for t in (head(w_av, h), head(w_av, h))]
    lane_pad = jnp.zeros((w_in.shape[0], LANES - ML_HEADS), w_in.dtype)
    packed = jnp.concatenate([w_qk, w_v, w_o] + q_cols + k_cols + v_cols + [w_i, lane_pad, w_f, lane_pad], axis=-1)
    return packed.astype(BF16)


def _inproj_kernel(x_ref, w_ref, cw_ref, cb_ref, q_ref, k_ref, v_ref, og_ref, aq_ref, ak_ref, av_ref, g_ref,
                   *scratch, blocks_per_seq):
    *ext_refs, xb_ref = scratch
    tm = x_ref.shape[0]
    halo = SUBLANES
    cs = ext_refs[0].shape[1]
    xb_ref[...] = x_ref[...].astype(BF16)

    @pl.when(pl.program_id(0) % blocks_per_seq == 0)
    def _():
        for ext_ref in ext_refs:
            ext_ref[0:halo, :] = jnp.zeros((halo, cs), F32)

    scale = ML_HEAD_DIM ** -0.5

    def mm(lo, hi):
        return _dot(xb_ref[...], w_ref[:, lo:hi])

    def conv_slice(idx):
        ext_ref = ext_refs[idx]
        c0 = idx * cs
        is_q = c0 < ML_WIDTH
        dst_ref, off = (q_ref, c0) if is_q else (k_ref, c0 - ML_WIDTH)
        rt = ML_CHUNK
        for r0 in range(0, tm, rt):
            conv = cb_ref[:, c0:c0 + cs]
            for j in range(CONV_WIDTH):
                start = halo + r0 - (CONV_WIDTH - 1) + j
                conv = conv + cw_ref[j:j + 1, c0:c0 + cs] * ext_ref[start:start + rt, :]
            act = conv * _sigmoid(conv)
            dst_ref[r0:r0 + rt, off:off + cs] = (act if is_q else act * scale).astype(BF16)
        ext_ref[0:halo, :] = ext_ref[tm:tm + halo, :]

    for idx in range(len(ext_refs)):
        ext_refs[idx][halo:halo + tm, :] = mm(C_QK + idx * cs, C_QK + (idx + 1) * cs)
        if idx > 0:
            conv_slice(idx - 1)
    half_v = ML_WIDTH // 2
    v_ref[:, 0:half_v] = mm(C_V, C_V + half_v).astype(BF16)
    conv_slice(len(ext_refs) - 1)
    v_ref[:, half_v:] = mm(C_V + half_v, C_O).astype(BF16)
    og_ref[...] = _sigmoid(mm(C_O, C_AQ)).astype(BF16)
    aq_ref[...] = mm(C_AQ, C_AK)
    ak_ref[...] = mm(C_AK, C_AV)
    av_ref[...] = mm(C_AV, C_G).astype(BF16)
    g_ref[...] = mm(C_G, C_END)


def _inproj(x2d, w_packed, conv_w, conv_b, tm, blocks_per_seq):
    t, d = x2d.shape
    widths = (ML_WIDTH, ML_WIDTH, ML_WIDTH, ML_WIDTH, C_AK - C_AQ, C_AV - C_AK, C_G - C_AV, C_END - C_G)
    dtypes = (BF16, BF16, BF16, BF16, F32, F32, BF16, F32)
    kern = functools.partial(_inproj_kernel, blocks_per_seq=blocks_per_seq)
    return pl.pallas_call(
        kern,
        grid=(t // tm,),
        in_specs=[pl.BlockSpec((tm, d), lambda i: (i, 0)),
                  pl.BlockSpec((d, C_END), lambda i: (0, 0)),
                  pl.BlockSpec((CONV_WIDTH, 2 * ML_WIDTH), lambda i: (0, 0)),
                  pl.BlockSpec((1, 2 * ML_WIDTH), lambda i: (0, 0))],
        out_specs=[pl.BlockSpec((tm, w), lambda i: (i, 0)) for w in widths],
        out_shape=[jax.ShapeDtypeStruct((t, w), dt) for w, dt in zip(widths, dtypes)],
        scratch_shapes=[pltpu.VMEM((tm + SUBLANES, 2 * LANES), F32)] * (2 * ML_WIDTH // (2 * LANES))
        + [pltpu.VMEM((tm, d), BF16)],
        compiler_params=pltpu.CompilerParams(dimension_semantics=("arbitrary",),
                                             vmem_limit_bytes=VMEM_LIMIT),
        name="inproj",
    )(x2d, w_packed, conv_w, conv_b)


def _time_scan(x, combine, identity):
    row = _iota(x.shape, 0)
    sh = 1
    while sh < x.shape[0]:
        x = combine(x, jnp.where(row >= sh, pltpu.roll(x, sh, 0), identity))
        sh *= 2
    return x


def _mlstm_kernel(q_ref, k_ref, v_ref, og_ref, g_ref, gb_ref, nw_ref, out_ref, ct_ref, m_ref, *, tq):
    s_idx = pl.program_id(1)
    L = ML_CHUNK
    D = ML_HEAD_DIM
    H = ML_HEADS
    heads = range(H)

    @pl.when(s_idx == 0)
    def _():
        ct_ref[...] = jnp.zeros_like(ct_ref)
        m_ref[...] = jnp.zeros_like(m_ref)

    causal = _iota((L, L), 1) <= _iota((L, L), 0)
    ones_blk = jnp.ones((L, D), BF16)
    mean_blk = jnp.full((D, D), 1.0 / D, BF16)
    m_prev = m_ref[0:1, :]
    head_lanes = _iota((L, LANES), 1) < H
    tile_of_lane = jnp.right_shift(_iota((LANES, H * L), 1), L.bit_length() - 1)
    spread = jnp.where(_iota((LANES, H * L), 0) == tile_of_lane, 1.0, 0.0).astype(BF16)

    def spread_heads(x):
        x = jnp.where(head_lanes, x, 0.0)
        hi = x.astype(BF16)
        lo = (x - hi.astype(F32)).astype(BF16)
        return _dot(hi, spread) + _dot(lo, spread)

    for c in range(tq // L):
        r0 = c * L
        gi = g_ref[r0:r0 + L, 0:LANES] + gb_ref[:, 0:LANES]
        gf = g_ref[r0:r0 + L, LANES:2 * LANES] + gb_ref[:, LANES:2 * LANES]
        b_cum = _time_scan(_log_sigmoid(gf), jnp.add, 0.0)
        r = gi - b_cum
        g = jnp.maximum(m_prev, _time_scan(r, jnp.maximum, NEG_INF))
        g_rep = spread_heads(g)
        b_rep = spread_heads(b_cum)
        b_last = b_cum[L - 1:L, :]
        m_new = jnp.maximum(b_last + m_prev, jnp.max(b_last + r, axis=0, keepdims=True))
        decay = jnp.exp(b_last + m_prev - m_new)
        shift = b_last - m_new
        r_t = r.T

        q_b = [q_ref[r0:r0 + L, h * D:(h + 1) * D] for h in heads]
        kt_b = [k_ref[r0:r0 + L, h * D:(h + 1) * D].T for h in heads]
        v_aug = [jnp.concatenate([v_ref[r0:r0 + L, h * D:(h + 1) * D], ones_blk], axis=-1) for h in heads]
        g_col = [g_rep[:, h * L:(h + 1) * L] for h in heads]
        w_intra = [jnp.exp(jnp.where(causal, r_t[h:h + 1, :] - g_col[h], NEG_INF)) for h in heads]
        s_b = [(_dot(q_b[h], kt_b[h]) * w_intra[h]).astype(BF16) for h in heads]
        ct = [ct_ref[h] for h in heads]
        inter = [_dot(q_b[h], ct[h].astype(BF16)) for h in heads]
        intra = [_dot(s_b[h], v_aug[h]) for h in heads]
        for h in heads:
            wi_col = jnp.exp(m_prev[:, h:h + 1] - g_col[h])
            clamp = jnp.exp(-(b_rep[:, h * L:(h + 1) * L] + g_col[h]))
            num = wi_col * inter[h][:, 0:D] + intra[h][:, 0:D]
            den = wi_col * inter[h][:, D:] + intra[h][:, D:]
            hh = num / jnp.maximum(jnp.abs(den), clamp)
            mu = _dot(hh.astype(BF16), mean_blk)
            hc = hh - mu
            var = _dot((hc * hc).astype(BF16), mean_blk)
            hn = hc * lax.rsqrt(var + LN_EPS) * nw_ref[:, h * D:(h + 1) * D]
            gate_o = og_ref[r0:r0 + L, h * D:(h + 1) * D].astype(F32)
            out_ref[r0:r0 + L, h * D:(h + 1) * D] = (gate_o * hn).astype(out_ref.dtype)
        for h in heads:
            w_row = jnp.exp(r_t[h:h + 1, :] + shift[:, h:h + 1])
            ktw = (kt_b[h].astype(F32) * w_row).astype(BF16)
            ct_ref[h] = decay[:, h:h + 1] * ct[h] + _dot(ktw, v_aug[h])
        m_prev = m_new

    m_ref[...] = jnp.broadcast_to(m_prev, m_ref.shape)


def _mlstm(q, k, v, og, g, gate_bias_row, norm_w_row, tq):
    b, s, _ = q.shape
    kern = functools.partial(_mlstm_kernel, tq=tq)

    def seq_spec(width):
        return pl.BlockSpec((None, tq, width), lambda bi, si: (bi, si, 0))

    def const_spec(shape):
        return pl.BlockSpec(shape, lambda bi, si: (0,) * len(shape))

    return pl.pallas_call(
        kern,
        grid=(b, s // tq),
        in_specs=[seq_spec(ML_WIDTH), seq_spec(ML_WIDTH), seq_spec(ML_WIDTH), seq_spec(ML_WIDTH),
                  seq_spec(2 * LANES), const_spec((1, 2 * LANES)), const_spec((1, ML_WIDTH))],
        out_specs=seq_spec(ML_WIDTH),
        out_shape=jax.ShapeDtypeStruct((b, s, ML_WIDTH), BF16),
        scratch_shapes=[pltpu.VMEM((ML_HEADS, ML_HEAD_DIM, 2 * ML_HEAD_DIM), F32),
                        pltpu.VMEM((SUBLANES, LANES), F32)],
        compiler_params=pltpu.CompilerParams(dimension_semantics=("arbitrary", "arbitrary"),
                                             vmem_limit_bytes=VMEM_LIMIT),
        name="mlstm",
    )(q, k, v, og, g, gate_bias_row, norm_w_row)


def _rope_tables(seq_len):
    half = ATT_HEAD_DIM // 2
    inv_freq = ROPE_THETA ** (-jnp.arange(half, dtype=F32) / half)
    ang = jnp.arange(seq_len, dtype=F32)[:, None] * inv_freq[None, :]
    cos = jnp.cos(ang)
    sin = jnp.sin(ang)
    cos_t = jnp.concatenate([cos, cos, cos, cos], axis=-1)
    sin_t = jnp.concatenate([-sin, -sin, sin, sin], axis=-1)
    return cos_t, sin_t


def _swa_kernel(sink_ref, aq_ref, ak_ref, av_ref, cos_ref, sin_ref, out_ref, kprev_ref, vprev_ref, *, nsub):
    step = pl.program_id(1)
    Lb = WINDOW
    half = ATT_HEAD_DIM // 2
    pairs = ATT_Q_HEADS // 2
    pairs_per_kv = pairs // ATT_KV_HEADS

    @pl.when(step == 0)
    def _():
        kprev_ref[...] = jnp.zeros_like(kprev_ref)
        vprev_ref[...] = jnp.zeros_like(vprev_ref)

    def rope(x, cos, sin):
        tiles = []
        for c in range(x.shape[-1] // LANES):
            xt = x[:, c * LANES:(c + 1) * LANES]
            tiles.append(xt * cos + pltpu.roll(xt, LANES // 2, 1) * sin)
        return jnp.concatenate(tiles, axis=-1)

    ql = _iota((Lb, 2 * Lb), 0)
    kj = _iota((Lb, 2 * Lb), 1)
    diff = Lb + ql - kj
    lane = _iota((Lb, LANES), 1)
    low_half = lane < ATT_HEAD_DIM
    first_head = (lane & half) == 0
    ones_blk = jnp.ones((2 * Lb, LANES), BF16)

    k_prev = kprev_ref
    v_prev = vprev_ref

    def block(j, carry):
        rows_j = pl.ds(pl.multiple_of(j * Lb, Lb), Lb)
        cos = cos_ref[rows_j, :]
        sin = sin_ref[rows_j, :]
        q = rope(aq_ref[rows_j, :], cos, sin) * (ATT_HEAD_DIM ** -0.5)
        k_cur = rope(ak_ref[rows_j, :], cos, sin).astype(BF16)
        v_cur = av_ref[rows_j, :]

        kpos = (step * nsub + j) * Lb + kj - Lb
        visible = jnp.where(diff >= 0, jnp.where(diff < WINDOW, jnp.where(kpos >= 0, 1, 0), 0), 0)
        bias = jnp.where(visible > 0, 0.0, NEG_INF).astype(F32)
        bias = jnp.concatenate([bias] * (2 * pairs_per_kv), axis=0)

        for g in range(ATT_KV_HEADS):
            kk = jnp.concatenate([k_prev[:, g * LANES:(g + 1) * LANES], k_cur[:, g * LANES:(g + 1) * LANES]], axis=0)
            vv = jnp.concatenate([v_prev[:, g * LANES:(g + 1) * LANES], v_cur[:, g * LANES:(g + 1) * LANES]], axis=0)
            vv_aug = jnp.concatenate([vv, ones_blk], axis=-1)
            rows = []
            sinks = []
            for p in range(pairs_per_kv):
                pair = g * pairs_per_kv + p
                q2 = q[:, pair * LANES:(pair + 1) * LANES]
                rows.append(jnp.where(first_head, q2, 0.0))
                rows.append(jnp.where(first_head, 0.0, q2))
                sinks.append(jnp.full((Lb, LANES), sink_ref[2 * pair], F32))
                sinks.append(jnp.full((Lb, LANES), sink_ref[2 * pair + 1], F32))
            qs = jnp.concatenate(rows, axis=0).astype(BF16)
            sink = jnp.concatenate(sinks, axis=0)

            sc = lax.dot_general(qs, kk, (((1,), (1,)), ((), ())), preferred_element_type=F32) + bias
            m = jnp.maximum(jnp.broadcast_to(jnp.max(sc, axis=-1, keepdims=True), sink.shape), sink)
            p_un = jnp.exp(sc - jnp.concatenate([m, m], axis=-1))
            acc = _dot(p_un.astype(BF16), vv_aug)
            o = acc[:, 0:LANES] / (acc[:, LANES:] + jnp.exp(sink - m))
            for p in range(pairs_per_kv):
                pair = g * pairs_per_kv + p
                even = o[(2 * p) * Lb:(2 * p + 1) * Lb, :]
                odd = o[(2 * p + 1) * Lb:(2 * p + 2) * Lb, :]
                out_ref[rows_j, pair * LANES:(pair + 1) * LANES] = (
                    jnp.where(low_half, even, odd).astype(out_ref.dtype))
        kprev_ref[...] = k_cur
        vprev_ref[...] = v_cur
        return carry

    lax.fori_loop(0, nsub, block, 0)


def _swa(aq, ak, av, cos_t, sin_t, sinks, nsub):
    b, s, _ = aq.shape
    kvw = ATT_KV_HEADS * LANES
    rows = nsub * WINDOW

    def seq_spec(width):
        return pl.BlockSpec((None, rows, width), lambda bi, si: (bi, si, 0))

    tab_spec = pl.BlockSpec((rows, LANES), lambda bi, si: (si, 0))
    return pl.pallas_call(
        functools.partial(_swa_kernel, nsub=nsub),
        grid=(b, s // rows),
        in_specs=[pl.BlockSpec(memory_space=pltpu.SMEM),
                  seq_spec(ATT_WIDTH), seq_spec(kvw), seq_spec(kvw), tab_spec, tab_spec],
        out_specs=seq_spec(ATT_WIDTH),
        out_shape=jax.ShapeDtypeStruct((b, s, ATT_WIDTH), BF16),
        scratch_shapes=[pltpu.VMEM((WINDOW, kvw), BF16), pltpu.VMEM((WINDOW, kvw), BF16)],
        compiler_params=pltpu.CompilerParams(dimension_semantics=("arbitrary", "arbitrary"),
                                             vmem_limit_bytes=VMEM_LIMIT),
        name="swa",
    )(sinks, aq, ak, av, cos_t, sin_t)


def _store_token_tiles(ref, val, row0=0, rows_per_token=None):
    n, w = val.shape
    segs = w // LANES
    rpt = rows_per_token or segs
    for j in range(segs):
        ref[pl.ds(row0 + j, n, stride=rpt), :] = val[:, j * LANES:(j + 1) * LANES]


def _load_token_tiles(ref, n, segs, row0=0, rows_per_token=None):
    rpt = rows_per_token or segs
    return jnp.concatenate([ref[pl.ds(row0 + j, n, stride=rpt), :] for j in range(segs)], axis=-1)


ROUTER_ROWS = 48


def _pack_router(w_group, b_group, w_expert, b_expert):
    d = w_group.shape[0]
    wt = jnp.zeros((ROUTER_ROWS, d), F32)
    wt = wt.at[0:N_GROUPS].set(w_group.T).at[SUBLANES:SUBLANES + N_EXPERTS].set(w_expert.T)
    bias = jnp.zeros((ROUTER_ROWS,), F32)
    bias = bias.at[0:N_GROUPS].set(b_group).at[SUBLANES:SUBLANES + N_EXPERTS].set(b_expert)
    hi = wt.astype(BF16)
    lo = (wt - hi.astype(F32)).astype(BF16)
    return jnp.concatenate([hi, lo], axis=0), jnp.broadcast_to(bias[:, None], (ROUTER_ROWS, LANES))


def _outproj_kernel(x_ref, ml_ref, att_ref, wo_ref, lnw_ref, lnb_ref, wrt_ref, brt_ref,
                    x1t_ref, meta_ref, cnt_ref, *, alpha):
    step = pl.program_id(0)
    tm, d = x_ref.shape
    segs = d // LANES
    pt = tm
    rr = wrt_ref.shape[0] // 2
    nt = (((1,), (1,)), ((), ()))
    row = _iota((SUBLANES, pt), 0).astype(F32)
    lane = _iota((pt, LANES), 1).astype(F32)

    def first_argmax(vals):
        top = jnp.max(vals, axis=0, keepdims=True)
        idx = jnp.min(jnp.where(vals == top, row, float(SUBLANES)), axis=0, keepdims=True)
        return top, idx

    @pl.when(step == 0)
    def _():
        cnt_ref[...] = jnp.zeros_like(cnt_ref)

    counts = jnp.zeros((1, LANES), F32)
    for r0 in range(0, tm, pt):
        y = (_dot(ml_ref[r0:r0 + pt, :], wo_ref[0:ML_WIDTH, :])
             + _dot(att_ref[r0:r0 + pt, :], wo_ref[ML_WIDTH:, :]))
        x1 = _layer_norm(alpha * x_ref[r0:r0 + pt, :] + y, lnw_ref[...], lnb_ref[...])
        _store_token_tiles(x1t_ref.at[pl.ds(r0 * segs, pt * segs)], x1)

        x1_hi = x1.astype(BF16)
        x1_lo = (x1 - x1_hi.astype(F32)).astype(BF16)
        both = lax.dot_general(wrt_ref[...], x1_hi, nt, preferred_element_type=F32)
        cross = lax.dot_general(wrt_ref[0:rr, :], x1_lo, nt, preferred_element_type=F32)
        logits = both[0:rr] + both[rr:2 * rr] + cross + jnp.concatenate([brt_ref[...]] * (pt // LANES), axis=1)

        g_logits = jnp.where(row < N_GROUPS, logits[0:SUBLANES], NEG_INF)
        g_top, g_idx = first_argmax(g_logits)
        g_p = 1.0 / jnp.sum(jnp.exp(g_logits - g_top), axis=0, keepdims=True)

        e_logits = logits[SUBLANES:2 * SUBLANES]
        for grp in range(1, N_GROUPS):
            e_logits = jnp.where(g_idx == grp, logits[(1 + grp) * SUBLANES:(2 + grp) * SUBLANES], e_logits)
        v1, a1 = first_argmax(e_logits)
        v2, a2 = first_argmax(jnp.where(row == a1, NEG_INF, e_logits))
        r = jnp.exp(v2 - v1)
        w1 = g_p / (1.0 + r)
        w2 = g_p * r / (1.0 + r)

        lo = jnp.minimum(a1, a2)
        hi = jnp.maximum(a1, a2)
        w_lo = jnp.where(a1 < a2, w1, w2)
        w_hi = jnp.where(a1 < a2, w2, w1)
        pair_idx = (EXPERTS_PER_GROUP - 1) * lo - lo * (lo - 1.0) * 0.5 + (hi - lo - 1.0)
        cls = g_idx * PAIRS_PER_GROUP + pair_idx

        meta_t = jnp.where(row == 0.0, cls, jnp.where(row == 1.0, w_lo, jnp.where(row == 2.0, w_hi, 0.0)))
        meta = jnp.concatenate([meta_t, jnp.zeros((LANES - SUBLANES, pt), F32)], axis=0).T
        meta_ref[r0:r0 + pt, :] = meta
        counts = counts + jnp.sum(jnp.where(lane == meta[:, 0:1], 1.0, 0.0), axis=0, keepdims=True)

    cnt_ref[0:1, :] += counts


def _outproj(x2d, ml2d, att2d, w_out_b, ln_w, ln_b, w_router, b_router, alpha, tm):
    t, d = x2d.shape
    kern = functools.partial(_outproj_kernel, alpha=alpha)

    def const_spec(shape):
        return pl.BlockSpec(shape, lambda i: (0,) * len(shape))

    return pl.pallas_call(
        kern,
        grid=(t // tm,),
        in_specs=[pl.BlockSpec((tm, d), lambda i: (i, 0)),
                  pl.BlockSpec((tm, ML_WIDTH), lambda i: (i, 0)),
                  pl.BlockSpec((tm, ATT_WIDTH), lambda i: (i, 0)),
                  const_spec((ML_WIDTH + ATT_WIDTH, d)), const_spec((1, d)), const_spec((1, d)),
                  const_spec((2 * ROUTER_ROWS, d)), const_spec((ROUTER_ROWS, LANES))],
        out_specs=[pl.BlockSpec((tm * (d // LANES), LANES), lambda i: (i, 0)),
                   pl.BlockSpec((tm, LANES), lambda i: (i, 0)), const_spec((SUBLANES, LANES))],
        out_shape=[jax.ShapeDtypeStruct((t * (d // LANES), LANES), F32),
                   jax.ShapeDtypeStruct((t, LANES), F32),
                   jax.ShapeDtypeStruct((SUBLANES, LANES), F32)],
        compiler_params=pltpu.CompilerParams(dimension_semantics=("arbitrary",),
                                             vmem_limit_bytes=VMEM_LIMIT),
        name="outproj",
    )(x2d, ml2d, att2d, w_out_b, ln_w, ln_b, w_router, b_router)


def _class_expert_table():
    tab = np.zeros((SUBLANES, LANES), np.float32)
    for g in range(N_GROUPS):
        idx = 0
        for lo in range(EXPERTS_PER_GROUP):
            for hi in range(lo + 1, EXPERTS_PER_GROUP):
                c = g * PAIRS_PER_GROUP + idx
                tab[0, c] = g * EXPERTS_PER_GROUP + lo
                tab[1, c] = g * EXPERTS_PER_GROUP + hi
                idx += 1
    return tab


def _rank_kernel(meta_ref, cnt_ref, tab_ref, pos_ref, tile_ref, base_ref, run_ref, *, tb, n_tiles_pad):
    step = pl.program_id(0)
    lane8 = _iota((SUBLANES, LANES), 1)

    @pl.when(step == 0)
    def _():
        cnt = jnp.broadcast_to(cnt_ref[0:1, :], (SUBLANES, LANES))
        tiles = jnp.floor((cnt + (MOE_TILE - 1.0)) * (1.0 / MOE_TILE))
        cum = tiles
        sh = 1
        while sh < LANES:
            cum = cum + jnp.where(lane8 >= sh, pltpu.roll(cum, sh, 1), 0.0)
            sh *= 2
        excl = cum - tiles
        base_ref[...] = excl * MOE_TILE
        run_ref[...] = jnp.zeros_like(run_ref)

        ti = _iota((n_tiles_pad, LANES), 0).astype(F32)
        lane = _iota((n_tiles_pad, LANES), 1)
        done = jnp.where(lane < N_CLASSES, jnp.where(cum[0:1, :] <= ti, 1.0, 0.0), 0.0)
        t_cls = jnp.sum(done, axis=-1, keepdims=True)
        sel = jnp.where(lane.astype(F32) == t_cls, 1.0, 0.0)
        cnt_i = jnp.sum(sel * cnt[0:1, :], axis=-1, keepdims=True)
        first_i = jnp.sum(sel * excl[0:1, :], axis=-1, keepdims=True)
        rows_i = jnp.clip(cnt_i - MOE_TILE * (ti[:, 0:1] - first_i), 0.0, float(MOE_TILE))
        e_lo = jnp.sum(sel * tab_ref[0:1, :], axis=-1, keepdims=True)
        e_hi = jnp.sum(sel * tab_ref[1:2, :], axis=-1, keepdims=True)
        n_tiles = jnp.sum(jnp.where(lane < N_CLASSES, jnp.broadcast_to(tiles[0:1, :], (n_tiles_pad, LANES)), 0.0),
                          axis=-1, keepdims=True)
        info = jnp.where(lane == 0, rows_i,
                         jnp.where(lane == 1, e_lo, jnp.where(lane == 2, e_hi, jnp.where(lane == 3, n_tiles, 0.0))))
        tile_ref[...] = info.astype(I32)

    cls = meta_ref[:, 0:1]
    lane = _iota((tb, LANES), 1).astype(F32)
    onehot = jnp.where(lane == cls, 1.0, 0.0)
    strict_lower = jnp.where(_iota((tb, tb), 1) < _iota((tb, tb), 0), 1.0, 0.0).astype(BF16)
    before = _dot(strict_lower, onehot.astype(BF16))
    slot = jnp.sum(onehot * (before + run_ref[0:1, :] + base_ref[0:1, :]), axis=-1, keepdims=True)
    run_ref[...] = run_ref[...] + jnp.sum(onehot, axis=0, keepdims=True)
    slot_t = jnp.broadcast_to(slot, (tb, LANES)).T
    pos_ref[...] = slot_t[0:SUBLANES, :].astype(I32)


def _rank(meta, counts, tb, n_tiles_pad):
    t = meta.shape[0]
    kern = functools.partial(_rank_kernel, tb=tb, n_tiles_pad=n_tiles_pad)
    tab = jnp.asarray(_class_expert_table())
    return pl.pallas_call(
        kern,
        grid=(t // tb,),
        in_specs=[pl.BlockSpec((tb, LANES), lambda i: (i, 0)),
                  pl.BlockSpec((SUBLANES, LANES), lambda i: (0, 0)),
                  pl.BlockSpec((SUBLANES, LANES), lambda i: (0, 0))],
        out_specs=[pl.BlockSpec((SUBLANES, tb), lambda i: (0, i)),
                   pl.BlockSpec((n_tiles_pad, LANES), lambda i: (0, 0))],
        out_shape=[jax.ShapeDtypeStruct((SUBLANES, t), I32),
                   jax.ShapeDtypeStruct((n_tiles_pad, LANES), I32)],
        scratch_shapes=[pltpu.VMEM((SUBLANES, LANES), F32), pltpu.VMEM((SUBLANES, LANES), F32)],
        compiler_params=pltpu.CompilerParams(dimension_semantics=("arbitrary",),
                                             vmem_limit_bytes=VMEM_LIMIT),
        name="rank",
    )(meta, counts, tab)


DMA_UNROLL = 8


def _pad_chunks():
    size, out = 1, []
    while size <= MOE_TILE:
        out.append(size)
        size *= 2
    return out[::-1]


def _dispatch_kernel(pos_ref, rows_ref, src_ref, dst_ref, zero_ref, sems, *, tb, rpt, n_tiles):
    step = pl.program_id(0)
    base = step * tb

    def pad_copies(act):
        def one_tile(i, carry):
            valid = rows_ref[i]
            pad = MOE_TILE - valid
            off = i * MOE_TILE + valid
            for size in _pad_chunks():
                hit = (pad & size) != 0

                @pl.when(hit)
                def _():
                    act(pltpu.make_async_copy(zero_ref.at[pl.ds(0, size * rpt)],
                                              dst_ref.at[pl.ds(pl.multiple_of(off * rpt, rpt), size * rpt)],
                                              sems.at[1]))
                off = off + jnp.where(hit, size, 0)
            return carry
        lax.fori_loop(0, n_tiles, one_tile, 0)

    @pl.when(step == 0)
    def _():
        zero_ref[...] = jnp.zeros_like(zero_ref)
        pad_copies(lambda cp: cp.start())

    def issue(grp, carry):
        for u in range(DMA_UNROLL):
            j = grp * DMA_UNROLL + u
            src = src_ref.at[pl.ds(pl.multiple_of(j * rpt, rpt), rpt)]
            dst = dst_ref.at[pl.ds(pl.multiple_of(pos_ref[base + j] * rpt, rpt), rpt)]
            pltpu.make_async_copy(src, dst, sems.at[0]).start(priority=u % 2)
        return carry

    lax.fori_loop(0, tb // DMA_UNROLL, issue, 0)
    pltpu.make_async_copy(src_ref, dst_ref.at[pl.ds(0, tb * rpt)], sems.at[0]).wait()

    @pl.when(step == pl.num_programs(0) - 1)
    def _():
        pad_copies(lambda cp: cp.wait())


def _dispatch(pos, tile_rows, src, n_slots, rpt, tb):
    t = pos.shape[0]
    n_tiles = tile_rows.shape[0]
    kern = functools.partial(_dispatch_kernel, tb=tb, rpt=rpt, n_tiles=n_tiles)
    pad_rows = _pad_chunks()[0] * rpt
    return pl.pallas_call(
        kern,
        grid_spec=pltpu.PrefetchScalarGridSpec(
            num_scalar_prefetch=2,
            grid=(t // tb,),
            in_specs=[pl.BlockSpec((tb * rpt, LANES), lambda i, pos_ref, rows_ref: (i, 0))],
            out_specs=pl.BlockSpec(memory_space=pl.ANY),
            scratch_shapes=[pltpu.VMEM((pad_rows, LANES), src.dtype), pltpu.SemaphoreType.DMA((2,))],
        ),
        out_shape=jax.ShapeDtypeStruct((n_slots * rpt, LANES), src.dtype),
        compiler_params=pltpu.CompilerParams(dimension_semantics=("arbitrary",),
                                             has_side_effects=True, vmem_limit_bytes=VMEM_LIMIT),
        name="dispatch",
    )(pos, tile_rows, src)


def _collect_kernel(pos_ref, ys_ref, x1t_ref, meta_ref, lnw_ref, lnb_ref, out_ref, buf_ref, sems,
                    *, tb, alpha, d):
    segs = d // LANES
    rpt = 2 * segs
    step = pl.program_id(0)
    n_steps = pl.num_programs(0)

    def gather(blk, slot):
        base = blk * tb

        def issue(grp, carry):
            for u in range(DMA_UNROLL):
                j = grp * DMA_UNROLL + u
                src = ys_ref.at[pl.ds(pos_ref[base + j], 1)]
                dst = buf_ref.at[slot, pl.ds(j, 1)]
                pltpu.make_async_copy(src, dst, sems.at[slot]).start(priority=u % 2)
            return carry

        lax.fori_loop(0, tb // DMA_UNROLL, issue, 0)

    @pl.when(step == 0)
    def _():
        gather(0, 0)

    @pl.when(step + 1 < n_steps)
    def _():
        gather(step + 1, (step + 1) % 2)

    slot = step % 2
    pltpu.make_async_copy(ys_ref.at[pl.ds(0, tb)], buf_ref.at[slot], sems.at[slot]).wait()

    x1 = _load_token_tiles(x1t_ref, tb, segs)
    y_lo = buf_ref[slot, :, 0:d]
    y_hi = buf_ref[slot, :, d:2 * d]
    meta = meta_ref[...]
    z = alpha * x1 + meta[:, 1:2] * y_lo + meta[:, 2:3] * y_hi
    out_ref[...] = _layer_norm(z, lnw_ref[...], lnb_ref[...])


def _collect(pos, ys, x1t, meta, ln_w, ln_b, alpha, d, tb):
    t = pos.shape[0]
    segs = d // LANES
    kern = functools.partial(_collect_kernel, tb=tb, alpha=alpha, d=d)
    return pl.pallas_call(
        kern,
        grid_spec=pltpu.PrefetchScalarGridSpec(
            num_scalar_prefetch=1,
            grid=(t // tb,),
            in_specs=[pl.BlockSpec(memory_space=pl.ANY),
                      pl.BlockSpec((tb * segs, LANES), lambda i, pos_ref: (i, 0)),
                      pl.BlockSpec((tb, LANES), lambda i, pos_ref: (i, 0)),
                      pl.BlockSpec((1, d), lambda i, pos_ref: (0, 0)),
                      pl.BlockSpec((1, d), lambda i, pos_ref: (0, 0))],
            out_specs=pl.BlockSpec((tb, d), lambda i, pos_ref: (i, 0)),
            scratch_shapes=[pltpu.VMEM((2, tb, 2 * d), F32), pltpu.SemaphoreType.DMA((2,))],
        ),
        out_shape=jax.ShapeDtypeStruct((t, d), F32),
        compiler_params=pltpu.CompilerParams(dimension_semantics=("arbitrary",),
                                             vmem_limit_bytes=VMEM_LIMIT),
        name="collect",
    )(pos, ys, x1t, meta, ln_w, ln_b)


MOE_TILES_PER_STEP = 1


def _moe_kernel(rows_ref, elo_ref, ehi_ref, nt_ref, xs_ref, *refs, d):
    *w_refs, ys_ref = refs
    i = pl.program_id(0)
    segs = d // LANES

    @pl.when(rows_ref[i * MOE_TILES_PER_STEP] > 0)
    def _():
        for k in range(MOE_TILES_PER_STEP):
            wgu_lo, wd_lo, wgu_hi, wd_hi = w_refs[4 * k:4 * k + 4]
            xs_k = xs_ref.at[pl.ds(k * MOE_TILE * segs, MOE_TILE * segs)]
            xb = _load_token_tiles(xs_k, MOE_TILE, segs).astype(BF16)

            def expert(wgu_ref, wd_ref):
                de = wd_ref.shape[0]
                gate_up = _dot(xb, wgu_ref[...])
                gate = gate_up[:, 0:de]
                hidden = gate * _sigmoid(gate) * gate_up[:, de:]
                return _dot(hidden.astype(BF16), wd_ref[...])

            ys_ref[k * MOE_TILE:(k + 1) * MOE_TILE, 0:d] = expert(wgu_lo, wd_lo)
            ys_ref[k * MOE_TILE:(k + 1) * MOE_TILE, d:2 * d] = expert(wgu_hi, wd_hi)

    @pl.when(rows_ref[i * MOE_TILES_PER_STEP] <= 0)
    def _():
        ys_ref[...] = jnp.zeros_like(ys_ref)


def _moe(info, xs, wgu, wd, d, n_tiles_pad):
    de = wd.shape[1]
    segs = d // LANES
    tps = MOE_TILES_PER_STEP
    kern = functools.partial(_moe_kernel, d=d)

    def last_live(tile, nt_ref):
        return jnp.minimum(tile, jnp.maximum(nt_ref[0] - 1, 0))

    def up_spec(k, which):
        return pl.BlockSpec((None, d, 2 * de),
                            lambda i, rows, elo, ehi, nt: ((elo, ehi)[which][last_live(i * tps + k, nt)], 0, 0))

    def down_spec(k, which):
        return pl.BlockSpec((None, de, d),
                            lambda i, rows, elo, ehi, nt: ((elo, ehi)[which][last_live(i * tps + k, nt)], 0, 0))

    w_specs = []
    for k in range(tps):
        w_specs += [up_spec(k, 0), down_spec(k, 0), up_spec(k, 1), down_spec(k, 1)]
    rows, elo, ehi, nt = info[:, 0], info[:, 1], info[:, 2], info[0:1, 3]
    return pl.pallas_call(
        kern,
        grid_spec=pltpu.PrefetchScalarGridSpec(
            num_scalar_prefetch=4,
            grid=(n_tiles_pad // tps,),
            in_specs=[pl.BlockSpec((tps * MOE_TILE * segs, LANES),
                                   lambda i, rows, elo, ehi, nt: (last_live(i * tps, nt) // tps, 0))] + w_specs,
            out_specs=pl.BlockSpec((tps * MOE_TILE, 2 * d), lambda i, rows, elo, ehi, nt: (i, 0)),
        ),
        out_shape=jax.ShapeDtypeStruct((n_tiles_pad * MOE_TILE, 2 * d), F32),
        compiler_params=pltpu.CompilerParams(dimension_semantics=("arbitrary",),
                                             vmem_limit_bytes=VMEM_LIMIT),
        name="moe",
    )(rows, elo, ehi, nt, xs, *([wgu, wd, wgu, wd] * tps))


def _pick_block(n, target):
    blk = min(n, target)
    while n % blk:
        blk //= 2
    return blk


def kernel(x, w_in, conv_w, conv_b, mlstm_gate_bias, mlstm_norm_w, attn_sinks, w_out, ln1_w, ln1_b,
           w_group_router, b_group_router, w_expert_router, b_expert_router,
           w_exp_gate, w_exp_up, w_exp_down, ln2_w, ln2_b):
    b, s, d = x.shape
    t = b * s
    depth = w_in.shape[0]
    alpha = (2.0 * depth) ** 0.25
    assert s % ML_CHUNK == 0 and s % WINDOW == 0 and d % LANES == 0

    tm = _pick_block(s, 1024)
    tq = _pick_block(s, 8 * ML_CHUNK)
    tb_rank = _pick_block(t, 512)
    tb_dma = _pick_block(t, 4096)
    tb_col = _pick_block(t, 1024)
    n_tiles_pad = -(-(t // MOE_TILE + N_CLASSES) // SUBLANES) * SUBLANES
    cos_t, sin_t = _rope_tables(s)

    for l in range(depth):
        x2d = x.reshape(t, d)
        q, k, v, og, aq, ak, av, g = _inproj(x2d, _pack_w_in(w_in[l]), conv_w[l], conv_b[l][None, :], tm, s // tm)
        bias_pad = jnp.zeros((LANES - ML_HEADS,), F32)
        gate_bias_row = jnp.concatenate(
            [mlstm_gate_bias[l, 0], bias_pad, mlstm_gate_bias[l, 1], bias_pad])[None, :]
        ml = _mlstm(q.reshape(b, s, -1), k.reshape(b, s, -1), v.reshape(b, s, -1), og.reshape(b, s, -1),
                    g.reshape(b, s, -1), gate_bias_row, mlstm_norm_w[l][None, :], tq)
        att = _swa(aq.reshape(b, s, -1), ak.reshape(b, s, -1), av.reshape(b, s, -1), cos_t, sin_t, attn_sinks[l],
                   _pick_block(s // WINDOW, 8))

        w_router, b_router = _pack_router(w_group_router[l], b_group_router[l],
                                          w_expert_router[l], b_expert_router[l])
        x1t, meta, counts = _outproj(x2d, ml.reshape(t, -1), att.reshape(t, -1), w_out[l].astype(BF16),
                                     ln1_w[l][None, :], ln1_b[l][None, :], w_router, b_router, alpha, tm)

        pos2d, info = _rank(meta, counts, tb_rank, n_tiles_pad)
        pos = pos2d[0]
        xs = _dispatch(pos, info[:, 0], x1t, n_tiles_pad * MOE_TILE, d // LANES, tb_dma)
        w_gate_up = jnp.concatenate([w_exp_gate[l].astype(BF16), w_exp_up[l].astype(BF16)], axis=-1)
        ys = _moe(info, xs, w_gate_up, w_exp_down[l].astype(BF16), d, n_tiles_pad)
        out = _collect(pos, ys, x1t, meta, ln2_w[l][None, :], ln2_b[l][None, :], alpha, d, tb_col)
        x = out.reshape(b, s, d)
    return x
```

```python
import functools
import math

import numpy as np
import jax
import jax.numpy as jnp
from jax import lax
from jax.experimental import pallas as pl
from jax.experimental.pallas import tpu as pltpu

F32 = jnp.float32
BF16 = jnp.bfloat16
I32 = jnp.int32

ML_HEADS = 4
ML_HEAD_DIM = 128
ML_WIDTH = ML_HEADS * ML_HEAD_DIM
ML_CHUNK = 128
CONV_WIDTH = 4
ATT_Q_HEADS = 8
ATT_KV_HEADS = 2
ATT_HEAD_DIM = 64
ATT_WIDTH = ATT_Q_HEADS * ATT_HEAD_DIM
ATT_KV_WIDTH = ATT_KV_HEADS * ATT_HEAD_DIM
WINDOW = 128
ROPE_THETA = 10000.0
N_GROUPS = 4
EXPERTS_PER_GROUP = 8
N_EXPERTS = N_GROUPS * EXPERTS_PER_GROUP
PAIRS_PER_GROUP = EXPERTS_PER_GROUP * (EXPERTS_PER_GROUP - 1) // 2
N_CLASSES = N_GROUPS * PAIRS_PER_GROUP
LN_EPS = 1e-5

LANES = 128
SUBLANES = 8
MOE_TILE = 320
VMEM_LIMIT = 56 * 1024 * 1024

NEG_INF = float("-inf")


def _sigmoid(x):
    return 1.0 / (1.0 + jnp.exp(-x))


def _log_sigmoid(x):
    return jnp.minimum(x, 0.0) - jnp.log(1.0 + jnp.exp(-jnp.abs(x)))


def _iota(shape, dim):
    return lax.broadcasted_iota(I32, shape, dim)


def _dot(a, b):
    return jnp.dot(a, b, preferred_element_type=F32)


def _dot_exact(a, b):
    return jnp.dot(a, b, preferred_element_type=F32, precision=lax.Precision.HIGHEST)


def _layer_norm(z, w, b):
    mu = jnp.mean(z, axis=-1, keepdims=True)
    zc = z - mu
    var = jnp.mean(zc * zc, axis=-1, keepdims=True)
    return zc * lax.rsqrt(var + LN_EPS) * w + b


C_QK = 0
C_V = C_QK + 2 * ML_WIDTH
C_O = C_V + ML_WIDTH
C_AQ = C_O + ML_WIDTH
C_AK = C_AQ + ATT_WIDTH
C_AV = C_AK + ATT_KV_HEADS * LANES
C_G = C_AV + ATT_KV_HEADS * LANES
C_END = C_G + 2 * LANES


def _pack_w_in(w_in):
    sizes = (2 * ML_WIDTH, ML_WIDTH, ML_WIDTH, ML_HEADS, ML_HEADS, ATT_WIDTH, ATT_KV_WIDTH, ATT_KV_WIDTH)
    splits = np.cumsum(sizes)[:-1].tolist()
    w_qk, w_v, w_o, w_i, w_f, w_aq, w_ak, w_av = jnp.split(w_in.astype(BF16), splits, axis=-1)
    half = ATT_HEAD_DIM // 2

    def head(w, h):
        return w[:, h * ATT_HEAD_DIM:(h + 1) * ATT_HEAD_DIM]

    def q_tile(a, b):
        return [a[:, :half], b[:, :half], a[:, half:], b[:, half:]]

    q_cols = [t for p in range(ATT_Q_HEADS // 2) for t in q_tile(head(w_aq, 2 * p), head(w_aq, 2 * p + 1))]
    k_cols = [t for h in range(ATT_KV_HEADS) for t in q_tile(head(w_ak, h), head(w_ak, h))]
    v_cols = [t for h in range(ATT_KV_HEADS) for t in (head(w_av, h), head(w_av, h))]
    lane_pad = jnp.zeros((w_in.shape[0], LANES - ML_HEADS), BF16)
    packed = jnp.concatenate([w_qk, w_v, w_o] + q_cols + k_cols + v_cols + [w_i, lane_pad, w_f, lane_pad], axis=-1)
    return packed.astype(BF16)


def _inproj_kernel(x_ref, w_ref, cw_ref, cb_ref, q_ref, k_ref, v_ref, og_ref, aq_ref, ak_ref, av_ref, g_ref,
                   *scratch, blocks_per_seq):
    *ext_refs, xb_ref = scratch
    tm = x_ref.shape[0]
    halo = SUBLANES
    cs = ext_refs[0].shape[1]
    xb_ref[...] = x_ref[...].astype(BF16)

    @pl.when(pl.program_id(0) % blocks_per_seq == 0)
    def _():
        for ext_ref in ext_refs:
            ext_ref[0:halo, :] = jnp.zeros((halo, cs), F32)

    scale = ML_HEAD_DIM ** -0.5

    def mm(lo, hi):
        return _dot(xb_ref[...], w_ref[:, lo:hi])

    def conv_slice(idx):
        ext_ref = ext_refs[idx]
        c0 = idx * cs
        is_q = c0 < ML_WIDTH
        dst_ref, off = (q_ref, c0) if is_q else (k_ref, c0 - ML_WIDTH)
        rt = ML_CHUNK
        for r0 in range(0, tm, rt):
            conv = cb_ref[:, c0:c0 + cs]
            for j in range(CONV_WIDTH):
                start = halo + r0 - (CONV_WIDTH - 1) + j
                conv = conv + cw_ref[j:j + 1, c0:c0 + cs] * ext_ref[start:start + rt, :]
            act = conv * _sigmoid(conv)
            dst_ref[r0:r0 + rt, off:off + cs] = (act if is_q else act * scale).astype(BF16)
        ext_ref[0:halo, :] = ext_ref[tm:tm + halo, :]

    for idx in range(len(ext_refs)):
        ext_refs[idx][halo:halo + tm, :] = mm(C_QK + idx * cs, C_QK + (idx + 1) * cs)
        if idx > 0:
            conv_slice(idx - 1)
    half_v = ML_WIDTH // 2
    v_ref[:, 0:half_v] = mm(C_V, C_V + half_v).astype(BF16)
    conv_slice(len(ext_refs) - 1)
    v_ref[:, half_v:] = mm(C_V + half_v, C_O).astype(BF16)
    og_ref[...] = _sigmoid(mm(C_O, C_AQ)).astype(BF16)
    aq_ref[...] = mm(C_AQ, C_AK)
    ak_ref[...] = mm(C_AK, C_AV)
    av_ref[...] = mm(C_AV, C_G).astype(BF16)
    g_ref[...] = mm(C_G, C_END)


def _inproj(x2d, w_packed, conv_w, conv_b, tm, blocks_per_seq):
    t, d = x2d.shape
    widths = (ML_WIDTH, ML_WIDTH, ML_WIDTH, ML_WIDTH, C_AK - C_AQ, C_AV - C_AK, C_G - C_AV, C_END - C_G)
    dtypes = (BF16, BF16, BF16, BF16, F32, F32, BF16, F32)
    kern = functools.partial(_inproj_kernel, blocks_per_seq=blocks_per_seq)
    return pl.pallas_call(
        kern,
        grid=(t // tm,),
        in_specs=[pl.BlockSpec((tm, d), lambda i: (i, 0)),
                  pl.BlockSpec((d, C_END), lambda i: (0, 0)),
                  pl.BlockSpec((CONV_WIDTH, 2 * ML_WIDTH), lambda i: (0, 0)),
                  pl.BlockSpec((1, 2 * ML_WIDTH), lambda i: (0, 0))],
        out_specs=[pl.BlockSpec((tm, w), lambda i: (i, 0)) for w in widths],
        out_shape=[jax.ShapeDtypeStruct((t, w), dt) for w, dt in zip(widths, dtypes)],
        scratch_shapes=[pltpu.VMEM((tm + SUBLANES, 2 * LANES), F32)] * (2 * ML_WIDTH // (2 * LANES))
        + [pltpu.VMEM((tm, d), BF16)],
        compiler_params=pltpu.CompilerParams(dimension_semantics=("arbitrary",),
                                             vmem_limit_bytes=VMEM_LIMIT),
        name="inproj",
    )(x2d, w_packed, conv_w, conv_b)


def _time_scan(x, combine, identity):
    row = _iota(x.shape, 0)
    sh = 1
    while sh < x.shape[0]:
        x = combine(x, jnp.where(row >= sh, pltpu.roll(x, sh, 0), identity))
        sh *= 2
    return x


def _mlstm_kernel(q_ref, k_ref, v_ref, og_ref, g_ref, gb_ref, nw_ref, out_ref, ct_ref, m_ref, *, tq):
    s_idx = pl.program_id(1)
    L = ML_CHUNK
    D = ML_HEAD_DIM
    H = ML_HEADS
    heads = range(H)

    @pl.when(s_idx == 0)
    def _():
        ct_ref[...] = jnp.zeros_like(ct_ref)
        m_ref[...] = jnp.zeros_like(m_ref)

    causal = _iota((L, L), 1) <= _iota((L, L), 0)
    ones_blk = jnp.ones((L, D), BF16)
    mean_blk = jnp.full((D, D), 1.0 / D, BF16)
    m_prev = m_ref[0:1, :]
    head_lanes = _iota((L, LANES), 1) < H
    tile_of_lane = jnp.right_shift(_iota((LANES, H * L), 1), L.bit_length() - 1)
    spread = jnp.where(_iota((LANES, H * L), 0) == tile_of_lane, 1.0, 0.0).astype(BF16)

    def spread_heads(x):
        x = jnp.where(head_lanes, x, 0.0)
        hi = x.astype(BF16)
        lo = (x - hi.astype(F32)).astype(BF16)
        return _dot(hi, spread) + _dot(lo, spread)

    for c in range(tq // L):
        r0 = c * L
        gi = g_ref[r0:r0 + L, 0:LANES] + gb_ref[:, 0:LANES]
        gf = g_ref[r0:r0 + L, LANES:2 * LANES] + gb_ref[:, LANES:2 * LANES]
        b_cum = _time_scan(_log_sigmoid(gf), jnp.add, 0.0)
        r = gi - b_cum
        g = jnp.maximum(m_prev, _time_scan(r, jnp.maximum, NEG_INF))
        g_rep = spread_heads(g)
        b_rep = spread_heads(b_cum)
        b_last = b_cum[L - 1:L, :]
        m_new = jnp.maximum(b_last + m_prev, jnp.max(b_last + r, axis=0, keepdims=True))
        decay = jnp.exp(b_last + m_prev - m_new)
        shift = b_last - m_new
        r_t = r.T

        q_b = [q_ref[r0:r0 + L, h * D:(h + 1) * D] for h in heads]
        kt_b = [k_ref[r0:r0 + L, h * D:(h + 1) * D].T for h in heads]
        v_aug = [jnp.concatenate([v_ref[r0:r0 + L, h * D:(h + 1) * D], ones_blk], axis=-1) for h in heads]
        g_col = [g_rep[:, h * L:(h + 1) * L] for h in heads]
        w_intra = [jnp.exp(jnp.where(causal, r_t[h:h + 1, :] - g_col[h], NEG_INF)) for h in heads]
        s_b = [(_dot(q_b[h], kt_b[h]) * w_intra[h]).astype(BF16) for h in heads]
        ct = [ct_ref[h] for h in heads]
        inter = [_dot(q_b[h], ct[h].astype(BF16)) for h in heads]
        intra = [_dot(s_b[h], v_aug[h]) for h in heads]
        for h in heads:
            wi_col = jnp.exp(m_prev[:, h:h + 1] - g_col[h])
            clamp = jnp.exp(-(b_rep[:, h * L:(h + 1) * L] + g_col[h]))
            num = wi_col * inter[h][:, 0:D] + intra[h][:, 0:D]
            den = wi_col * inter[h][:, D:] + intra[h][:, D:]
            hh = num / jnp.maximum(jnp.abs(den), clamp)
            mu = _dot(hh.astype(BF16), mean_blk)
            hc = hh - mu
            var = _dot((hc * hc).astype(BF16), mean_blk)
            hn = hc * lax.rsqrt(var + LN_EPS) * nw_ref[:, h * D:(h + 1) * D]
            gate_o = og_ref[r0:r0 + L, h * D:(h + 1) * D].astype(F32)
            out_ref[r0:r0 + L, h * D:(h + 1) * D] = (gate_o * hn).astype(out_ref.dtype)
        for h in heads:
            w_row = jnp.exp(r_t[h:h + 1, :] + shift[:, h:h + 1])
            ktw = (kt_b[h].astype(F32) * w_row).astype(BF16)
            ct_ref[h] = decay[:, h:h + 1] * ct[h] + _dot(ktw, v_aug[h])
        m_prev = m_new

    m_ref[...] = jnp.broadcast_to(m_prev, m_ref.shape)


def _mlstm(q, k, v, og, g, gate_bias_row, norm_w_row, tq):
    b, s, _ = q.shape
    kern = functools.partial(_mlstm_kernel, tq=tq)

    def seq_spec(width):
        return pl.BlockSpec((None, tq, width), lambda bi, si: (bi, si, 0))

    def const_spec(shape):
        return pl.BlockSpec(shape, lambda bi, si: (0,) * len(shape))

    return pl.pallas_call(
        kern,
        grid=(b, s // tq),
        in_specs=[seq_spec(ML_WIDTH), seq_spec(ML_WIDTH), seq_spec(ML_WIDTH), seq_spec(ML_WIDTH),
                  seq_spec(2 * LANES), const_spec((1, 2 * LANES)), const_spec((1, ML_WIDTH))],
        out_specs=seq_spec(ML_WIDTH),
        out_shape=jax.ShapeDtypeStruct((b, s, ML_WIDTH), BF16),
        scratch_shapes=[pltpu.VMEM((ML_HEADS, ML_HEAD_DIM, 2 * ML_HEAD_DIM), F32),
                        pltpu.VMEM((SUBLANES, LANES), F32)],
        compiler_params=pltpu.CompilerParams(dimension_semantics=("arbitrary", "arbitrary"),
                                             vmem_limit_bytes=VMEM_LIMIT),
        name="mlstm",
    )(q, k, v, og, g, gate_bias_row, norm_w_row)


def _rope_tables(seq_len):
    half = ATT_HEAD_DIM // 2
    inv_freq = ROPE_THETA ** (-jnp.arange(half, dtype=F32) / half)
    ang = jnp.arange(seq_len, dtype=F32)[:, None] * inv_freq[None, :]
    cos = jnp.cos(ang)
    sin = jnp.sin(ang)
    cos_t = jnp.concatenate([cos, cos, cos, cos], axis=-1)
    sin_t = jnp.concatenate([-sin, -sin, sin, sin], axis=-1)
    return cos_t, sin_t


def _swa_kernel(sink_ref, aq_ref, ak_ref, av_ref, cos_ref, sin_ref, out_ref, kprev_ref, vprev_ref, *, nsub):
    step = pl.program_id(1)
    Lb = WINDOW
    half = ATT_HEAD_DIM // 2
    pairs = ATT_Q_HEADS // 2
    pairs_per_kv = pairs // ATT_KV_HEADS

    @pl.when(step == 0)
    def _():
        kprev_ref[...] = jnp.zeros_like(kprev_ref)
        vprev_ref[...] = jnp.zeros_like(vprev_ref)

    def rope(x, cos, sin):
        tiles = []
        for c in range(x.shape[-1] // LANES):
            xt = x[:, c * LANES:(c + 1) * LANES]
            tiles.append(xt * cos + pltpu.roll(xt, LANES // 2, 1) * sin)
        return jnp.concatenate(tiles, axis=-1)

    ql = _iota((Lb, 2 * Lb), 0)
    kj = _iota((Lb, 2 * Lb), 1)
    diff = Lb + ql - kj
    lane = _iota((Lb, LANES), 1)
    low_half = lane < ATT_HEAD_DIM
    first_head = (lane & half) == 0
    ones_blk = jnp.ones((2 * Lb, LANES), BF16)

    k_prev = kprev_ref
    v_prev = vprev_ref

    def block(j, carry):
        rows_j = pl.ds(pl.multiple_of(j * Lb, Lb), Lb)
        cos = cos_ref[rows_j, :]
        sin = sin_ref[rows_j, :]
        q = rope(aq_ref[rows_j, :], cos, sin) * (ATT_HEAD_DIM ** -0.5)
        k_cur = rope(ak_ref[rows_j, :], cos, sin).astype(BF16)
        v_cur = av_ref[rows_j, :]

        kpos = (step * nsub + j) * Lb + kj - Lb
        visible = jnp.where(diff >= 0, jnp.where(diff < WINDOW, jnp.where(kpos >= 0, 1, 0), 0), 0)
        bias = jnp.where(visible > 0, 0.0, NEG_INF).astype(F32)
        bias = jnp.concatenate([bias] * (2 * pairs_per_kv), axis=0)

        for g in range(ATT_KV_HEADS):
            kk = jnp.concatenate([k_prev[:, g * LANES:(g + 1) * LANES], k_cur[:, g * LANES:(g + 1) * LANES]], axis=0)
            vv = jnp.concatenate([v_prev[:, g * LANES:(g + 1) * LANES], v_cur[:, g * LANES:(g + 1) * LANES]], axis=0)
            vv_aug = jnp.concatenate([vv, ones_blk], axis=-1)
            rows = []
            sinks = []
            for p in range(pairs_per_kv):
                pair = g * pairs_per_kv + p
                q2 = q[:, pair * LANES:(pair + 1) * LANES]
                rows.append(jnp.where(first_head, q2, 0.0))
                rows.append(jnp.where(first_head, 0.0, q2))
                sinks.append(jnp.full((Lb, LANES), sink_ref[2 * pair], F32))
                sinks.append(jnp.full((Lb, LANES), sink_ref[2 * pair + 1], F32))
            qs = jnp.concatenate(rows, axis=0).astype(BF16)
            sink = jnp.concatenate(sinks, axis=0)

            sc = lax.dot_general(qs, kk, (((1,), (1,)), ((), ())), preferred_element_type=F32) + bias
            m = jnp.maximum(jnp.broadcast_to(jnp.max(sc, axis=-1, keepdims=True), sink.shape), sink)
            p_un = jnp.exp(sc - jnp.concatenate([m, m], axis=-1))
            acc = _dot(p_un.astype(BF16), vv_aug)
            o = acc[:, 0:LANES] / (acc[:, LANES:] + jnp.exp(sink - m))
            for p in range(pairs_per_kv):
                pair = g * pairs_per_kv + p
                even = o[(2 * p) * Lb:(2 * p + 1) * Lb, :]
                odd = o[(2 * p + 1) * Lb:(2 * p + 2) * Lb, :]
                out_ref[rows_j, pair * LANES:(pair + 1) * LANES] = (
                    jnp.where(low_half, even, odd).astype(out_ref.dtype))
        kprev_ref[...] = k_cur
        vprev_ref[...] = v_cur
        return carry

    lax.fori_loop(0, nsub, block, 0)


def _swa(aq, ak, av, cos_t, sin_t, sinks, nsub):
    b, s, _ = aq.shape
    kvw = ATT_KV_HEADS * LANES
    rows = nsub * WINDOW

    def seq_spec(width):
        return pl.BlockSpec((None, rows, width), lambda bi, si: (bi, si, 0))

    tab_spec = pl.BlockSpec((rows, LANES), lambda bi, si: (si, 0))
    return pl.pallas_call(
        functools.partial(_swa_kernel, nsub=nsub),
        grid=(b, s // rows),
        in_specs=[pl.BlockSpec(memory_space=pltpu.SMEM),
                  seq_spec(ATT_WIDTH), seq_spec(kvw), seq_spec(kvw), tab_spec, tab_spec],
        out_specs=seq_spec(ATT_WIDTH),
        out_shape=jax.ShapeDtypeStruct((b, s, ATT_WIDTH), BF16),
        scratch_shapes=[pltpu.VMEM((WINDOW, kvw), BF16), pltpu.VMEM((WINDOW, kvw), BF16)],
        compiler_params=pltpu.CompilerParams(dimension_semantics=("arbitrary", "arbitrary"),
                                             vmem_limit_bytes=VMEM_LIMIT),
        name="swa",
    )(sinks, aq, ak, av, cos_t, sin_t)


def _store_token_tiles(ref, val, row0=0, rows_per_token=None):
    n, w = val.shape
    segs = w // LANES
    rpt = rows_per_token or segs
    for j in range(segs):
        ref[pl.ds(row0 + j, n, stride=rpt), :] = val[:, j * LANES:(j + 1) * LANES]


def _load_token_tiles(ref, n, segs, row0=0, rows_per_token=None):
    rpt = rows_per_token or segs
    return jnp.concatenate([ref[pl.ds(row0 + j, n, stride=rpt), :] for j in range(segs)], axis=-1)


ROUTER_ROWS = 48


def _pack_router(w_group, b_group, w_expert, b_expert):
    d = w_group.shape[0]
    wt = jnp.zeros((ROUTER_ROWS, d), F32)
    wt = wt.at[0:N_GROUPS].set(w_group.T).at[SUBLANES:SUBLANES + N_EXPERTS].set(w_expert.T)
    bias = jnp.zeros((ROUTER_ROWS,), F32)
    bias = bias.at[0:N_GROUPS].set(b_group).at[SUBLANES:SUBLANES + N_EXPERTS].set(b_expert)
    hi = wt.astype(BF16)
    lo = (wt - hi.astype(F32)).astype(BF16)
    return jnp.concatenate([hi, lo], axis=0), jnp.broadcast_to(bias[:, None], (ROUTER_ROWS, LANES))


def _outproj_kernel(x_ref, ml_ref, att_ref, wo_ref, lnw_ref, lnb_ref, wrt_ref, brt_ref,
                    x1t_ref, meta_ref, cnt_ref, *, alpha):
    step = pl.program_id(0)
    tm, d = x_ref.shape
    segs = d // LANES
    pt = tm
    rr = wrt_ref.shape[0] // 2
    nt = (((1,), (1,)), ((), ()))
    row = _iota((SUBLANES, pt), 0).astype(F32)
    lane = _iota((pt, LANES), 1).astype(F32)

    def first_argmax(vals):
        top = jnp.max(vals, axis=0, keepdims=True)
        idx = jnp.min(jnp.where(vals == top, row, float(SUBLANES)), axis=0, keepdims=True)
        return top, idx

    @pl.when(step == 0)
    def _():
        cnt_ref[...] = jnp.zeros_like(cnt_ref)

    counts = jnp.zeros((1, LANES), F32)
    for r0 in range(0, tm, pt):
        y = (_dot(ml_ref[r0:r0 + pt, :], wo_ref[0:ML_WIDTH, :])
             + _dot(att_ref[r0:r0 + pt, :], wo_ref[ML_WIDTH:, :]))
        x1 = _layer_norm(alpha * x_ref[r0:r0 + pt, :] + y, lnw_ref[...], lnb_ref[...])
        _store_token_tiles(x1t_ref.at[pl.ds(r0 * segs, pt * segs)], x1)

        x1_hi = x1.astype(BF16)
        x1_lo = (x1 - x1_hi.astype(F32)).astype(BF16)
        both = lax.dot_general(wrt_ref[...], x1_hi, nt, preferred_element_type=F32)
        cross = lax.dot_general(wrt_ref[0:rr, :], x1_lo, nt, preferred_element_type=F32)
        logits = both[0:rr] + both[rr:2 * rr] + cross + jnp.concatenate([brt_ref[...]] * (pt // LANES), axis=1)

        g_logits = jnp.where(row < N_GROUPS, logits[0:SUBLANES], NEG_INF)
        g_top, g_idx = first_argmax(g_logits)
        g_p = 1.0 / jnp.sum(jnp.exp(g_logits - g_top), axis=0, keepdims=True)

        e_logits = logits[SUBLANES:2 * SUBLANES]
        for grp in range(1, N_GROUPS):
            e_logits = jnp.where(g_idx == grp, logits[(1 + grp) * SUBLANES:(2 + grp) * SUBLANES], e_logits)
        v1, a1 = first_argmax(e_logits)
        v2, a2 = first_argmax(jnp.where(row == a1, NEG_INF, e_logits))
        r = jnp.exp(v2 - v1)
        w1 = g_p / (1.0 + r)
        w2 = g_p * r / (1.0 + r)

        lo = jnp.minimum(a1, a2)
        hi = jnp.maximum(a1, a2)
        w_lo = jnp.where(a1 < a2, w1, w2)
        w_hi = jnp.where(a1 < a2, w2, w1)
        pair_idx = (EXPERTS_PER_GROUP - 1) * lo - lo * (lo - 1.0) * 0.5 + (hi - lo - 1.0)
        cls = g_idx * PAIRS_PER_GROUP + pair_idx

        meta_t = jnp.where(row == 0.0, cls, jnp.where(row == 1.0, w_lo, jnp.where(row == 2.0, w_hi, 0.0)))
        meta = jnp.concatenate([meta_t, jnp.zeros((LANES - SUBLANES, pt), F32)], axis=0).T
        meta_ref[r0:r0 + pt, :] = meta
        counts = counts + jnp.sum(jnp.where(lane == meta[:, 0:1], 1.0, 0.0), axis=0, keepdims=True)

    cnt_ref[0:1, :] += counts


def _outproj(x2d, ml2d, att2d, w_out_b, ln_w, ln_b, w_router, b_router, alpha, tm):
    t, d = x2d.shape
    kern = functools.partial(_outproj_kernel, alpha=alpha)

    def const_spec(shape):
        return pl.BlockSpec(shape, lambda i: (0,) * len(shape))

    return pl.pallas_call(
        kern,
        grid=(t // tm,),
        in_specs=[pl.BlockSpec((tm, d), lambda i: (i, 0)),
                  pl.BlockSpec((tm, ML_WIDTH), lambda i: (i, 0)),
                  pl.BlockSpec((tm, ATT_WIDTH), lambda i: (i, 0)),
                  const_spec((ML_WIDTH + ATT_WIDTH, d)), const_spec((1, d)), const_spec((1, d)),
                  const_spec((2 * ROUTER_ROWS, d)), const_spec((ROUTER_ROWS, LANES))],
        out_specs=[pl.BlockSpec((tm * (d // LANES), LANES), lambda i: (i, 0)),
                   pl.BlockSpec((tm, LANES), lambda i: (i, 0)), const_spec((SUBLANES, LANES))],
        out_shape=[jax.ShapeDtypeStruct((t * (d // LANES), LANES), F32),
                   jax.ShapeDtypeStruct((t, LANES), F32),
                   jax.ShapeDtypeStruct((SUBLANES, LANES), F32)],
        compiler_params=pltpu.CompilerParams(dimension_semantics=("arbitrary",),
                                             vmem_limit_bytes=VMEM_LIMIT),
        name="outproj",
    )(x2d, ml2d, att2d, w_out_b, ln_w, ln_b, w_router, b_router)


def _class_expert_table():
    tab = np.zeros((SUBLANES, LANES), np.float32)
    for g in range(N_GROUPS):
        idx = 0
        for lo in range(EXPERTS_PER_GROUP):
            for hi in range(lo + 1, EXPERTS_PER_GROUP):
                c = g * PAIRS_PER_GROUP + idx
                tab[0, c] = g * EXPERTS_PER_GROUP + lo
                tab[1, c] = g * EXPERTS_PER_GROUP + hi
                idx += 1
    return tab


def _rank_kernel(meta_ref, cnt_ref, tab_ref, pos_ref, tile_ref, base_ref, run_ref, *, tb, n_tiles_pad):
    step = pl.program_id(0)
    lane8 = _iota((SUBLANES, LANES), 1)

    @pl.when(step == 0)
    def _():
        cnt = jnp.broadcast_to(cnt_ref[0:1, :], (SUBLANES, LANES))
        tiles = jnp.floor((cnt + (MOE_TILE - 1.0)) * (1.0 / MOE_TILE))
        cum = tiles
        sh = 1
        while sh < LANES:
            cum = cum + jnp.where(lane8 >= sh, pltpu.roll(cum, sh, 1), 0.0)
            sh *= 2
        excl = cum - tiles
        base_ref[...] = excl * MOE_TILE
        run_ref[...] = jnp.zeros_like(run_ref)

        ti = _iota((n_tiles_pad, LANES), 0).astype(F32)
        lane = _iota((n_tiles_pad, LANES), 1)
        done = jnp.where(lane < N_CLASSES, jnp.where(cum[0:1, :] <= ti, 1.0, 0.0), 0.0)
        t_cls = jnp.sum(done, axis=-1, keepdims=True)
        sel = jnp.where(lane.astype(F32) == t_cls, 1.0, 0.0)
        cnt_i = jnp.sum(sel * cnt[0:1, :], axis=-1, keepdims=True)
        first_i = jnp.sum(sel * excl[0:1, :], axis=-1, keepdims=True)
        rows_i = jnp.clip(cnt_i - MOE_TILE * (ti[:, 0:1] - first_i), 0.0, float(MOE_TILE))
        e_lo = jnp.sum(sel * tab_ref[0:1, :], axis=-1, keepdims=True)
        e_hi = jnp.sum(sel * tab_ref[1:2, :], axis=-1, keepdims=True)
        n_tiles = jnp.sum(jnp.where(lane < N_CLASSES, jnp.broadcast_to(tiles[0:1, :], (n_tiles_pad, LANES)), 0.0),
                          axis=-1, keepdims=True)
        info = jnp.where(lane == 0, rows_i,
                         jnp.where(lane == 1, e_lo, jnp.where(lane == 2, e_hi, jnp.where(lane == 3, n_tiles, 0.0))))
        tile_ref[...] = info.astype(I32)

    cls = meta_ref[:, 0:1]
    lane = _iota((tb, LANES), 1).astype(F32)
    onehot = jnp.where(lane == cls, 1.0, 0.0)
    strict_lower = jnp.where(_iota((tb, tb), 1) < _iota((tb, tb), 0), 1.0, 0.0).astype(BF16)
    before = _dot(strict_lower, onehot.astype(BF16))
    slot = jnp.sum(onehot * (before + run_ref[0:1, :] + base_ref[0:1, :]), axis=-1, keepdims=True)
    run_ref[...] = run_ref[...] + jnp.sum(onehot, axis=0, keepdims=True)
    slot_t = jnp.broadcast_to(slot, (tb, LANES)).T
    pos_ref[...] = slot_t[0:SUBLANES, :].astype(I32)


def _rank(meta, counts, tb, n_tiles_pad):
    t = meta.shape[0]
    kern = functools.partial(_rank_kernel, tb=tb, n_tiles_pad=n_tiles_pad)
    tab = jnp.asarray(_class_expert_table())
    return pl.pallas_call(
        kern,
        grid=(t // tb,),
        in_specs=[pl.BlockSpec((tb, LANES), lambda i: (i, 0)),
                  pl.BlockSpec((SUBLANES, LANES), lambda i: (0, 0)),
                  pl.BlockSpec((SUBLANES, LANES), lambda i: (0, 0))],
        out_specs=[pl.BlockSpec((SUBLANES, tb), lambda i: (0, i)),
                   pl.BlockSpec((n_tiles_pad, LANES), lambda i: (0, 0))],
        out_shape=[jax.ShapeDtypeStruct((SUBLANES, t), I32),
                   jax.ShapeDtypeStruct((n_tiles_pad, LANES), I32)],
        scratch_shapes=[pltpu.VMEM((SUBLANES, LANES), F32), pltpu.VMEM((SUBLANES, LANES), F32)],
        compiler_params=pltpu.CompilerParams(dimension_semantics=("arbitrary",),
                                             vmem_limit_bytes=VMEM_LIMIT),
        name="rank",
    )(meta, counts, tab)


DMA_UNROLL = 8


def _pad_chunks():
    size, out = 1, []
    while size <= MOE_TILE:
        out.append(size)
        size *= 2
    return out[::-1]


def _dispatch_kernel(pos_ref, rows_ref, src_ref, dst_ref, zero_ref, sems, *, tb, rpt, n_tiles):
    step = pl.program_id(0)
    base = step * tb

    def pad_copies(act):
        def one_tile(i, carry):
            valid = rows_ref[i]
            pad = MOE_TILE - valid
            off = i * MOE_TILE + valid
            for size in _pad_chunks():
                hit = (pad & size) != 0

                @pl.when(hit)
                def _():
                    act(pltpu.make_async_copy(zero_ref.at[pl.ds(0, size * rpt)],
                                              dst_ref.at[pl.ds(pl.multiple_of(off * rpt, rpt), size * rpt)],
                                              sems.at[1]))
                off = off + jnp.where(hit, size, 0)
            return carry
        lax.fori_loop(0, n_tiles, one_tile, 0)

    @pl.when(step == 0)
    def _():
        zero_ref[...] = jnp.zeros_like(zero_ref)
        pad_copies(lambda cp: cp.start())

    def issue(grp, carry):
        for u in range(DMA_UNROLL):
            j = grp * DMA_UNROLL + u
            src = src_ref.at[pl.ds(pl.multiple_of(j * rpt, rpt), rpt)]
            dst = dst_ref.at[pl.ds(pl.multiple_of(pos_ref[base + j] * rpt, rpt), rpt)]
            pltpu.make_async_copy(src, dst, sems.at[0]).start(priority=u % 2)
        return carry

    lax.fori_loop(0, tb // DMA_UNROLL, issue, 0)
    pltpu.make_async_copy(src_ref, dst_ref.at[pl.ds(0, tb * rpt)], sems.at[0]).wait()

    @pl.when(step == pl.num_programs(0) - 1)
    def _():
        pad_copies(lambda cp: cp.wait())


def _dispatch(pos, tile_rows, src, n_slots, rpt, tb):
    t = pos.shape[0]
    n_tiles = tile_rows.shape[0]
    kern = functools.partial(_dispatch_kernel, tb=tb, rpt=rpt, n_tiles=n_tiles)
    pad_rows = _pad_chunks()[0] * rpt
    return pl.pallas_call(
        kern,
        grid_spec=pltpu.PrefetchScalarGridSpec(
            num_scalar_prefetch=2,
            grid=(t // tb,),
            in_specs=[pl.BlockSpec((tb * rpt, LANES), lambda i, pos_ref, rows_ref: (i, 0))],
            out_specs=pl.BlockSpec(memory_space=pl.ANY),
            scratch_shapes=[pltpu.VMEM((pad_rows, LANES), src.dtype), pltpu.SemaphoreType.DMA((2,))],
        ),
        out_shape=jax.ShapeDtypeStruct((n_slots * rpt, LANES), src.dtype),
        compiler_params=pltpu.CompilerParams(dimension_semantics=("arbitrary",),
                                             has_side_effects=True, vmem_limit_bytes=VMEM_LIMIT),
        name="dispatch",
    )(pos, tile_rows, src)


def _collect_kernel(pos_ref, ys_ref, x1t_ref, meta_ref, lnw_ref, lnb_ref, out_ref, buf_ref, sems,
                    *, tb, alpha, d):
    segs = d // LANES
    rpt = 2 * segs
    step = pl.program_id(0)
    n_steps = pl.num_programs(0)

    def gather(blk, slot):
        base = blk * tb

        def issue(grp, carry):
            for u in range(DMA_UNROLL):
                j = grp * DMA_UNROLL + u
                src = ys_ref.at[pl.ds(pos_ref[base + j], 1)]
                dst = buf_ref.at[slot, pl.ds(j, 1)]
                pltpu.make_async_copy(src, dst, sems.at[slot]).start(priority=u % 2)
            return carry

        lax.fori_loop(0, tb // DMA_UNROLL, issue, 0)

    @pl.when(step == 0)
    def _():
        gather(0, 0)

    @pl.when(step + 1 < n_steps)
    def _():
        gather(step + 1, (step + 1) % 2)

    slot = step % 2
    pltpu.make_async_copy(ys_ref.at[pl.ds(0, tb)], buf_ref.at[slot], sems.at[slot]).wait()

    x1 = _load_token_tiles(x1t_ref, tb, segs)
    y_lo = buf_ref[slot, :, 0:d]
    y_hi = buf_ref[slot, :, d:2 * d]
    meta = meta_ref[...]
    z = alpha * x1 + meta[:, 1:2] * y_lo + meta[:, 2:3] * y_hi
    out_ref[...] = _layer_norm(z, lnw_ref[...], lnb_ref[...])


def _collect(pos, ys, x1t, meta, ln_w, ln_b, alpha, d, tb):
    t = pos.shape[0]
    segs = d // LANES
    kern = functools.partial(_collect_kernel, tb=tb, alpha=alpha, d=d)
    return pl.pallas_call(
        kern,
        grid_spec=pltpu.PrefetchScalarGridSpec(
            num_scalar_prefetch=1,
            grid=(t // tb,),
            in_specs=[pl.BlockSpec(memory_space=pl.ANY),
                      pl.BlockSpec((tb * segs, LANES), lambda i, pos_ref: (i, 0)),
                      pl.BlockSpec((tb, LANES), lambda i, pos_ref: (i, 0)),
                      pl.BlockSpec((1, d), lambda i, pos_ref: (0, 0)),
                      pl.BlockSpec((1, d), lambda i, pos_ref: (0, 0))],
            out_specs=pl.BlockSpec((tb, d), lambda i, pos_ref: (i, 0)),
            scratch_shapes=[pltpu.VMEM((2, tb, 2 * d), F32), pltpu.SemaphoreType.DMA((2,))],
        ),
        out_shape=jax.ShapeDtypeStruct((t, d), F32),
        compiler_params=pltpu.CompilerParams(dimension_semantics=("arbitrary",),
                                             vmem_limit_bytes=VMEM_LIMIT),
        name="collect",
    )(pos, ys, x1t, meta, ln_w, ln_b)


MOE_TILES_PER_STEP = 1


def _moe_kernel(rows_ref, elo_ref, ehi_ref, nt_ref, xs_ref, *refs, d):
    *w_refs, ys_ref = refs
    i = pl.program_id(0)
    segs = d // LANES

    @pl.when(rows_ref[i * MOE_TILES_PER_STEP] > 0)
    def _():
        for k in range(MOE_TILES_PER_STEP):
            wgu_lo, wd_lo, wgu_hi, wd_hi = w_refs[4 * k:4 * k + 4]
            xs_k = xs_ref.at[pl.ds(k * MOE_TILE * segs, MOE_TILE * segs)]
            xb = _load_token_tiles(xs_k, MOE_TILE, segs).astype(BF16)

            def expert(wgu_ref, wd_ref):
                de = wd_ref.shape[0]
                gate_up = _dot(xb, wgu_ref[...])
                gate = gate_up[:, 0:de]
                hidden = gate * _sigmoid(gate) * gate_up[:, de:]
                return _dot(hidden.astype(BF16), wd_ref[...])

            ys_ref[k * MOE_TILE:(k + 1) * MOE_TILE, 0:d] = expert(wgu_lo, wd_lo)
            ys_ref[k * MOE_TILE:(k + 1) * MOE_TILE, d:2 * d] = expert(wgu_hi, wd_hi)

    @pl.when(rows_ref[i * MOE_TILES_PER_STEP] <= 0)
    def _():
        ys_ref[...] = jnp.zeros_like(ys_ref)


def _moe(info, xs, wgu, wd, d, n_tiles_pad):
    de = wd.shape[1]
    segs = d // LANES
    tps = MOE_TILES_PER_STEP
    kern = functools.partial(_moe_kernel, d=d)

    def last_live(tile, nt_ref):
        return jnp.minimum(tile, jnp.maximum(nt_ref[0] - 1, 0))

    def up_spec(k, which):
        return pl.BlockSpec((None, d, 2 * de),
                            lambda i, rows, elo, ehi, nt: ((elo, ehi)[which][last_live(i * tps + k, nt)], 0, 0))

    def down_spec(k, which):
        return pl.BlockSpec((None, de, d),
                            lambda i, rows, elo, ehi, nt: ((elo, ehi)[which][last_live(i * tps + k, nt)], 0, 0))

    w_specs = []
    for k in range(tps):
        w_specs += [up_spec(k, 0), down_spec(k, 0), up_spec(k, 1), down_spec(k, 1)]
    rows, elo, ehi, nt = info[:, 0], info[:, 1], info[:, 2], info[0:1, 3]
    return pl.pallas_call(
        kern,
        grid_spec=pltpu.PrefetchScalarGridSpec(
            num_scalar_prefetch=4,
            grid=(n_tiles_pad // tps,),
            in_specs=[pl.BlockSpec((tps * MOE_TILE * segs, LANES),
                                   lambda i, rows, elo, ehi, nt: (last_live(i * tps, nt) // tps, 0))] + w_specs,
            out_specs=pl.BlockSpec((tps * MOE_TILE, 2 * d), lambda i, rows, elo, ehi, nt: (i, 0)),
        ),
        out_shape=jax.ShapeDtypeStruct((n_tiles_pad * MOE_TILE, 2 * d), F32),
        compiler_params=pltpu.CompilerParams(dimension_semantics=("arbitrary",),
                                             vmem_limit_bytes=VMEM_LIMIT),
        name="moe",
    )(rows, elo, ehi, nt, xs, *([wgu, wd, wgu, wd] * tps))


def _pick_block(n, target):
    blk = min(n, target)
    while n % blk:
        blk //= 2
    return blk


def kernel(x, w_in, conv_w, conv_b, mlstm_gate_bias, mlstm_norm_w, attn_sinks, w_out, ln1_w, ln1_b,
           w_group_router, b_group_router, w_expert_router, b_expert_router,
           w_exp_gate, w_exp_up, w_exp_down, ln2_w, ln2_b):
    b, s, d = x.shape
    t = b * s
    depth = w_in.shape[0]
    alpha = (2.0 * depth) ** 0.25
    assert s % ML_CHUNK == 0 and s % WINDOW == 0 and d % LANES == 0

    tm = _pick_block(s, 1024)
    tq = _pick_block(s, 8 * ML_CHUNK)
    tb_rank = _pick_block(t, 512)
    tb_dma = _pick_block(t, 4096)
    tb_col = _pick_block(t, 1024)
    n_tiles_pad = -(-(t // MOE_TILE + N_CLASSES) // SUBLANES) * SUBLANES
    cos_t, sin_t = _rope_tables(s)

    for l in range(depth):
        x2d = x.reshape(t, d)
        q, k, v, og, aq, ak, av, g = _inproj(x2d, _pack_w_in(w_in[l]), conv_w[l], conv_b[l][None, :], tm, s // tm)
        bias_pad = jnp.zeros((LANES - ML_HEADS,), F32)
        gate_bias_row = jnp.concatenate(
            [mlstm_gate_bias[l, 0], bias_pad, mlstm_gate_bias[l, 1], bias_pad])[None, :]
        ml = _mlstm(q.reshape(b, s, -1), k.reshape(b, s, -1), v.reshape(b, s, -1), og.reshape(b, s, -1),
                    g.reshape(b, s, -1), gate_bias_row, mlstm_norm_w[l][None, :], tq)
        att = _swa(aq.reshape(b, s, -1), ak.reshape(b, s, -1), av.reshape(b, s, -1), cos_t, sin_t, attn_sinks[l],
                   _pick_block(s // WINDOW, 8))

        w_router, b_router = _pack_router(w_group_router[l], b_group_router[l],
                                          w_expert_router[l], b_expert_router[l])
        x1t, meta, counts = _outproj(x2d, ml.reshape(t, -1), att.reshape(t, -1), w_out[l].astype(BF16),
                                     ln1_w[l][None, :], ln1_b[l][None, :], w_router, b_router, alpha, tm)

        pos2d, info = _rank(meta, counts, tb_rank, n_tiles_pad)
        pos = pos2d[0]
        xs = _dispatch(pos, info[:, 0], x1t, n_tiles_pad * MOE_TILE, d // LANES, tb_dma)
        w_gate_up = jnp.concatenate([w_exp_gate[l], w_exp_up[l]], axis=-1).astype(BF16)
        ys = _moe(info, xs, w_gate_up, w_exp_down[l].astype(BF16), d, n_tiles_pad)
        out = _collect(pos, ys, x1t, meta, ln2_w[l][None, :], ln2_b[l][None, :], alpha, d, tb_col)
        x = out.reshape(b, s, d)
    return x
```

```python
import functools
import math

import numpy as np
import jax
import jax.numpy as jnp
from jax import lax
from jax.experimental import pallas as pl
from jax.experimental.pallas import tpu as pltpu

F32 = jnp.float32
BF16 = jnp.bfloat16
I32 = jnp.int32

ML_HEADS = 4
ML_HEAD_DIM = 128
ML_WIDTH = ML_HEADS * ML_HEAD_DIM
ML_CHUNK = 128
CONV_WIDTH = 4
ATT_Q_HEADS = 8
ATT_KV_HEADS = 2
ATT_HEAD_DIM = 64
ATT_WIDTH = ATT_Q_HEADS * ATT_HEAD_DIM
ATT_KV_WIDTH = ATT_KV_HEADS * ATT_HEAD_DIM
WINDOW = 128
ROPE_THETA = 10000.0
N_GROUPS = 4
EXPERTS_PER_GROUP = 8
N_EXPERTS = N_GROUPS * EXPERTS_PER_GROUP
PAIRS_PER_GROUP = EXPERTS_PER_GROUP * (EXPERTS_PER_GROUP - 1) // 2
N_CLASSES = N_GROUPS * PAIRS_PER_GROUP
LN_EPS = 1e-5

LANES = 128
SUBLANES = 8
MOE_TILE = 320
VMEM_LIMIT = 56 * 1024 * 1024

NEG_INF = float("-inf")


def _sigmoid(x):
    return 1.0 / (1.0 + jnp.exp(-x))


def _log_sigmoid(x):
    return jnp.minimum(x, 0.0) - jnp.log(1.0 + jnp.exp(-jnp.abs(x)))


def _iota(shape, dim):
    return lax.broadcasted_iota(I32, shape, dim)


def _dot(a, b):
    return jnp.dot(a, b, preferred_element_type=F32)


def _dot_exact(a, b):
    return jnp.dot(a, b, preferred_element_type=F32, precision=lax.Precision.HIGHEST)


def _layer_norm(z, w, b):
    mu = jnp.mean(z, axis=-1, keepdims=True)
    zc = z - mu
    var = jnp.mean(zc * zc, axis=-1, keepdims=True)
    return zc * lax.rsqrt(var + LN_EPS) * w + b


C_QK = 0
C_V = C_QK + 2 * ML_WIDTH
C_O = C_V + ML_WIDTH
C_AQ = C_O + ML_WIDTH
C_AK = C_AQ + ATT_WIDTH
C_AV = C_AK + ATT_KV_HEADS * LANES
C_G = C_AV + ATT_KV_HEADS * LANES
C_END = C_G + 2 * LANES


def _pack_w_in(w_in):
    sizes = (2 * ML_WIDTH, ML_WIDTH, ML_WIDTH, ML_HEADS, ML_HEADS, ATT_WIDTH, ATT_KV_WIDTH, ATT_KV_WIDTH)
    splits = np.cumsum(sizes)[:-1].tolist()
    w_qk, w_v, w_o, w_i, w_f, w_aq, w_ak, w_av = jnp.split(w_in, splits, axis=-1)
    half = ATT_HEAD_DIM // 2

    def head(w, h):
        return w[:, h * ATT_HEAD_DIM:(h + 1) * ATT_HEAD_DIM]

    def q_tile(a, b):
        return [a[:, :half], b[:, :half], a[:, half:], b[:, half:]]

    q_cols = [t for p in range(ATT_Q_HEADS // 2) for t in q_tile(head(w_aq, 2 * p), head(w_aq, 2 * p + 1))]
    k_cols = [t for h in range(ATT_KV_HEADS) for t in q_tile(head(w_ak, h), head(w_ak, h))]
    v_cols = [t for h in range(ATT_KV_HEADS) for t in (head(w_av, h), head(w_av, h))]
    lane_pad = jnp.zeros((w_in.shape[0], LANES - ML_HEADS), w_in.dtype)
    packed = jnp.concatenate([w_qk, w_v, w_o] + q_cols + k_cols + v_cols + [w_i, lane_pad, w_f, lane_pad], axis=-1)
    return packed.astype(BF16)


def _inproj_kernel(x_ref, w_ref, cw_ref, cb_ref, q_ref, k_ref, v_ref, og_ref, aq_ref, ak_ref, av_ref, g_ref,
                   *scratch, blocks_per_seq):
    *ext_refs, xb_ref = scratch
    tm = x_ref.shape[0]
    halo = SUBLANES
    cs = ext_refs[0].shape[1]
    xb_ref[...] = x_ref[...].astype(BF16)

    @pl.when(pl.program_id(0) % blocks_per_seq == 0)
    def _():
        for ext_ref in ext_refs:
            ext_ref[0:halo, :] = jnp.zeros((halo, cs), F32)

    scale = ML_HEAD_DIM ** -0.5

    def mm(lo, hi):
        return _dot(xb_ref[...], w_ref[:, lo:hi])

    def conv_slice(idx):
        ext_ref = ext_refs[idx]
        c0 = idx * cs
        is_q = c0 < ML_WIDTH
        dst_ref, off = (q_ref, c0) if is_q else (k_ref, c0 - ML_WIDTH)
        rt = ML_CHUNK
        for r0 in range(0, tm, rt):
            conv = cb_ref[:, c0:c0 + cs]
            for j in range(CONV_WIDTH):
                start = halo + r0 - (CONV_WIDTH - 1) + j
                conv = conv + cw_ref[j:j + 1, c0:c0 + cs] * ext_ref[start:start + rt, :]
            act = conv * _sigmoid(conv)
            dst_ref[r0:r0 + rt, off:off + cs] = (act if is_q else act * scale).astype(BF16)
        ext_ref[0:halo, :] = ext_ref[tm:tm + halo, :]

    for idx in range(len(ext_refs)):
        ext_refs[idx][halo:halo + tm, :] = mm(C_QK + idx * cs, C_QK + (idx + 1) * cs)
        if idx > 0:
            conv_slice(idx - 1)
    half_v = ML_WIDTH // 2
    v_ref[:, 0:half_v] = mm(C_V, C_V + half_v).astype(BF16)
    conv_slice(len(ext_refs) - 1)
    v_ref[:, half_v:] = mm(C_V + half_v, C_O).astype(BF16)
    og_ref[...] = _sigmoid(mm(C_O, C_AQ)).astype(BF16)
    aq_ref[...] = mm(C_AQ, C_AK)
    ak_ref[...] = mm(C_AK, C_AV)
    av_ref[...] = mm(C_AV, C_G).astype(BF16)
    g_ref[...] = mm(C_G, C_END)


def _inproj(x2d, w_packed, conv_w, conv_b, tm, blocks_per_seq):
    t, d = x2d.shape
    widths = (ML_WIDTH, ML_WIDTH, ML_WIDTH, ML_WIDTH, C_AK - C_AQ, C_AV - C_AK, C_G - C_AV, C_END - C_G)
    dtypes = (BF16, BF16, BF16, BF16, F32, F32, BF16, F32)
    kern = functools.partial(_inproj_kernel, blocks_per_seq=blocks_per_seq)
    return pl.pallas_call(
        kern,
        grid=(t // tm,),
        in_specs=[pl.BlockSpec((tm, d), lambda i: (i, 0)),
                  pl.BlockSpec((d, C_END), lambda i: (0, 0)),
                  pl.BlockSpec((CONV_WIDTH, 2 * ML_WIDTH), lambda i: (0, 0)),
                  pl.BlockSpec((1, 2 * ML_WIDTH), lambda i: (0, 0))],
        out_specs=[pl.BlockSpec((tm, w), lambda i: (i, 0)) for w in widths],
        out_shape=[jax.ShapeDtypeStruct((t, w), dt) for w, dt in zip(widths, dtypes)],
        scratch_shapes=[pltpu.VMEM((tm + SUBLANES, 2 * LANES), F32)] * (2 * ML_WIDTH // (2 * LANES))
        + [pltpu.VMEM((tm, d), BF16)],
        compiler_params=pltpu.CompilerParams(dimension_semantics=("arbitrary",),
                                             vmem_limit_bytes=VMEM_LIMIT),
        name="inproj",
    )(x2d, w_packed, conv_w, conv_b)


def _time_scan(x, combine, identity):
    row = _iota(x.shape, 0)
    sh = 1
    while sh < x.shape[0]:
        x = combine(x, jnp.where(row >= sh, pltpu.roll(x, sh, 0), identity))
        sh *= 2
    return x


def _mlstm_kernel(q_ref, k_ref, v_ref, og_ref, g_ref, gb_ref, nw_ref, out_ref, ct_ref, m_ref, *, tq):
    s_idx = pl.program_id(1)
    L = ML_CHUNK
    D = ML_HEAD_DIM
    H = ML_HEADS
    heads = range(H)

    @pl.when(s_idx == 0)
    def _():
        ct_ref[...] = jnp.zeros_like(ct_ref)
        m_ref[...] = jnp.zeros_like(m_ref)

    causal = _iota((L, L), 1) <= _iota((L, L), 0)
    ones_blk = jnp.ones((L, D), BF16)
    mean_blk = jnp.full((D, D), 1.0 / D, BF16)
    m_prev = m_ref[0:1, :]
    head_lanes = _iota((L, LANES), 1) < H
    tile_of_lane = jnp.right_shift(_iota((LANES, H * L), 1), L.bit_length() - 1)
    spread = jnp.where(_iota((LANES, H * L), 0) == tile_of_lane, 1.0, 0.0).astype(BF16)

    def spread_heads(x):
        x = jnp.where(head_lanes, x, 0.0)
        hi = x.astype(BF16)
        lo = (x - hi.astype(F32)).astype(BF16)
        return _dot(hi, spread) + _dot(lo, spread)

    for c in range(tq // L):
        r0 = c * L
        gi = g_ref[r0:r0 + L, 0:LANES] + gb_ref[:, 0:LANES]
        gf = g_ref[r0:r0 + L, LANES:2 * LANES] + gb_ref[:, LANES:2 * LANES]
        b_cum = _time_scan(_log_sigmoid(gf), jnp.add, 0.0)
        r = gi - b_cum
        g = jnp.maximum(m_prev, _time_scan(r, jnp.maximum, NEG_INF))
        g_rep = spread_heads(g)
        b_rep = spread_heads(b_cum)
        b_last = b_cum[L - 1:L, :]
        m_new = jnp.maximum(b_last + m_prev, jnp.max(b_last + r, axis=0, keepdims=True))
        decay = jnp.exp(b_last + m_prev - m_new)
        shift = b_last - m_new
        r_t = r.T

        q_b = [q_ref[r0:r0 + L, h * D:(h + 1) * D] for h in heads]
        kt_b = [k_ref[r0:r0 + L, h * D:(h + 1) * D].T for h in heads]
        v_aug = [jnp.concatenate([v_ref[r0:r0 + L, h * D:(h + 1) * D], ones_blk], axis=-1) for h in heads]
        g_col = [g_rep[:, h * L:(h + 1) * L] for h in heads]
        w_intra = [jnp.exp(jnp.where(causal, r_t[h:h + 1, :] - g_col[h], NEG_INF)) for h in heads]
        s_b = [(_dot(q_b[h], kt_b[h]) * w_intra[h]).astype(BF16) for h in heads]
        ct = [ct_ref[h] for h in heads]
        inter = [_dot(q_b[h], ct[h].astype(BF16)) for h in heads]
        intra = [_dot(s_b[h], v_aug[h]) for h in heads]
        for h in heads:
            wi_col = jnp.exp(m_prev[:, h:h + 1] - g_col[h])
            clamp = jnp.exp(-(b_rep[:, h * L:(h + 1) * L] + g_col[h]))
            num = wi_col * inter[h][:, 0:D] + intra[h][:, 0:D]
            den = wi_col * inter[h][:, D:] + intra[h][:, D:]
            hh = num / jnp.maximum(jnp.abs(den), clamp)
            mu = _dot(hh.astype(BF16), mean_blk)
            hc = hh - mu
            var = _dot((hc * hc).astype(BF16), mean_blk)
            hn = hc * lax.rsqrt(var + LN_EPS) * nw_ref[:, h * D:(h + 1) * D]
            gate_o = og_ref[r0:r0 + L, h * D:(h + 1) * D].astype(F32)
            out_ref[r0:r0 + L, h * D:(h + 1) * D] = (gate_o * hn).astype(out_ref.dtype)
        for h in heads:
            w_row = jnp.exp(r_t[h:h + 1, :] + shift[:, h:h + 1])
            ktw = (kt_b[h].astype(F32) * w_row).astype(BF16)
            ct_ref[h] = decay[:, h:h + 1] * ct[h] + _dot(ktw, v_aug[h])
        m_prev = m_new

    m_ref[...] = jnp.broadcast_to(m_prev, m_ref.shape)


def _mlstm(q, k, v, og, g, gate_bias_row, norm_w_row, tq):
    b, s, _ = q.shape
    kern = functools.partial(_mlstm_kernel, tq=tq)

    def seq_spec(width):
        return pl.BlockSpec((None, tq, width), lambda bi, si: (bi, si, 0))

    def const_spec(shape):
        return pl.BlockSpec(shape, lambda bi, si: (0,) * len(shape))

    return pl.pallas_call(
        kern,
        grid=(b, s // tq),
        in_specs=[seq_spec(ML_WIDTH), seq_spec(ML_WIDTH), seq_spec(ML_WIDTH), seq_spec(ML_WIDTH),
                  seq_spec(2 * LANES), const_spec((1, 2 * LANES)), const_spec((1, ML_WIDTH))],
        out_specs=seq_spec(ML_WIDTH),
        out_shape=jax.ShapeDtypeStruct((b, s, ML_WIDTH), BF16),
        scratch_shapes=[pltpu.VMEM((ML_HEADS, ML_HEAD_DIM, 2 * ML_HEAD_DIM), F32),
                        pltpu.VMEM((SUBLANES, LANES), F32)],
        compiler_params=pltpu.CompilerParams(dimension_semantics=("arbitrary", "arbitrary"),
                                             vmem_limit_bytes=VMEM_LIMIT),
        name="mlstm",
    )(q, k, v, og, g, gate_bias_row, norm_w_row)


def _rope_tables(seq_len):
    half = ATT_HEAD_DIM // 2
    inv_freq = ROPE_THETA ** (-jnp.arange(half, dtype=F32) / half)
    ang = jnp.arange(seq_len, dtype=F32)[:, None] * inv_freq[None, :]
    cos = jnp.cos(ang)
    sin = jnp.sin(ang)
    cos_t = jnp.concatenate([cos, cos, cos, cos], axis=-1)
    sin_t = jnp.concatenate([-sin, -sin, sin, sin], axis=-1)
    return cos_t, sin_t


def _swa_kernel(sink_ref, aq_ref, ak_ref, av_ref, cos_ref, sin_ref, out_ref, kprev_ref, vprev_ref, *, nsub):
    step = pl.program_id(1)
    Lb = WINDOW
    half = ATT_HEAD_DIM // 2
    pairs = ATT_Q_HEADS // 2
    pairs_per_kv = pairs // ATT_KV_HEADS

    @pl.when(step == 0)
    def _():
        kprev_ref[...] = jnp.zeros_like(kprev_ref)
        vprev_ref[...] = jnp.zeros_like(vprev_ref)

    def rope(x, cos, sin):
        tiles = []
        for c in range(x.shape[-1] // LANES):
            xt = x[:, c * LANES:(c + 1) * LANES]
            tiles.append(xt * cos + pltpu.roll(xt, LANES // 2, 1) * sin)
        return jnp.concatenate(tiles, axis=-1)

    ql = _iota((Lb, 2 * Lb), 0)
    kj = _iota((Lb, 2 * Lb), 1)
    diff = Lb + ql - kj
    lane = _iota((Lb, LANES), 1)
    low_half = lane < ATT_HEAD_DIM
    first_head = (lane & half) == 0
    ones_blk = jnp.ones((2 * Lb, LANES), BF16)

    k_prev = kprev_ref
    v_prev = vprev_ref

    def block(j, carry):
        rows_j = pl.ds(pl.multiple_of(j * Lb, Lb), Lb)
        cos = cos_ref[rows_j, :]
        sin = sin_ref[rows_j, :]
        q = rope(aq_ref[rows_j, :], cos, sin) * (ATT_HEAD_DIM ** -0.5)
        k_cur = rope(ak_ref[rows_j, :], cos, sin).astype(BF16)
        v_cur = av_ref[rows_j, :]

        kpos = (step * nsub + j) * Lb + kj - Lb
        visible = jnp.where(diff >= 0, jnp.where(diff < WINDOW, jnp.where(kpos >= 0, 1, 0), 0), 0)
        bias = jnp.where(visible > 0, 0.0, NEG_INF).astype(F32)
        bias = jnp.concatenate([bias] * (2 * pairs_per_kv), axis=0)

        for g in range(ATT_KV_HEADS):
            kk = jnp.concatenate([k_prev[:, g * LANES:(g + 1) * LANES], k_cur[:, g * LANES:(g + 1) * LANES]], axis=0)
            vv = jnp.concatenate([v_prev[:, g * LANES:(g + 1) * LANES], v_cur[:, g * LANES:(g + 1) * LANES]], axis=0)
            vv_aug = jnp.concatenate([vv, ones_blk], axis=-1)
            rows = []
            sinks = []
            for p in range(pairs_per_kv):
                pair = g * pairs_per_kv + p
                q2 = q[:, pair * LANES:(pair + 1) * LANES]
                rows.append(jnp.where(first_head, q2, 0.0))
                rows.append(jnp.where(first_head, 0.0, q2))
                sinks.append(jnp.full((Lb, LANES), sink_ref[2 * pair], F32))
                sinks.append(jnp.full((Lb, LANES), sink_ref[2 * pair + 1], F32))
            qs = jnp.concatenate(rows, axis=0).astype(BF16)
            sink = jnp.concatenate(sinks, axis=0)

            sc = lax.dot_general(qs, kk, (((1,), (1,)), ((), ())), preferred_element_type=F32) + bias
            m = jnp.maximum(jnp.broadcast_to(jnp.max(sc, axis=-1, keepdims=True), sink.shape), sink)
            p_un = jnp.exp(sc - jnp.concatenate([m, m], axis=-1))
            acc = _dot(p_un.astype(BF16), vv_aug)
            o = acc[:, 0:LANES] / (acc[:, LANES:] + jnp.exp(sink - m))
            for p in range(pairs_per_kv):
                pair = g * pairs_per_kv + p
                even = o[(2 * p) * Lb:(2 * p + 1) * Lb, :]
                odd = o[(2 * p + 1) * Lb:(2 * p + 2) * Lb, :]
                out_ref[rows_j, pair * LANES:(pair + 1) * LANES] = (
                    jnp.where(low_half, even, odd).astype(out_ref.dtype))
        kprev_ref[...] = k_cur
        vprev_ref[...] = v_cur
        return carry

    lax.fori_loop(0, nsub, block, 0)


def _swa(aq, ak, av, cos_t, sin_t, sinks, nsub):
    b, s, _ = aq.shape
    kvw = ATT_KV_HEADS * LANES
    rows = nsub * WINDOW

    def seq_spec(width):
        return pl.BlockSpec((None, rows, width), lambda bi, si: (bi, si, 0))

    tab_spec = pl.BlockSpec((rows, LANES), lambda bi, si: (si, 0))
    return pl.pallas_call(
        functools.partial(_swa_kernel, nsub=nsub),
        grid=(b, s // rows),
        in_specs=[pl.BlockSpec(memory_space=pltpu.SMEM),
                  seq_spec(ATT_WIDTH), seq_spec(kvw), seq_spec(kvw), tab_spec, tab_spec],
        out_specs=seq_spec(ATT_WIDTH),
        out_shape=jax.ShapeDtypeStruct((b, s, ATT_WIDTH), BF16),
        scratch_shapes=[pltpu.VMEM((WINDOW, kvw), BF16), pltpu.VMEM((WINDOW, kvw), BF16)],
        compiler_params=pltpu.CompilerParams(dimension_semantics=("arbitrary", "arbitrary"),
                                             vmem_limit_bytes=VMEM_LIMIT),
        name="swa",
    )(sinks, aq, ak, av, cos_t, sin_t)


def _mixers_kernel(sink_ref, q_ref, k_ref, v_ref, og_ref, g_ref, gb_ref, nw_ref, aq_ref, ak_ref, av_ref,
                   cos_ref, sin_ref, ml_ref, att_ref, ct_ref, m_ref, kprev_ref, vprev_ref, *, tq, nsub):
    _swa_kernel(sink_ref, aq_ref, ak_ref, av_ref, cos_ref, sin_ref, att_ref, kprev_ref, vprev_ref, nsub=nsub)
    _mlstm_kernel(q_ref, k_ref, v_ref, og_ref, g_ref, gb_ref, nw_ref, ml_ref, ct_ref, m_ref, tq=tq)


def _mixers(q, k, v, og, g, gate_bias_row, norm_w_row, aq, ak, av, cos_t, sin_t, sinks, tq):
    b, s, _ = q.shape
    kvw = ATT_KV_HEADS * LANES
    assert tq % WINDOW == 0
    kern = functools.partial(_mixers_kernel, tq=tq, nsub=tq // WINDOW)

    def seq_spec(width):
        return pl.BlockSpec((None, tq, width), lambda bi, si: (bi, si, 0))

    def const_spec(shape):
        return pl.BlockSpec(shape, lambda bi, si: (0,) * len(shape))

    tab_spec = pl.BlockSpec((tq, LANES), lambda bi, si: (si, 0))
    return pl.pallas_call(
        kern,
        grid=(b, s // tq),
        in_specs=[pl.BlockSpec(memory_space=pltpu.SMEM),
                  seq_spec(ML_WIDTH), seq_spec(ML_WIDTH), seq_spec(ML_WIDTH), seq_spec(ML_WIDTH),
                  seq_spec(2 * LANES), const_spec((1, 2 * LANES)), const_spec((1, ML_WIDTH)),
                  seq_spec(ATT_WIDTH), seq_spec(kvw), seq_spec(kvw), tab_spec, tab_spec],
        out_specs=[seq_spec(ML_WIDTH), seq_spec(ATT_WIDTH)],
        out_shape=[jax.ShapeDtypeStruct((b, s, ML_WIDTH), BF16), jax.ShapeDtypeStruct((b, s, ATT_WIDTH), BF16)],
        scratch_shapes=[pltpu.VMEM((ML_HEADS, ML_HEAD_DIM, 2 * ML_HEAD_DIM), F32),
                        pltpu.VMEM((SUBLANES, LANES), F32),
                        pltpu.VMEM((WINDOW, kvw), BF16), pltpu.VMEM((WINDOW, kvw), BF16)],
        compiler_params=pltpu.CompilerParams(dimension_semantics=("arbitrary", "arbitrary"),
                                             vmem_limit_bytes=VMEM_LIMIT),
        name="mixers",
    )(sinks, q, k, v, og, g, gate_bias_row, norm_w_row, aq, ak, av, cos_t, sin_t)


def _store_token_tiles(ref, val, row0=0, rows_per_token=None):
    n, w = val.shape
    segs = w // LANES
    rpt = rows_per_token or segs
    for j in range(segs):
        ref[pl.ds(row0 + j, n, stride=rpt), :] = val[:, j * LANES:(j + 1) * LANES]


def _load_token_tiles(ref, n, segs, row0=0, rows_per_token=None):
    rpt = rows_per_token or segs
    return jnp.concatenate([ref[pl.ds(row0 + j, n, stride=rpt), :] for j in range(segs)], axis=-1)


ROUTER_ROWS = 48


def _pack_router(w_group, b_group, w_expert, b_expert):
    d = w_group.shape[0]
    wt = jnp.zeros((ROUTER_ROWS, d), F32)
    wt = wt.at[0:N_GROUPS].set(w_group.T).at[SUBLANES:SUBLANES + N_EXPERTS].set(w_expert.T)
    bias = jnp.zeros((ROUTER_ROWS,), F32)
    bias = bias.at[0:N_GROUPS].set(b_group).at[SUBLANES:SUBLANES + N_EXPERTS].set(b_expert)
    hi = wt.astype(BF16)
    lo = (wt - hi.astype(F32)).astype(BF16)
    return jnp.concatenate([hi, lo], axis=0), jnp.broadcast_to(bias[:, None], (ROUTER_ROWS, LANES))


def _outproj_kernel(x_ref, ml_ref, att_ref, wo_ref, lnw_ref, lnb_ref, wrt_ref, brt_ref,
                    x1t_ref, meta_ref, cnt_ref, *, alpha):
    step = pl.program_id(0)
    tm, d = x_ref.shape
    segs = d // LANES
    pt = tm
    rr = wrt_ref.shape[0] // 2
    nt = (((1,), (1,)), ((), ()))
    row = _iota((SUBLANES, pt), 0).astype(F32)
    lane = _iota((pt, LANES), 1).astype(F32)

    def first_argmax(vals):
        top = jnp.max(vals, axis=0, keepdims=True)
        idx = jnp.min(jnp.where(vals == top, row, float(SUBLANES)), axis=0, keepdims=True)
        return top, idx

    @pl.when(step == 0)
    def _():
        cnt_ref[...] = jnp.zeros_like(cnt_ref)

    counts = jnp.zeros((1, LANES), F32)
    for r0 in range(0, tm, pt):
        y = (_dot(ml_ref[r0:r0 + pt, :], wo_ref[0:ML_WIDTH, :])
             + _dot(att_ref[r0:r0 + pt, :], wo_ref[ML_WIDTH:, :]))
        x1 = _layer_norm(alpha * x_ref[r0:r0 + pt, :] + y, lnw_ref[...], lnb_ref[...])
        _store_token_tiles(x1t_ref.at[pl.ds(r0 * segs, pt * segs)], x1)

        x1_hi = x1.astype(BF16)
        x1_lo = (x1 - x1_hi.astype(F32)).astype(BF16)
        both = lax.dot_general(wrt_ref[...], x1_hi, nt, preferred_element_type=F32)
        cross = lax.dot_general(wrt_ref[0:rr, :], x1_lo, nt, preferred_element_type=F32)
        logits = both[0:rr] + both[rr:2 * rr] + cross + jnp.concatenate([brt_ref[...]] * (pt // LANES), axis=1)

        g_logits = jnp.where(row < N_GROUPS, logits[0:SUBLANES], NEG_INF)
        g_top, g_idx = first_argmax(g_logits)
        g_p = 1.0 / jnp.sum(jnp.exp(g_logits - g_top), axis=0, keepdims=True)

        e_logits = logits[SUBLANES:2 * SUBLANES]
        for grp in range(1, N_GROUPS):
            e_logits = jnp.where(g_idx == grp, logits[(1 + grp) * SUBLANES:(2 + grp) * SUBLANES], e_logits)
        v1, a1 = first_argmax(e_logits)
        v2, a2 = first_argmax(jnp.where(row == a1, NEG_INF, e_logits))
        r = jnp.exp(v2 - v1)
        w1 = g_p / (1.0 + r)
        w2 = g_p * r / (1.0 + r)

        lo = jnp.minimum(a1, a2)
        hi = jnp.maximum(a1, a2)
        w_lo = jnp.where(a1 < a2, w1, w2)
        w_hi = jnp.where(a1 < a2, w2, w1)
        pair_idx = (EXPERTS_PER_GROUP - 1) * lo - lo * (lo - 1.0) * 0.5 + (hi - lo - 1.0)
        cls = g_idx * PAIRS_PER_GROUP + pair_idx

        meta_t = jnp.where(row == 0.0, cls, jnp.where(row == 1.0, w_lo, jnp.where(row == 2.0, w_hi, 0.0)))
        meta = jnp.concatenate([meta_t, jnp.zeros((LANES - SUBLANES, pt), F32)], axis=0).T
        meta_ref[r0:r0 + pt, :] = meta
        counts = counts + jnp.sum(jnp.where(lane == meta[:, 0:1], 1.0, 0.0), axis=0, keepdims=True)

    cnt_ref[0:1, :] += counts


def _outproj(x2d, ml2d, att2d, w_out_b, ln_w, ln_b, w_router, b_router, alpha, tm):
    t, d = x2d.shape
    kern = functools.partial(_outproj_kernel, alpha=alpha)

    def const_spec(shape):
        return pl.BlockSpec(shape, lambda i: (0,) * len(shape))

    return pl.pallas_call(
        kern,
        grid=(t // tm,),
        in_specs=[pl.BlockSpec((tm, d), lambda i: (i, 0)),
                  pl.BlockSpec((tm, ML_WIDTH), lambda i: (i, 0)),
                  pl.BlockSpec((tm, ATT_WIDTH), lambda i: (i, 0)),
                  const_spec((ML_WIDTH + ATT_WIDTH, d)), const_spec((1, d)), const_spec((1, d)),
                  const_spec((2 * ROUTER_ROWS, d)), const_spec((ROUTER_ROWS, LANES))],
        out_specs=[pl.BlockSpec((tm * (d // LANES), LANES), lambda i: (i, 0)),
                   pl.BlockSpec((tm, LANES), lambda i: (i, 0)), const_spec((SUBLANES, LANES))],
        out_shape=[jax.ShapeDtypeStruct((t * (d // LANES), LANES), F32),
                   jax.ShapeDtypeStruct((t, LANES), F32),
                   jax.ShapeDtypeStruct((SUBLANES, LANES), F32)],
        compiler_params=pltpu.CompilerParams(dimension_semantics=("arbitrary",),
                                             vmem_limit_bytes=VMEM_LIMIT),
        name="outproj",
    )(x2d, ml2d, att2d, w_out_b, ln_w, ln_b, w_router, b_router)


def _class_expert_table():
    tab = np.zeros((SUBLANES, LANES), np.float32)
    for g in range(N_GROUPS):
        idx = 0
        for lo in range(EXPERTS_PER_GROUP):
            for hi in range(lo + 1, EXPERTS_PER_GROUP):
                c = g * PAIRS_PER_GROUP + idx
                tab[0, c] = g * EXPERTS_PER_GROUP + lo
                tab[1, c] = g * EXPERTS_PER_GROUP + hi
                idx += 1
    return tab


def _rank_kernel(meta_ref, cnt_ref, tab_ref, pos_ref, tile_ref, base_ref, run_ref, *, tb, n_tiles_pad):
    step = pl.program_id(0)
    lane8 = _iota((SUBLANES, LANES), 1)

    @pl.when(step == 0)
    def _():
        cnt = jnp.broadcast_to(cnt_ref[0:1, :], (SUBLANES, LANES))
        tiles = jnp.floor((cnt + (MOE_TILE - 1.0)) * (1.0 / MOE_TILE))
        cum = tiles
        sh = 1
        while sh < LANES:
            cum = cum + jnp.where(lane8 >= sh, pltpu.roll(cum, sh, 1), 0.0)
            sh *= 2
        excl = cum - tiles
        base_ref[...] = excl * MOE_TILE
        run_ref[...] = jnp.zeros_like(run_ref)

        ti = _iota((n_tiles_pad, LANES), 0).astype(F32)
        lane = _iota((n_tiles_pad, LANES), 1)
        done = jnp.where(lane < N_CLASSES, jnp.where(cum[0:1, :] <= ti, 1.0, 0.0), 0.0)
        t_cls = jnp.sum(done, axis=-1, keepdims=True)
        sel = jnp.where(lane.astype(F32) == t_cls, 1.0, 0.0)
        cnt_i = jnp.sum(sel * cnt[0:1, :], axis=-1, keepdims=True)
        first_i = jnp.sum(sel * excl[0:1, :], axis=-1, keepdims=True)
        rows_i = jnp.clip(cnt_i - MOE_TILE * (ti[:, 0:1] - first_i), 0.0, float(MOE_TILE))
        e_lo = jnp.sum(sel * tab_ref[0:1, :], axis=-1, keepdims=True)
        e_hi = jnp.sum(sel * tab_ref[1:2, :], axis=-1, keepdims=True)
        n_tiles = jnp.sum(jnp.where(lane < N_CLASSES, jnp.broadcast_to(tiles[0:1, :], (n_tiles_pad, LANES)), 0.0),
                          axis=-1, keepdims=True)
        info = jnp.where(lane == 0, rows_i,
                         jnp.where(lane == 1, e_lo, jnp.where(lane == 2, e_hi, jnp.where(lane == 3, n_tiles, 0.0))))
        tile_ref[...] = info.astype(I32)

    cls = meta_ref[:, 0:1]
    lane = _iota((tb, LANES), 1).astype(F32)
    onehot = jnp.where(lane == cls, 1.0, 0.0)
    strict_lower = jnp.where(_iota((tb, tb), 1) < _iota((tb, tb), 0), 1.0, 0.0).astype(BF16)
    before = _dot(strict_lower, onehot.astype(BF16))
    slot = jnp.sum(onehot * (before + run_ref[0:1, :] + base_ref[0:1, :]), axis=-1, keepdims=True)
    run_ref[...] = run_ref[...] + jnp.sum(onehot, axis=0, keepdims=True)
    slot_t = jnp.broadcast_to(slot, (tb, LANES)).T
    pos_ref[...] = slot_t[0:SUBLANES, :].astype(I32)


def _rank(meta, counts, tb, n_tiles_pad):
    t = meta.shape[0]
    kern = functools.partial(_rank_kernel, tb=tb, n_tiles_pad=n_tiles_pad)
    tab = jnp.asarray(_class_expert_table())
    return pl.pallas_call(
        kern,
        grid=(t // tb,),
        in_specs=[pl.BlockSpec((tb, LANES), lambda i: (i, 0)),
                  pl.BlockSpec((SUBLANES, LANES), lambda i: (0, 0)),
                  pl.BlockSpec((SUBLANES, LANES), lambda i: (0, 0))],
        out_specs=[pl.BlockSpec((SUBLANES, tb), lambda i: (0, i)),
                   pl.BlockSpec((n_tiles_pad, LANES), lambda i: (0, 0))],
        out_shape=[jax.ShapeDtypeStruct((SUBLANES, t), I32),
                   jax.ShapeDtypeStruct((n_tiles_pad, LANES), I32)],
        scratch_shapes=[pltpu.VMEM((SUBLANES, LANES), F32), pltpu.VMEM((SUBLANES, LANES), F32)],
        compiler_params=pltpu.CompilerParams(dimension_semantics=("arbitrary",),
                                             vmem_limit_bytes=VMEM_LIMIT),
        name="rank",
    )(meta, counts, tab)


DMA_UNROLL = 8


def _pad_chunks():
    size, out = 1, []
    while size <= MOE_TILE:
        out.append(size)
        size *= 2
    return out[::-1]


def _dispatch_kernel(pos_ref, rows_ref, src_ref, dst_ref, zero_ref, sems, *, tb, rpt, n_tiles):
    step = pl.program_id(0)
    base = step * tb

    def pad_copies(act):
        def one_tile(i, carry):
            valid = rows_ref[i]
            pad = MOE_TILE - valid
            off = i * MOE_TILE + valid
            for size in _pad_chunks():
                hit = (pad & size) != 0

                @pl.when(hit)
                def _():
                    act(pltpu.make_async_copy(zero_ref.at[pl.ds(0, size * rpt)],
                                              dst_ref.at[pl.ds(pl.multiple_of(off * rpt, rpt), size * rpt)],
                                              sems.at[1]))
                off = off + jnp.where(hit, size, 0)
            return carry
        lax.fori_loop(0, n_tiles, one_tile, 0)

    @pl.when(step == 0)
    def _():
        zero_ref[...] = jnp.zeros_like(zero_ref)
        pad_copies(lambda cp: cp.start())

    def issue(grp, carry):
        for u in range(DMA_UNROLL):
            j = grp * DMA_UNROLL + u
            src = src_ref.at[pl.ds(pl.multiple_of(j * rpt, rpt), rpt)]
            dst = dst_ref.at[pl.ds(pl.multiple_of(pos_ref[base + j] * rpt, rpt), rpt)]
            pltpu.make_async_copy(src, dst, sems.at[0]).start(priority=u % 2)
        return carry

    lax.fori_loop(0, tb // DMA_UNROLL, issue, 0)
    pltpu.make_async_copy(src_ref, dst_ref.at[pl.ds(0, tb * rpt)], sems.at[0]).wait()

    @pl.when(step == pl.num_programs(0) - 1)
    def _():
        pad_copies(lambda cp: cp.wait())


def _dispatch(pos, tile_rows, src, n_slots, rpt, tb):
    t = pos.shape[0]
    n_tiles = tile_rows.shape[0]
    kern = functools.partial(_dispatch_kernel, tb=tb, rpt=rpt, n_tiles=n_tiles)
    pad_rows = _pad_chunks()[0] * rpt
    return pl.pallas_call(
        kern,
        grid_spec=pltpu.PrefetchScalarGridSpec(
            num_scalar_prefetch=2,
            grid=(t // tb,),
            in_specs=[pl.BlockSpec((tb * rpt, LANES), lambda i, pos_ref, rows_ref: (i, 0))],
            out_specs=pl.BlockSpec(memory_space=pl.ANY),
            scratch_shapes=[pltpu.VMEM((pad_rows, LANES), src.dtype), pltpu.SemaphoreType.DMA((2,))],
        ),
        out_shape=jax.ShapeDtypeStruct((n_slots * rpt, LANES), src.dtype),
        compiler_params=pltpu.CompilerParams(dimension_semantics=("arbitrary",),
                                             has_side_effects=True, vmem_limit_bytes=VMEM_LIMIT),
        name="dispatch",
    )(pos, tile_rows, src)


def _collect_kernel(pos_ref, ys_ref, x1t_ref, meta_ref, lnw_ref, lnb_ref, out_ref, buf_ref, sems,
                    *, tb, alpha, d):
    segs = d // LANES
    rpt = 2 * segs
    step = pl.program_id(0)
    n_steps = pl.num_programs(0)

    def gather(blk, slot):
        base = blk * tb

        def issue(grp, carry):
            for u in range(DMA_UNROLL):
                j = grp * DMA_UNROLL + u
                src = ys_ref.at[pl.ds(pos_ref[base + j], 1)]
                dst = buf_ref.at[slot, pl.ds(j, 1)]
                pltpu.make_async_copy(src, dst, sems.at[slot]).start(priority=u % 2)
            return carry

        lax.fori_loop(0, tb // DMA_UNROLL, issue, 0)

    @pl.when(step == 0)
    def _():
        gather(0, 0)

    @pl.when(step + 1 < n_steps)
    def _():
        gather(step + 1, (step + 1) % 2)

    slot = step % 2
    pltpu.make_async_copy(ys_ref.at[pl.ds(0, tb)], buf_ref.at[slot], sems.at[slot]).wait()

    x1 = _load_token_tiles(x1t_ref, tb, segs)
    y_lo = buf_ref[slot, :, 0:d]
    y_hi = buf_ref[slot, :, d:2 * d]
    meta = meta_ref[...]
    z = alpha * x1 + meta[:, 1:2] * y_lo + meta[:, 2:3] * y_hi
    out_ref[...] = _layer_norm(z, lnw_ref[...], lnb_ref[...])


def _collect(pos, ys, x1t, meta, ln_w, ln_b, alpha, d, tb):
    t = pos.shape[0]
    segs = d // LANES
    kern = functools.partial(_collect_kernel, tb=tb, alpha=alpha, d=d)
    return pl.pallas_call(
        kern,
        grid_spec=pltpu.PrefetchScalarGridSpec(
            num_scalar_prefetch=1,
            grid=(t // tb,),
            in_specs=[pl.BlockSpec(memory_space=pl.ANY),
                      pl.BlockSpec((tb * segs, LANES), lambda i, pos_ref: (i, 0)),
                      pl.BlockSpec((tb, LANES), lambda i, pos_ref: (i, 0)),
                      pl.BlockSpec((1, d), lambda i, pos_ref: (0, 0)),
                      pl.BlockSpec((1, d), lambda i, pos_ref: (0, 0))],
            out_specs=pl.BlockSpec((tb, d), lambda i, pos_ref: (i, 0)),
            scratch_shapes=[pltpu.VMEM((2, tb, 2 * d), F32), pltpu.SemaphoreType.DMA((2,))],
        ),
        out_shape=jax.ShapeDtypeStruct((t, d), F32),
        compiler_params=pltpu.CompilerParams(dimension_semantics=("arbitrary",),
                                             vmem_limit_bytes=VMEM_LIMIT),
        name="collect",
    )(pos, ys, x1t, meta, ln_w, ln_b)


MOE_TILES_PER_STEP = 1


def _moe_kernel(rows_ref, elo_ref, ehi_ref, nt_ref, xs_ref, *refs, d):
    *w_refs, ys_ref = refs
    i = pl.program_id(0)
    segs = d // LANES

    @pl.when(rows_ref[i * MOE_TILES_PER_STEP] > 0)
    def _():
        for k in range(MOE_TILES_PER_STEP):
            wgu_lo, wd_lo, wgu_hi, wd_hi = w_refs[4 * k:4 * k + 4]
            xs_k = xs_ref.at[pl.ds(k * MOE_TILE * segs, MOE_TILE * segs)]
            xb = _load_token_tiles(xs_k, MOE_TILE, segs).astype(BF16)

            def expert(wgu_ref, wd_ref):
                de = wd_ref.shape[0]
                gate_up = _dot(xb, wgu_ref[...])
                gate = gate_up[:, 0:de]
                hidden = gate * _sigmoid(gate) * gate_up[:, de:]
                return _dot(hidden.astype(BF16), wd_ref[...])

            ys_ref[k * MOE_TILE:(k + 1) * MOE_TILE, 0:d] = expert(wgu_lo, wd_lo)
            ys_ref[k * MOE_TILE:(k + 1) * MOE_TILE, d:2 * d] = expert(wgu_hi, wd_hi)

    @pl.when(rows_ref[i * MOE_TILES_PER_STEP] <= 0)
    def _():
        ys_ref[...] = jnp.zeros_like(ys_ref)


def _moe(info, xs, wgu, wd, d, n_tiles_pad):
    de = wd.shape[1]
    segs = d // LANES
    tps = MOE_TILES_PER_STEP
    kern = functools.partial(_moe_kernel, d=d)

    def last_live(tile, nt_ref):
        return jnp.minimum(tile, jnp.maximum(nt_ref[0] - 1, 0))

    def up_spec(k, which):
        return pl.BlockSpec((None, d, 2 * de),
                            lambda i, rows, elo, ehi, nt: ((elo, ehi)[which][last_live(i * tps + k, nt)], 0, 0))

    def down_spec(k, which):
        return pl.BlockSpec((None, de, d),
                            lambda i, rows, elo, ehi, nt: ((elo, ehi)[which][last_live(i * tps + k, nt)], 0, 0))

    w_specs = []
    for k in range(tps):
        w_specs += [up_spec(k, 0), down_spec(k, 0), up_spec(k, 1), down_spec(k, 1)]
    rows, elo, ehi, nt = info[:, 0], info[:, 1], info[:, 2], info[0:1, 3]
    return pl.pallas_call(
        kern,
        grid_spec=pltpu.PrefetchScalarGridSpec(
            num_scalar_prefetch=4,
            grid=(n_tiles_pad // tps,),
            in_specs=[pl.BlockSpec((tps * MOE_TILE * segs, LANES),
                                   lambda i, rows, elo, ehi, nt: (last_live(i * tps, nt) // tps, 0))] + w_specs,
            out_specs=pl.BlockSpec((tps * MOE_TILE, 2 * d), lambda i, rows, elo, ehi, nt: (i, 0)),
        ),
        out_shape=jax.ShapeDtypeStruct((n_tiles_pad * MOE_TILE, 2 * d), F32),
        compiler_params=pltpu.CompilerParams(dimension_semantics=("arbitrary",),
                                             vmem_limit_bytes=VMEM_LIMIT),
        name="moe",
    )(rows, elo, ehi, nt, xs, *([wgu, wd, wgu, wd] * tps))


def _pick_block(n, target):
    blk = min(n, target)
    while n % blk:
        blk //= 2
    return blk


def kernel(x, w_in, conv_w, conv_b, mlstm_gate_bias, mlstm_norm_w, attn_sinks, w_out, ln1_w, ln1_b,
           w_group_router, b_group_router, w_expert_router, b_expert_router,
           w_exp_gate, w_exp_up, w_exp_down, ln2_w, ln2_b):
    b, s, d = x.shape
    t = b * s
    depth = w_in.shape[0]
    alpha = (2.0 * depth) ** 0.25
    assert s % ML_CHUNK == 0 and s % WINDOW == 0 and d % LANES == 0

    tm = _pick_block(s, 1024)
    tq = _pick_block(s, 8 * ML_CHUNK)
    tb_rank = _pick_block(t, 512)
    tb_dma = _pick_block(t, 4096)
    tb_col = _pick_block(t, 1024)
    n_tiles_pad = -(-(t // MOE_TILE + N_CLASSES) // SUBLANES) * SUBLANES
    cos_t, sin_t = _rope_tables(s)

    for l in range(depth):
        x2d = x.reshape(t, d)
        q, k, v, og, aq, ak, av, g = _inproj(x2d, _pack_w_in(w_in[l]), conv_w[l], conv_b[l][None, :], tm, s // tm)
        bias_pad = jnp.zeros((LANES - ML_HEADS,), F32)
        gate_bias_row = jnp.concatenate(
            [mlstm_gate_bias[l, 0], bias_pad, mlstm_gate_bias[l, 1], bias_pad])[None, :]
        ml, att = _mixers(q.reshape(b, s, -1), k.reshape(b, s, -1), v.reshape(b, s, -1), og.reshape(b, s, -1),
                          g.reshape(b, s, -1), gate_bias_row, mlstm_norm_w[l][None, :],
                          aq.reshape(b, s, -1), ak.reshape(b, s, -1), av.reshape(b, s, -1), cos_t, sin_t,
                          attn_sinks[l], tq)

        w_router, b_router = _pack_router(w_group_router[l], b_group_router[l],
                                          w_expert_router[l], b_expert_router[l])
        x1t, meta, counts = _outproj(x2d, ml.reshape(t, -1), att.reshape(t, -1), w_out[l].astype(BF16),
                                     ln1_w[l][None, :], ln1_b[l][None, :], w_router, b_router, alpha, tm)

        pos2d, info = _rank(meta, counts, tb_rank, n_tiles_pad)
        pos = pos2d[0]
        xs = _dispatch(pos, info[:, 0], x1t, n_tiles_pad * MOE_TILE, d // LANES, tb_dma)
        w_gate_up = jnp.concatenate([w_exp_gate[l].astype(BF16), w_exp_up[l].astype(BF16)], axis=-1)
        ys = _moe(info, xs, w_gate_up, w_exp_down[l].astype(BF16), d, n_tiles_pad)
        out = _collect(pos, ys, x1t, meta, ln2_w[l][None, :], ln2_b[l][None, :], alpha, d, tb_col)
        x = out.reshape(b, s, d)
    return x
```

```python
import functools
import math

import numpy as np
import jax
import jax.numpy as jnp
from jax import lax
from jax.experimental import pallas as pl
from jax.experimental.pallas import tpu as pltpu

F32 = jnp.float32
BF16 = jnp.bfloat16
I32 = jnp.int32

ML_HEADS = 4
ML_HEAD_DIM = 128
ML_WIDTH = ML_HEADS * ML_HEAD_DIM
ML_CHUNK = 128
CONV_WIDTH = 4
ATT_Q_HEADS = 8
ATT_KV_HEADS = 2
ATT_HEAD_DIM = 64
ATT_WIDTH = ATT_Q_HEADS * ATT_HEAD_DIM
ATT_KV_WIDTH = ATT_KV_HEADS * ATT_HEAD_DIM
WINDOW = 128
ROPE_THETA = 10000.0
N_GROUPS = 4
EXPERTS_PER_GROUP = 8
N_EXPERTS = N_GROUPS * EXPERTS_PER_GROUP
PAIRS_PER_GROUP = EXPERTS_PER_GROUP * (EXPERTS_PER_GROUP - 1) // 2
N_CLASSES = N_GROUPS * PAIRS_PER_GROUP
LN_EPS = 1e-5

LANES = 128
SUBLANES = 8
MOE_TILE = 320
VMEM_LIMIT = 56 * 1024 * 1024

NEG_INF = float("-inf")


def _sigmoid(x):
    return 1.0 / (1.0 + jnp.exp(-x))


def _log_sigmoid(x):
    return jnp.minimum(x, 0.0) - jnp.log(1.0 + jnp.exp(-jnp.abs(x)))


def _iota(shape, dim):
    return lax.broadcasted_iota(I32, shape, dim)


def _dot(a, b):
    return jnp.dot(a, b, preferred_element_type=F32)


def _dot_exact(a, b):
    return jnp.dot(a, b, preferred_element_type=F32, precision=lax.Precision.HIGHEST)


def _layer_norm(z, w, b):
    mu = jnp.mean(z, axis=-1, keepdims=True)
    zc = z - mu
    var = jnp.mean(zc * zc, axis=-1, keepdims=True)
    return zc * lax.rsqrt(var + LN_EPS) * w + b


C_QK = 0
C_V = C_QK + 2 * ML_WIDTH
C_O = C_V + ML_WIDTH
C_AQ = C_O + ML_WIDTH
C_AK = C_AQ + ATT_WIDTH
C_AV = C_AK + ATT_KV_HEADS * LANES
C_G = C_AV + ATT_KV_HEADS * LANES
C_END = C_G + 2 * LANES


def _pack_w_in(w_in):
    sizes = (2 * ML_WIDTH, ML_WIDTH, ML_WIDTH, ML_HEADS, ML_HEADS, ATT_WIDTH, ATT_KV_WIDTH, ATT_KV_WIDTH)
    splits = np.cumsum(sizes)[:-1].tolist()
    w_qk, w_v, w_o, w_i, w_f, w_aq, w_ak, w_av = jnp.split(w_in, splits, axis=-1)
    half = ATT_HEAD_DIM // 2

    def head(w, h):
        return w[:, h * ATT_HEAD_DIM:(h + 1) * ATT_HEAD_DIM]

    def q_tile(a, b):
        return [a[:, :half], b[:, :half], a[:, half:], b[:, half:]]

    q_cols = [t for p in range(ATT_Q_HEADS // 2) for t in q_tile(head(w_aq, 2 * p), head(w_aq, 2 * p + 1))]
    k_cols = [t for h in range(ATT_KV_HEADS) for t in q_tile(head(w_ak, h), head(w_ak, h))]
    v_cols = [t for h in range(ATT_KV_HEADS) for t in (head(w_av, h), head(w_av, h))]
    lane_pad = jnp.zeros((w_in.shape[0], LANES - ML_HEADS), w_in.dtype)
    packed = jnp.concatenate([w_qk, w_v, w_o] + q_cols + k_cols + v_cols + [w_i, lane_pad, w_f, lane_pad], axis=-1)
    return packed.astype(BF16)


def _inproj_kernel(x_ref, w_ref, cw_ref, cb_ref, q_ref, k_ref, v_ref, og_ref, aq_ref, ak_ref, av_ref, g_ref,
                   *scratch, blocks_per_seq):
    *ext_refs, xb_ref = scratch
    tm = x_ref.shape[0]
    halo = SUBLANES
    cs = ext_refs[0].shape[1]
    xb_ref[...] = x_ref[...].astype(BF16)

    @pl.when(pl.program_id(0) % blocks_per_seq == 0)
    def _():
        for ext_ref in ext_refs:
            ext_ref[0:halo, :] = jnp.zeros((halo, cs), F32)

    scale = ML_HEAD_DIM ** -0.5

    def mm(lo, hi):
        return _dot(xb_ref[...], w_ref[:, lo:hi])

    def conv_slice(idx):
        ext_ref = ext_refs[idx]
        c0 = idx * cs
        is_q = c0 < ML_WIDTH
        dst_ref, off = (q_ref, c0) if is_q else (k_ref, c0 - ML_WIDTH)
        rt = ML_CHUNK
        for r0 in range(0, tm, rt):
            conv = cb_ref[:, c0:c0 + cs]
            for j in range(CONV_WIDTH):
                start = halo + r0 - (CONV_WIDTH - 1) + j
                conv = conv + cw_ref[j:j + 1, c0:c0 + cs] * ext_ref[start:start + rt, :]
            act = conv * _sigmoid(conv)
            dst_ref[r0:r0 + rt, off:off + cs] = (act if is_q else act * scale).astype(BF16)
        ext_ref[0:halo, :] = ext_ref[tm:tm + halo, :]

    for idx in range(len(ext_refs)):
        ext_refs[idx][halo:halo + tm, :] = mm(C_QK + idx * cs, C_QK + (idx + 1) * cs)
        if idx > 0:
            conv_slice(idx - 1)
    half_v = ML_WIDTH // 2
    v_ref[:, 0:half_v] = mm(C_V, C_V + half_v).astype(BF16)
    conv_slice(len(ext_refs) - 1)
    v_ref[:, half_v:] = mm(C_V + half_v, C_O).astype(BF16)
    og_ref[...] = _sigmoid(mm(C_O, C_AQ)).astype(BF16)
    aq_ref[...] = mm(C_AQ, C_AK)
    ak_ref[...] = mm(C_AK, C_AV)
    av_ref[...] = mm(C_AV, C_G).astype(BF16)
    g_ref[...] = mm(C_G, C_END)


def _inproj(x2d, w_packed, conv_w, conv_b, tm, blocks_per_seq):
    t, d = x2d.shape
    widths = (ML_WIDTH, ML_WIDTH, ML_WIDTH, ML_WIDTH, C_AK - C_AQ, C_AV - C_AK, C_G - C_AV, C_END - C_G)
    dtypes = (BF16, BF16, BF16, BF16, F32, F32, BF16, F32)
    kern = functools.partial(_inproj_kernel, blocks_per_seq=blocks_per_seq)
    return pl.pallas_call(
        kern,
        grid=(t // tm,),
        in_specs=[pl.BlockSpec((tm, d), lambda i: (i, 0)),
                  pl.BlockSpec((d, C_END), lambda i: (0, 0)),
                  pl.BlockSpec((CONV_WIDTH, 2 * ML_WIDTH), lambda i: (0, 0)),
                  pl.BlockSpec((1, 2 * ML_WIDTH), lambda i: (0, 0))],
        out_specs=[pl.BlockSpec((tm, w), lambda i: (i, 0)) for w in widths],
        out_shape=[jax.ShapeDtypeStruct((t, w), dt) for w, dt in zip(widths, dtypes)],
        scratch_shapes=[pltpu.VMEM((tm + SUBLANES, 2 * LANES), F32)] * (2 * ML_WIDTH // (2 * LANES))
        + [pltpu.VMEM((tm, d), BF16)],
        compiler_params=pltpu.CompilerParams(dimension_semantics=("arbitrary",),
                                             vmem_limit_bytes=VMEM_LIMIT),
        name="inproj",
    )(x2d, w_packed, conv_w, conv_b)


def _time_scan(x, combine, identity):
    row = _iota(x.shape, 0)
    sh = 1
    while sh < x.shape[0]:
        x = combine(x, jnp.where(row >= sh, pltpu.roll(x, sh, 0), identity))
        sh *= 2
    return x


def _mlstm_kernel(q_ref, k_ref, v_ref, og_ref, g_ref, gb_ref, nw_ref, out_ref, ct_ref, m_ref, *, tq):
    s_idx = pl.program_id(1)
    L = ML_CHUNK
    D = ML_HEAD_DIM
    H = ML_HEADS
    heads = range(H)

    @pl.when(s_idx == 0)
    def _():
        ct_ref[...] = jnp.zeros_like(ct_ref)
        m_ref[...] = jnp.zeros_like(m_ref)

    causal = _iota((L, L), 1) <= _iota((L, L), 0)
    ones_blk = jnp.ones((L, D), BF16)
    mean_blk = jnp.full((D, D), 1.0 / D, BF16)
    m_prev = m_ref[0:1, :]
    head_lanes = _iota((L, LANES), 1) < H
    tile_of_lane = jnp.right_shift(_iota((LANES, H * L), 1), L.bit_length() - 1)
    spread = jnp.where(_iota((LANES, H * L), 0) == tile_of_lane, 1.0, 0.0).astype(BF16)

    def spread_heads(x):
        x = jnp.where(head_lanes, x, 0.0)
        hi = x.astype(BF16)
        lo = (x - hi.astype(F32)).astype(BF16)
        return _dot(hi, spread) + _dot(lo, spread)

    for c in range(tq // L):
        r0 = c * L
        gi = g_ref[r0:r0 + L, 0:LANES] + gb_ref[:, 0:LANES]
        gf = g_ref[r0:r0 + L, LANES:2 * LANES] + gb_ref[:, LANES:2 * LANES]
        b_cum = _time_scan(_log_sigmoid(gf), jnp.add, 0.0)
        r = gi - b_cum
        g = jnp.maximum(m_prev, _time_scan(r, jnp.maximum, NEG_INF))
        g_rep = spread_heads(g)
        b_rep = spread_heads(b_cum)
        b_last = b_cum[L - 1:L, :]
        m_new = jnp.maximum(b_last + m_prev, jnp.max(b_last + r, axis=0, keepdims=True))
        decay = jnp.exp(b_last + m_prev - m_new)
        shift = b_last - m_new
        r_t = r.T

        q_b = [q_ref[r0:r0 + L, h * D:(h + 1) * D] for h in heads]
        kt_b = [k_ref[r0:r0 + L, h * D:(h + 1) * D].T for h in heads]
        v_aug = [jnp.concatenate([v_ref[r0:r0 + L, h * D:(h + 1) * D], ones_blk], axis=-1) for h in heads]
        g_col = [g_rep[:, h * L:(h + 1) * L] for h in heads]
        w_intra = [jnp.exp(jnp.where(causal, r_t[h:h + 1, :] - g_col[h], NEG_INF)) for h in heads]
        s_b = [(_dot(q_b[h], kt_b[h]) * w_intra[h]).astype(BF16) for h in heads]
        ct = [ct_ref[h] for h in heads]
        inter = [_dot(q_b[h], ct[h].astype(BF16)) for h in heads]
        intra = [_dot(s_b[h], v_aug[h]) for h in heads]
        for h in heads:
            wi_col = jnp.exp(m_prev[:, h:h + 1] - g_col[h])
            clamp = jnp.exp(-(b_rep[:, h * L:(h + 1) * L] + g_col[h]))
            num = wi_col * inter[h][:, 0:D] + intra[h][:, 0:D]
            den = wi_col * inter[h][:, D:] + intra[h][:, D:]
            hh = num / jnp.maximum(jnp.abs(den), clamp)
            mu = _dot(hh.astype(BF16), mean_blk)
            hc = hh - mu
            var = _dot((hc * hc).astype(BF16), mean_blk)
            hn = hc * lax.rsqrt(var + LN_EPS) * nw_ref[:, h * D:(h + 1) * D]
            gate_o = og_ref[r0:r0 + L, h * D:(h + 1) * D].astype(F32)
            out_ref[r0:r0 + L, h * D:(h + 1) * D] = (gate_o * hn).astype(out_ref.dtype)
        for h in heads:
            w_row = jnp.exp(r_t[h:h + 1, :] + shift[:, h:h + 1])
            ktw = (kt_b[h].astype(F32) * w_row).astype(BF16)
            ct_ref[h] = decay[:, h:h + 1] * ct[h] + _dot(ktw, v_aug[h])
        m_prev = m_new

    m_ref[...] = jnp.broadcast_to(m_prev, m_ref.shape)


def _mlstm(q, k, v, og, g, gate_bias_row, norm_w_row, tq):
    b, s, _ = q.shape
    kern = functools.partial(_mlstm_kernel, tq=tq)

    def seq_spec(width):
        return pl.BlockSpec((None, tq, width), lambda bi, si: (bi, si, 0))

    def const_spec(shape):
        return pl.BlockSpec(shape, lambda bi, si: (0,) * len(shape))

    return pl.pallas_call(
        kern,
        grid=(b, s // tq),
        in_specs=[seq_spec(ML_WIDTH), seq_spec(ML_WIDTH), seq_spec(ML_WIDTH), seq_spec(ML_WIDTH),
                  seq_spec(2 * LANES), const_spec((1, 2 * LANES)), const_spec((1, ML_WIDTH))],
        out_specs=seq_spec(ML_WIDTH),
        out_shape=jax.ShapeDtypeStruct((b, s, ML_WIDTH), BF16),
        scratch_shapes=[pltpu.VMEM((ML_HEADS, ML_HEAD_DIM, 2 * ML_HEAD_DIM), F32),
                        pltpu.VMEM((SUBLANES, LANES), F32)],
        compiler_params=pltpu.CompilerParams(dimension_semantics=("arbitrary", "arbitrary"),
                                             vmem_limit_bytes=VMEM_LIMIT),
        name="mlstm",
    )(q, k, v, og, g, gate_bias_row, norm_w_row)


def _rope_tables(seq_len):
    half = ATT_HEAD_DIM // 2
    inv_freq = ROPE_THETA ** (-jnp.arange(half, dtype=F32) / half)
    ang = jnp.arange(seq_len, dtype=F32)[:, None] * inv_freq[None, :]
    cos = jnp.cos(ang)
    sin = jnp.sin(ang)
    cos_t = jnp.concatenate([cos, cos, cos, cos], axis=-1)
    sin_t = jnp.concatenate([-sin, -sin, sin, sin], axis=-1)
    return cos_t, sin_t


def _swa_kernel(sink_ref, aq_ref, ak_ref, av_ref, cos_ref, sin_ref, out_ref, kprev_ref, vprev_ref, *, nsub):
    step = pl.program_id(1)
    Lb = WINDOW
    half = ATT_HEAD_DIM // 2
    pairs = ATT_Q_HEADS // 2
    pairs_per_kv = pairs // ATT_KV_HEADS

    @pl.when(step == 0)
    def _():
        kprev_ref[...] = jnp.zeros_like(kprev_ref)
        vprev_ref[...] = jnp.zeros_like(vprev_ref)

    def rope(x, cos, sin):
        tiles = []
        for c in range(x.shape[-1] // LANES):
            xt = x[:, c * LANES:(c + 1) * LANES]
            tiles.append(xt * cos + pltpu.roll(xt, LANES // 2, 1) * sin)
        return jnp.concatenate(tiles, axis=-1)

    ql = _iota((Lb, 2 * Lb), 0)
    kj = _iota((Lb, 2 * Lb), 1)
    diff = Lb + ql - kj
    lane = _iota((Lb, LANES), 1)
    low_half = lane < ATT_HEAD_DIM
    first_head = (lane & half) == 0
    ones_blk = jnp.ones((2 * Lb, LANES), BF16)

    k_prev = kprev_ref
    v_prev = vprev_ref

    def block(j, carry):
        rows_j = pl.ds(pl.multiple_of(j * Lb, Lb), Lb)
        cos = cos_ref[rows_j, :]
        sin = sin_ref[rows_j, :]
        q = rope(aq_ref[rows_j, :], cos, sin) * (ATT_HEAD_DIM ** -0.5)
        k_cur = rope(ak_ref[rows_j, :], cos, sin).astype(BF16)
        v_cur = av_ref[rows_j, :]

        kpos = (step * nsub + j) * Lb + kj - Lb
        visible = jnp.where(diff >= 0, jnp.where(diff < WINDOW, jnp.where(kpos >= 0, 1, 0), 0), 0)
        bias = jnp.where(visible > 0, 0.0, NEG_INF).astype(F32)
        bias = jnp.concatenate([bias] * (2 * pairs_per_kv), axis=0)

        for g in range(ATT_KV_HEADS):
            kk = jnp.concatenate([k_prev[:, g * LANES:(g + 1) * LANES], k_cur[:, g * LANES:(g + 1) * LANES]], axis=0)
            vv = jnp.concatenate([v_prev[:, g * LANES:(g + 1) * LANES], v_cur[:, g * LANES:(g + 1) * LANES]], axis=0)
            vv_aug = jnp.concatenate([vv, ones_blk], axis=-1)
            rows = []
            sinks = []
            for p in range(pairs_per_kv):
                pair = g * pairs_per_kv + p
                q2 = q[:, pair * LANES:(pair + 1) * LANES]
                rows.append(jnp.where(first_head, q2, 0.0))
                rows.append(jnp.where(first_head, 0.0, q2))
                sinks.append(jnp.full((Lb, LANES), sink_ref[2 * pair], F32))
                sinks.append(jnp.full((Lb, LANES), sink_ref[2 * pair + 1], F32))
            qs = jnp.concatenate(rows, axis=0).astype(BF16)
            sink = jnp.concatenate(sinks, axis=0)

            sc = lax.dot_general(qs, kk, (((1,), (1,)), ((), ())), preferred_element_type=F32) + bias
            m = jnp.maximum(jnp.broadcast_to(jnp.max(sc, axis=-1, keepdims=True), sink.shape), sink)
            p_un = jnp.exp(sc - jnp.concatenate([m, m], axis=-1))
            acc = _dot(p_un.astype(BF16), vv_aug)
            o = acc[:, 0:LANES] / (acc[:, LANES:] + jnp.exp(sink - m))
            for p in range(pairs_per_kv):
                pair = g * pairs_per_kv + p
                even = o[(2 * p) * Lb:(2 * p + 1) * Lb, :]
                odd = o[(2 * p + 1) * Lb:(2 * p + 2) * Lb, :]
                out_ref[rows_j, pair * LANES:(pair + 1) * LANES] = (
                    jnp.where(low_half, even, odd).astype(out_ref.dtype))
        kprev_ref[...] = k_cur
        vprev_ref[...] = v_cur
        return carry

    lax.fori_loop(0, nsub, block, 0)


def _swa(aq, ak, av, cos_t, sin_t, sinks, nsub):
    b, s, _ = aq.shape
    kvw = ATT_KV_HEADS * LANES
    rows = nsub * WINDOW

    def seq_spec(width):
        return pl.BlockSpec((None, rows, width), lambda bi, si: (bi, si, 0))

    tab_spec = pl.BlockSpec((rows, LANES), lambda bi, si: (si, 0))
    return pl.pallas_call(
        functools.partial(_swa_kernel, nsub=nsub),
        grid=(b, s // rows),
        in_specs=[pl.BlockSpec(memory_space=pltpu.SMEM),
                  seq_spec(ATT_WIDTH), seq_spec(kvw), seq_spec(kvw), tab_spec, tab_spec],
        out_specs=seq_spec(ATT_WIDTH),
        out_shape=jax.ShapeDtypeStruct((b, s, ATT_WIDTH), BF16),
        scratch_shapes=[pltpu.VMEM((WINDOW, kvw), BF16), pltpu.VMEM((WINDOW, kvw), BF16)],
        compiler_params=pltpu.CompilerParams(dimension_semantics=("arbitrary", "arbitrary"),
                                             vmem_limit_bytes=VMEM_LIMIT),
        name="swa",
    )(sinks, aq, ak, av, cos_t, sin_t)


def _store_token_tiles(ref, val, row0=0, rows_per_token=None):
    n, w = val.shape
    segs = w // LANES
    rpt = rows_per_token or segs
    for j in range(segs):
        ref[pl.ds(row0 + j, n, stride=rpt), :] = val[:, j * LANES:(j + 1) * LANES]


def _load_token_tiles(ref, n, segs, row0=0, rows_per_token=None):
    rpt = rows_per_token or segs
    return jnp.concatenate([ref[pl.ds(row0 + j, n, stride=rpt), :] for j in range(segs)], axis=-1)


ROUTER_ROWS = 48


def _pack_router(w_group, b_group, w_expert, b_expert):
    d = w_group.shape[0]
    wt = jnp.zeros((ROUTER_ROWS, d), F32)
    wt = wt.at[0:N_GROUPS].set(w_group.T).at[SUBLANES:SUBLANES + N_EXPERTS].set(w_expert.T)
    bias = jnp.zeros((ROUTER_ROWS,), F32)
    bias = bias.at[0:N_GROUPS].set(b_group).at[SUBLANES:SUBLANES + N_EXPERTS].set(b_expert)
    hi = wt.astype(BF16)
    lo = (wt - hi.astype(F32)).astype(BF16)
    return jnp.concatenate([hi, lo], axis=0), jnp.broadcast_to(bias[:, None], (ROUTER_ROWS, LANES))


def _outproj_kernel(x_ref, ml_ref, att_ref, wo_ref, lnw_ref, lnb_ref, wrt_ref, brt_ref,
                    x1t_ref, meta_ref, cnt_ref, *, alpha):
    step = pl.program_id(0)
    tm, d = x_ref.shape
    segs = d // LANES
    pt = tm
    rr = wrt_ref.shape[0] // 2
    nt = (((1,), (1,)), ((), ()))
    row = _iota((SUBLANES, pt), 0).astype(F32)
    lane = _iota((pt, LANES), 1).astype(F32)

    def first_argmax(vals):
        top = jnp.max(vals, axis=0, keepdims=True)
        idx = jnp.min(jnp.where(vals == top, row, float(SUBLANES)), axis=0, keepdims=True)
        return top, idx

    @pl.when(step == 0)
    def _():
        cnt_ref[...] = jnp.zeros_like(cnt_ref)

    counts = jnp.zeros((1, LANES), F32)
    for r0 in range(0, tm, pt):
        y = (_dot(ml_ref[r0:r0 + pt, :], wo_ref[0:ML_WIDTH, :])
             + _dot(att_ref[r0:r0 + pt, :], wo_ref[ML_WIDTH:, :]))
        x1 = _layer_norm(alpha * x_ref[r0:r0 + pt, :] + y, lnw_ref[...], lnb_ref[...])
        _store_token_tiles(x1t_ref.at[pl.ds(r0 * segs, pt * segs)], x1)

        x1_hi = x1.astype(BF16)
        x1_lo = (x1 - x1_hi.astype(F32)).astype(BF16)
        both = lax.dot_general(wrt_ref[...], x1_hi, nt, preferred_element_type=F32)
        cross = lax.dot_general(wrt_ref[0:rr, :], x1_lo, nt, preferred_element_type=F32)
        logits = both[0:rr] + both[rr:2 * rr] + cross + jnp.concatenate([brt_ref[...]] * (pt // LANES), axis=1)

        g_logits = jnp.where(row < N_GROUPS, logits[0:SUBLANES], NEG_INF)
        g_top, g_idx = first_argmax(g_logits)
        g_p = 1.0 / jnp.sum(jnp.exp(g_logits - g_top), axis=0, keepdims=True)

        e_logits = logits[SUBLANES:2 * SUBLANES]
        for grp in range(1, N_GROUPS):
            e_logits = jnp.where(g_idx == grp, logits[(1 + grp) * SUBLANES:(2 + grp) * SUBLANES], e_logits)
        v1, a1 = first_argmax(e_logits)
        v2, a2 = first_argmax(jnp.where(row == a1, NEG_INF, e_logits))
        r = jnp.exp(v2 - v1)
        w1 = g_p / (1.0 + r)
        w2 = g_p * r / (1.0 + r)

        lo = jnp.minimum(a1, a2)
        hi = jnp.maximum(a1, a2)
        w_lo = jnp.where(a1 < a2, w1, w2)
        w_hi = jnp.where(a1 < a2, w2, w1)
        pair_idx = (EXPERTS_PER_GROUP - 1) * lo - lo * (lo - 1.0) * 0.5 + (hi - lo - 1.0)
        cls = g_idx * PAIRS_PER_GROUP + pair_idx

        meta_t = jnp.where(row == 0.0, cls, jnp.where(row == 1.0, w_lo, jnp.where(row == 2.0, w_hi, 0.0)))
        meta = jnp.concatenate([meta_t, jnp.zeros((LANES - SUBLANES, pt), F32)], axis=0).T
        meta_ref[r0:r0 + pt, :] = meta
        counts = counts + jnp.sum(jnp.where(lane == meta[:, 0:1], 1.0, 0.0), axis=0, keepdims=True)

    cnt_ref[0:1, :] += counts


def _outproj(x2d, ml2d, att2d, w_out_b, ln_w, ln_b, w_router, b_router, alpha, tm):
    t, d = x2d.shape
    kern = functools.partial(_outproj_kernel, alpha=alpha)

    def const_spec(shape):
        return pl.BlockSpec(shape, lambda i: (0,) * len(shape))

    return pl.pallas_call(
        kern,
        grid=(t // tm,),
        in_specs=[pl.BlockSpec((tm, d), lambda i: (i, 0)),
                  pl.BlockSpec((tm, ML_WIDTH), lambda i: (i, 0)),
                  pl.BlockSpec((tm, ATT_WIDTH), lambda i: (i, 0)),
                  const_spec((ML_WIDTH + ATT_WIDTH, d)), const_spec((1, d)), const_spec((1, d)),
                  const_spec((2 * ROUTER_ROWS, d)), const_spec((ROUTER_ROWS, LANES))],
        out_specs=[pl.BlockSpec((tm * (d // LANES), LANES), lambda i: (i, 0)),
                   pl.BlockSpec((tm, LANES), lambda i: (i, 0)), const_spec((SUBLANES, LANES))],
        out_shape=[jax.ShapeDtypeStruct((t * (d // LANES), LANES), F32),
                   jax.ShapeDtypeStruct((t, LANES), F32),
                   jax.ShapeDtypeStruct((SUBLANES, LANES), F32)],
        compiler_params=pltpu.CompilerParams(dimension_semantics=("arbitrary",),
                                             vmem_limit_bytes=VMEM_LIMIT),
        name="outproj",
    )(x2d, ml2d, att2d, w_out_b, ln_w, ln_b, w_router, b_router)


def _class_expert_table():
    tab = np.zeros((SUBLANES, LANES), np.float32)
    for g in range(N_GROUPS):
        idx = 0
        for lo in range(EXPERTS_PER_GROUP):
            for hi in range(lo + 1, EXPERTS_PER_GROUP):
                c = g * PAIRS_PER_GROUP + idx
                tab[0, c] = g * EXPERTS_PER_GROUP + lo
                tab[1, c] = g * EXPERTS_PER_GROUP + hi
                idx += 1
    return tab


def _rank_kernel(meta_ref, cnt_ref, tab_ref, pos_ref, tile_ref, base_ref, run_ref, *, tb, n_tiles_pad):
    step = pl.program_id(0)
    lane8 = _iota((SUBLANES, LANES), 1)

    @pl.when(step == 0)
    def _():
        cnt = jnp.broadcast_to(cnt_ref[0:1, :], (SUBLANES, LANES))
        tiles = jnp.floor((cnt + (MOE_TILE - 1.0)) * (1.0 / MOE_TILE))
        cum = tiles
        sh = 1
        while sh < LANES:
            cum = cum + jnp.where(lane8 >= sh, pltpu.roll(cum, sh, 1), 0.0)
            sh *= 2
        excl = cum - tiles
        base_ref[...] = excl * MOE_TILE
        run_ref[...] = jnp.zeros_like(run_ref)

        ti = _iota((n_tiles_pad, LANES), 0).astype(F32)
        lane = _iota((n_tiles_pad, LANES), 1)
        done = jnp.where(lane < N_CLASSES, jnp.where(cum[0:1, :] <= ti, 1.0, 0.0), 0.0)
        t_cls = jnp.sum(done, axis=-1, keepdims=True)
        sel = jnp.where(lane.astype(F32) == t_cls, 1.0, 0.0)
        cnt_i = jnp.sum(sel * cnt[0:1, :], axis=-1, keepdims=True)
        first_i = jnp.sum(sel * excl[0:1, :], axis=-1, keepdims=True)
        rows_i = jnp.clip(cnt_i - MOE_TILE * (ti[:, 0:1] - first_i), 0.0, float(MOE_TILE))
        e_lo = jnp.sum(sel * tab_ref[0:1, :], axis=-1, keepdims=True)
        e_hi = jnp.sum(sel * tab_ref[1:2, :], axis=-1, keepdims=True)
        n_tiles = jnp.sum(jnp.where(lane < N_CLASSES, jnp.broadcast_to(tiles[0:1, :], (n_tiles_pad, LANES)), 0.0),
                          axis=-1, keepdims=True)
        info = jnp.where(lane == 0, rows_i,
                         jnp.where(lane == 1, e_lo, jnp.where(lane == 2, e_hi, jnp.where(lane == 3, n_tiles, 0.0))))
        tile_ref[...] = info.astype(I32)

    cls = meta_ref[:, 0:1]
    lane = _iota((tb, LANES), 1).astype(F32)
    onehot = jnp.where(lane == cls, 1.0, 0.0)
    strict_lower = jnp.where(_iota((tb, tb), 1) < _iota((tb, tb), 0), 1.0, 0.0).astype(BF16)
    before = _dot(strict_lower, onehot.astype(BF16))
    slot = jnp.sum(onehot * (before + run_ref[0:1, :] + base_ref[0:1, :]), axis=-1, keepdims=True)
    run_ref[...] = run_ref[...] + jnp.sum(onehot, axis=0, keepdims=True)
    slot_t = jnp.broadcast_to(slot, (tb, LANES)).T
    pos_ref[...] = slot_t[0:SUBLANES, :].astype(I32)


def _rank(meta, counts, tb, n_tiles_pad):
    t = meta.shape[0]
    kern = functools.partial(_rank_kernel, tb=tb, n_tiles_pad=n_tiles_pad)
    tab = jnp.asarray(_class_expert_table())
    return pl.pallas_call(
        kern,
        grid=(t // tb,),
        in_specs=[pl.BlockSpec((tb, LANES), lambda i: (i, 0)),
                  pl.BlockSpec((SUBLANES, LANES), lambda i: (0, 0)),
                  pl.BlockSpec((SUBLANES, LANES), lambda i: (0, 0))],
        out_specs=[pl.BlockSpec((SUBLANES, tb), lambda i: (0, i)),
                   pl.BlockSpec((n_tiles_pad, LANES), lambda i: (0, 0))],
        out_shape=[jax.ShapeDtypeStruct((SUBLANES, t), I32),
                   jax.ShapeDtypeStruct((n_tiles_pad, LANES), I32)],
        scratch_shapes=[pltpu.VMEM((SUBLANES, LANES), F32), pltpu.VMEM((SUBLANES, LANES), F32)],
        compiler_params=pltpu.CompilerParams(dimension_semantics=("arbitrary",),
                                             vmem_limit_bytes=VMEM_LIMIT),
        name="rank",
    )(meta, counts, tab)


DMA_UNROLL = 8


def _pad_chunks():
    size, out = 1, []
    while size <= MOE_TILE:
        out.append(size)
        size *= 2
    return out[::-1]


def _dispatch_kernel(pos_ref, rows_ref, src_ref, dst_ref, zero_ref, sems, *, tb, rpt, n_tiles):
    step = pl.program_id(0)
    base = step * tb

    def pad_copies(act):
        def one_tile(i, carry):
            valid = rows_ref[i]
            pad = MOE_TILE - valid
            off = i * MOE_TILE + valid
            for size in _pad_chunks():
                hit = (pad & size) != 0

                @pl.when(hit)
                def _():
                    act(pltpu.make_async_copy(zero_ref.at[pl.ds(0, size * rpt)],
                                              dst_ref.at[pl.ds(pl.multiple_of(off * rpt, rpt), size * rpt)],
                                              sems.at[1]))
                off = off + jnp.where(hit, size, 0)
            return carry
        lax.fori_loop(0, n_tiles, one_tile, 0)

    @pl.when(step == 0)
    def _():
        zero_ref[...] = jnp.zeros_like(zero_ref)
        pad_copies(lambda cp: cp.start(priority=1))

    def issue(grp, carry):
        for u in range(DMA_UNROLL):
            j = grp * DMA_UNROLL + u
            src = src_ref.at[pl.ds(pl.multiple_of(j * rpt, rpt), rpt)]
            dst = dst_ref.at[pl.ds(pl.multiple_of(pos_ref[base + j] * rpt, rpt), rpt)]
            pltpu.make_async_copy(src, dst, sems.at[0]).start(priority=u % 2)
        return carry

    lax.fori_loop(0, tb // DMA_UNROLL, issue, 0)
    pltpu.make_async_copy(src_ref, dst_ref.at[pl.ds(0, tb * rpt)], sems.at[0]).wait()

    @pl.when(step == pl.num_programs(0) - 1)
    def _():
        pad_copies(lambda cp: cp.wait())


def _dispatch(pos, tile_rows, src, n_slots, rpt, tb):
    t = pos.shape[0]
    n_tiles = tile_rows.shape[0]
    kern = functools.partial(_dispatch_kernel, tb=tb, rpt=rpt, n_tiles=n_tiles)
    pad_rows = _pad_chunks()[0] * rpt
    return pl.pallas_call(
        kern,
        grid_spec=pltpu.PrefetchScalarGridSpec(
            num_scalar_prefetch=2,
            grid=(t // tb,),
            in_specs=[pl.BlockSpec((tb * rpt, LANES), lambda i, pos_ref, rows_ref: (i, 0))],
            out_specs=pl.BlockSpec(memory_space=pl.ANY),
            scratch_shapes=[pltpu.VMEM((pad_rows, LANES), src.dtype), pltpu.SemaphoreType.DMA((2,))],
        ),
        out_shape=jax.ShapeDtypeStruct((n_slots * rpt, LANES), src.dtype),
        compiler_params=pltpu.CompilerParams(dimension_semantics=("arbitrary",),
                                             has_side_effects=True, vmem_limit_bytes=VMEM_LIMIT),
        name="dispatch",
    )(pos, tile_rows, src)


def _collect_kernel(pos_ref, ys_ref, x1t_ref, meta_ref, lnw_ref, lnb_ref, out_ref, buf_ref, sems,
                    *, tb, alpha, d):
    segs = d // LANES
    rpt = 2 * segs
    step = pl.program_id(0)
    n_steps = pl.num_programs(0)

    def gather(blk, slot):
        base = blk * tb

        def issue(grp, carry):
            for u in range(DMA_UNROLL):
                j = grp * DMA_UNROLL + u
                src = ys_ref.at[pl.ds(pos_ref[base + j], 1)]
                dst = buf_ref.at[slot, pl.ds(j, 1)]
                pltpu.make_async_copy(src, dst, sems.at[slot]).start(priority=u % 2)
            return carry

        lax.fori_loop(0, tb // DMA_UNROLL, issue, 0)

    @pl.when(step == 0)
    def _():
        gather(0, 0)

    @pl.when(step + 1 < n_steps)
    def _():
        gather(step + 1, (step + 1) % 2)

    slot = step % 2
    pltpu.make_async_copy(ys_ref.at[pl.ds(0, tb)], buf_ref.at[slot], sems.at[slot]).wait()

    x1 = _load_token_tiles(x1t_ref, tb, segs)
    y_lo = buf_ref[slot, :, 0:d]
    y_hi = buf_ref[slot, :, d:2 * d]
    meta = meta_ref[...]
    z = alpha * x1 + meta[:, 1:2] * y_lo + meta[:, 2:3] * y_hi
    out_ref[...] = _layer_norm(z, lnw_ref[...], lnb_ref[...])


def _collect(pos, ys, x1t, meta, ln_w, ln_b, alpha, d, tb):
    t = pos.shape[0]
    segs = d // LANES
    kern = functools.partial(_collect_kernel, tb=tb, alpha=alpha, d=d)
    return pl.pallas_call(
        kern,
        grid_spec=pltpu.PrefetchScalarGridSpec(
            num_scalar_prefetch=1,
            grid=(t // tb,),
            in_specs=[pl.BlockSpec(memory_space=pl.ANY),
                      pl.BlockSpec((tb * segs, LANES), lambda i, pos_ref: (i, 0)),
                      pl.BlockSpec((tb, LANES), lambda i, pos_ref: (i, 0)),
                      pl.BlockSpec((1, d), lambda i, pos_ref: (0, 0)),
                      pl.BlockSpec((1, d), lambda i, pos_ref: (0, 0))],
            out_specs=pl.BlockSpec((tb, d), lambda i, pos_ref: (i, 0)),
            scratch_shapes=[pltpu.VMEM((2, tb, 2 * d), F32), pltpu.SemaphoreType.DMA((2,))],
        ),
        out_shape=jax.ShapeDtypeStruct((t, d), F32),
        compiler_params=pltpu.CompilerParams(dimension_semantics=("arbitrary",),
                                             vmem_limit_bytes=VMEM_LIMIT),
        name="collect",
    )(pos, ys, x1t, meta, ln_w, ln_b)


MOE_TILES_PER_STEP = 1


def _moe_kernel(rows_ref, elo_ref, ehi_ref, nt_ref, xs_ref, *refs, d):
    *w_refs, ys_ref = refs
    i = pl.program_id(0)
    segs = d // LANES

    @pl.when(rows_ref[i * MOE_TILES_PER_STEP] > 0)
    def _():
        for k in range(MOE_TILES_PER_STEP):
            wgu_lo, wd_lo, wgu_hi, wd_hi = w_refs[4 * k:4 * k + 4]
            xs_k = xs_ref.at[pl.ds(k * MOE_TILE * segs, MOE_TILE * segs)]
            xb = _load_token_tiles(xs_k, MOE_TILE, segs).astype(BF16)

            def expert(wgu_ref, wd_ref):
                de = wd_ref.shape[0]
                gate_up = _dot(xb, wgu_ref[...])
                gate = gate_up[:, 0:de]
                hidden = gate * _sigmoid(gate) * gate_up[:, de:]
                return _dot(hidden.astype(BF16), wd_ref[...])

            ys_ref[k * MOE_TILE:(k + 1) * MOE_TILE, 0:d] = expert(wgu_lo, wd_lo)
            ys_ref[k * MOE_TILE:(k + 1) * MOE_TILE, d:2 * d] = expert(wgu_hi, wd_hi)

    @pl.when(rows_ref[i * MOE_TILES_PER_STEP] <= 0)
    def _():
        ys_ref[...] = jnp.zeros_like(ys_ref)


def _moe(info, xs, wgu, wd, d, n_tiles_pad):
    de = wd.shape[1]
    segs = d // LANES
    tps = MOE_TILES_PER_STEP
    kern = functools.partial(_moe_kernel, d=d)

    def last_live(tile, nt_ref):
        return jnp.minimum(tile, jnp.maximum(nt_ref[0] - 1, 0))

    def up_spec(k, which):
        return pl.BlockSpec((None, d, 2 * de),
                            lambda i, rows, elo, ehi, nt: ((elo, ehi)[which][last_live(i * tps + k, nt)], 0, 0))

    def down_spec(k, which):
        return pl.BlockSpec((None, de, d),
                            lambda i, rows, elo, ehi, nt: ((elo, ehi)[which][last_live(i * tps + k, nt)], 0, 0))

    w_specs = []
    for k in range(tps):
        w_specs += [up_spec(k, 0), down_spec(k, 0), up_spec(k, 1), down_spec(k, 1)]
    rows, elo, ehi, nt = info[:, 0], info[:, 1], info[:, 2], info[0:1, 3]
    return pl.pallas_call(
        kern,
        grid_spec=pltpu.PrefetchScalarGridSpec(
            num_scalar_prefetch=4,
            grid=(n_tiles_pad // tps,),
            in_specs=[pl.BlockSpec((tps * MOE_TILE * segs, LANES),
                                   lambda i, rows, elo, ehi, nt: (last_live(i * tps, nt) // tps, 0))] + w_specs,
            out_specs=pl.BlockSpec((tps * MOE_TILE, 2 * d), lambda i, rows, elo, ehi, nt: (i, 0)),
        ),
        out_shape=jax.ShapeDtypeStruct((n_tiles_pad * MOE_TILE, 2 * d), F32),
        compiler_params=pltpu.CompilerParams(dimension_semantics=("arbitrary",),
                                             vmem_limit_bytes=VMEM_LIMIT),
        name="moe",
    )(rows, elo, ehi, nt, xs, *([wgu, wd, wgu, wd] * tps))


def _pick_block(n, target):
    blk = min(n, target)
    while n % blk:
        blk //= 2
    return blk


def kernel(x, w_in, conv_w, conv_b, mlstm_gate_bias, mlstm_norm_w, attn_sinks, w_out, ln1_w, ln1_b,
           w_group_router, b_group_router, w_expert_router, b_expert_router,
           w_exp_gate, w_exp_up, w_exp_down, ln2_w, ln2_b):
    b, s, d = x.shape
    t = b * s
    depth = w_in.shape[0]
    alpha = (2.0 * depth) ** 0.25
    assert s % ML_CHUNK == 0 and s % WINDOW == 0 and d % LANES == 0

    tm = _pick_block(s, 1024)
    tq = _pick_block(s, 8 * ML_CHUNK)
    tb_rank = _pick_block(t, 512)
    tb_dma = _pick_block(t, 4096)
    tb_col = _pick_block(t, 1024)
    n_tiles_pad = -(-(t // MOE_TILE + N_CLASSES) // SUBLANES) * SUBLANES
    cos_t, sin_t = _rope_tables(s)

    for l in range(depth):
        x2d = x.reshape(t, d)
        q, k, v, og, aq, ak, av, g = _inproj(x2d, _pack_w_in(w_in[l]), conv_w[l], conv_b[l][None, :], tm, s // tm)
        bias_pad = jnp.zeros((LANES - ML_HEADS,), F32)
        gate_bias_row = jnp.concatenate(
            [mlstm_gate_bias[l, 0], bias_pad, mlstm_gate_bias[l, 1], bias_pad])[None, :]
        ml = _mlstm(q.reshape(b, s, -1), k.reshape(b, s, -1), v.reshape(b, s, -1), og.reshape(b, s, -1),
                    g.reshape(b, s, -1), gate_bias_row, mlstm_norm_w[l][None, :], tq)
        att = _swa(aq.reshape(b, s, -1), ak.reshape(b, s, -1), av.reshape(b, s, -1), cos_t, sin_t, attn_sinks[l],
                   _pick_block(s // WINDOW, 8))

        w_router, b_router = _pack_router(w_group_router[l], b_group_router[l],
                                          w_expert_router[l], b_expert_router[l])
        x1t, meta, counts = _outproj(x2d, ml.reshape(t, -1), att.reshape(t, -1), w_out[l].astype(BF16),
                                     ln1_w[l][None, :], ln1_b[l][None, :], w_router, b_router, alpha, tm)

        pos2d, info = _rank(meta, counts, tb_rank, n_tiles_pad)
        pos = pos2d[0]
        xs = _dispatch(pos, info[:, 0], x1t, n_tiles_pad * MOE_TILE, d // LANES, tb_dma)
        w_gate_up = jnp.concatenate([w_exp_gate[l].astype(BF16), w_exp_up[l].astype(BF16)], axis=-1)
        ys = _moe(info, xs, w_gate_up, w_exp_down[l].astype(BF16), d, n_tiles_pad)
        out = _collect(pos, ys, x1t, meta, ln2_w[l][None, :], ln2_b[l][None, :], alpha, d, tb_col)
        x = out.reshape(b, s, d)
    return x
```
